```python
import math
import jax, jax.numpy as jnp
from jax import lax
import numpy as np

D_MODEL = 2048
BATCH = 8
SEQ = 8192
DEPTH = 2

N_EVEN = (DEPTH + 1) // 2
N_ODD = DEPTH // 2
NORM_EPS = 1e-6

ATTN_GROUPS = ((128, 1), (512, 4), (2048, 16))
ATTN_HEADS_PER_GROUP = 8
ATTN_HEAD_DIM = 64
ATTN_BLOCK = 128
ATTN_GROUP_WIDTH = ATTN_HEADS_PER_GROUP * ATTN_HEAD_DIM
ATTN_QKV_WIDTH = 3 * len(ATTN_GROUPS) * ATTN_GROUP_WIDTH
ATTN_OUT_WIDTH = ATTN_GROUP_WIDTH

POOL_WINDOWS = (2, 4, 8, 16)
POOL_GROUP_WIDTH = D_MODEL // 16
POOL_WIDTH = len(POOL_WINDOWS) * POOL_GROUP_WIDTH

EVEN_IN_WIDTH = ATTN_QKV_WIDTH + POOL_WIDTH
EVEN_OUT_WIDTH = ATTN_OUT_WIDTH + POOL_WIDTH

SSM_EXPAND = 2
SSM_D_INNER = SSM_EXPAND * D_MODEL
SSM_HEAD_DIM = 64
SSM_HEADS = SSM_D_INNER // SSM_HEAD_DIM
SSM_GROUPS = 8
SSM_STATE = 128
SSM_CONV = 4
SSM_CHUNK = 128
SSM_CONV_DIM = SSM_D_INNER + 2 * SSM_GROUPS * SSM_STATE
SSM_IN_WIDTH = SSM_D_INNER + SSM_CONV_DIM + SSM_HEADS

FFN_HIDDEN = 4 * D_MODEL

kernel_name = "hybrid_dilated_attn_pool_ssd_adaln"


def rmsnorm(x, g):
    xf = x.astype(jnp.float32)
    y = xf * lax.rsqrt(jnp.mean(xf * xf, axis=-1, keepdims=True) + NORM_EPS)
    return (y * g.astype(jnp.float32)).astype(x.dtype)


def modulate(h, shift, scale):
    return h * (1 + scale[:, None, :]) + shift[:, None, :]


def dilated_window_attention(q, k, v, dilation, n_back):
    b, s, h, e = q.shape
    L = s // dilation
    nb = -(-L // ATTN_BLOCK)
    Lp = nb * ATTN_BLOCK

    def to_sub(t):
        t = t.reshape(b, L, dilation, h, e).transpose(0, 2, 3, 1, 4)
        return jnp.pad(t, ((0, 0), (0, 0), (0, 0), (0, Lp - L), (0, 0)))

    def windows(t):
        t = jnp.pad(to_sub(t), ((0, 0), (0, 0), (0, 0), (ATTN_BLOCK, 0), (0, 0)))
        t = t.reshape(b, dilation, h, nb + 1, ATTN_BLOCK, e)
        return jnp.concatenate([t[:, :, :, :-1], t[:, :, :, 1:]], axis=4)

    qs = to_sub(q).reshape(b, dilation, h, nb, ATTN_BLOCK, e)
    ks, vs = windows(k), windows(v)
    scores = jnp.einsum('bdhnqe,bdhnke->bdhnqk', qs, ks).astype(jnp.float32) * (e ** -0.5)
    qi = jnp.arange(ATTN_BLOCK)[:, None]
    kj = jnp.arange(2 * ATTN_BLOCK)[None, :]
    dist = qi + ATTN_BLOCK - kj
    kpos = jnp.arange(nb)[:, None, None] * ATTN_BLOCK - ATTN_BLOCK + kj[None]
    mask = (dist >= 0) & (dist <= n_back) & (kpos >= 0)
    scores = jnp.where(mask, scores, -jnp.inf)
    m = jnp.max(scores, axis=-1, keepdims=True)
    p = jnp.exp(scores - m)
    den = jnp.sum(p, axis=-1, keepdims=True)
    o = jnp.einsum('bdhnqk,bdhnke->bdhnqe', (p / den).astype(v.dtype), vs)
    lse = (m + jnp.log(den))[..., 0]

    def from_sub(t):
        t = t.reshape(b, dilation, h, Lp, *t.shape[5:])[:, :, :, :L]
        t = jnp.moveaxis(t, 3, 1)
        return t.reshape(b, s, h, *t.shape[4:])

    return from_sub(o), from_sub(lse)


def multiscale_pool(u, pool_w, pool_scale):
    b, s, _ = u.shape
    ug = u.reshape(b, s, len(POOL_WINDOWS), POOL_GROUP_WIDTH).astype(jnp.float32)
    cs = jnp.pad(jnp.cumsum(ug, axis=1), ((0, 0), (1, 0), (0, 0), (0, 0)))
    t = jnp.arange(s)
    diffs = []
    for gi, w in enumerate(POOL_WINDOWS):
        csg = cs[:, :, gi]
        lo = jnp.maximum(t + 1 - w, 0)
        cnt = jnp.minimum(t + 1, w).astype(jnp.float32)
        mean = (csg[:, 1:] - csg[:, lo]) / cnt[None, :, None]
        diffs.append(mean - ug[:, :, gi])
    d = jnp.stack(diffs, axis=2).astype(u.dtype)
    y = jnp.einsum('bsgc,gce->bsge', d, pool_w)
    return y.reshape(b, s, POOL_WIDTH) * pool_scale


def attn_pool_mixer(h, w_in, pool_w, pool_scale, w_out):
    b, s, _ = h.shape
    proj = h @ w_in
    qkv = proj[..., :ATTN_QKV_WIDTH].reshape(b, s, 3, len(ATTN_GROUPS), ATTN_HEADS_PER_GROUP, ATTN_HEAD_DIM)
    u = proj[..., ATTN_QKV_WIDTH:]
    outs, lses = [], []
    for gi, (window, dil) in enumerate(ATTN_GROUPS):
        o, lse = dilated_window_attention(qkv[:, :, 0, gi], qkv[:, :, 1, gi], qkv[:, :, 2, gi], dil, window // dil)
        outs.append(o)
        lses.append(lse)
    wts = jax.nn.softmax(jnp.stack(lses, axis=0), axis=0)
    attn = jnp.sum(wts[..., None] * jnp.stack(outs, axis=0).astype(jnp.float32), axis=0)
    attn = attn.astype(h.dtype).reshape(b, s, ATTN_OUT_WIDTH)
    pool = multiscale_pool(u, pool_w, pool_scale)
    return jnp.concatenate([attn, pool], axis=-1) @ w_out


def causal_depthwise_conv(x, w, bias):
    y = lax.conv_general_dilated(x, w[:, None, :], window_strides=(1,), padding=[(SSM_CONV - 1, 0)],
                                 dimension_numbers=('NWC', 'WIO', 'NWC'), feature_group_count=x.shape[-1])
    return y + bias


def ssd_scan(x, dt, A, bm, cm):
    b, s, h, p = x.shape
    g, n = bm.shape[2], bm.shape[3]
    r = h // g
    q = SSM_CHUNK
    nc = s // q
    xc = x.astype(jnp.float32).reshape(b, nc, q, g, r, p)
    bc = bm.astype(jnp.float32).reshape(b, nc, q, g, n)
    cc = cm.astype(jnp.float32).reshape(b, nc, q, g, n)
    dtc = dt.reshape(b, nc, q, g, r)
    a = jnp.cumsum(dtc * A.reshape(g, r), axis=2)
    xdt = xc * dtc[..., None]
    causal = jnp.tril(jnp.ones((q, q), dtype=bool))[:, :, None, None]
    decay = jnp.exp(jnp.where(causal, a[:, :, :, None] - a[:, :, None], -jnp.inf))
    cb = jnp.einsum('bclgn,bcsgn->bclsg', cc, bc)
    y_diag = jnp.einsum('bclsgr,bcsgrp->bclgrp', cb[..., None] * decay, xdt)
    decay_to_end = jnp.exp(a[:, :, -1:] - a)
    states = jnp.einsum('bcsgn,bcsgrp->bcgrpn', bc, xdt * decay_to_end[..., None])
    chunk_decay = jnp.exp(a[:, :, -1])

    def step(carry, inp):
        st, dec = inp
        return carry * dec[..., None, None] + st, carry

    init = jnp.zeros((b, g, r, p, n), jnp.float32)
    _, h_in = lax.scan(step, init, (jnp.moveaxis(states, 1, 0), jnp.moveaxis(chunk_decay, 1, 0)))
    h_in = jnp.moveaxis(h_in, 0, 1)
    y_off = jnp.einsum('bclgn,bcgrpn->bclgrp', cc, h_in) * jnp.exp(a)[..., None]
    return (y_diag + y_off).reshape(b, s, h, p)


def gated_group_rmsnorm(y, z, g):
    yf = y.astype(jnp.float32) * jax.nn.silu(z.astype(jnp.float32))
    b, s, dim = yf.shape
    yg = yf.reshape(b, s, SSM_GROUPS, dim // SSM_GROUPS)
    yg = yg * lax.rsqrt(jnp.mean(yg * yg, axis=-1, keepdims=True) + NORM_EPS)
    return (yg.reshape(b, s, dim) * g.astype(jnp.float32)).astype(z.dtype)


def ssd_mixer(h, w_in, conv_w, conv_b, dt_bias, a_log, d_skip, norm_g, w_out):
    b, s, _ = h.shape
    proj = h @ w_in
    z = proj[..., :SSM_D_INNER]
    xbc = proj[..., SSM_D_INNER:SSM_D_INNER + SSM_CONV_DIM]
    dt = proj[..., SSM_D_INNER + SSM_CONV_DIM:]
    xbc = jax.nn.silu(causal_depthwise_conv(xbc, conv_w, conv_b))
    xs = xbc[..., :SSM_D_INNER].reshape(b, s, SSM_HEADS, SSM_HEAD_DIM)
    bm = xbc[..., SSM_D_INNER:SSM_D_INNER + SSM_GROUPS * SSM_STATE].reshape(b, s, SSM_GROUPS, SSM_STATE)
    cm = xbc[..., SSM_D_INNER + SSM_GROUPS * SSM_STATE:].reshape(b, s, SSM_GROUPS, SSM_STATE)
    dt = jax.nn.softplus((dt + dt_bias).astype(jnp.float32))
    A = -jnp.exp(a_log.astype(jnp.float32))
    y = ssd_scan(xs, dt, A, bm, cm)
    y = y + d_skip.astype(jnp.float32)[:, None] * xs.astype(jnp.float32)
    y = gated_group_rmsnorm(y.reshape(b, s, SSM_D_INNER), z, norm_g)
    return y @ w_out


def squared_relu_mlp(h, w1, w2):
    return jnp.square(jax.nn.relu(h @ w1)) @ w2


def _fwd_setup_inputs(seed: int = 0) -> dict:
    key = jax.random.key(seed)
    ks = jax.random.split(key, 24)
    f32 = jnp.float32
    nrm = lambda k, shape, scale: jax.random.normal(k, shape, f32) * scale
    D = D_MODEL
    dt0 = jnp.exp(jax.random.uniform(ks[16], (N_ODD, SSM_HEADS), f32, minval=math.log(1e-3), maxval=math.log(1e-1)))
    return {
        "x": nrm(ks[0], (BATCH, SEQ, D), 1.0),
        "c": nrm(ks[1], (BATCH, D), 1.0),
        "ada_w": nrm(ks[2], (DEPTH, D, 6 * D), 0.5 * D ** -0.5),
        "ada_b": nrm(ks[3], (DEPTH, 6 * D), 0.02),
        "norm_mix": 1.0 + nrm(ks[4], (DEPTH, D), 0.02),
        "norm_ffn": 1.0 + nrm(ks[5], (DEPTH, D), 0.02),
        "ffn_w1": nrm(ks[6], (DEPTH, D, FFN_HIDDEN), D ** -0.5),
        "ffn_w2": nrm(ks[7], (DEPTH, FFN_HIDDEN, D), FFN_HIDDEN ** -0.5),
        "even_w_in": nrm(ks[8], (N_EVEN, D, EVEN_IN_WIDTH), D ** -0.5),
        "pool_w": nrm(ks[9], (N_EVEN, len(POOL_WINDOWS), POOL_GROUP_WIDTH, POOL_GROUP_WIDTH), POOL_GROUP_WIDTH ** -0.5),
        "pool_scale": 1.0 + nrm(ks[10], (N_EVEN, POOL_WIDTH), 0.02),
        "even_w_out": nrm(ks[11], (N_EVEN, EVEN_OUT_WIDTH, D), EVEN_OUT_WIDTH ** -0.5),
        "ssm_w_in": nrm(ks[12], (N_ODD, D, SSM_IN_WIDTH), D ** -0.5),
        "ssm_conv_w": nrm(ks[13], (N_ODD, SSM_CONV, SSM_CONV_DIM), SSM_CONV ** -0.5),
        "ssm_conv_b": nrm(ks[14], (N_ODD, SSM_CONV_DIM), 0.02),
        "ssm_dt_bias": dt0 + jnp.log(-jnp.expm1(-dt0)),
        "ssm_a_log": jnp.log(jax.random.uniform(ks[15], (N_ODD, SSM_HEADS), f32, minval=1.0, maxval=16.0)),
        "ssm_d": 1.0 + nrm(ks[17], (N_ODD, SSM_HEADS), 0.02),
        "ssm_norm": 1.0 + nrm(ks[18], (N_ODD, SSM_D_INNER), 0.02),
        "ssm_w_out": nrm(ks[19], (N_ODD, SSM_D_INNER, D), SSM_D_INNER ** -0.5),
        "final_norm": 1.0 + nrm(ks[20], (D,), 0.02),
    }


def _fwd_reference(x, c, ada_w, ada_b, norm_mix, norm_ffn, ffn_w1, ffn_w2, even_w_in, pool_w, pool_scale,
              even_w_out, ssm_w_in, ssm_conv_w, ssm_conv_b, ssm_dt_bias, ssm_a_log, ssm_d, ssm_norm,
              ssm_w_out, final_norm):
    cond = jax.nn.silu(c)
    for i in range(DEPTH):
        mod = cond @ ada_w[i] + ada_b[i]
        sh1, sc1, g1, sh2, sc2, g2 = jnp.split(mod, 6, axis=-1)
        h = modulate(rmsnorm(x, norm_mix[i]), sh1, sc1)
        j = i // 2
        if i % 2 == 0:
            y = attn_pool_mixer(h, even_w_in[j], pool_w[j], pool_scale[j], even_w_out[j])
        else:
            y = ssd_mixer(h, ssm_w_in[j], ssm_conv_w[j], ssm_conv_b[j], ssm_dt_bias[j], ssm_a_log[j],
                          ssm_d[j], ssm_norm[j], ssm_w_out[j])
        x = x + g1[:, None, :] * y
        h = modulate(rmsnorm(x, norm_ffn[i]), sh2, sc2)
        x = x + g2[:, None, :] * squared_relu_mlp(h, ffn_w1[i], ffn_w2[i])
    return rmsnorm(x, final_norm)


import jax as _jax
import jax.numpy as _jnp

TWIN_FORMAT = 'train_step'
FWD_PARAMS = ['x', 'c', 'ada_w', 'ada_b', 'norm_mix', 'norm_ffn', 'ffn_w1', 'ffn_w2', 'even_w_in', 'pool_w', 'pool_scale', 'even_w_out', 'ssm_w_in', 'ssm_conv_w', 'ssm_conv_b', 'ssm_dt_bias', 'ssm_a_log', 'ssm_d', 'ssm_norm', 'ssm_w_out', 'final_norm']
TWIN_WEIGHTS = ['ada_w', 'ada_b', 'norm_mix', 'norm_ffn', 'ffn_w1', 'ffn_w2', 'even_w_in', 'pool_w', 'pool_scale', 'even_w_out', 'ssm_w_in', 'ssm_conv_w', 'ssm_conv_b', 'ssm_dt_bias', 'ssm_a_log', 'ssm_d', 'ssm_norm', 'ssm_w_out', 'final_norm']
TWIN_DIFF_INPUT = 'x'
TWIN_INPUTS = ['x', 'c', 'ada_w', 'ada_b', 'norm_mix', 'norm_ffn', 'ffn_w1', 'ffn_w2', 'even_w_in', 'pool_w', 'pool_scale', 'even_w_out', 'ssm_w_in', 'ssm_conv_w', 'ssm_conv_b', 'ssm_dt_bias', 'ssm_a_log', 'ssm_d', 'ssm_norm', 'ssm_w_out', 'final_norm', 'loss_target', 'm_ada_w', 'm_ada_b', 'm_norm_mix', 'm_norm_ffn', 'm_ffn_w1', 'm_ffn_w2', 'm_even_w_in', 'm_pool_w', 'm_pool_scale', 'm_even_w_out', 'm_ssm_w_in', 'm_ssm_conv_w', 'm_ssm_conv_b', 'm_ssm_dt_bias', 'm_ssm_a_log', 'm_ssm_d', 'm_ssm_norm', 'm_ssm_w_out', 'm_final_norm', 'v_ada_w', 'v_ada_b', 'v_norm_mix', 'v_norm_ffn', 'v_ffn_w1', 'v_ffn_w2', 'v_even_w_in', 'v_pool_w', 'v_pool_scale', 'v_even_w_out', 'v_ssm_w_in', 'v_ssm_conv_w', 'v_ssm_conv_b', 'v_ssm_dt_bias', 'v_ssm_a_log', 'v_ssm_d', 'v_ssm_norm', 'v_ssm_w_out', 'v_final_norm']
TWIN_OUTPUTS = ['loss', 'grad_x', 'grad_ada_w', 'grad_ada_b', 'grad_norm_mix', 'grad_norm_ffn', 'grad_ffn_w1', 'grad_ffn_w2', 'grad_even_w_in', 'grad_pool_w', 'grad_pool_scale', 'grad_even_w_out', 'grad_ssm_w_in', 'grad_ssm_conv_w', 'grad_ssm_conv_b', 'grad_ssm_dt_bias', 'grad_ssm_a_log', 'grad_ssm_d', 'grad_ssm_norm', 'grad_ssm_w_out', 'grad_final_norm', 'delta_ada_w', 'delta_ada_b', 'delta_norm_mix', 'delta_norm_ffn', 'delta_ffn_w1', 'delta_ffn_w2', 'delta_even_w_in', 'delta_pool_w', 'delta_pool_scale', 'delta_even_w_out', 'delta_ssm_w_in', 'delta_ssm_conv_w', 'delta_ssm_conv_b', 'delta_ssm_dt_bias', 'delta_ssm_a_log', 'delta_ssm_d', 'delta_ssm_norm', 'delta_ssm_w_out', 'delta_final_norm', 'new_m_ada_w', 'new_m_ada_b', 'new_m_norm_mix', 'new_m_norm_ffn', 'new_m_ffn_w1', 'new_m_ffn_w2', 'new_m_even_w_in', 'new_m_pool_w', 'new_m_pool_scale', 'new_m_even_w_out', 'new_m_ssm_w_in', 'new_m_ssm_conv_w', 'new_m_ssm_conv_b', 'new_m_ssm_dt_bias', 'new_m_ssm_a_log', 'new_m_ssm_d', 'new_m_ssm_norm', 'new_m_ssm_w_out', 'new_m_final_norm', 'new_v_ada_w', 'new_v_ada_b', 'new_v_norm_mix', 'new_v_norm_ffn', 'new_v_ffn_w1', 'new_v_ffn_w2', 'new_v_even_w_in', 'new_v_pool_w', 'new_v_pool_scale', 'new_v_even_w_out', 'new_v_ssm_w_in', 'new_v_ssm_conv_w', 'new_v_ssm_conv_b', 'new_v_ssm_dt_bias', 'new_v_ssm_a_log', 'new_v_ssm_d', 'new_v_ssm_norm', 'new_v_ssm_w_out', 'new_v_final_norm']
TWIN_LEAF_KINDS = {'loss': 'loss', 'grad_x': 'grad_x', 'grad_ada_w': 'grad_w', 'grad_ada_b': 'grad_w', 'grad_norm_mix': 'grad_w', 'grad_norm_ffn': 'grad_w', 'grad_ffn_w1': 'grad_w', 'grad_ffn_w2': 'grad_w', 'grad_even_w_in': 'grad_w', 'grad_pool_w': 'grad_w', 'grad_pool_scale': 'grad_w', 'grad_even_w_out': 'grad_w', 'grad_ssm_w_in': 'grad_w', 'grad_ssm_conv_w': 'grad_w', 'grad_ssm_conv_b': 'grad_w', 'grad_ssm_dt_bias': 'grad_w', 'grad_ssm_a_log': 'grad_w', 'grad_ssm_d': 'grad_w', 'grad_ssm_norm': 'grad_w', 'grad_ssm_w_out': 'grad_w', 'grad_final_norm': 'grad_w', 'delta_ada_w': 'delta_w', 'delta_ada_b': 'delta_w', 'delta_norm_mix': 'delta_w', 'delta_norm_ffn': 'delta_w', 'delta_ffn_w1': 'delta_w', 'delta_ffn_w2': 'delta_w', 'delta_even_w_in': 'delta_w', 'delta_pool_w': 'delta_w', 'delta_pool_scale': 'delta_w', 'delta_even_w_out': 'delta_w', 'delta_ssm_w_in': 'delta_w', 'delta_ssm_conv_w': 'delta_w', 'delta_ssm_conv_b': 'delta_w', 'delta_ssm_dt_bias': 'delta_w', 'delta_ssm_a_log': 'delta_w', 'delta_ssm_d': 'delta_w', 'delta_ssm_norm': 'delta_w', 'delta_ssm_w_out': 'delta_w', 'delta_final_norm': 'delta_w', 'new_m_ada_w': 'new_m', 'new_m_ada_b': 'new_m', 'new_m_norm_mix': 'new_m', 'new_m_norm_ffn': 'new_m', 'new_m_ffn_w1': 'new_m', 'new_m_ffn_w2': 'new_m', 'new_m_even_w_in': 'new_m', 'new_m_pool_w': 'new_m', 'new_m_pool_scale': 'new_m', 'new_m_even_w_out': 'new_m', 'new_m_ssm_w_in': 'new_m', 'new_m_ssm_conv_w': 'new_m', 'new_m_ssm_conv_b': 'new_m', 'new_m_ssm_dt_bias': 'new_m', 'new_m_ssm_a_log': 'new_m', 'new_m_ssm_d': 'new_m', 'new_m_ssm_norm': 'new_m', 'new_m_ssm_w_out': 'new_m', 'new_m_final_norm': 'new_m', 'new_v_ada_w': 'new_v', 'new_v_ada_b': 'new_v', 'new_v_norm_mix': 'new_v', 'new_v_norm_ffn': 'new_v', 'new_v_ffn_w1': 'new_v', 'new_v_ffn_w2': 'new_v', 'new_v_even_w_in': 'new_v', 'new_v_pool_w': 'new_v', 'new_v_pool_scale': 'new_v', 'new_v_even_w_out': 'new_v', 'new_v_ssm_w_in': 'new_v', 'new_v_ssm_conv_w': 'new_v', 'new_v_ssm_conv_b': 'new_v', 'new_v_ssm_dt_bias': 'new_v', 'new_v_ssm_a_log': 'new_v', 'new_v_ssm_d': 'new_v', 'new_v_ssm_norm': 'new_v', 'new_v_ssm_w_out': 'new_v', 'new_v_final_norm': 'new_v'}


def _forward(args):
    return _fwd_reference(*[args[k] for k in FWD_PARAMS])


def _output_shape():
    def fwd():
        inp = _fwd_setup_inputs(0)
        return _fwd_reference(*[inp[k] for k in FWD_PARAMS])
    out = _jax.eval_shape(fwd)
    return out.shape, out.dtype

N_MICROBATCH = 1
ADAM_LR = 0.001
ADAM_B1 = 0.9
ADAM_B2 = 0.999
ADAM_EPS = 1e-08
ADAM_WD = 0.01
ADAM_STEP = 10
PER_EXAMPLE_BATCH_AXIS = {'x': 0, 'c': 0, 'loss_target': 0}
SHARED_INPUTS = []
_WEIGHT_DTYPES = {'ada_w': _jnp.float32, 'ada_b': _jnp.float32, 'norm_mix': _jnp.float32, 'norm_ffn': _jnp.float32, 'ffn_w1': _jnp.float32, 'ffn_w2': _jnp.float32, 'even_w_in': _jnp.float32, 'pool_w': _jnp.float32, 'pool_scale': _jnp.float32, 'even_w_out': _jnp.float32, 'ssm_w_in': _jnp.float32, 'ssm_conv_w': _jnp.float32, 'ssm_conv_b': _jnp.float32, 'ssm_dt_bias': _jnp.float32, 'ssm_a_log': _jnp.float32, 'ssm_d': _jnp.float32, 'ssm_norm': _jnp.float32, 'ssm_w_out': _jnp.float32, 'final_norm': _jnp.float32}
MOMENT_SCALE = {'ada_w': 5.921885e-02, 'ada_b': 1.075394e-01, 'norm_mix': 3.930970e-02, 'norm_ffn': 5.364282e-02, 'ffn_w1': 2.804269e-02, 'ffn_w2': 5.215839e-02, 'even_w_in': 1.792677e-02, 'pool_w': 5.093652e-02, 'pool_scale': 5.217306e-02, 'even_w_out': 2.679997e-02, 'ssm_w_in': 2.269903e-02, 'ssm_conv_w': 2.043907e-02, 'ssm_conv_b': 2.604293e-02, 'ssm_dt_bias': 5.040882e-02, 'ssm_a_log': 9.669621e-02, 'ssm_d': 1.113080e-01, 'ssm_norm': 2.302056e-02, 'ssm_w_out': 3.397593e-02, 'final_norm': 3.225486e+01}


def _to_microbatches(a, axis):
    t = _jnp.moveaxis(a, axis, 0)
    t = t.reshape((N_MICROBATCH, t.shape[0] // N_MICROBATCH) + t.shape[1:])
    return _jnp.moveaxis(t, 1, axis + 1)


def setup_inputs(seed: int = 0) -> dict:
    inp = _fwd_setup_inputs(seed)
    key = _jax.random.fold_in(_jax.random.key(seed), 7919)
    shape, _ = _output_shape()
    out = dict(inp)
    out["loss_target"] = _jax.random.normal(_jax.random.fold_in(key, 0), shape, _jnp.float32)
    for i, name in enumerate(TWIN_WEIGHTS):
        w = inp[name].astype(_jnp.float32)
        if MOMENT_SCALE is None:
            s = _jnp.sqrt(_jnp.mean(_jnp.square(w)) + 1e-30)
        else:
            s = MOMENT_SCALE[name]
        km, kv = _jax.random.split(_jax.random.fold_in(key, i + 1))
        out[name] = w
        out["m_" + name] = s * _jax.random.normal(km, w.shape, _jnp.float32)
        out["v_" + name] = (s * s) * _jax.random.uniform(kv, w.shape, _jnp.float32, 0.5, 1.5)
    if N_MICROBATCH > 1:
        for name, axis in PER_EXAMPLE_BATCH_AXIS.items():
            out[name] = _to_microbatches(out[name], axis)
    return {'x': out['x'], 'c': out['c'], 'ada_w': out['ada_w'], 'ada_b': out['ada_b'], 'norm_mix': out['norm_mix'], 'norm_ffn': out['norm_ffn'], 'ffn_w1': out['ffn_w1'], 'ffn_w2': out['ffn_w2'], 'even_w_in': out['even_w_in'], 'pool_w': out['pool_w'], 'pool_scale': out['pool_scale'], 'even_w_out': out['even_w_out'], 'ssm_w_in': out['ssm_w_in'], 'ssm_conv_w': out['ssm_conv_w'], 'ssm_conv_b': out['ssm_conv_b'], 'ssm_dt_bias': out['ssm_dt_bias'], 'ssm_a_log': out['ssm_a_log'], 'ssm_d': out['ssm_d'], 'ssm_norm': out['ssm_norm'], 'ssm_w_out': out['ssm_w_out'], 'final_norm': out['final_norm'], 'loss_target': out['loss_target'], 'm_ada_w': out['m_ada_w'], 'm_ada_b': out['m_ada_b'], 'm_norm_mix': out['m_norm_mix'], 'm_norm_ffn': out['m_norm_ffn'], 'm_ffn_w1': out['m_ffn_w1'], 'm_ffn_w2': out['m_ffn_w2'], 'm_even_w_in': out['m_even_w_in'], 'm_pool_w': out['m_pool_w'], 'm_pool_scale': out['m_pool_scale'], 'm_even_w_out': out['m_even_w_out'], 'm_ssm_w_in': out['m_ssm_w_in'], 'm_ssm_conv_w': out['m_ssm_conv_w'], 'm_ssm_conv_b': out['m_ssm_conv_b'], 'm_ssm_dt_bias': out['m_ssm_dt_bias'], 'm_ssm_a_log': out['m_ssm_a_log'], 'm_ssm_d': out['m_ssm_d'], 'm_ssm_norm': out['m_ssm_norm'], 'm_ssm_w_out': out['m_ssm_w_out'], 'm_final_norm': out['m_final_norm'], 'v_ada_w': out['v_ada_w'], 'v_ada_b': out['v_ada_b'], 'v_norm_mix': out['v_norm_mix'], 'v_norm_ffn': out['v_norm_ffn'], 'v_ffn_w1': out['v_ffn_w1'], 'v_ffn_w2': out['v_ffn_w2'], 'v_even_w_in': out['v_even_w_in'], 'v_pool_w': out['v_pool_w'], 'v_pool_scale': out['v_pool_scale'], 'v_even_w_out': out['v_even_w_out'], 'v_ssm_w_in': out['v_ssm_w_in'], 'v_ssm_conv_w': out['v_ssm_conv_w'], 'v_ssm_conv_b': out['v_ssm_conv_b'], 'v_ssm_dt_bias': out['v_ssm_dt_bias'], 'v_ssm_a_log': out['v_ssm_a_log'], 'v_ssm_d': out['v_ssm_d'], 'v_ssm_norm': out['v_ssm_norm'], 'v_ssm_w_out': out['v_ssm_w_out'], 'v_final_norm': out['v_final_norm']}


def _loss(weights, diff, rest, loss_target):
    with _jax.named_scope("forward"):
        args = {**rest, TWIN_DIFF_INPUT: diff, **{k: w.astype(_WEIGHT_DTYPES[k]) for k, w in weights.items()}}
        y = _forward(args)
    with _jax.named_scope("loss_head"):
        err = _jnp.square(y.astype(_jnp.float32) - loss_target)
        return 0.5 * _jnp.sum(_jnp.mean(err, axis=-1)) if err.ndim else 0.5 * err


def _adamw(w, g, m, v):
    m = ADAM_B1 * m + (1.0 - ADAM_B1) * g
    v = ADAM_B2 * v + (1.0 - ADAM_B2) * _jnp.square(g)
    m_hat = m / (1.0 - ADAM_B1 ** ADAM_STEP)
    v_hat = v / (1.0 - ADAM_B2 ** ADAM_STEP)
    delta = -ADAM_LR * (m_hat / (_jnp.sqrt(v_hat) + ADAM_EPS) + ADAM_WD * w)
    return delta, m, v


def reference(x, c, ada_w, ada_b, norm_mix, norm_ffn, ffn_w1, ffn_w2, even_w_in, pool_w, pool_scale, even_w_out, ssm_w_in, ssm_conv_w, ssm_conv_b, ssm_dt_bias, ssm_a_log, ssm_d, ssm_norm, ssm_w_out, final_norm, loss_target, m_ada_w, m_ada_b, m_norm_mix, m_norm_ffn, m_ffn_w1, m_ffn_w2, m_even_w_in, m_pool_w, m_pool_scale, m_even_w_out, m_ssm_w_in, m_ssm_conv_w, m_ssm_conv_b, m_ssm_dt_bias, m_ssm_a_log, m_ssm_d, m_ssm_norm, m_ssm_w_out, m_final_norm, v_ada_w, v_ada_b, v_norm_mix, v_norm_ffn, v_ffn_w1, v_ffn_w2, v_even_w_in, v_pool_w, v_pool_scale, v_even_w_out, v_ssm_w_in, v_ssm_conv_w, v_ssm_conv_b, v_ssm_dt_bias, v_ssm_a_log, v_ssm_d, v_ssm_norm, v_ssm_w_out, v_final_norm):
    given = dict(x=x, c=c, ada_w=ada_w, ada_b=ada_b, norm_mix=norm_mix, norm_ffn=norm_ffn, ffn_w1=ffn_w1, ffn_w2=ffn_w2, even_w_in=even_w_in, pool_w=pool_w, pool_scale=pool_scale, even_w_out=even_w_out, ssm_w_in=ssm_w_in, ssm_conv_w=ssm_conv_w, ssm_conv_b=ssm_conv_b, ssm_dt_bias=ssm_dt_bias, ssm_a_log=ssm_a_log, ssm_d=ssm_d, ssm_norm=ssm_norm, ssm_w_out=ssm_w_out, final_norm=final_norm, loss_target=loss_target, m_ada_w=m_ada_w, m_ada_b=m_ada_b, m_norm_mix=m_norm_mix, m_norm_ffn=m_norm_ffn, m_ffn_w1=m_ffn_w1, m_ffn_w2=m_ffn_w2, m_even_w_in=m_even_w_in, m_pool_w=m_pool_w, m_pool_scale=m_pool_scale, m_even_w_out=m_even_w_out, m_ssm_w_in=m_ssm_w_in, m_ssm_conv_w=m_ssm_conv_w, m_ssm_conv_b=m_ssm_conv_b, m_ssm_dt_bias=m_ssm_dt_bias, m_ssm_a_log=m_ssm_a_log, m_ssm_d=m_ssm_d, m_ssm_norm=m_ssm_norm, m_ssm_w_out=m_ssm_w_out, m_final_norm=m_final_norm, v_ada_w=v_ada_w, v_ada_b=v_ada_b, v_norm_mix=v_norm_mix, v_norm_ffn=v_norm_ffn, v_ffn_w1=v_ffn_w1, v_ffn_w2=v_ffn_w2, v_even_w_in=v_even_w_in, v_pool_w=v_pool_w, v_pool_scale=v_pool_scale, v_even_w_out=v_even_w_out, v_ssm_w_in=v_ssm_w_in, v_ssm_conv_w=v_ssm_conv_w, v_ssm_conv_b=v_ssm_conv_b, v_ssm_dt_bias=v_ssm_dt_bias, v_ssm_a_log=v_ssm_a_log, v_ssm_d=v_ssm_d, v_ssm_norm=v_ssm_norm, v_ssm_w_out=v_ssm_w_out, v_final_norm=v_final_norm)
    weights = {n: given[n] for n in TWIN_WEIGHTS}
    shared = {n: given[n] for n in SHARED_INPUTS}
    per_example = {n: given[n] for n in ['x', 'c']}
    grad_fn = _jax.value_and_grad(_loss, argnums=(0, 1))

    def one_microbatch(ex, loss_target):
        ex = dict(ex)
        diff = ex.pop(TWIN_DIFF_INPUT)
        return grad_fn(weights, diff, {**shared, **ex}, loss_target)

    if N_MICROBATCH == 1:
        loss, (grad_w, grad_x) = one_microbatch(per_example, given["loss_target"])
    else:
        def body(carry, xs):
            loss_sum, grad_sum = carry
            l_k, (gw_k, gx_k) = one_microbatch(xs[0], xs[1])
            with _jax.named_scope("update"):
                return (loss_sum + l_k, _jax.tree.map(_jnp.add, grad_sum, gw_k)), gx_k

        init = (_jnp.zeros((), _jnp.float32), _jax.tree.map(_jnp.zeros_like, weights))
        (loss, grad_w), grad_x = _jax.lax.scan(body, init, (per_example, given["loss_target"]))
    with _jax.named_scope("update"):
        delta_w, new_m, new_v = {}, {}, {}
        for n in TWIN_WEIGHTS:
            delta_w[n], new_m[n], new_v[n] = _adamw(weights[n], grad_w[n], given["m_" + n], given["v_" + n])
    return (loss, grad_x, *[grad_w[n] for n in TWIN_WEIGHTS], *[delta_w[n] for n in TWIN_WEIGHTS],
            *[new_m[n] for n in TWIN_WEIGHTS], *[new_v[n] for n in TWIN_WEIGHTS])
```

```python
import functools
import math

import jax
import jax.numpy as jnp
from jax import lax
from jax.experimental import pallas as pl
from jax.experimental.pallas import tpu as pltpu

F32 = jnp.float32
BF16 = jnp.bfloat16
MESH = pl.DeviceIdType.MESH
ANY = pl.BlockSpec(memory_space=pl.ANY)
VMEM_FULL = pl.BlockSpec(memory_space=pltpu.VMEM)

NORM_EPS = 1e-6
N_CHIPS = 4
N_DEV = 8
LANES = 128
VMEM_LIMIT = 56 << 20

D_MODEL = 2048
ATTN_GROUPS = ((128, 1), (512, 4), (2048, 16))
ATTN_BLOCK = 128
ATTN_W = 512
HEAD_DIM = 64
POOL_WINDOWS = (2, 4, 8, 16)
POOL_W = 512
N_SLABS = 10
SSM_INNER = 4096
SSM_HEADS = 64
SSM_GROUPS = 8
SSM_STATE = 128
SSM_CHUNK = 128
SSM_CONV_DIM = 6144
SSM_GW = SSM_INNER // SSM_GROUPS
FFN_HIDDEN = 8192

ADAM_LR, ADAM_B1, ADAM_B2, ADAM_EPS, ADAM_WD, ADAM_STEP = 0.001, 0.9, 0.999, 1e-08, 0.01, 10


def _cparams(sem=None):
    return pltpu.CompilerParams(dimension_semantics=sem, vmem_limit_bytes=VMEM_LIMIT)


def _place():
    return lax.axis_index("x"), lax.axis_index("y"), lax.axis_index("c")


def _allgather8(v, name):
    m_per, n = v.shape

    def body(x_ref, out_ref, send_sems, recv_sems, local_sem):
        x, y, c = _place()
        me, sibling = (x, y, c), (x, y, 1 - c)
        chips = [(1 - x, y), (x, 1 - y), (1 - x, 1 - y)]

        def rows(px, py, pc):
            return out_ref.at[pl.ds((4 * px + 2 * py + pc) * m_per, m_per), :]

        def copy(k, block, to, src=None):
            return pltpu.make_async_remote_copy(
                src_ref=rows(*block) if src is None else src, dst_ref=rows(*block),
                send_sem=send_sems.at[k], recv_sem=recv_sems.at[k], device_id=to, device_id_type=MESH)

        mine = pltpu.make_async_copy(x_ref, rows(*me), local_sem)
        mine.start()
        first = [copy(0, me, sibling, src=x_ref)]
        first += [copy(1 + j, me, (*chip, c), src=x_ref) for j, chip in enumerate(chips)]
        for cp in first:
            cp.start()
        passed = [copy(4 + j, (*chip, c), sibling) for j, chip in enumerate(chips)]
        for j, chip in enumerate(chips):
            copy(1 + j, (*chip, c), me).wait_recv()
            passed[j].start()
        copy(0, sibling, me).wait_recv()
        for j, chip in enumerate(chips):
            copy(4 + j, (*chip, 1 - c), me).wait_recv()
        for cp in first + passed:
            cp.wait_send()
        mine.wait()

    return pl.pallas_call(
        body, name=name,
        out_shape=jax.ShapeDtypeStruct((N_DEV * m_per, n), v.dtype),
        in_specs=[VMEM_FULL], out_specs=VMEM_FULL,
        scratch_shapes=[pltpu.SemaphoreType.DMA((7,)), pltpu.SemaphoreType.DMA((7,)), pltpu.SemaphoreType.DMA],
    )(v)


def _chip_allgather(w, name):
    r, ccols = w.shape
    h = r // 2

    def body(w_ref, out_ref, send_sems, recv_sems, local_sem):
        x, y, c = _place()
        k_me = 2 * x + y
        sibling = (x, y, 1 - c)
        chips = [(1 - x, y), (x, 1 - y), (1 - x, 1 - y)]

        def half(chip, cc):
            return out_ref.at[2 * chip[0] + chip[1], pl.ds(cc * h, h), :]

        def copy(k, dst, to, src):
            return pltpu.make_async_remote_copy(src_ref=src, dst_ref=dst, send_sem=send_sems.at[k],
                                                recv_sem=recv_sems.at[k], device_id=to, device_id_type=MESH)

        mine = pltpu.make_async_copy(w_ref, out_ref.at[k_me], local_sem)
        mine.start()
        my_half = w_ref.at[pl.ds(c * h, h), :]
        first = [copy(j, half((x, y), c), (*chip, c), my_half) for j, chip in enumerate(chips)]
        for cp in first:
            cp.start()
        passed = [copy(3 + j, half(chip, c), sibling, half(chip, c)) for j, chip in enumerate(chips)]
        for j, chip in enumerate(chips):
            copy(j, half(chip, c), sibling, my_half).wait_recv()
            passed[j].start()
        for j, chip in enumerate(chips):
            copy(3 + j, half(chip, 1 - c), sibling, my_half).wait_recv()
        for cp in first + passed:
            cp.wait_send()
        mine.wait()

    return pl.pallas_call(
        body, name=name,
        out_shape=jax.ShapeDtypeStruct((N_CHIPS, r, ccols), w.dtype),
        in_specs=[ANY], out_specs=ANY,
        scratch_shapes=[pltpu.SemaphoreType.DMA((6,)), pltpu.SemaphoreType.DMA((6,)), pltpu.SemaphoreType.DMA],
    )(w)


def _sibling_send_other_half(g, name):
    n, r, ccols = g.shape
    h = r // 2

    def body(g_ref, out_ref, send_sem, recv_sem):
        x, y, c = _place()
        cp = pltpu.make_async_remote_copy(
            src_ref=g_ref.at[:, pl.ds((1 - c) * h, h), :], dst_ref=out_ref, send_sem=send_sem, recv_sem=recv_sem,
            device_id=(x, y, 1 - c), device_id_type=MESH)
        cp.start()
        cp.wait()

    return pl.pallas_call(
        body, name=name, out_shape=jax.ShapeDtypeStruct((n, h, ccols), g.dtype),
        in_specs=[ANY], out_specs=ANY,
        scratch_shapes=[pltpu.SemaphoreType.DMA, pltpu.SemaphoreType.DMA],
    )(g)


def _chip_alltoall(p, name):
    n, h, ccols = p.shape

    def body(p_ref, out_ref, send_sems, recv_sems, local_sem):
        x, y, c = _place()
        k_me = 2 * x + y
        chips = [(1 - x, y), (x, 1 - y), (1 - x, 1 - y)]
        mine = pltpu.make_async_copy(p_ref.at[k_me], out_ref.at[k_me], local_sem)
        mine.start()
        cps = []
        for j, chip in enumerate(chips):
            cps.append(pltpu.make_async_remote_copy(
                src_ref=p_ref.at[2 * chip[0] + chip[1]], dst_ref=out_ref.at[k_me],
                send_sem=send_sems.at[j], recv_sem=recv_sems.at[j], device_id=(*chip, c), device_id_type=MESH))
        for cp in cps:
            cp.start()
        for j, chip in enumerate(chips):
            pltpu.make_async_remote_copy(
                src_ref=p_ref.at[k_me], dst_ref=out_ref.at[2 * chip[0] + chip[1]],
                send_sem=send_sems.at[j], recv_sem=recv_sems.at[j], device_id=(*chip, c), device_id_type=MESH).wait_recv()
        for cp in cps:
            cp.wait_send()
        mine.wait()

    return pl.pallas_call(
        body, name=name, out_shape=jax.ShapeDtypeStruct((n, h, ccols), p.dtype),
        in_specs=[ANY], out_specs=ANY,
        scratch_shapes=[pltpu.SemaphoreType.DMA((3,)), pltpu.SemaphoreType.DMA((3,)), pltpu.SemaphoreType.DMA],
    )(p)


def _sibling_complete(f, name):
    h, ccols = f.shape

    def body(f_ref, out_ref, send_sem, recv_sem, local_sem):
        x, y, c = _place()
        mine = pltpu.make_async_copy(f_ref, out_ref.at[pl.ds(c * h, h), :], local_sem)
        mine.start()
        cp = pltpu.make_async_remote_copy(
            src_ref=f_ref, dst_ref=out_ref.at[pl.ds(c * h, h), :], send_sem=send_sem, recv_sem=recv_sem,
            device_id=(x, y, 1 - c), device_id_type=MESH)
        cp.start()
        pltpu.make_async_remote_copy(
            src_ref=f_ref, dst_ref=out_ref.at[pl.ds((1 - c) * h, h), :], send_sem=send_sem, recv_sem=recv_sem,
            device_id=(x, y, 1 - c), device_id_type=MESH).wait_recv()
        cp.wait_send()
        mine.wait()

    return pl.pallas_call(
        body, name=name, out_shape=jax.ShapeDtypeStruct((2 * h, ccols), f.dtype),
        in_specs=[ANY], out_specs=ANY,
        scratch_shapes=[pltpu.SemaphoreType.DMA, pltpu.SemaphoreType.DMA, pltpu.SemaphoreType.DMA],
    )(f)


def _row_tile(rows, cols, itemsize, budget=2 << 20):
    t = rows
    while t % 2 == 0 and t * cols * itemsize > budget and (t // 2) % 16 == 0:
        t //= 2
    return t


def _add_own_half(g, recv, name):
    n, r, ccols = g.shape
    h = r // 2
    t = _row_tile(h, ccols, 4)
    nt = h // t
    c_idx = lax.axis_index("c").astype(jnp.int32).reshape(1)

    def body(c_ref, g_ref, r_ref, o_ref):
        o_ref[...] = (g_ref[...].astype(F32) + r_ref[...].astype(F32)).astype(o_ref.dtype)

    grid_spec = pltpu.PrefetchScalarGridSpec(
        num_scalar_prefetch=1, grid=(n, nt),
        in_specs=[pl.BlockSpec((None, t, ccols), lambda j, i, c_ref: (j, c_ref[0] * nt + i, 0)),
                  pl.BlockSpec((None, t, ccols), lambda j, i, c_ref: (j, i, 0))],
        out_specs=pl.BlockSpec((None, t, ccols), lambda j, i, c_ref: (j, i, 0)))
    return pl.pallas_call(body, name=name, grid_spec=grid_spec,
                          out_shape=jax.ShapeDtypeStruct((n, h, ccols), BF16),
                          compiler_params=_cparams(("parallel", "parallel")))(c_idx, g, recv)


def _sum_slots(q, name):
    n, h, ccols = q.shape
    t = _row_tile(h, ccols, 4)

    def body(q_ref, o_ref):
        acc = q_ref[0].astype(F32)
        for j in range(1, n):
            acc = acc + q_ref[j].astype(F32)
        o_ref[...] = acc

    return pl.pallas_call(
        body, name=name, grid=(h // t,),
        in_specs=[pl.BlockSpec((n, t, ccols), lambda i: (0, i, 0))],
        out_specs=pl.BlockSpec((t, ccols), lambda i: (i, 0)),
        out_shape=jax.ShapeDtypeStruct((h, ccols), F32), compiler_params=_cparams(("parallel",)))(q)


def _reduce_scatter(g, tag):
    recv = _sibling_send_other_half(g, f"rs_pair_{tag}")
    part = _add_own_half(g, recv, f"rs_add2_{tag}")
    got = _chip_alltoall(part, f"rs_a2a_{tag}")
    fin = _sum_slots(got, f"rs_add4_{tag}")
    return _sibling_complete(fin, f"rs_fin_{tag}")


def _sum_devices(v, name):
    n, r, ccols = v.shape
    t = 8
    while r % (t * 2) == 0 and t * 2 * ccols * 4 * n <= (8 << 20):
        t *= 2

    def body(v_ref, o_ref):
        acc = v_ref[0]
        for j in range(1, n):
            acc = acc + v_ref[j]
        o_ref[...] = acc

    return pl.pallas_call(
        body, name=name, grid=(r // t,),
        in_specs=[pl.BlockSpec((n, t, ccols), lambda i: (0, i, 0))],
        out_specs=pl.BlockSpec((t, ccols), lambda i: (i, 0)),
        out_shape=jax.ShapeDtypeStruct((r, ccols), F32), compiler_params=_cparams(("parallel",)))(v)


def _adamw(w, g, m, v, name):
    r, ccols = w.shape
    t = _row_tile(r, ccols, 4, budget=1 << 20)
    c1 = 1.0 / (1.0 - ADAM_B1 ** ADAM_STEP)
    c2 = 1.0 / (1.0 - ADAM_B2 ** ADAM_STEP)

    def body(w_ref, g_ref, m_ref, v_ref, d_ref, nm_ref, nv_ref):
        gg = g_ref[...]
        nm = ADAM_B1 * m_ref[...] + (1.0 - ADAM_B1) * gg
        nv = ADAM_B2 * v_ref[...] + (1.0 - ADAM_B2) * (gg * gg)
        d_ref[...] = -ADAM_LR * ((nm * c1) / (jnp.sqrt(nv * c2) + ADAM_EPS) + ADAM_WD * w_ref[...])
        nm_ref[...] = nm
        nv_ref[...] = nv

    spec = pl.BlockSpec((t, ccols), lambda i: (i, 0))
    sds = jax.ShapeDtypeStruct((r, ccols), F32)
    return pl.pallas_call(body, name=name, grid=(r // t,), in_specs=[spec] * 4, out_specs=[spec] * 3,
                          out_shape=[sds] * 3, compiler_params=_cparams(("parallel",)))(w, g, m, v)


NN = (((1,), (0,)), ((), ()))
NT = (((1,), (1,)), ((), ()))
TN = (((0,), (0,)), ((), ()))


def _mm_core(name, a, b, *, dims, grid, a_spec, b_spec, out_shapes, out_specs, acc_shape,
             extras=(), extra_specs=(), epilogue=None):
    nk = grid[2]
    n_ex, n_out = len(extras), len(out_shapes)
    if epilogue is None:
        def epilogue(acc, ex, outs):
            outs[0][...] = acc.astype(outs[0].dtype)

    def body(*refs):
        a_ref, b_ref = refs[0], refs[1]
        ex = refs[2:2 + n_ex]
        outs = refs[2 + n_ex:2 + n_ex + n_out]
        part = lax.dot_general(a_ref[...].astype(BF16), b_ref[...].astype(BF16), dims, preferred_element_type=F32)
        if nk == 1:
            epilogue(part, ex, outs)
        else:
            acc = refs[-1]
            k = pl.program_id(2)

            @pl.when(k == 0)
            def _():
                acc[...] = part

            @pl.when(k > 0)
            def _():
                acc[...] += part

            @pl.when(k == nk - 1)
            def _():
                epilogue(acc[...], ex, outs)

    scratch = [] if nk == 1 else [pltpu.VMEM(acc_shape, F32)]
    res = pl.pallas_call(
        body, name=name, grid=grid, in_specs=[a_spec, b_spec, *extra_specs], out_specs=list(out_specs),
        out_shape=list(out_shapes), scratch_shapes=scratch,
        compiler_params=_cparams(("parallel", "parallel", "arbitrary")))(a, b, *extras)
    return res


def _mm_nn(name, a, b, tm, tn, tk, out_dtypes=(BF16,), extras=(), extra_specs=(), epilogue=None):
    m, kk = a.shape
    n = b.shape[1]
    tm, tn, tk = min(tm, m), min(tn, n), min(tk, kk)
    grid = (m // tm, n // tn, kk // tk)
    return _mm_core(
        name, a, b, dims=NN, grid=grid,
        a_spec=pl.BlockSpec((tm, tk), lambda i, j, k: (i, k)), b_spec=pl.BlockSpec((tk, tn), lambda i, j, k: (k, j)),
        out_shapes=[jax.ShapeDtypeStruct((m, n), dt) for dt in out_dtypes],
        out_specs=[pl.BlockSpec((tm, tn), lambda i, j, k: (i, j)) for _ in out_dtypes],
        acc_shape=(tm, tn), extras=extras, extra_specs=extra_specs, epilogue=epilogue)


def _mm_nt(name, a, b, tm, tn, tk, out_dtypes=(BF16,), extras=(), extra_specs=(), epilogue=None):
    m, kk = a.shape
    n = b.shape[0]
    tm, tn, tk = min(tm, m), min(tn, n), min(tk, kk)
    grid = (m // tm, n // tn, kk // tk)
    return _mm_core(
        name, a, b, dims=NT, grid=grid,
        a_spec=pl.BlockSpec((tm, tk), lambda i, j, k: (i, k)), b_spec=pl.BlockSpec((tn, tk), lambda i, j, k: (j, k)),
        out_shapes=[jax.ShapeDtypeStruct((m, n), dt) for dt in out_dtypes],
        out_specs=[pl.BlockSpec((tm, tn), lambda i, j, k: (i, j)) for _ in out_dtypes],
        acc_shape=(tm, tn), extras=extras, extra_specs=extra_specs, epilogue=epilogue)


def _mm_tn(name, a, b, tm, tn, tk, out_dtype=BF16):
    kk, m = a.shape
    n = b.shape[1]
    tm, tn, tk = min(tm, m), min(tn, n), min(tk, kk)
    grid = (m // tm, n // tn, kk // tk)
    return _mm_core(
        name, a, b, dims=TN, grid=grid,
        a_spec=pl.BlockSpec((tk, tm), lambda i, j, k: (k, i)), b_spec=pl.BlockSpec((tk, tn), lambda i, j, k: (k, j)),
        out_shapes=[jax.ShapeDtypeStruct((m, n), out_dtype)],
        out_specs=[pl.BlockSpec((tm, tn), lambda i, j, k: (i, j))], acc_shape=(tm, tn))[0]


def _resid_gate_epilogue(acc, ex, outs):
    outs[0][...] = ex[0][...] + ex[1][...] * acc
    outs[1][...] = acc.astype(BF16)


def _mm_resid(name, a, b, resid, gate, tm, tn, tk):
    return _mm_nn(
        name, a, b, tm, tn, tk, out_dtypes=(F32, BF16), extras=(resid, gate),
        extra_specs=(pl.BlockSpec((tm, tn), lambda i, j, k: (i, j)), pl.BlockSpec((1, tn), lambda i, j, k: (0, j))),
        epilogue=_resid_gate_epilogue)


TOK_TILE = 512


def _colsum8(v):
    t, ccols = v.shape
    return jnp.sum(v.reshape(t // 8, 8, ccols), axis=0)


def _norm_mod_fwd(x, g, sc, sh, name):
    s, d = x.shape
    t = TOK_TILE

    def body(x_ref, g_ref, sc_ref, sh_ref, h_ref):
        xv = x_ref[...]
        n = xv * lax.rsqrt(jnp.mean(xv * xv, axis=-1, keepdims=True) + NORM_EPS)
        h_ref[...] = ((n * g_ref[...]) * (1.0 + sc_ref[...]) + sh_ref[...]).astype(BF16)

    row = pl.BlockSpec((1, d), lambda i: (0, 0))
    return pl.pallas_call(
        body, name=name, grid=(s // t,), in_specs=[pl.BlockSpec((t, d), lambda i: (i, 0)), row, row, row],
        out_specs=pl.BlockSpec((t, d), lambda i: (i, 0)), out_shape=jax.ShapeDtypeStruct((s, d), BF16),
        compiler_params=_cparams(("parallel",)))(x, g, sc, sh)


def _norm_mod_bwd(x, dh, resid, g, sc, name):
    s, d = x.shape
    t = TOK_TILE

    def body(x_ref, dh_ref, r_ref, g_ref, sc_ref, dx_ref, dsh_ref, dsc_ref, dg_ref):
        i = pl.program_id(0)
        xv = x_ref[...]
        rstd = lax.rsqrt(jnp.mean(xv * xv, axis=-1, keepdims=True) + NORM_EPS)
        n = xv * rstd
        dhv = dh_ref[...].astype(F32)
        gv = g_ref[...]
        dyn = dhv * (1.0 + sc_ref[...])
        dn = dyn * gv
        dx_ref[...] = r_ref[...] + rstd * (dn - n * jnp.mean(dn * n, axis=-1, keepdims=True))

        @pl.when(i == 0)
        def _():
            dsh_ref[...] = jnp.zeros_like(dsh_ref)
            dsc_ref[...] = jnp.zeros_like(dsc_ref)
            dg_ref[...] = jnp.zeros_like(dg_ref)

        dsh_ref[...] += _colsum8(dhv)
        dsc_ref[...] += _colsum8(dhv * (n * gv))
        dg_ref[...] += _colsum8(dyn * n)

    tile = pl.BlockSpec((t, d), lambda i: (i, 0))
    row = pl.BlockSpec((1, d), lambda i: (0, 0))
    acc = pl.BlockSpec((8, d), lambda i: (0, 0))
    acc_s = jax.ShapeDtypeStruct((8, d), F32)
    return pl.pallas_call(
        body, name=name, grid=(s // t,), in_specs=[tile, tile, tile, row, row],
        out_specs=[tile, acc, acc, acc], out_shape=[jax.ShapeDtypeStruct((s, d), F32), acc_s, acc_s, acc_s],
        compiler_params=_cparams(("arbitrary",)))(x, dh, resid, g, sc)


def _gate_bwd(dxo, y, gate, name):
    s, d = dxo.shape
    t = TOK_TILE

    def body(dx_ref, y_ref, g_ref, dy_ref, dg_ref):
        i = pl.program_id(0)
        dxv = dx_ref[...]
        dy_ref[...] = (dxv * g_ref[...]).astype(BF16)

        @pl.when(i == 0)
        def _():
            dg_ref[...] = jnp.zeros_like(dg_ref)

        dg_ref[...] += _colsum8(dxv * y_ref[...].astype(F32))

    tile = pl.BlockSpec((t, d), lambda i: (i, 0))
    return pl.pallas_call(
        body, name=name, grid=(s // t,), in_specs=[tile, tile, pl.BlockSpec((1, d), lambda i: (0, 0))],
        out_specs=[tile, pl.BlockSpec((8, d), lambda i: (0, 0))],
        out_shape=[jax.ShapeDtypeStruct((s, d), BF16), jax.ShapeDtypeStruct((8, d), F32)],
        compiler_params=_cparams(("arbitrary",)))(dxo, y, gate)


def _final_loss(x, g, target, name):
    s, d = x.shape
    t = TOK_TILE
    inv_d = 1.0 / d

    def body(x_ref, g_ref, t_ref, dx_ref, sq_ref, dg_ref):
        i = pl.program_id(0)
        xv = x_ref[...]
        rstd = lax.rsqrt(jnp.mean(xv * xv, axis=-1, keepdims=True) + NORM_EPS)
        n = xv * rstd
        gv = g_ref[...]
        err = n * gv - t_ref[...]
        dout = err * inv_d
        dn = dout * gv
        dx_ref[...] = rstd * (dn - n * jnp.mean(dn * n, axis=-1, keepdims=True))

        @pl.when(i == 0)
        def _():
            sq_ref[...] = jnp.zeros_like(sq_ref)
            dg_ref[...] = jnp.zeros_like(dg_ref)

        sq_ref[...] += _colsum8(err * err)
        dg_ref[...] += _colsum8(dout * n)

    tile = pl.BlockSpec((t, d), lambda i: (i, 0))
    acc = pl.BlockSpec((8, d), lambda i: (0, 0))
    acc_s = jax.ShapeDtypeStruct((8, d), F32)
    return pl.pallas_call(
        body, name=name, grid=(s // t,), in_specs=[tile, pl.BlockSpec((1, d), lambda i: (0, 0)), tile],
        out_specs=[tile, acc, acc], out_shape=[jax.ShapeDtypeStruct((s, d), F32), acc_s, acc_s],
        compiler_params=_cparams(("arbitrary",)))(x, g, target)


NEG_BIG = -1e30


def _head_masks(rows):
    lane = lax.broadcasted_iota(jnp.int32, (rows, LANES), 1)
    return [(lane // HEAD_DIM) == hh for hh in range(2)]


def _attn_fwd(slabs, gi, name):
    window, dil = ATTN_GROUPS[gi]
    n_back = window // dil
    _, s, w = slabs.shape
    big_l = s // dil
    nb = big_l // ATTN_BLOCK
    blk = ATTN_BLOCK
    view = slabs.reshape(N_SLABS, big_l, dil * w)
    scale = HEAD_DIM ** -0.5

    def body(q_ref, kp_ref, kc_ref, vp_ref, vc_ref, o_ref, lse_ref):
        n = pl.program_id(1)
        qi = lax.broadcasted_iota(jnp.int32, (blk, 2 * blk), 0)
        kj = lax.broadcasted_iota(jnp.int32, (blk, 2 * blk), 1)
        dist = qi + blk - kj
        ok = (dist >= 0) & (dist <= n_back) & ((kj >= blk) | (n > 0))
        mq = _head_masks(blk)
        mk = _head_masks(2 * blk)
        for p in range(w // LANES):
            cols = slice(p * LANES, (p + 1) * LANES)
            qp = q_ref[:, cols]
            k2 = jnp.concatenate([kp_ref[:, cols], kc_ref[:, cols]], axis=0)
            v2 = jnp.concatenate([vp_ref[:, cols], vc_ref[:, cols]], axis=0)
            o_pair = jnp.zeros((blk, LANES), F32)
            lse_pair = jnp.zeros((blk, LANES), F32)
            for hh in range(2):
                qm = jnp.where(mq[hh], qp, jnp.zeros_like(qp))
                sc = lax.dot_general(qm, k2, NT, preferred_element_type=F32) * scale
                sc = jnp.where(ok, sc, NEG_BIG)
                mx = jnp.max(sc, axis=-1, keepdims=True)
                pe = jnp.exp(sc - mx)
                den = jnp.sum(pe, axis=-1, keepdims=True)
                pn = (pe / den).astype(BF16)
                vm = jnp.where(mk[hh], v2, jnp.zeros_like(v2))
                o_pair = o_pair + jnp.dot(pn, vm, preferred_element_type=F32)
                lse_pair = jnp.where(mq[hh], mx + jnp.log(den), lse_pair)
            o_ref[:, cols] = o_pair
            lse_ref[:, cols] = lse_pair

    def spec(slab, prev):
        if prev:
            return pl.BlockSpec((None, blk, w), lambda r, n: (slab, jnp.maximum(n - 1, 0), r))
        return pl.BlockSpec((None, blk, w), lambda r, n: (slab, n, r))

    out_spec = pl.BlockSpec((blk, w), lambda r, n: (n, r))
    sds = jax.ShapeDtypeStruct((big_l, dil * w), F32)
    o, lse = pl.pallas_call(
        body, name=name, grid=(dil, nb),
        in_specs=[spec(gi, False), spec(3 + gi, True), spec(3 + gi, False), spec(6 + gi, True), spec(6 + gi, False)],
        out_specs=[out_spec, out_spec], out_shape=[sds, sds],
        compiler_params=_cparams(("parallel", "arbitrary")))(view, view, view, view, view)
    return o.reshape(s, w), lse.reshape(s, w)


def _attn_merge(outs, lses, name):
    s, w = outs[0].shape
    t = TOK_TILE

    def body(o0, o1, o2, l0, l1, l2, a_ref, lt_ref):
        ls = [l0[...], l1[...], l2[...]]
        mx = jnp.maximum(jnp.maximum(ls[0], ls[1]), ls[2])
        es = [jnp.exp(l - mx) for l in ls]
        den = es[0] + es[1] + es[2]
        num = es[0] * o0[...] + es[1] * o1[...] + es[2] * o2[...]
        a_ref[...] = (num / den).astype(BF16)
        lt_ref[...] = mx + jnp.log(den)

    tile = pl.BlockSpec((t, w), lambda i: (i, 0))
    return pl.pallas_call(
        body, name=name, grid=(s // t,), in_specs=[tile] * 6, out_specs=[tile, tile],
        out_shape=[jax.ShapeDtypeStruct((s, w), BF16), jax.ShapeDtypeStruct((s, w), F32)],
        compiler_params=_cparams(("parallel",)))(*outs, *lses)


def _attn_bwd(slabs, dap, ap, lse_tot, gi, name):
    window, dil = ATTN_GROUPS[gi]
    n_back = window // dil
    _, s, w = slabs.shape
    big_l = s // dil
    nb = big_l // ATTN_BLOCK
    blk = ATTN_BLOCK
    view = slabs.reshape(N_SLABS, big_l, dil * w)
    dap_v = dap.reshape(big_l, dil * 2 * w)
    ap_v = ap.reshape(big_l, dil * 2 * w)
    lse_v = lse_tot.reshape(big_l, dil * w)
    scale = HEAD_DIM ** -0.5

    def body(qc_ref, qn_ref, kp_ref, kc_ref, vp_ref, vc_ref, dc_ref, dn_ref, ac_ref, an_ref, lc_ref, ln_ref, out_ref):
        m = pl.program_id(1)
        qi = lax.broadcasted_iota(jnp.int32, (blk, 2 * blk), 0)
        kj = lax.broadcasted_iota(jnp.int32, (blk, 2 * blk), 1)
        dist = qi + blk - kj
        ok = (dist >= 0) & (dist <= n_back) & ((kj >= blk) | (m > 0))
        qi1 = lax.broadcasted_iota(jnp.int32, (blk, blk), 0)
        kj1 = lax.broadcasted_iota(jnp.int32, (blk, blk), 1)
        ok_next = (qi1 + blk - kj1 <= n_back) & (m + 1 < nb)
        mq = _head_masks(blk)
        mk = _head_masks(2 * blk)

        def per_head(q_t, d_t, a_t, l_t, hh):
            qm = jnp.where(mq[hh], q_t, jnp.zeros_like(q_t))
            dm = jnp.where(mq[hh], d_t, jnp.zeros_like(d_t))
            delta = jnp.sum(jnp.where(mq[hh], d_t.astype(F32) * a_t.astype(F32), 0.0), axis=-1, keepdims=True)
            lse_h = jnp.max(jnp.where(mq[hh], l_t, NEG_BIG), axis=-1, keepdims=True)
            return qm, dm, delta, lse_h

        for p in range(w // LANES):
            cols = slice(p * LANES, (p + 1) * LANES)
            k2 = jnp.concatenate([kp_ref[:, cols], kc_ref[:, cols]], axis=0)
            v2 = jnp.concatenate([vp_ref[:, cols], vc_ref[:, cols]], axis=0)
            kc, vc = kc_ref[:, cols], vc_ref[:, cols]
            dq_pair = jnp.zeros((blk, LANES), F32)
            dk_pair = jnp.zeros((blk, LANES), F32)
            dv_pair = jnp.zeros((blk, LANES), F32)
            for hh in range(2):
                qm, dm, delta, lse_h = per_head(qc_ref[:, cols], dc_ref[:, cols], ac_ref[:, cols], lc_ref[:, cols], hh)
                sc = lax.dot_general(qm, k2, NT, preferred_element_type=F32) * scale
                pr = jnp.exp(jnp.where(ok, sc, NEG_BIG) - lse_h)
                dp = lax.dot_general(dm, v2, NT, preferred_element_type=F32)
                ds = pr * (dp - delta)
                ds_b = ds.astype(BF16)
                km = jnp.where(mk[hh], k2, jnp.zeros_like(k2))
                dq_pair = dq_pair + jnp.dot(ds_b, km, preferred_element_type=F32)
                dk_pair = dk_pair + lax.dot_general(ds_b[:, blk:], qm, TN, preferred_element_type=F32)
                dv_pair = dv_pair + lax.dot_general(pr[:, blk:].astype(BF16), dm, TN, preferred_element_type=F32)
                qm, dm, delta, lse_h = per_head(qn_ref[:, cols], dn_ref[:, cols], an_ref[:, cols], ln_ref[:, cols], hh)
                sc = lax.dot_general(qm, kc, NT, preferred_element_type=F32) * scale
                pr = jnp.exp(jnp.where(ok_next, sc, NEG_BIG) - lse_h)
                dp = lax.dot_general(dm, vc, NT, preferred_element_type=F32)
                ds_b = (pr * (dp - delta)).astype(BF16)
                dk_pair = dk_pair + lax.dot_general(ds_b, qm, TN, preferred_element_type=F32)
                dv_pair = dv_pair + lax.dot_general(pr.astype(BF16), dm, TN, preferred_element_type=F32)
            out_ref[0, :, cols] = (dq_pair * scale).astype(BF16)
            out_ref[1, :, cols] = (dk_pair * scale).astype(BF16)
            out_ref[2, :, cols] = dv_pair.astype(BF16)

    def slab_spec(slab, shift):
        if shift < 0:
            return pl.BlockSpec((None, blk, w), lambda r, n: (slab, jnp.maximum(n - 1, 0), r))
        if shift > 0:
            return pl.BlockSpec((None, blk, w), lambda r, n: (slab, jnp.minimum(n + 1, nb - 1), r))
        return pl.BlockSpec((None, blk, w), lambda r, n: (slab, n, r))

    def tok_spec(stride, shift):
        if shift > 0:
            return pl.BlockSpec((blk, w), lambda r, n: (jnp.minimum(n + 1, nb - 1), stride * r))
        return pl.BlockSpec((blk, w), lambda r, n: (n, stride * r))

    out = pl.pallas_call(
        body, name=name, grid=(dil, nb),
        in_specs=[slab_spec(gi, 0), slab_spec(gi, 1), slab_spec(3 + gi, -1), slab_spec(3 + gi, 0),
                  slab_spec(6 + gi, -1), slab_spec(6 + gi, 0),
                  tok_spec(2, 0), tok_spec(2, 1), tok_spec(2, 0), tok_spec(2, 1), tok_spec(1, 0), tok_spec(1, 1)],
        out_specs=pl.BlockSpec((3, blk, w), lambda r, n: (0, n, r)),
        out_shape=jax.ShapeDtypeStruct((3, big_l, dil * w), BF16),
        compiler_params=_cparams(("parallel", "arbitrary")))(
            view, view, view, view, view, view, dap_v, dap_v, ap_v, ap_v, lse_v, lse_v)
    return out.reshape(3, s, w)


def _shift_down(cur, prev, j):
    row = lax.broadcasted_iota(jnp.int32, cur.shape, 0)
    return jnp.where(row >= j, pltpu.roll(cur, j, 0), pltpu.roll(prev, j, 0))


def _shift_up(cur, nxt, j):
    t = cur.shape[0]
    row = lax.broadcasted_iota(jnp.int32, cur.shape, 0)
    return jnp.where(row < t - j, pltpu.roll(cur, t - j, 0), pltpu.roll(nxt, t - j, 0))


def _window_sum_down(cur, prev, w):
    s, sp, step = cur, prev, 1
    while step < w:
        s_new = s + _shift_down(s, sp, step)
        sp = sp + pltpu.roll(sp, step, 0)
        s, step = s_new, step * 2
    return s


def _window_sum_up(cur, nxt, w):
    t = cur.shape[0]
    s, sn, step = cur, nxt, 1
    while step < w:
        s_new = s + _shift_up(s, sn, step)
        sn = sn + pltpu.roll(sn, t - step, 0)
        s, step = s_new, step * 2
    return s


def _pool_count(tile_idx, t, w):
    row = lax.broadcasted_iota(jnp.int32, (t, LANES), 0) + tile_idx * t
    return jnp.minimum(row + 1, w).astype(F32)


def _pool_fwd(slabs, attn, pool_w, pool_scale, name):
    _, s, w = slabs.shape
    t = TOK_TILE
    gw = POOL_W // len(POOL_WINDOWS)

    def body(u_ref, up_ref, a_ref, w_ref, sc_ref, ap_ref, d_ref):
        i = pl.program_id(0)
        ap_ref[:, :w] = a_ref[...]
        for gi, win in enumerate(POOL_WINDOWS):
            cols = slice(gi * gw, (gi + 1) * gw)
            u = u_ref[:, cols].astype(F32)
            up = jnp.where(i > 0, up_ref[:, cols].astype(F32), 0.0)
            d = (_window_sum_down(u, up, win) / _pool_count(i, t, win) - u).astype(BF16)
            d_ref[:, cols] = d
            y = jnp.dot(d, w_ref[gi].astype(BF16), preferred_element_type=F32)
            ap_ref[:, w + gi * gw:w + (gi + 1) * gw] = (y * sc_ref[:, cols]).astype(BF16)

    return pl.pallas_call(
        body, name=name, grid=(s // t,),
        in_specs=[pl.BlockSpec((None, t, w), lambda i: (N_SLABS - 1, i, 0)),
                  pl.BlockSpec((None, t, w), lambda i: (N_SLABS - 1, jnp.maximum(i - 1, 0), 0)),
                  pl.BlockSpec((t, w), lambda i: (i, 0)),
                  pl.BlockSpec((len(POOL_WINDOWS), gw, gw), lambda i: (0, 0, 0)),
                  pl.BlockSpec((1, w), lambda i: (0, 0))],
        out_specs=[pl.BlockSpec((t, 2 * w), lambda i: (i, 0)), pl.BlockSpec((t, w), lambda i: (i, 0))],
        out_shape=[jax.ShapeDtypeStruct((s, 2 * w), BF16), jax.ShapeDtypeStruct((s, w), BF16)],
        compiler_params=_cparams(("parallel",)))(slabs, slabs, attn, pool_w, pool_scale)


def _pool_bwd(dap, d, pool_w, pool_scale, name):
    s, w = d.shape
    t = TOK_TILE
    nt = s // t
    gw = POOL_W // len(POOL_WINDOWS)

    def body(dy_ref, dyn_ref, d_ref, w_ref, sc_ref, du_ref, dw_ref, dsc_ref):
        i = pl.program_id(0)

        @pl.when(i == 0)
        def _():
            dw_ref[...] = jnp.zeros_like(dw_ref)
            dsc_ref[...] = jnp.zeros_like(dsc_ref)

        for gi, win in enumerate(POOL_WINDOWS):
            cols = slice(gi * gw, (gi + 1) * gw)
            wb = w_ref[gi].astype(BF16)
            scale = sc_ref[:, cols]
            dv = d_ref[:, cols]
            dy = dy_ref[:, cols].astype(F32)
            y = jnp.dot(dv, wb, preferred_element_type=F32)
            dsc_ref[:, cols] += _colsum8(dy * y)
            dyp = (dy * scale).astype(BF16)
            dw_ref[gi] += lax.dot_general(dv, dyp, TN, preferred_element_type=F32)
            dd = lax.dot_general(dyp, wb, NT, preferred_element_type=F32)
            dypn = (dyn_ref[:, cols].astype(F32) * scale).astype(BF16)
            ddn = lax.dot_general(dypn, wb, NT, preferred_element_type=F32)
            e = dd / _pool_count(i, t, win)
            en = jnp.where(i + 1 < nt, ddn / _pool_count(i + 1, t, win), 0.0)
            du_ref[:, cols] = (_window_sum_up(e, en, win) - dd).astype(BF16)

    return pl.pallas_call(
        body, name=name, grid=(nt,),
        in_specs=[pl.BlockSpec((t, w), lambda i: (i, 1)),
                  pl.BlockSpec((t, w), lambda i: (jnp.minimum(i + 1, nt - 1), 1)),
                  pl.BlockSpec((t, w), lambda i: (i, 0)),
                  pl.BlockSpec((len(POOL_WINDOWS), gw, gw), lambda i: (0, 0, 0)),
                  pl.BlockSpec((1, w), lambda i: (0, 0))],
        out_specs=[pl.BlockSpec((t, w), lambda i: (i, 0)),
                   pl.BlockSpec((len(POOL_WINDOWS), gw, gw), lambda i: (0, 0, 0)),
                   pl.BlockSpec((8, w), lambda i: (0, 0))],
        out_shape=[jax.ShapeDtypeStruct((s, w), BF16), jax.ShapeDtypeStruct((len(POOL_WINDOWS), gw, gw), F32),
                   jax.ShapeDtypeStruct((8, w), F32)],
        compiler_params=_cparams(("arbitrary",)))(dap, dap, d, pool_w, pool_scale)


CONV_TAPS = 4
CONV_COLS = 512


def _sigmoid(v):
    return 1.0 / (1.0 + jnp.exp(-v))


def _conv_pre(x, xprev, w_ref, b_ref):
    pre = b_ref[...] + w_ref[CONV_TAPS - 1:CONV_TAPS, :] * x
    for k in range(CONV_TAPS - 1):
        pre = pre + w_ref[k:k + 1, :] * _shift_down(x, xprev, CONV_TAPS - 1 - k)
    return pre


def _conv_fwd(xbc, conv_w, conv_b, name):
    s, c = xbc.shape
    t, tc = TOK_TILE, CONV_COLS

    def body(x_ref, xp_ref, w_ref, b_ref, o_ref):
        i = pl.program_id(0)
        x = x_ref[...].astype(F32)
        xp = jnp.where(i > 0, xp_ref[...].astype(F32), 0.0)
        pre = _conv_pre(x, xp, w_ref, b_ref)
        o_ref[...] = (pre * _sigmoid(pre)).astype(BF16)

    return pl.pallas_call(
        body, name=name, grid=(s // t, c // tc),
        in_specs=[pl.BlockSpec((t, tc), lambda i, j: (i, j)),
                  pl.BlockSpec((t, tc), lambda i, j: (jnp.maximum(i - 1, 0), j)),
                  pl.BlockSpec((CONV_TAPS, tc), lambda i, j: (0, j)), pl.BlockSpec((1, tc), lambda i, j: (0, j))],
        out_specs=pl.BlockSpec((t, tc), lambda i, j: (i, j)), out_shape=jax.ShapeDtypeStruct((s, c), BF16),
        compiler_params=_cparams(("parallel", "parallel")))(xbc, xbc, conv_w, conv_b)


def _conv_bwd(xbc, dact, conv_w, conv_b, name):
    s, c = xbc.shape
    t, tc = TOK_TILE, CONV_COLS
    nt = s // t

    def body(xp_ref, x_ref, xn_ref, da_ref, dan_ref, w_ref, b_ref, dx_ref, dw_ref, db_ref):
        i = pl.program_id(1)

        @pl.when(i == 0)
        def _():
            dw_ref[...] = jnp.zeros_like(dw_ref)
            db_ref[...] = jnp.zeros_like(db_ref)

        x = x_ref[...].astype(F32)
        xp = jnp.where(i > 0, xp_ref[...].astype(F32), 0.0)
        xn = xn_ref[...].astype(F32)

        def dsilu(pre):
            sg = _sigmoid(pre)
            return sg * (1.0 + pre * (1.0 - sg))

        dpre = da_ref[...] * dsilu(_conv_pre(x, xp, w_ref, b_ref))
        dpre_n = jnp.where(i + 1 < nt, dan_ref[...] * dsilu(_conv_pre(xn, x, w_ref, b_ref)), 0.0)
        dx = w_ref[CONV_TAPS - 1:CONV_TAPS, :] * dpre
        dw_ref[CONV_TAPS - 1] += _colsum8(dpre * x)
        for k in range(CONV_TAPS - 1):
            j = CONV_TAPS - 1 - k
            dx = dx + w_ref[k:k + 1, :] * _shift_up(dpre, dpre_n, j)
            dw_ref[k] += _colsum8(dpre * _shift_down(x, xp, j))
        dx_ref[...] = dx.astype(BF16)
        db_ref[...] += _colsum8(dpre)

    def xspec(shift):
        if shift < 0:
            return pl.BlockSpec((t, tc), lambda j, i: (jnp.maximum(i - 1, 0), j))
        if shift > 0:
            return pl.BlockSpec((t, tc), lambda j, i: (jnp.minimum(i + 1, nt - 1), j))
        return pl.BlockSpec((t, tc), lambda j, i: (i, j))

    return pl.pallas_call(
        body, name=name, grid=(c // tc, nt),
        in_specs=[xspec(-1), xspec(0), xspec(1), xspec(0), xspec(1),
                  pl.BlockSpec((CONV_TAPS, tc), lambda j, i: (0, j)), pl.BlockSpec((1, tc), lambda j, i: (0, j))],
        out_specs=[xspec(0), pl.BlockSpec((CONV_TAPS, 8, tc), lambda j, i: (0, 0, j)),
                   pl.BlockSpec((8, tc), lambda j, i: (0, j))],
        out_shape=[jax.ShapeDtypeStruct((s, c), BF16), jax.ShapeDtypeStruct((CONV_TAPS, 8, c), F32),
                   jax.ShapeDtypeStruct((8, c), F32)],
        compiler_params=_cparams(("parallel", "arbitrary")))(xbc, xbc, xbc, dact, dact, conv_w, conv_b)


GN_TILE = 256


def _gated_norm_fwd(y, z, g, name):
    s, c = y.shape
    t = GN_TILE

    def body(y_ref, z_ref, g_ref, o_ref):
        for gi in range(SSM_GROUPS):
            cols = slice(gi * SSM_GW, (gi + 1) * SSM_GW)
            zv = z_ref[:, cols].astype(F32)
            yf = y_ref[:, cols] * (zv * _sigmoid(zv))
            r = lax.rsqrt(jnp.mean(yf * yf, axis=-1, keepdims=True) + NORM_EPS)
            o_ref[:, cols] = (yf * r * g_ref[:, cols]).astype(BF16)

    tile = pl.BlockSpec((t, c), lambda i: (i, 0))
    return pl.pallas_call(
        body, name=name, grid=(s // t,), in_specs=[tile, tile, pl.BlockSpec((1, c), lambda i: (0, 0))],
        out_specs=tile, out_shape=jax.ShapeDtypeStruct((s, c), BF16),
        compiler_params=_cparams(("parallel",)))(y, z, g)


def _gated_norm_bwd(y, z, dout, g, name):
    s, c = y.shape
    t = GN_TILE

    def body(y_ref, z_ref, do_ref, g_ref, dy_ref, dz_ref, dg_ref):
        i = pl.program_id(0)

        @pl.when(i == 0)
        def _():
            dg_ref[...] = jnp.zeros_like(dg_ref)

        for gi in range(SSM_GROUPS):
            cols = slice(gi * SSM_GW, (gi + 1) * SSM_GW)
            zv = z_ref[:, cols].astype(F32)
            yv = y_ref[:, cols]
            sg = _sigmoid(zv)
            sz = zv * sg
            yf = yv * sz
            r = lax.rsqrt(jnp.mean(yf * yf, axis=-1, keepdims=True) + NORM_EPS)
            n = yf * r
            dout = do_ref[:, cols]
            dn = dout * g_ref[:, cols]
            dg_ref[:, cols] += _colsum8(dout * n)
            dyf = r * (dn - n * jnp.mean(dn * n, axis=-1, keepdims=True))
            dy_ref[:, cols] = dyf * sz
            dz_ref[:, cols] = (dyf * yv * (sg * (1.0 + zv * (1.0 - sg)))).astype(BF16)

    tile = pl.BlockSpec((t, c), lambda i: (i, 0))
    return pl.pallas_call(
        body, name=name, grid=(s // t,), in_specs=[tile, tile, tile, pl.BlockSpec((1, c), lambda i: (0, 0))],
        out_specs=[tile, tile, pl.BlockSpec((8, c), lambda i: (0, 0))],
        out_shape=[jax.ShapeDtypeStruct((s, c), F32), jax.ShapeDtypeStruct((s, c), BF16),
                   jax.ShapeDtypeStruct((8, c), F32)],
        compiler_params=_cparams(("arbitrary",)))(y, z, dout, g)


def _split_dot(x, e, dims, terms):
    r, acc = x, None
    for i in range(terms):
        p = r.astype(BF16)
        part = lax.dot_general(p, e, dims, preferred_element_type=F32)
        acc = part if acc is None else acc + part
        if i + 1 < terms:
            r = r - p.astype(F32)
    return acc


def _split_dot_r(e, x, dims, terms):
    r, acc = x, None
    for i in range(terms):
        p = r.astype(BF16)
        part = lax.dot_general(e, p, dims, preferred_element_type=F32)
        acc = part if acc is None else acc + part
        if i + 1 < terms:
            r = r - p.astype(F32)
    return acc


def _ssd_prep(dtr_ref, bias_ref, alog_ref):
    q = SSM_CHUNK
    dtr = dtr_ref[...] + bias_ref[...]
    dt = jnp.maximum(dtr, 0.0) + jnp.log(1.0 + jnp.exp(-jnp.abs(dtr)))
    a_neg = -jnp.exp(alog_ref[...])
    dta = dt * a_neg
    row = lax.broadcasted_iota(jnp.int32, (q, q), 0)
    col = lax.broadcasted_iota(jnp.int32, (q, q), 1)
    causal = row >= col
    a = _split_dot_r(causal.astype(BF16), dta, NN, 3)
    aq_row = jnp.sum(dta, axis=0, keepdims=True)
    aq_hb = _split_dot(dta, jnp.ones((q, LANES), BF16), TN, 3)
    return dict(dtr=dtr, dt=dt, a_neg=a_neg, a=a, a_t=jnp.transpose(a), aq_row=aq_row, aq_hb=aq_hb,
                ea=jnp.exp(a), fa=jnp.exp(aq_row - a), causal=causal)


def _ssd_group_mats(g):
    hrow = lax.broadcasted_iota(jnp.int32, (LANES, SSM_GW), 0)
    jcol = lax.broadcasted_iota(jnp.int32, (LANES, SSM_GW), 1)
    eg = (hrow == 8 * g + jcol // HEAD_DIM).astype(BF16)
    jrow = lax.broadcasted_iota(jnp.int32, (SSM_GW, LANES), 0)
    hcol = lax.broadcasted_iota(jnp.int32, (SSM_GW, LANES), 1)
    eg_t = (hcol == 8 * g + jrow // HEAD_DIM).astype(BF16)
    hrow2 = lax.broadcasted_iota(jnp.int32, (LANES, 8 * LANES), 0)
    jcol2 = lax.broadcasted_iota(jnp.int32, (LANES, 8 * LANES), 1)
    sel = (hrow2 == 8 * g + jcol2 // LANES).astype(BF16)
    return eg, eg_t, sel


def _ssd_fwd(xbc, dt_raw, dt_bias, a_log, d_e, name):
    s = xbc.shape[0]
    q = SSM_CHUNK
    nc = s // q

    def body(x_ref, dtr_ref, bias_ref, alog_ref, de_ref, y_ref, hin_ref, state, at_ref):
        c = pl.program_id(0)

        @pl.when(c == 0)
        def _():
            state[...] = jnp.zeros_like(state)

        hin_ref[...] = state[...].astype(BF16)
        pr = _ssd_prep(dtr_ref, bias_ref, alog_ref)
        at_ref[...] = pr["a_t"]
        exp_aq_hb = jnp.exp(pr["aq_hb"])
        stack3 = jnp.concatenate([pr["dt"], pr["ea"], pr["fa"]], axis=0)
        mq = _head_masks(q)
        for g in range(SSM_GROUPS):
            eg, eg_t, sel = _ssd_group_mats(g)
            cols = slice(g * SSM_GW, (g + 1) * SSM_GW)
            xg = x_ref[:, cols]
            bg = x_ref[:, SSM_INNER + g * SSM_STATE:SSM_INNER + (g + 1) * SSM_STATE]
            cg = x_ref[:, SSM_INNER + SSM_GROUPS * SSM_STATE + g * SSM_STATE:
                       SSM_INNER + SSM_GROUPS * SSM_STATE + (g + 1) * SSM_STATE]
            e3 = _split_dot(stack3, eg, NN, 2)
            dt_e, ea_e, fa_e = e3[:q], e3[q:2 * q], e3[2 * q:]
            xf = xg.astype(F32)
            xdt = xf * dt_e
            xdt_b = xdt.astype(BF16)
            cb = lax.dot_general(cg, bg, NT, preferred_element_type=F32)
            colb = _split_dot(pr["a"], sel, NN, 3)
            hg = state[cols, :]
            y_g = lax.dot_general(cg, hg.astype(BF16), NT, preferred_element_type=F32) * ea_e + de_ref[:, cols] * xf
            pairs = []
            for pp in range(4):
                xp = xdt_b[:, pp * LANES:(pp + 1) * LANES]
                yp = jnp.zeros((q, LANES), F32)
                for hh in range(2):
                    hi = 2 * pp + hh
                    diff = colb[:, hi * LANES:(hi + 1) * LANES] - at_ref[8 * g + hi:8 * g + hi + 1, :]
                    lmat = jnp.exp(jnp.where(pr["causal"], diff, NEG_BIG))
                    m_b = (cb * lmat).astype(BF16)
                    yp = yp + jnp.dot(m_b, jnp.where(mq[hh], xp, jnp.zeros_like(xp)), preferred_element_type=F32)
                pairs.append(yp)
            y_ref[:, cols] = y_g + jnp.concatenate(pairs, axis=1)
            s_g = lax.dot_general((xdt * fa_e).astype(BF16), bg, TN, preferred_element_type=F32)
            dec_g = _split_dot_r(eg, exp_aq_hb, TN, 2)
            state[cols, :] = dec_g * hg + s_g

    row128 = pl.BlockSpec((1, LANES), lambda c: (0, 0))
    return pl.pallas_call(
        body, name=name, grid=(nc,),
        in_specs=[pl.BlockSpec((q, SSM_CONV_DIM), lambda c: (c, 0)), pl.BlockSpec((q, LANES), lambda c: (c, 0)),
                  row128, row128, pl.BlockSpec((1, SSM_INNER), lambda c: (0, 0))],
        out_specs=[pl.BlockSpec((q, SSM_INNER), lambda c: (c, 0)),
                   pl.BlockSpec((None, SSM_INNER, SSM_STATE), lambda c: (c, 0, 0))],
        out_shape=[jax.ShapeDtypeStruct((s, SSM_INNER), F32), jax.ShapeDtypeStruct((nc, SSM_INNER, SSM_STATE), BF16)],
        scratch_shapes=[pltpu.VMEM((SSM_INNER, SSM_STATE), F32), pltpu.VMEM((LANES, q), F32)],
        compiler_params=_cparams(("arbitrary",)))(xbc, dt_raw, dt_bias, a_log, d_e)


def _ssd_bwd(xbc, dt_raw, dt_bias, a_log, d_e, hin, dy, name):
    s = xbc.shape[0]
    q = SSM_CHUNK
    nc = s // q

    def body(x_ref, dtr_ref, bias_ref, alog_ref, de_ref, hin_ref, dy_ref,
             dx_ref, ddt_ref, da_acc, db_acc, dd_acc, gst, at_ref):
        i = pl.program_id(0)

        @pl.when(i == 0)
        def _():
            gst[...] = jnp.zeros_like(gst)
            da_acc[...] = jnp.zeros_like(da_acc)
            db_acc[...] = jnp.zeros_like(db_acc)
            dd_acc[...] = jnp.zeros_like(dd_acc)

        pr = _ssd_prep(dtr_ref, bias_ref, alog_ref)
        at_ref[...] = pr["a_t"]
        exp_aq_hb = jnp.exp(pr["aq_hb"])
        stack3 = jnp.concatenate([pr["dt"], pr["ea"], pr["fa"]], axis=0)
        mq = _head_masks(q)
        lane_h = lax.broadcasted_iota(jnp.int32, (q, LANES), 1)
        sub_h = lax.broadcasted_iota(jnp.int32, (LANES, q), 0)
        da = jnp.zeros((q, LANES), F32)
        da_tn = jnp.zeros((LANES, q), F32)
        daq = jnp.zeros((1, LANES), F32)
        ddt = jnp.zeros((q, LANES), F32)
        for g in range(SSM_GROUPS):
            eg, eg_t, sel = _ssd_group_mats(g)
            cols = slice(g * SSM_GW, (g + 1) * SSM_GW)
            bcols = slice(SSM_INNER + g * SSM_STATE, SSM_INNER + (g + 1) * SSM_STATE)
            ccols = slice(SSM_INNER + SSM_GROUPS * SSM_STATE + g * SSM_STATE,
                          SSM_INNER + SSM_GROUPS * SSM_STATE + (g + 1) * SSM_STATE)
            xg, bg, cg = x_ref[:, cols], x_ref[:, bcols], x_ref[:, ccols]
            e3 = _split_dot(stack3, eg, NN, 2)
            dt_e, ea_e, fa_e = e3[:q], e3[q:2 * q], e3[2 * q:]
            xf = xg.astype(F32)
            xdt = xf * dt_e
            xdt_b = xdt.astype(BF16)
            xdtf = xdt * fa_e
            xdtf_b = xdtf.astype(BF16)
            cb = lax.dot_general(cg, bg, NT, preferred_element_type=F32)
            colb = _split_dot(pr["a"], sel, NN, 3)
            dyg = dy_ref[:, cols]
            dd_acc[:, cols] += _colsum8(dyg * xf)
            hg_b = hin_ref[cols, :]
            gg = gst[cols, :]
            gg_b = gg.astype(BF16)
            dye_b = (dyg * ea_e).astype(BF16)
            dc_g = jnp.dot(dye_b, hg_b, preferred_element_type=F32)
            dh_g = lax.dot_general(dye_b, cg, TN, preferred_element_type=F32)
            yoff = lax.dot_general(cg, hg_b, NT, preferred_element_type=F32) * ea_e
            da = da + _split_dot(dyg * yoff, eg_t, NN, 2)
            db_g = jnp.dot(xdtf_b, gg_b, preferred_element_type=F32)
            tmat = lax.dot_general(bg, gg_b, NT, preferred_element_type=F32)
            dxdt = fa_e * tmat
            qmat = _split_dot(xdtf * tmat, eg_t, NN, 2)
            da = da - qmat
            daq = daq + jnp.sum(qmat, axis=0, keepdims=True)
            gh = _split_dot(gg * hg_b.astype(F32), eg_t, TN, 2)
            daq = daq + jnp.sum(gh, axis=0, keepdims=True) * jnp.exp(pr["aq_row"])
            dcb = jnp.zeros((q, q), F32)
            pairs = []
            for pp in range(4):
                xp = xdt_b[:, pp * LANES:(pp + 1) * LANES]
                dyp = dyg[:, pp * LANES:(pp + 1) * LANES]
                dxp = jnp.zeros((q, LANES), F32)
                for hh in range(2):
                    hi = 2 * pp + hh
                    h = 8 * g + hi
                    diff = colb[:, hi * LANES:(hi + 1) * LANES] - at_ref[h:h + 1, :]
                    lmat = jnp.exp(jnp.where(pr["causal"], diff, NEG_BIG))
                    mmat = cb * lmat
                    dyh = jnp.where(mq[hh], dyp, 0.0).astype(BF16)
                    dm = lax.dot_general(dyh, xp, NT, preferred_element_type=F32)
                    dxp = dxp + lax.dot_general(mmat.astype(BF16), dyh, TN, preferred_element_type=F32)
                    wmat = dm * mmat
                    da = da + jnp.where(lane_h == h, jnp.sum(wmat, axis=-1, keepdims=True), 0.0)
                    da_tn = da_tn + jnp.where(sub_h == h, jnp.sum(wmat, axis=0, keepdims=True), 0.0)
                    dcb = dcb + dm * lmat
                pairs.append(dxp)
            dxdt = dxdt + jnp.concatenate(pairs, axis=1)
            dcb_b = dcb.astype(BF16)
            dc_g = dc_g + jnp.dot(dcb_b, bg, preferred_element_type=F32)
            db_g = db_g + lax.dot_general(dcb_b, cg, TN, preferred_element_type=F32)
            ddt = ddt + _split_dot(dxdt * xf, eg_t, NN, 2)
            dx_ref[:, cols] = dxdt * dt_e + de_ref[:, cols] * dyg
            dx_ref[:, bcols] = db_g
            dx_ref[:, ccols] = dc_g
            dec_g = _split_dot_r(eg, exp_aq_hb, TN, 2)
            gst[cols, :] = dh_g + dec_g * gg
        row = lax.broadcasted_iota(jnp.int32, (q, LANES), 0)
        da_all = da - jnp.transpose(da_tn) + jnp.where(row == q - 1, daq, 0.0)
        upper = jnp.logical_not(pr["causal"]) | (lax.broadcasted_iota(jnp.int32, (q, q), 0)
                                                 == lax.broadcasted_iota(jnp.int32, (q, q), 1))
        rcs = _split_dot_r(upper.astype(BF16), da_all, NN, 3)
        ddt_all = ddt + pr["a_neg"] * rcs
        da_acc[...] += _colsum8(pr["dt"] * rcs)
        ddtr = jnp.where(lane_h < SSM_HEADS, ddt_all * _sigmoid(pr["dtr"]), 0.0)
        ddt_ref[...] = ddtr
        db_acc[...] += _colsum8(ddtr)

    rev = lambda i: (nc - 1 - i, 0)
    row128 = pl.BlockSpec((1, LANES), lambda i: (0, 0))
    acc128 = pl.BlockSpec((8, LANES), lambda i: (0, 0))
    return pl.pallas_call(
        body, name=name, grid=(nc,),
        in_specs=[pl.BlockSpec((q, SSM_CONV_DIM), rev), pl.BlockSpec((q, LANES), rev), row128, row128,
                  pl.BlockSpec((1, SSM_INNER), lambda i: (0, 0)),
                  pl.BlockSpec((None, SSM_INNER, SSM_STATE), lambda i: (nc - 1 - i, 0, 0)),
                  pl.BlockSpec((q, SSM_INNER), rev)],
        out_specs=[pl.BlockSpec((q, SSM_CONV_DIM), rev), pl.BlockSpec((q, LANES), rev), acc128, acc128,
                   pl.BlockSpec((8, SSM_INNER), lambda i: (0, 0))],
        out_shape=[jax.ShapeDtypeStruct((s, SSM_CONV_DIM), F32), jax.ShapeDtypeStruct((s, LANES), F32),
                   jax.ShapeDtypeStruct((8, LANES), F32), jax.ShapeDtypeStruct((8, LANES), F32),
                   jax.ShapeDtypeStruct((8, SSM_INNER), F32)],
        scratch_shapes=[pltpu.VMEM((SSM_INNER, SSM_STATE), F32), pltpu.VMEM((LANES, q), F32)],
        compiler_params=_cparams(("arbitrary",)))(xbc, dt_raw, dt_bias, a_log, d_e, hin, dy)


MM = (1024, 512, 2048)
MM_TN = (1024, 2048, 512)


def _relu2_epilogue(acc, ex, outs):
    r = jnp.maximum(acc, 0.0)
    outs[0][...] = (r * r).astype(BF16)
    outs[1][...] = r.astype(BF16)


def _relu2_bwd_epilogue(acc, ex, outs):
    outs[0][...] = (acc * (2.0 * ex[0][...].astype(F32))).astype(BF16)


def _add_epilogue(acc, ex, outs):
    outs[0][...] = acc + ex[0][...]


def _tile_spec(tm, tn):
    return pl.BlockSpec((tm, tn), lambda i, j, k: (i, j))


def _sum8(acc):
    return jnp.sum(acc, axis=0)


def _row(v):
    return v.reshape(1, -1)


def _ffn_fwd(xin, norm_g, sc, sh, gate, w1, w2, tag):
    h = _norm_mod_fwd(xin, norm_g, sc, sh, f"norm_ffn_fwd_{tag}")
    f, r = _mm_nn(f"mm_ffn1_{tag}", h, w1, *MM, out_dtypes=(BF16, BF16), epilogue=_relu2_epilogue)
    xout, yf = _mm_resid(f"mm_ffn2_{tag}", f, w2, xin, gate, *MM)
    return xout, (h, f, r, yf)


def _ffn_bwd(dxo, xin, saved, norm_g, sc, gate, w1, w2, tag):
    h, f, r, yf = saved
    tm, tn, tk = MM
    dyf, dgate = _gate_bwd(dxo, yf, gate, f"gate_bwd_ffn_{tag}")
    da = _mm_nt(f"mm_ffn2_bwd_{tag}", dyf, w2, *MM, extras=(r,), extra_specs=(_tile_spec(tm, tn),),
                epilogue=_relu2_bwd_epilogue)[0]
    dw2 = _mm_tn(f"mm_ffn2_dw_{tag}", f, dyf, *MM_TN)
    dh = _mm_nt(f"mm_ffn1_bwd_{tag}", da, w1, *MM, out_dtypes=(F32,))[0]
    dw1 = _mm_tn(f"mm_ffn1_dw_{tag}", h, da, *MM_TN)
    dxin, dsh, dsc, dng = _norm_mod_bwd(xin, dh, dxo, norm_g, sc, f"norm_ffn_bwd_{tag}")
    return dxin, dw1, dw2, (_sum8(dsh), _sum8(dsc), _sum8(dgate), _sum8(dng))


def _local_step(x, target, mod, w, p):
    s, d = x.shape
    tm, tn, tk = MM
    mods = [[_row(mod[l, j * d:(j + 1) * d]) for j in range(6)] for l in range(2)]
    g = {}

    sh1, sc1, g1, sh2, sc2, g2 = mods[0]
    nm0, nf0 = _row(p["norm_mix"][0]), _row(p["norm_ffn"][0])
    h0 = _norm_mod_fwd(x, nm0, sc1, sh1, "norm_mix_fwd_0")
    slabs = _mm_core(
        "mm_even_in", h0, w["even_in"], dims=NN, grid=(s // tm, N_SLABS, 1),
        a_spec=pl.BlockSpec((tm, d), lambda i, j, k: (i, 0)), b_spec=pl.BlockSpec((d, ATTN_W), lambda i, j, k: (0, j)),
        out_shapes=[jax.ShapeDtypeStruct((N_SLABS, s, ATTN_W), BF16)],
        out_specs=[pl.BlockSpec((None, tm, ATTN_W), lambda i, j, k: (j, i, 0))], acc_shape=None)[0]
    outs, lses = zip(*[_attn_fwd(slabs, gi, f"attn_fwd_{gi}") for gi in range(3)])
    attn, lse_tot = _attn_merge(outs, lses, "attn_merge")
    pool_scale = _row(p["pool_scale"])
    ap, pool_d = _pool_fwd(slabs, attn, p["pool_w"], pool_scale, "pool_fwd")
    x1, y0 = _mm_resid("mm_even_out", ap, w["even_out"], x, g1, *MM)
    x2, ffn0 = _ffn_fwd(x1, nf0, sc2, sh2, g2, w["w1"][0], w["w2"][0], "0")

    sh1b, sc1b, g1b, sh2b, sc2b, g2b = mods[1]
    nm1, nf1 = _row(p["norm_mix"][1]), _row(p["norm_ffn"][1])
    h1 = _norm_mod_fwd(x2, nm1, sc1b, sh1b, "norm_mix_fwd_1")
    z = _mm_nn("mm_ssm_z", h1, w["ssm_z"], *MM)[0]
    xbc_raw = _mm_nn("mm_ssm_xbc", h1, w["ssm_xbc"], *MM)[0]
    dt_raw = _mm_nn("mm_ssm_dt", h1, w["ssm_dt"], tm, LANES, tk, out_dtypes=(F32,))[0]
    xbc = _conv_fwd(xbc_raw, p["conv_w"], _row(p["conv_b"]), "conv_fwd")
    y_ssd, hin = _ssd_fwd(xbc, dt_raw, p["dt_bias"], p["a_log"], p["d_e"], "ssd_fwd")
    ssm_norm = _row(p["ssm_norm"])
    yn = _gated_norm_fwd(y_ssd, z, ssm_norm, "gated_norm_fwd")
    x3, y1 = _mm_resid("mm_ssm_out", yn, w["ssm_out"], x2, g1b, *MM)
    x4, ffn1 = _ffn_fwd(x3, nf1, sc2b, sh2b, g2b, w["w1"][1], w["w2"][1], "1")

    dx4, sq, dfn = _final_loss(x4, _row(p["final_norm"]), target, "final_loss")
    loss_share = (0.5 / d) * jnp.sum(sq)
    g["final_norm"] = _sum8(dfn)

    dx3, dw1_1, dw2_1, (dsh2b, dsc2b, dg2b, dnf1) = _ffn_bwd(dx4, x3, ffn1, nf1, sc2b, g2b, w["w1"][1], w["w2"][1], "1")
    dy1, dg1b = _gate_bwd(dx3, y1, g1b, "gate_bwd_ssm")
    dyn = _mm_nt("mm_ssm_out_bwd", dy1, w["ssm_out"], *MM, out_dtypes=(F32,))[0]
    g["ssm_out"] = _mm_tn("mm_ssm_out_dw", yn, dy1, *MM_TN)
    dy_ssd, dz, dgn = _gated_norm_bwd(y_ssd, z, dyn, ssm_norm, "gated_norm_bwd")
    g["ssm_norm"] = _sum8(dgn)
    dact, ddt, da_acc, db_acc, dd_acc = _ssd_bwd(xbc, dt_raw, p["dt_bias"], p["a_log"], p["d_e"], hin, dy_ssd, "ssd_bwd")
    g["dt_bias"] = _sum8(db_acc)[:SSM_HEADS]
    g["a_log"] = _sum8(da_acc)[:SSM_HEADS] * (-jnp.exp(p["a_log"][0, :SSM_HEADS]))
    g["ssm_d"] = jnp.sum(_sum8(dd_acc).reshape(SSM_HEADS, HEAD_DIM), axis=1)
    dxbc, dcw, dcb = _conv_bwd(xbc_raw, dact, p["conv_w"], _row(p["conv_b"]), "conv_bwd")
    g["conv_w"] = jnp.sum(dcw, axis=1)
    g["conv_b"] = _sum8(dcb)
    add_spec = (_tile_spec(tm, tn),)
    dh1 = _mm_nt("mm_ssm_z_bwd", dz, w["ssm_z"], *MM, out_dtypes=(F32,))[0]
    dh1 = _mm_nt("mm_ssm_xbc_bwd", dxbc, w["ssm_xbc"], *MM, out_dtypes=(F32,), extras=(dh1,), extra_specs=add_spec,
                 epilogue=_add_epilogue)[0]
    dh1 = _mm_nt("mm_ssm_dt_bwd", ddt, w["ssm_dt"], *MM, out_dtypes=(F32,), extras=(dh1,), extra_specs=add_spec,
                 epilogue=_add_epilogue)[0]
    g["ssm_in"] = jnp.concatenate(
        [_mm_tn("mm_ssm_z_dw", h1, dz, *MM_TN), _mm_tn("mm_ssm_xbc_dw", h1, dxbc, *MM_TN),
         _mm_tn("mm_ssm_dt_dw", h1, ddt, MM_TN[0], LANES, MM_TN[2])[:, :SSM_HEADS]], axis=1)
    dx2, dsh1b, dsc1b, dnm1 = _norm_mod_bwd(x2, dh1, dx3, nm1, sc1b, "norm_mix_bwd_1")
    dmod1 = jnp.concatenate([_sum8(dsh1b), _sum8(dsc1b), _sum8(dg1b), dsh2b, dsc2b, dg2b])

    dx1, dw1_0, dw2_0, (dsh2, dsc2, dg2, dnf0) = _ffn_bwd(dx2, x1, ffn0, nf0, sc2, g2, w["w1"][0], w["w2"][0], "0")
    dy0, dg1 = _gate_bwd(dx1, y0, g1, "gate_bwd_even")
    dap = _mm_nt("mm_even_out_bwd", dy0, w["even_out"], *MM)[0]
    g["even_out"] = _mm_tn("mm_even_out_dw", ap, dy0, *MM_TN)
    du, dpw, dpsc = _pool_bwd(dap, pool_d, p["pool_w"], pool_scale, "pool_bwd")
    g["pool_w"] = dpw
    g["pool_scale"] = _sum8(dpsc)
    dqkv = [_attn_bwd(slabs, dap, ap, lse_tot, gi, f"attn_bwd_{gi}") for gi in range(3)]
    dslabs = jnp.concatenate([dqkv[gi][t:t + 1] for t in range(3) for gi in range(3)] + [du[None]], axis=0)
    dh0 = _mm_core(
        "mm_even_in_bwd", dslabs, w["even_in"], dims=NT, grid=(s // tm, d // 1024, N_SLABS),
        a_spec=pl.BlockSpec((None, tm, ATTN_W), lambda i, j, k: (k, i, 0)),
        b_spec=pl.BlockSpec((1024, ATTN_W), lambda i, j, k: (j, k)),
        out_shapes=[jax.ShapeDtypeStruct((s, d), F32)], out_specs=[_tile_spec(tm, 1024)], acc_shape=(tm, 1024))[0]
    g["even_in"] = _mm_core(
        "mm_even_in_dw", h0, dslabs, dims=TN, grid=(d // 1024, N_SLABS, s // 512),
        a_spec=pl.BlockSpec((512, 1024), lambda i, j, k: (k, i)),
        b_spec=pl.BlockSpec((None, 512, ATTN_W), lambda i, j, k: (j, k, 0)),
        out_shapes=[jax.ShapeDtypeStruct((d, N_SLABS * ATTN_W), BF16)], out_specs=[_tile_spec(1024, ATTN_W)],
        acc_shape=(1024, ATTN_W))[0]
    gx, dsh1, dsc1, dnm0 = _norm_mod_bwd(x, dh0, dx1, nm0, sc1, "norm_mix_bwd_0")
    dmod0 = jnp.concatenate([_sum8(dsh1), _sum8(dsc1), _sum8(dg1), dsh2, dsc2, dg2])

    g["w1"] = (dw1_0, dw1_1)
    g["w2"] = (dw2_0, dw2_1)
    g["norm_mix"] = jnp.stack([_sum8(dnm0), _sum8(dnm1)])
    g["norm_ffn"] = jnp.stack([dnf0, dnf1])
    return loss_share, gx, g, jnp.stack([dmod0, dmod1])


SMALL_COLS = 512
SMALL_ORDER = ("dmod", "norm_mix", "norm_ffn", "pool_w", "pool_scale", "conv_w", "conv_b", "dt_bias", "a_log",
               "ssm_d", "ssm_norm", "final_norm")


def _cols_to_full(gathered):
    n, k, ns = gathered.shape
    return jnp.transpose(gathered, (1, 0, 2)).reshape(k, n * ns)


def _full_to_cols(full):
    k, n4 = full.shape
    return jnp.transpose(full.reshape(k, N_CHIPS, n4 // N_CHIPS), (1, 0, 2))


def _pack(parts, cols):
    flat = jnp.concatenate([v.reshape(-1) for v in parts])
    rows = -(-flat.shape[0] // (cols * 8)) * 8
    return jnp.pad(flat, (0, rows * cols - flat.shape[0])).reshape(rows, cols)


def _unpack(packed, shapes):
    flat, out, at = packed.reshape(-1), [], 0
    for shp in shapes:
        n = math.prod(shp)
        out.append(flat[at:at + n].reshape(shp))
        at += n
    return out


def kernel(x, c, ada_w, ada_b, norm_mix, norm_ffn, ffn_w1, ffn_w2, even_w_in, pool_w, pool_scale, even_w_out, ssm_w_in, ssm_conv_w, ssm_conv_b, ssm_dt_bias, ssm_a_log, ssm_d, ssm_norm, ssm_w_out, final_norm, loss_target, m_ada_w, m_ada_b, m_norm_mix, m_norm_ffn, m_ffn_w1, m_ffn_w2, m_even_w_in, m_pool_w, m_pool_scale, m_even_w_out, m_ssm_w_in, m_ssm_conv_w, m_ssm_conv_b, m_ssm_dt_bias, m_ssm_a_log, m_ssm_d, m_ssm_norm, m_ssm_w_out, m_final_norm, v_ada_w, v_ada_b, v_norm_mix, v_norm_ffn, v_ffn_w1, v_ffn_w2, v_even_w_in, v_pool_w, v_pool_scale, v_even_w_out, v_ssm_w_in, v_ssm_conv_w, v_ssm_conv_b, v_ssm_dt_bias, v_ssm_a_log, v_ssm_d, v_ssm_norm, v_ssm_w_out, v_final_norm):
    names = ("ada_w", "ada_b", "norm_mix", "norm_ffn", "ffn_w1", "ffn_w2", "even_w_in", "pool_w", "pool_scale",
             "even_w_out", "ssm_w_in", "ssm_conv_w", "ssm_conv_b", "ssm_dt_bias", "ssm_a_log", "ssm_d", "ssm_norm",
             "ssm_w_out", "final_norm")
    wts = dict(zip(names, (ada_w, ada_b, norm_mix, norm_ffn, ffn_w1, ffn_w2, even_w_in, pool_w, pool_scale, even_w_out,
                           ssm_w_in, ssm_conv_w, ssm_conv_b, ssm_dt_bias, ssm_a_log, ssm_d, ssm_norm, ssm_w_out, final_norm)))
    m_in = dict(zip(names, (m_ada_w, m_ada_b, m_norm_mix, m_norm_ffn, m_ffn_w1, m_ffn_w2, m_even_w_in, m_pool_w, m_pool_scale,
                            m_even_w_out, m_ssm_w_in, m_ssm_conv_w, m_ssm_conv_b, m_ssm_dt_bias, m_ssm_a_log, m_ssm_d,
                            m_ssm_norm, m_ssm_w_out, m_final_norm)))
    v_in = dict(zip(names, (v_ada_w, v_ada_b, v_norm_mix, v_norm_ffn, v_ffn_w1, v_ffn_w2, v_even_w_in, v_pool_w, v_pool_scale,
                            v_even_w_out, v_ssm_w_in, v_ssm_conv_w, v_ssm_conv_b, v_ssm_dt_bias, v_ssm_a_log, v_ssm_d,
                            v_ssm_norm, v_ssm_w_out, v_final_norm)))
    d = D_MODEL
    s = x.shape[1]
    ix, iy, ic = _place()
    chip = 2 * ix + iy
    example = 4 * ix + 2 * iy + ic

    c_all = _allgather8(c.reshape(8, d // 8), "gather_c").reshape(N_DEV, d)
    cond = c_all * jax.nn.sigmoid(c_all)
    cond16 = jnp.pad(cond, ((0, 8), (0, 0)))
    ada_cols = ada_w.shape[2]
    bias_shard = lax.dynamic_slice_in_dim(ada_b, chip * ada_cols, ada_cols, axis=1)
    mod_parts = [
        _mm_nn(f"mm_ada_{l}", cond16, ada_w[l], 16, 512, d, out_dtypes=(F32,), extras=(_row(bias_shard[l]),),
               extra_specs=(pl.BlockSpec((1, 512), lambda i, j, k: (0, j)),), epilogue=_add_epilogue)[0][:8]
        for l in range(2)]
    mod_all = _allgather8(jnp.concatenate(mod_parts, axis=0), "gather_mod").reshape(N_CHIPS, 2, 2, 8, ada_cols)
    mod_mine = lax.dynamic_index_in_dim(mod_all[:, 0], example, axis=2, keepdims=False)
    mod = jnp.transpose(mod_mine, (1, 0, 2)).reshape(2, N_CHIPS * ada_cols)

    def gathered(wt, tag):
        lead = wt.shape[:-2]
        flat = wt.astype(BF16).reshape(-1, wt.shape[-1])
        return _chip_allgather(flat, f"gather_{tag}").reshape(N_CHIPS, *lead, *wt.shape[-2:])

    gw1 = gathered(ffn_w1, "ffn_w1")
    gw2 = gathered(ffn_w2, "ffn_w2")
    ssm_in_full = _cols_to_full(gathered(ssm_w_in, "ssm_in")[:, 0])
    n_xbc = SSM_INNER + SSM_CONV_DIM
    full = {
        "w1": [_cols_to_full(gw1[:, l]) for l in range(2)],
        "w2": [gw2[:, l].reshape(FFN_HIDDEN, d) for l in range(2)],
        "even_in": _cols_to_full(gathered(even_w_in, "even_in")[:, 0]),
        "even_out": _cols_to_full(gathered(even_w_out, "even_out")[:, 0]),
        "ssm_z": ssm_in_full[:, :SSM_INNER],
        "ssm_xbc": ssm_in_full[:, SSM_INNER:n_xbc],
        "ssm_dt": jnp.pad(ssm_in_full[:, n_xbc:], ((0, 0), (0, LANES - SSM_HEADS))),
        "ssm_out": gathered(ssm_w_out, "ssm_out")[:, 0].reshape(SSM_INNER, d),
    }
    pad_h = ((0, 0), (0, LANES - SSM_HEADS))
    cw, nw = ssm_conv_w.shape[2], ssm_norm.shape[1]
    sm = jnp.concatenate([ssm_conv_w[0].reshape(-1), ssm_conv_b.reshape(-1), jnp.pad(ssm_norm[0], (0, cw - nw)),
                          jnp.zeros((2 * cw,), F32)]).reshape(8, cw)
    sm_all = _allgather8(sm, "gather_ssm_small").reshape(N_CHIPS, 2, 8, cw)[:, 0]
    small = {
        "norm_mix": norm_mix, "norm_ffn": norm_ffn, "pool_w": pool_w[0], "pool_scale": pool_scale[0],
        "final_norm": final_norm,
        "conv_w": jnp.transpose(sm_all[:, :CONV_TAPS], (1, 0, 2)).reshape(CONV_TAPS, N_CHIPS * cw),
        "conv_b": sm_all[:, CONV_TAPS].reshape(N_CHIPS * cw),
        "ssm_norm": sm_all[:, CONV_TAPS + 1, :nw].reshape(N_CHIPS * nw),
        "dt_bias": jnp.pad(ssm_dt_bias, pad_h), "a_log": jnp.pad(ssm_a_log, pad_h),
        "d_e": jnp.repeat(ssm_d[0], HEAD_DIM).reshape(1, SSM_INNER),
    }

    loss_share, gx, g, dmod = _local_step(x[0], loss_target[0], mod, full, small)
    loss = lax.psum(loss_share, ("x", "y", "c"))

    g["dmod"] = dmod
    small_shapes = [g[k].shape for k in SMALL_ORDER]
    packed = _pack([g[k] for k in SMALL_ORDER], SMALL_COLS)
    every = _allgather8(packed, "gather_small").reshape(N_DEV, *packed.shape)
    summed = dict(zip(SMALL_ORDER, _unpack(_sum_devices(every, "sum_small"), small_shapes)))
    dmod_all = every.reshape(N_DEV, -1)[:, :2 * 6 * d].reshape(N_DEV, 2, 6 * d)

    grads = {}
    dmod_shard = lax.dynamic_slice_in_dim(dmod_all, chip * ada_cols, ada_cols, axis=2)
    grads["ada_w"] = jnp.stack([
        _mm_tn(f"mm_ada_dw_{l}", cond16, jnp.pad(dmod_shard[:, l], ((0, 8), (0, 0))), 1024, 512, 16, out_dtype=F32)
        for l in range(2)])
    grads["ada_b"] = summed["dmod"]
    grads["norm_mix"] = summed["norm_mix"]
    grads["norm_ffn"] = summed["norm_ffn"]
    grads["pool_w"] = summed["pool_w"][None]
    grads["pool_scale"] = summed["pool_scale"][None]
    grads["ssm_conv_w"] =lax.dynamic_slice_in_dim(summed["conv_w"], chip * cw, cw, axis=1)[None]
    grads["ssm_conv_b"] = lax.dynamic_slice_in_dim(summed["conv_b"], chip * cw, cw, axis=0)[None]
    grads["ssm_dt_bias"] = summed["dt_bias"][None]
    grads["ssm_a_log"] = summed["a_log"][None]
    grads["ssm_d"] = summed["ssm_d"][None]
    grads["ssm_norm"] =lax.dynamic_slice_in_dim(summed["ssm_norm"], chip * nw, nw, axis=0)[None]
    grads["final_norm"] = summed["final_norm"]

    def scattered(slots, tag, shape):
        n = slots.shape[0]
        return _reduce_scatter(slots.reshape(n, -1, slots.shape[-1]), tag).reshape(shape)

    grads["ffn_w1"] = scattered(jnp.stack([_full_to_cols(t) for t in g["w1"]], axis=1), "ffn_w1", ffn_w1.shape)
    grads["ffn_w2"] = scattered(jnp.stack([t.reshape(N_CHIPS, -1, d) for t in g["w2"]], axis=1), "ffn_w2", ffn_w2.shape)
    grads["even_w_in"] = scattered(_full_to_cols(g["even_in"]), "even_in", even_w_in.shape)
    grads["even_w_out"] = scattered(_full_to_cols(g["even_out"]), "even_out", even_w_out.shape)
    grads["ssm_w_in"] = scattered(_full_to_cols(g["ssm_in"]), "ssm_in", ssm_w_in.shape)
    grads["ssm_w_out"] = scattered(g["ssm_out"].reshape(N_CHIPS, -1, d), "ssm_out", ssm_w_out.shape)

    big = ("ada_w", "ffn_w1", "ffn_w2", "even_w_in", "even_w_out", "ssm_w_in", "ssm_w_out")
    delta, new_m, new_v = {}, {}, {}
    for k in big:
        shp = wts[k].shape
        two_d = lambda t: t.reshape(-1, shp[-1])
        res = _adamw(two_d(wts[k]), two_d(grads[k]), two_d(m_in[k]), two_d(v_in[k]), f"adamw_{k}")
        delta[k], new_m[k], new_v[k] = [t.reshape(shp) for t in res]
    little = [k for k in names if k not in big]
    shapes = [wts[k].shape for k in little]
    res = _adamw(*[_pack([src[k] for k in little], LANES) for src in (wts, grads, m_in, v_in)], "adamw_small")
    for out, packed_out in zip((delta, new_m, new_v), res):
        out.update(zip(little, _unpack(packed_out, shapes)))

    return (loss, gx[None], *[grads[k] for k in names], *[delta[k] for k in names],
            *[new_m[k] for k in names], *[new_v[k] for k in names])
```

```python
import functools
import math

import jax
import jax.numpy as jnp
from jax import lax
from jax.experimental import pallas as pl
from jax.experimental.pallas import tpu as pltpu

F32 = jnp.float32
BF16 = jnp.bfloat16
MESH = pl.DeviceIdType.MESH
ANY = pl.BlockSpec(memory_space=pl.ANY)
VMEM_FULL = pl.BlockSpec(memory_space=pltpu.VMEM)

NORM_EPS = 1e-6
N_CHIPS = 4
N_DEV = 8
LANES = 128
VMEM_LIMIT = 56 << 20

D_MODEL = 2048
ATTN_GROUPS = ((128, 1), (512, 4), (2048, 16))
ATTN_BLOCK = 128
ATTN_W = 512
HEAD_DIM = 64
POOL_WINDOWS = (2, 4, 8, 16)
POOL_W = 512
N_SLABS = 10
SSM_INNER = 4096
SSM_HEADS = 64
SSM_GROUPS = 8
SSM_STATE = 128
SSM_CHUNK = 128
SSM_CONV_DIM = 6144
SSM_GW = SSM_INNER // SSM_GROUPS
FFN_HIDDEN = 8192

ADAM_LR, ADAM_B1, ADAM_B2, ADAM_EPS, ADAM_WD, ADAM_STEP = 0.001, 0.9, 0.999, 1e-08, 0.01, 10


def _cparams(sem=None):
    return pltpu.CompilerParams(dimension_semantics=sem, vmem_limit_bytes=VMEM_LIMIT)


def _place():
    return lax.axis_index("x"), lax.axis_index("y"), lax.axis_index("c")


def _chip_index():
    return 2 * lax.axis_index("x") + lax.axis_index("y")


def _allgather8(v, name):
    m_per, n = v.shape

    def body(x_ref, out_ref, send_sems, recv_sems, local_sem):
        x, y, c = _place()
        me, sibling = (x, y, c), (x, y, 1 - c)
        chips = [(1 - x, y), (x, 1 - y), (1 - x, 1 - y)]

        def rows(px, py, pc):
            return out_ref.at[pl.ds((4 * px + 2 * py + pc) * m_per, m_per), :]

        def copy(k, block, to, src=None):
            return pltpu.make_async_remote_copy(
                src_ref=rows(*block) if src is None else src, dst_ref=rows(*block),
                send_sem=send_sems.at[k], recv_sem=recv_sems.at[k], device_id=to, device_id_type=MESH)

        mine = pltpu.make_async_copy(x_ref, rows(*me), local_sem)
        mine.start()
        first = [copy(0, me, sibling, src=x_ref)]
        first += [copy(1 + j, me, (*chip, c), src=x_ref) for j, chip in enumerate(chips)]
        for cp in first:
            cp.start()
        passed = [copy(4 + j, (*chip, c), sibling) for j, chip in enumerate(chips)]
        for j, chip in enumerate(chips):
            copy(1 + j, (*chip, c), me).wait_recv()
            passed[j].start()
        copy(0, sibling, me).wait_recv()
        for j, chip in enumerate(chips):
            copy(4 + j, (*chip, 1 - c), me).wait_recv()
        for cp in first + passed:
            cp.wait_send()
        mine.wait()

    return pl.pallas_call(
        body, name=name,
        out_shape=jax.ShapeDtypeStruct((N_DEV * m_per, n), v.dtype),
        in_specs=[VMEM_FULL], out_specs=VMEM_FULL,
        scratch_shapes=[pltpu.SemaphoreType.DMA((7,)), pltpu.SemaphoreType.DMA((7,)), pltpu.SemaphoreType.DMA],
    )(v)


def _chip_allgather(w, name):
    r, ccols = w.shape
    h = r // 2

    def body(w_ref, out_ref, send_sems, recv_sems):
        x, y, c = _place()
        sibling = (x, y, 1 - c)
        chips = [(1 - x, y), (x, 1 - y), (1 - x, 1 - y)]

        def half(chip, cc):
            return out_ref.at[2 * chip[0] + chip[1], pl.ds(cc * h, h), :]

        def copy(k, dst, to, src):
            return pltpu.make_async_remote_copy(src_ref=src, dst_ref=dst, send_sem=send_sems.at[k],
                                                recv_sem=recv_sems.at[k], device_id=to, device_id_type=MESH)

        my_half = w_ref.at[pl.ds(c * h, h), :]
        first = [copy(j, half((x, y), c), (*chip, c), my_half) for j, chip in enumerate(chips)]
        for cp in first:
            cp.start()
        passed = [copy(3 + j, half(chip, c), sibling, half(chip, c)) for j, chip in enumerate(chips)]
        for j, chip in enumerate(chips):
            copy(j, half(chip, c), sibling, my_half).wait_recv()
            passed[j].start()
        for j, chip in enumerate(chips):
            copy(3 + j, half(chip, 1 - c), sibling, my_half).wait_recv()
        for cp in first + passed:
            cp.wait_send()

    out = pl.pallas_call(
        body, name=name,
        out_shape=jax.ShapeDtypeStruct((N_CHIPS, r, ccols), w.dtype),
        in_specs=[ANY], out_specs=ANY,
        scratch_shapes=[pltpu.SemaphoreType.DMA((6,)), pltpu.SemaphoreType.DMA((6,))],
    )(w)
    return lax.dynamic_update_index_in_dim(out, w, _chip_index(), 0)


def _sibling_send_other_half(g, name):
    n, r, ccols = g.shape
    h = r // 2

    def body(g_ref, out_ref, send_sem, recv_sem):
        x, y, c = _place()
        cp = pltpu.make_async_remote_copy(
            src_ref=g_ref.at[:, pl.ds((1 - c) * h, h), :], dst_ref=out_ref, send_sem=send_sem, recv_sem=recv_sem,
            device_id=(x, y, 1 - c), device_id_type=MESH)
        cp.start()
        cp.wait()

    return pl.pallas_call(
        body, name=name, out_shape=jax.ShapeDtypeStruct((n, h, ccols), g.dtype),
        in_specs=[ANY], out_specs=ANY,
        scratch_shapes=[pltpu.SemaphoreType.DMA, pltpu.SemaphoreType.DMA],
    )(g)


def _chip_alltoall(p, name):
    n, h, ccols = p.shape

    def body(p_ref, out_ref, send_sems, recv_sems):
        x, y, c = _place()
        k_me = 2 * x + y
        chips = [(1 - x, y), (x, 1 - y), (1 - x, 1 - y)]
        cps = []
        for j, chip in enumerate(chips):
            cps.append(pltpu.make_async_remote_copy(
                src_ref=p_ref.at[2 * chip[0] + chip[1]], dst_ref=out_ref.at[k_me],
                send_sem=send_sems.at[j], recv_sem=recv_sems.at[j], device_id=(*chip, c), device_id_type=MESH))
        for cp in cps:
            cp.start()
        for j, chip in enumerate(chips):
            pltpu.make_async_remote_copy(
                src_ref=p_ref.at[k_me], dst_ref=out_ref.at[2 * chip[0] + chip[1]],
                send_sem=send_sems.at[j], recv_sem=recv_sems.at[j], device_id=(*chip, c), device_id_type=MESH).wait_recv()
        for cp in cps:
            cp.wait_send()

    out = pl.pallas_call(
        body, name=name, out_shape=jax.ShapeDtypeStruct((n, h, ccols), p.dtype),
        in_specs=[ANY], out_specs=ANY,
        scratch_shapes=[pltpu.SemaphoreType.DMA((3,)), pltpu.SemaphoreType.DMA((3,))],
    )(p)
    k_me = _chip_index()
    return lax.dynamic_update_index_in_dim(out, lax.dynamic_index_in_dim(p, k_me, 0, keepdims=False), k_me, 0)


def _sibling_complete(f, name):
    h, ccols = f.shape

    def body(f_ref, out_ref, send_sem, recv_sem):
        x, y, c = _place()
        cp = pltpu.make_async_remote_copy(src_ref=f_ref, dst_ref=out_ref, send_sem=send_sem, recv_sem=recv_sem,
                                          device_id=(x, y, 1 - c), device_id_type=MESH)
        cp.start()
        cp.wait()

    other = pl.pallas_call(
        body, name=name, out_shape=jax.ShapeDtypeStruct((h, ccols), f.dtype),
        in_specs=[ANY], out_specs=ANY,
        scratch_shapes=[pltpu.SemaphoreType.DMA, pltpu.SemaphoreType.DMA],
    )(f)
    south = lax.axis_index("c") == 0
    return jnp.concatenate([jnp.where(south, f, other), jnp.where(south, other, f)], axis=0)


def _row_tile(rows, cols, itemsize, budget=2 << 20):
    t = rows
    while t % 2 == 0 and t * cols * itemsize > budget and (t // 2) % 16 == 0:
        t //= 2
    return t


def _add_own_half(g, recv, name):
    n, r, ccols = g.shape
    h = r // 2
    t = _row_tile(h, ccols, 4)
    nt = h // t
    c_idx = lax.axis_index("c").astype(jnp.int32).reshape(1)

    def body(c_ref, g_ref, r_ref, o_ref):
        o_ref[...] = (g_ref[...].astype(F32) + r_ref[...].astype(F32)).astype(o_ref.dtype)

    grid_spec = pltpu.PrefetchScalarGridSpec(
        num_scalar_prefetch=1, grid=(n, nt),
        in_specs=[pl.BlockSpec((None, t, ccols), lambda j, i, c_ref: (j, c_ref[0] * nt + i, 0)),
                  pl.BlockSpec((None, t, ccols), lambda j, i, c_ref: (j, i, 0))],
        out_specs=pl.BlockSpec((None, t, ccols), lambda j, i, c_ref: (j, i, 0)))
    return pl.pallas_call(body, name=name, grid_spec=grid_spec,
                          out_shape=jax.ShapeDtypeStruct((n, h, ccols), BF16),
                          compiler_params=_cparams(("parallel", "parallel")))(c_idx, g, recv)


def _sum_slots(q, name):
    n, h, ccols = q.shape
    t = _row_tile(h, ccols, 4)

    def body(q_ref, o_ref):
        acc = q_ref[0].astype(F32)
        for j in range(1, n):
            acc = acc + q_ref[j].astype(F32)
        o_ref[...] = acc

    return pl.pallas_call(
        body, name=name, grid=(h // t,),
        in_specs=[pl.BlockSpec((n, t, ccols), lambda i: (0, i, 0))],
        out_specs=pl.BlockSpec((t, ccols), lambda i: (i, 0)),
        out_shape=jax.ShapeDtypeStruct((h, ccols), F32), compiler_params=_cparams(("parallel",)))(q)


def _reduce_scatter(g, tag):
    recv = _sibling_send_other_half(g, f"rs_pair_{tag}")
    part = _add_own_half(g, recv, f"rs_add2_{tag}")
    got = _chip_alltoall(part, f"rs_a2a_{tag}")
    fin = _sum_slots(got, f"rs_add4_{tag}")
    return _sibling_complete(fin, f"rs_fin_{tag}")


def _sum_devices(v, name):
    n, r, ccols = v.shape
    t = 8
    while r % (t * 2) == 0 and t * 2 * ccols * 4 * n <= (8 << 20):
        t *= 2

    def body(v_ref, o_ref):
        acc = v_ref[0]
        for j in range(1, n):
            acc = acc + v_ref[j]
        o_ref[...] = acc

    return pl.pallas_call(
        body, name=name, grid=(r // t,),
        in_specs=[pl.BlockSpec((n, t, ccols), lambda i: (0, i, 0))],
        out_specs=pl.BlockSpec((t, ccols), lambda i: (i, 0)),
        out_shape=jax.ShapeDtypeStruct((r, ccols), F32), compiler_params=_cparams(("parallel",)))(v)


def _adamw(w, g, m, v, name):
    r, ccols = w.shape
    t = _row_tile(r, ccols, 4, budget=1 << 20)
    c1 = 1.0 / (1.0 - ADAM_B1 ** ADAM_STEP)
    c2 = 1.0 / (1.0 - ADAM_B2 ** ADAM_STEP)

    def body(w_ref, g_ref, m_ref, v_ref, d_ref, nm_ref, nv_ref):
        gg = g_ref[...]
        nm = ADAM_B1 * m_ref[...] + (1.0 - ADAM_B1) * gg
        nv = ADAM_B2 * v_ref[...] + (1.0 - ADAM_B2) * (gg * gg)
        d_ref[...] = -ADAM_LR * ((nm * c1) / (jnp.sqrt(nv * c2) + ADAM_EPS) + ADAM_WD * w_ref[...])
        nm_ref[...] = nm
        nv_ref[...] = nv

    spec = pl.BlockSpec((t, ccols), lambda i: (i, 0))
    sds = jax.ShapeDtypeStruct((r, ccols), F32)
    return pl.pallas_call(body, name=name, grid=(r // t,), in_specs=[spec] * 4, out_specs=[spec] * 3,
                          out_shape=[sds] * 3, compiler_params=_cparams(("parallel",)))(w, g, m, v)


NN = (((1,), (0,)), ((), ()))
NT = (((1,), (1,)), ((), ()))
TN = (((0,), (0,)), ((), ()))


def _mm_core(name, a, b, *, dims, grid, a_spec, b_spec, out_shapes, out_specs, acc_shape,
             extras=(), extra_specs=(), epilogue=None):
    nk = grid[2]
    n_ex, n_out = len(extras), len(out_shapes)
    if epilogue is None:
        def epilogue(acc, ex, outs):
            outs[0][...] = acc.astype(outs[0].dtype)

    def body(*refs):
        a_ref, b_ref = refs[0], refs[1]
        ex = refs[2:2 + n_ex]
        outs = refs[2 + n_ex:2 + n_ex + n_out]
        part = lax.dot_general(a_ref[...].astype(BF16), b_ref[...].astype(BF16), dims, preferred_element_type=F32)
        if nk == 1:
            epilogue(part, ex, outs)
        else:
            acc = refs[-1]
            k = pl.program_id(2)

            @pl.when(k == 0)
            def _():
                acc[...] = part

            @pl.when(k > 0)
            def _():
                acc[...] += part

            @pl.when(k == nk - 1)
            def _():
                epilogue(acc[...], ex, outs)

    scratch = [] if nk == 1 else [pltpu.VMEM(acc_shape, F32)]
    res = pl.pallas_call(
        body, name=name, grid=grid, in_specs=[a_spec, b_spec, *extra_specs], out_specs=list(out_specs),
        out_shape=list(out_shapes), scratch_shapes=scratch,
        compiler_params=_cparams(("parallel", "parallel", "arbitrary")))(a, b, *extras)
    return res


def _mm_nn(name, a, b, tm, tn, tk, out_dtypes=(BF16,), extras=(), extra_specs=(), epilogue=None):
    m, kk = a.shape
    n = b.shape[1]
    tm, tn, tk = min(tm, m), min(tn, n), min(tk, kk)
    grid = (m // tm, n // tn, kk // tk)
    return _mm_core(
        name, a, b, dims=NN, grid=grid,
        a_spec=pl.BlockSpec((tm, tk), lambda i, j, k: (i, k)), b_spec=pl.BlockSpec((tk, tn), lambda i, j, k: (k, j)),
        out_shapes=[jax.ShapeDtypeStruct((m, n), dt) for dt in out_dtypes],
        out_specs=[pl.BlockSpec((tm, tn), lambda i, j, k: (i, j)) for _ in out_dtypes],
        acc_shape=(tm, tn), extras=extras, extra_specs=extra_specs, epilogue=epilogue)


def _mm_nt(name, a, b, tm, tn, tk, out_dtypes=(BF16,), extras=(), extra_specs=(), epilogue=None):
    m, kk = a.shape
    n = b.shape[0]
    tm, tn, tk = min(tm, m), min(tn, n), min(tk, kk)
    grid = (m // tm, n // tn, kk // tk)
    return _mm_core(
        name, a, b, dims=NT, grid=grid,
        a_spec=pl.BlockSpec((tm, tk), lambda i, j, k: (i, k)), b_spec=pl.BlockSpec((tn, tk), lambda i, j, k: (j, k)),
        out_shapes=[jax.ShapeDtypeStruct((m, n), dt) for dt in out_dtypes],
        out_specs=[pl.BlockSpec((tm, tn), lambda i, j, k: (i, j)) for _ in out_dtypes],
        acc_shape=(tm, tn), extras=extras, extra_specs=extra_specs, epilogue=epilogue)


def _mm_tn(name, a, b, tm, tn, tk, out_dtype=BF16):
    kk, m = a.shape
    n = b.shape[1]
    tm, tn, tk = min(tm, m), min(tn, n), min(tk, kk)
    grid = (m // tm, n // tn, kk // tk)
    return _mm_core(
        name, a, b, dims=TN, grid=grid,
        a_spec=pl.BlockSpec((tk, tm), lambda i, j, k: (k, i)), b_spec=pl.BlockSpec((tk, tn), lambda i, j, k: (k, j)),
        out_shapes=[jax.ShapeDtypeStruct((m, n), out_dtype)],
        out_specs=[pl.BlockSpec((tm, tn), lambda i, j, k: (i, j))], acc_shape=(tm, tn))[0]


def _resid_gate_epilogue(acc, ex, outs):
    outs[0][...] = ex[0][...] + ex[1][...] * acc
    outs[1][...] = acc.astype(BF16)


def _mm_resid(name, a, b, resid, gate, tm, tn, tk):
    return _mm_nn(
        name, a, b, tm, tn, tk, out_dtypes=(F32, BF16), extras=(resid, gate),
        extra_specs=(pl.BlockSpec((tm, tn), lambda i, j, k: (i, j)), pl.BlockSpec((1, tn), lambda i, j, k: (0, j))),
        epilogue=_resid_gate_epilogue)


TOK_TILE = 512


def _colsum8(v):
    t, ccols = v.shape
    return jnp.sum(v.reshape(t // 8, 8, ccols), axis=0)


def _norm_mod_fwd(x, g, sc, sh, name):
    s, d = x.shape
    t = TOK_TILE

    def body(x_ref, g_ref, sc_ref, sh_ref, h_ref):
        xv = x_ref[...]
        n = xv * lax.rsqrt(jnp.mean(xv * xv, axis=-1, keepdims=True) + NORM_EPS)
        h_ref[...] = ((n * g_ref[...]) * (1.0 + sc_ref[...]) + sh_ref[...]).astype(BF16)

    row = pl.BlockSpec((1, d), lambda i: (0, 0))
    return pl.pallas_call(
        body, name=name, grid=(s // t,), in_specs=[pl.BlockSpec((t, d), lambda i: (i, 0)), row, row, row],
        out_specs=pl.BlockSpec((t, d), lambda i: (i, 0)), out_shape=jax.ShapeDtypeStruct((s, d), BF16),
        compiler_params=_cparams(("parallel",)))(x, g, sc, sh)


def _norm_mod_bwd(x, dh, resid, g, sc, name):
    s, d = x.shape
    t = TOK_TILE

    def body(x_ref, dh_ref, r_ref, g_ref, sc_ref, dx_ref, dsh_ref, dsc_ref, dg_ref):
        i = pl.program_id(0)
        xv = x_ref[...]
        rstd = lax.rsqrt(jnp.mean(xv * xv, axis=-1, keepdims=True) + NORM_EPS)
        n = xv * rstd
        dhv = dh_ref[...].astype(F32)
        gv = g_ref[...]
        dyn = dhv * (1.0 + sc_ref[...])
        dn = dyn * gv
        dx_ref[...] = r_ref[...] + rstd * (dn - n * jnp.mean(dn * n, axis=-1, keepdims=True))

        @pl.when(i == 0)
        def _():
            dsh_ref[...] = jnp.zeros_like(dsh_ref)
            dsc_ref[...] = jnp.zeros_like(dsc_ref)
            dg_ref[...] = jnp.zeros_like(dg_ref)

        dsh_ref[...] += _colsum8(dhv)
        dsc_ref[...] += _colsum8(dhv * (n * gv))
        dg_ref[...] += _colsum8(dyn * n)

    tile = pl.BlockSpec((t, d), lambda i: (i, 0))
    row = pl.BlockSpec((1, d), lambda i: (0, 0))
    acc = pl.BlockSpec((8, d), lambda i: (0, 0))
    acc_s = jax.ShapeDtypeStruct((8, d), F32)
    return pl.pallas_call(
        body, name=name, grid=(s // t,), in_specs=[tile, tile, tile, row, row],
        out_specs=[tile, acc, acc, acc], out_shape=[jax.ShapeDtypeStruct((s, d), F32), acc_s, acc_s, acc_s],
        compiler_params=_cparams(("arbitrary",)))(x, dh, resid, g, sc)


def _gate_bwd(dxo, y, gate, name):
    s, d = dxo.shape
    t = TOK_TILE

    def body(dx_ref, y_ref, g_ref, dy_ref, dg_ref):
        i = pl.program_id(0)
        dxv = dx_ref[...]
        dy_ref[...] = (dxv * g_ref[...]).astype(BF16)

        @pl.when(i == 0)
        def _():
            dg_ref[...] = jnp.zeros_like(dg_ref)

        dg_ref[...] += _colsum8(dxv * y_ref[...].astype(F32))

    tile = pl.BlockSpec((t, d), lambda i: (i, 0))
    return pl.pallas_call(
        body, name=name, grid=(s // t,), in_specs=[tile, tile, pl.BlockSpec((1, d), lambda i: (0, 0))],
        out_specs=[tile, pl.BlockSpec((8, d), lambda i: (0, 0))],
        out_shape=[jax.ShapeDtypeStruct((s, d), BF16), jax.ShapeDtypeStruct((8, d), F32)],
        compiler_params=_cparams(("arbitrary",)))(dxo, y, gate)


def _final_loss(x, g, target, name):
    s, d = x.shape
    t = TOK_TILE
    inv_d = 1.0 / d

    def body(x_ref, g_ref, t_ref, dx_ref, sq_ref, dg_ref):
        i = pl.program_id(0)
        xv = x_ref[...]
        rstd = lax.rsqrt(jnp.mean(xv * xv, axis=-1, keepdims=True) + NORM_EPS)
        n = xv * rstd
        gv = g_ref[...]
        err = n * gv - t_ref[...]
        dout = err * inv_d
        dn = dout * gv
        dx_ref[...] = rstd * (dn - n * jnp.mean(dn * n, axis=-1, keepdims=True))

        @pl.when(i == 0)
        def _():
            sq_ref[...] = jnp.zeros_like(sq_ref)
            dg_ref[...] = jnp.zeros_like(dg_ref)

        sq_ref[...] += _colsum8(err * err)
        dg_ref[...] += _colsum8(dout * n)

    tile = pl.BlockSpec((t, d), lambda i: (i, 0))
    acc = pl.BlockSpec((8, d), lambda i: (0, 0))
    acc_s = jax.ShapeDtypeStruct((8, d), F32)
    return pl.pallas_call(
        body, name=name, grid=(s // t,), in_specs=[tile, pl.BlockSpec((1, d), lambda i: (0, 0)), tile],
        out_specs=[tile, acc, acc], out_shape=[jax.ShapeDtypeStruct((s, d), F32), acc_s, acc_s],
        compiler_params=_cparams(("arbitrary",)))(x, g, target)


NEG_BIG = -1e30


def _head_masks(rows):
    lane = lax.broadcasted_iota(jnp.int32, (rows, LANES), 1)
    return [(lane // HEAD_DIM) == hh for hh in range(2)]


def _group_view(slabs, gi):
    dil = ATTN_GROUPS[gi][1]
    _, s, w = slabs.shape
    if dil == 1:
        return slabs, (gi, 3 + gi, 6 + gi)
    return slabs[gi:9:3].reshape(3, s // dil, dil * w), (0, 1, 2)


def _attn_fwd(slabs, gi, name):
    window, dil = ATTN_GROUPS[gi]
    n_back = window // dil
    _, s, w = slabs.shape
    big_l = s // dil
    nb = big_l // ATTN_BLOCK
    blk = ATTN_BLOCK
    view, (iq, ik, iv) = _group_view(slabs, gi)
    scale = HEAD_DIM ** -0.5

    def body(q_ref, kp_ref, kc_ref, vp_ref, vc_ref, o_ref, lse_ref):
        n = pl.program_id(1)
        qi = lax.broadcasted_iota(jnp.int32, (blk, 2 * blk), 0)
        kj = lax.broadcasted_iota(jnp.int32, (blk, 2 * blk), 1)
        dist = qi + blk - kj
        ok = (dist >= 0) & (dist <= n_back) & ((kj >= blk) | (n > 0))
        mq = _head_masks(blk)
        mk = _head_masks(2 * blk)
        for p in range(w // LANES):
            cols = slice(p * LANES, (p + 1) * LANES)
            qp = q_ref[:, cols]
            k2 = jnp.concatenate([kp_ref[:, cols], kc_ref[:, cols]], axis=0)
            v2 = jnp.concatenate([vp_ref[:, cols], vc_ref[:, cols]], axis=0)
            o_pair = jnp.zeros((blk, LANES), F32)
            lse_pair = jnp.zeros((blk, LANES), F32)
            for hh in range(2):
                qm = jnp.where(mq[hh], qp, jnp.zeros_like(qp))
                sc = lax.dot_general(qm, k2, NT, preferred_element_type=F32) * scale
                sc = jnp.where(ok, sc, NEG_BIG)
                mx = jnp.max(sc, axis=-1, keepdims=True)
                pe = jnp.exp(sc - mx)
                den = jnp.sum(pe, axis=-1, keepdims=True)
                pn = (pe / den).astype(BF16)
                vm = jnp.where(mk[hh], v2, jnp.zeros_like(v2))
                o_pair = o_pair + jnp.dot(pn, vm, preferred_element_type=F32)
                lse_pair = jnp.where(mq[hh], mx + jnp.log(den), lse_pair)
            o_ref[:, cols] = o_pair
            lse_ref[:, cols] = lse_pair

    def spec(slab, prev):
        if prev:
            return pl.BlockSpec((None, blk, w), lambda r, n: (slab, jnp.maximum(n - 1, 0), r))
        return pl.BlockSpec((None, blk, w), lambda r, n: (slab, n, r))

    out_spec = pl.BlockSpec((blk, w), lambda r, n: (n, r))
    sds = jax.ShapeDtypeStruct((big_l, dil * w), F32)
    o, lse = pl.pallas_call(
        body, name=name, grid=(dil, nb),
        in_specs=[spec(iq, False), spec(ik, True), spec(ik, False), spec(iv, True), spec(iv, False)],
        out_specs=[out_spec, out_spec], out_shape=[sds, sds],
        compiler_params=_cparams(("parallel", "arbitrary")))(view, view, view, view, view)
    return o.reshape(s, w), lse.reshape(s, w)


def _attn_merge(outs, lses, name):
    s, w = outs[0].shape
    t = TOK_TILE

    def body(o0, o1, o2, l0, l1, l2, a_ref, lt_ref):
        ls = [l0[...], l1[...], l2[...]]
        mx = jnp.maximum(jnp.maximum(ls[0], ls[1]), ls[2])
        es = [jnp.exp(l - mx) for l in ls]
        den = es[0] + es[1] + es[2]
        num = es[0] * o0[...] + es[1] * o1[...] + es[2] * o2[...]
        a_ref[...] = (num / den).astype(BF16)
        lt_ref[...] = mx + jnp.log(den)

    tile = pl.BlockSpec((t, w), lambda i: (i, 0))
    return pl.pallas_call(
        body, name=name, grid=(s // t,), in_specs=[tile] * 6, out_specs=[tile, tile],
        out_shape=[jax.ShapeDtypeStruct((s, w), BF16), jax.ShapeDtypeStruct((s, w), F32)],
        compiler_params=_cparams(("parallel",)))(*outs, *lses)


def _attn_bwd(slabs, dap, ap, lse_tot, gi, name):
    window, dil = ATTN_GROUPS[gi]
    n_back = window // dil
    _, s, w = slabs.shape
    big_l = s // dil
    nb = big_l // ATTN_BLOCK
    blk = ATTN_BLOCK
    view, (iq, ik, iv) = _group_view(slabs, gi)
    dap_v = dap.reshape(big_l, dil * 2 * w)
    ap_v = ap.reshape(big_l, dil * 2 * w)
    lse_v = lse_tot.reshape(big_l, dil * w)
    scale = HEAD_DIM ** -0.5

    def body(qc_ref, qn_ref, kp_ref, kc_ref, vp_ref, vc_ref, dc_ref, dn_ref, ac_ref, an_ref, lc_ref, ln_ref, out_ref):
        m = pl.program_id(1)
        qi = lax.broadcasted_iota(jnp.int32, (blk, 2 * blk), 0)
        kj = lax.broadcasted_iota(jnp.int32, (blk, 2 * blk), 1)
        dist = qi + blk - kj
        ok = (dist >= 0) & (dist <= n_back) & ((kj >= blk) | (m > 0))
        qi1 = lax.broadcasted_iota(jnp.int32, (blk, blk), 0)
        kj1 = lax.broadcasted_iota(jnp.int32, (blk, blk), 1)
        ok_next = (qi1 + blk - kj1 <= n_back) & (m + 1 < nb)
        mq = _head_masks(blk)
        mk = _head_masks(2 * blk)

        def per_head(q_t, d_t, a_t, l_t, hh):
            qm = jnp.where(mq[hh], q_t, jnp.zeros_like(q_t))
            dm = jnp.where(mq[hh], d_t, jnp.zeros_like(d_t))
            delta = jnp.sum(jnp.where(mq[hh], d_t.astype(F32) * a_t.astype(F32), 0.0), axis=-1, keepdims=True)
            lse_h = jnp.max(jnp.where(mq[hh], l_t, NEG_BIG), axis=-1, keepdims=True)
            return qm, dm, delta, lse_h

        for p in range(w // LANES):
            cols = slice(p * LANES, (p + 1) * LANES)
            k2 = jnp.concatenate([kp_ref[:, cols], kc_ref[:, cols]], axis=0)
            v2 = jnp.concatenate([vp_ref[:, cols], vc_ref[:, cols]], axis=0)
            kc, vc = kc_ref[:, cols], vc_ref[:, cols]
            dq_pair = jnp.zeros((blk, LANES), F32)
            dk_pair = jnp.zeros((blk, LANES), F32)
            dv_pair = jnp.zeros((blk, LANES), F32)
            for hh in range(2):
                qm, dm, delta, lse_h = per_head(qc_ref[:, cols], dc_ref[:, cols], ac_ref[:, cols], lc_ref[:, cols], hh)
                sc = lax.dot_general(qm, k2, NT, preferred_element_type=F32) * scale
                pr = jnp.exp(jnp.where(ok, sc, NEG_BIG) - lse_h)
                dp = lax.dot_general(dm, v2, NT, preferred_element_type=F32)
                ds = pr * (dp - delta)
                ds_b = ds.astype(BF16)
                km = jnp.where(mk[hh], k2, jnp.zeros_like(k2))
                dq_pair = dq_pair + jnp.dot(ds_b, km, preferred_element_type=F32)
                dk_pair = dk_pair + lax.dot_general(ds_b[:, blk:], qm, TN, preferred_element_type=F32)
                dv_pair = dv_pair + lax.dot_general(pr[:, blk:].astype(BF16), dm, TN, preferred_element_type=F32)
                qm, dm, delta, lse_h = per_head(qn_ref[:, cols], dn_ref[:, cols], an_ref[:, cols], ln_ref[:, cols], hh)
                sc = lax.dot_general(qm, kc, NT, preferred_element_type=F32) * scale
                pr = jnp.exp(jnp.where(ok_next, sc, NEG_BIG) - lse_h)
                dp = lax.dot_general(dm, vc, NT, preferred_element_type=F32)
                ds_b = (pr * (dp - delta)).astype(BF16)
                dk_pair = dk_pair + lax.dot_general(ds_b, qm, TN, preferred_element_type=F32)
                dv_pair = dv_pair + lax.dot_general(pr.astype(BF16), dm, TN, preferred_element_type=F32)
            out_ref[0, :, cols] = (dq_pair * scale).astype(BF16)
            out_ref[1, :, cols] = (dk_pair * scale).astype(BF16)
            out_ref[2, :, cols] = dv_pair.astype(BF16)

    def slab_spec(slab, shift):
        if shift < 0:
            return pl.BlockSpec((None, blk, w), lambda r, n: (slab, jnp.maximum(n - 1, 0), r))
        if shift > 0:
            return pl.BlockSpec((None, blk, w), lambda r, n: (slab, jnp.minimum(n + 1, nb - 1), r))
        return pl.BlockSpec((None, blk, w), lambda r, n: (slab, n, r))

    def tok_spec(stride, shift):
        if shift > 0:
            return pl.BlockSpec((blk, w), lambda r, n: (jnp.minimum(n + 1, nb - 1), stride * r))
        return pl.BlockSpec((blk, w), lambda r, n: (n, stride * r))

    out = pl.pallas_call(
        body, name=name, grid=(dil, nb),
        in_specs=[slab_spec(iq, 0), slab_spec(iq, 1), slab_spec(ik, -1), slab_spec(ik, 0),
                  slab_spec(iv, -1), slab_spec(iv, 0),
                  tok_spec(2, 0), tok_spec(2, 1), tok_spec(2, 0), tok_spec(2, 1), tok_spec(1, 0), tok_spec(1, 1)],
        out_specs=pl.BlockSpec((3, blk, w), lambda r, n: (0, n, r)),
        out_shape=jax.ShapeDtypeStruct((3, big_l, dil * w), BF16),
        compiler_params=_cparams(("parallel", "arbitrary")))(
            view, view, view, view, view, view, dap_v, dap_v, ap_v, ap_v, lse_v, lse_v)
    return out.reshape(3, s, w)


def _shift_down(cur, prev, j):
    row = lax.broadcasted_iota(jnp.int32, cur.shape, 0)
    return jnp.where(row >= j, pltpu.roll(cur, j, 0), pltpu.roll(prev, j, 0))


def _shift_up(cur, nxt, j):
    t = cur.shape[0]
    row = lax.broadcasted_iota(jnp.int32, cur.shape, 0)
    return jnp.where(row < t - j, pltpu.roll(cur, t - j, 0), pltpu.roll(nxt, t - j, 0))


def _window_sum_down(cur, prev, w):
    s, sp, step = cur, prev, 1
    while step < w:
        s_new = s + _shift_down(s, sp, step)
        sp = sp + pltpu.roll(sp, step, 0)
        s, step = s_new, step * 2
    return s


def _window_sum_up(cur, nxt, w):
    t = cur.shape[0]
    s, sn, step = cur, nxt, 1
    while step < w:
        s_new = s + _shift_up(s, sn, step)
        sn = sn + pltpu.roll(sn, t - step, 0)
        s, step = s_new, step * 2
    return s


def _pool_count(tile_idx, t, w):
    row = lax.broadcasted_iota(jnp.int32, (t, LANES), 0) + tile_idx * t
    return jnp.minimum(row + 1, w).astype(F32)


def _pool_fwd(slabs, attn, pool_w, pool_scale, name):
    _, s, w = slabs.shape
    t = TOK_TILE
    gw = POOL_W // len(POOL_WINDOWS)

    def body(u_ref, up_ref, a_ref, w_ref, sc_ref, ap_ref, d_ref):
        i = pl.program_id(0)
        ap_ref[:, :w] = a_ref[...]
        for gi, win in enumerate(POOL_WINDOWS):
            cols = slice(gi * gw, (gi + 1) * gw)
            u = u_ref[:, cols].astype(F32)
            up = jnp.where(i > 0, up_ref[:, cols].astype(F32), 0.0)
            d = (_window_sum_down(u, up, win) / _pool_count(i, t, win) - u).astype(BF16)
            d_ref[:, cols] = d
            y = jnp.dot(d, w_ref[gi].astype(BF16), preferred_element_type=F32)
            ap_ref[:, w + gi * gw:w + (gi + 1) * gw] = (y * sc_ref[:, cols]).astype(BF16)

    return pl.pallas_call(
        body, name=name, grid=(s // t,),
        in_specs=[pl.BlockSpec((None, t, w), lambda i: (N_SLABS - 1, i, 0)),
                  pl.BlockSpec((None, t, w), lambda i: (N_SLABS - 1, jnp.maximum(i - 1, 0), 0)),
                  pl.BlockSpec((t, w), lambda i: (i, 0)),
                  pl.BlockSpec((len(POOL_WINDOWS), gw, gw), lambda i: (0, 0, 0)),
                  pl.BlockSpec((1, w), lambda i: (0, 0))],
        out_specs=[pl.BlockSpec((t, 2 * w), lambda i: (i, 0)), pl.BlockSpec((t, w), lambda i: (i, 0))],
        out_shape=[jax.ShapeDtypeStruct((s, 2 * w), BF16), jax.ShapeDtypeStruct((s, w), BF16)],
        compiler_params=_cparams(("parallel",)))(slabs, slabs, attn, pool_w, pool_scale)


def _pool_bwd(dap, d, pool_w, pool_scale, name):
    s, w = d.shape
    t = TOK_TILE
    nt = s // t
    gw = POOL_W // len(POOL_WINDOWS)

    def body(dy_ref, dyn_ref, d_ref, w_ref, sc_ref, du_ref, dw_ref, dsc_ref):
        i = pl.program_id(0)

        @pl.when(i == 0)
        def _():
            dw_ref[...] = jnp.zeros_like(dw_ref)
            dsc_ref[...] = jnp.zeros_like(dsc_ref)

        for gi, win in enumerate(POOL_WINDOWS):
            cols = slice(gi * gw, (gi + 1) * gw)
            wb = w_ref[gi].astype(BF16)
            scale = sc_ref[:, cols]
            dv = d_ref[:, cols]
            dy = dy_ref[:, cols].astype(F32)
            y = jnp.dot(dv, wb, preferred_element_type=F32)
            dsc_ref[:, cols] += _colsum8(dy * y)
            dyp = (dy * scale).astype(BF16)
            dw_ref[gi] += lax.dot_general(dv, dyp, TN, preferred_element_type=F32)
            dd = lax.dot_general(dyp, wb, NT, preferred_element_type=F32)
            dypn = (dyn_ref[:, cols].astype(F32) * scale).astype(BF16)
            ddn = lax.dot_general(dypn, wb, NT, preferred_element_type=F32)
            e = dd / _pool_count(i, t, win)
            en = jnp.where(i + 1 < nt, ddn / _pool_count(i + 1, t, win), 0.0)
            du_ref[:, cols] = (_window_sum_up(e, en, win) - dd).astype(BF16)

    return pl.pallas_call(
        body, name=name, grid=(nt,),
        in_specs=[pl.BlockSpec((t, w), lambda i: (i, 1)),
                  pl.BlockSpec((t, w), lambda i: (jnp.minimum(i + 1, nt - 1), 1)),
                  pl.BlockSpec((t, w), lambda i: (i, 0)),
                  pl.BlockSpec((len(POOL_WINDOWS), gw, gw), lambda i: (0, 0, 0)),
                  pl.BlockSpec((1, w), lambda i: (0, 0))],
        out_specs=[pl.BlockSpec((t, w), lambda i: (i, 0)),
                   pl.BlockSpec((len(POOL_WINDOWS), gw, gw), lambda i: (0, 0, 0)),
                   pl.BlockSpec((8, w), lambda i: (0, 0))],
        out_shape=[jax.ShapeDtypeStruct((s, w), BF16), jax.ShapeDtypeStruct((len(POOL_WINDOWS), gw, gw), F32),
                   jax.ShapeDtypeStruct((8, w), F32)],
        compiler_params=_cparams(("arbitrary",)))(dap, dap, d, pool_w, pool_scale)


CONV_TAPS = 4
CONV_COLS = 512


def _sigmoid(v):
    return 1.0 / (1.0 + jnp.exp(-v))


def _conv_pre(x, xprev, w_ref, b_ref):
    pre = b_ref[...] + w_ref[CONV_TAPS - 1:CONV_TAPS, :] * x
    for k in range(CONV_TAPS - 1):
        pre = pre + w_ref[k:k + 1, :] * _shift_down(x, xprev, CONV_TAPS - 1 - k)
    return pre


def _conv_fwd(xbc, conv_w, conv_b, name):
    s, c = xbc.shape
    t, tc = TOK_TILE, CONV_COLS

    def body(x_ref, xp_ref, w_ref, b_ref, o_ref):
        i = pl.program_id(0)
        x = x_ref[...].astype(F32)
        xp = jnp.where(i > 0, xp_ref[...].astype(F32), 0.0)
        pre = _conv_pre(x, xp, w_ref, b_ref)
        o_ref[...] = (pre * _sigmoid(pre)).astype(BF16)

    return pl.pallas_call(
        body, name=name, grid=(s // t, c // tc),
        in_specs=[pl.BlockSpec((t, tc), lambda i, j: (i, j)),
                  pl.BlockSpec((t, tc), lambda i, j: (jnp.maximum(i - 1, 0), j)),
                  pl.BlockSpec((CONV_TAPS, tc), lambda i, j: (0, j)), pl.BlockSpec((1, tc), lambda i, j: (0, j))],
        out_specs=pl.BlockSpec((t, tc), lambda i, j: (i, j)), out_shape=jax.ShapeDtypeStruct((s, c), BF16),
        compiler_params=_cparams(("parallel", "parallel")))(xbc, xbc, conv_w, conv_b)


def _conv_bwd(xbc, dact, conv_w, conv_b, name):
    s, c = xbc.shape
    t, tc = TOK_TILE, CONV_COLS
    nt = s // t

    def body(xp_ref, x_ref, xn_ref, da_ref, dan_ref, w_ref, b_ref, dx_ref, dw_ref, db_ref):
        i = pl.program_id(1)

        @pl.when(i == 0)
        def _():
            dw_ref[...] = jnp.zeros_like(dw_ref)
            db_ref[...] = jnp.zeros_like(db_ref)

        x = x_ref[...].astype(F32)
        xp = jnp.where(i > 0, xp_ref[...].astype(F32), 0.0)
        xn = xn_ref[...].astype(F32)

        def dsilu(pre):
            sg = _sigmoid(pre)
            return sg * (1.0 + pre * (1.0 - sg))

        dpre = da_ref[...] * dsilu(_conv_pre(x, xp, w_ref, b_ref))
        dpre_n = jnp.where(i + 1 < nt, dan_ref[...] * dsilu(_conv_pre(xn, x, w_ref, b_ref)), 0.0)
        dx = w_ref[CONV_TAPS - 1:CONV_TAPS, :] * dpre
        dw_ref[CONV_TAPS - 1] += _colsum8(dpre * x)
        for k in range(CONV_TAPS - 1):
            j = CONV_TAPS - 1 - k
            dx = dx + w_ref[k:k + 1, :] * _shift_up(dpre, dpre_n, j)
            dw_ref[k] += _colsum8(dpre * _shift_down(x, xp, j))
        dx_ref[...] = dx.astype(BF16)
        db_ref[...] += _colsum8(dpre)

    def xspec(shift):
        if shift < 0:
            return pl.BlockSpec((t, tc), lambda j, i: (jnp.maximum(i - 1, 0), j))
        if shift > 0:
            return pl.BlockSpec((t, tc), lambda j, i: (jnp.minimum(i + 1, nt - 1), j))
        return pl.BlockSpec((t, tc), lambda j, i: (i, j))

    return pl.pallas_call(
        body, name=name, grid=(c // tc, nt),
        in_specs=[xspec(-1), xspec(0), xspec(1), xspec(0), xspec(1),
                  pl.BlockSpec((CONV_TAPS, tc), lambda j, i: (0, j)), pl.BlockSpec((1, tc), lambda j, i: (0, j))],
        out_specs=[xspec(0), pl.BlockSpec((CONV_TAPS, 8, tc), lambda j, i: (0, 0, j)),
                   pl.BlockSpec((8, tc), lambda j, i: (0, j))],
        out_shape=[jax.ShapeDtypeStruct((s, c), BF16), jax.ShapeDtypeStruct((CONV_TAPS, 8, c), F32),
                   jax.ShapeDtypeStruct((8, c), F32)],
        compiler_params=_cparams(("parallel", "arbitrary")))(xbc, xbc, xbc, dact, dact, conv_w, conv_b)


GN_TILE = 256


def _gated_norm_fwd(y, z, g, name):
    s, c = y.shape
    t = GN_TILE

    def body(y_ref, z_ref, g_ref, o_ref):
        for gi in range(SSM_GROUPS):
            cols = slice(gi * SSM_GW, (gi + 1) * SSM_GW)
            zv = z_ref[:, cols].astype(F32)
            yf = y_ref[:, cols] * (zv * _sigmoid(zv))
            r = lax.rsqrt(jnp.mean(yf * yf, axis=-1, keepdims=True) + NORM_EPS)
            o_ref[:, cols] = (yf * r * g_ref[:, cols]).astype(BF16)

    tile = pl.BlockSpec((t, c), lambda i: (i, 0))
    return pl.pallas_call(
        body, name=name, grid=(s // t,), in_specs=[tile, tile, pl.BlockSpec((1, c), lambda i: (0, 0))],
        out_specs=tile, out_shape=jax.ShapeDtypeStruct((s, c), BF16),
        compiler_params=_cparams(("parallel",)))(y, z, g)


def _gated_norm_bwd(y, z, dout, g, name):
    s, c = y.shape
    t = GN_TILE

    def body(y_ref, z_ref, do_ref, g_ref, dy_ref, dz_ref, dg_ref):
        i = pl.program_id(0)

        @pl.when(i == 0)
        def _():
            dg_ref[...] = jnp.zeros_like(dg_ref)

        for gi in range(SSM_GROUPS):
            cols = slice(gi * SSM_GW, (gi + 1) * SSM_GW)
            zv = z_ref[:, cols].astype(F32)
            yv = y_ref[:, cols]
            sg = _sigmoid(zv)
            sz = zv * sg
            yf = yv * sz
            r = lax.rsqrt(jnp.mean(yf * yf, axis=-1, keepdims=True) + NORM_EPS)
            n = yf * r
            dout = do_ref[:, cols]
            dn = dout * g_ref[:, cols]
            dg_ref[:, cols] += _colsum8(dout * n)
            dyf = r * (dn - n * jnp.mean(dn * n, axis=-1, keepdims=True))
            dy_ref[:, cols] = dyf * sz
            dz_ref[:, cols] = (dyf * yv * (sg * (1.0 + zv * (1.0 - sg)))).astype(BF16)

    tile = pl.BlockSpec((t, c), lambda i: (i, 0))
    return pl.pallas_call(
        body, name=name, grid=(s // t,), in_specs=[tile, tile, tile, pl.BlockSpec((1, c), lambda i: (0, 0))],
        out_specs=[tile, tile, pl.BlockSpec((8, c), lambda i: (0, 0))],
        out_shape=[jax.ShapeDtypeStruct((s, c), F32), jax.ShapeDtypeStruct((s, c), BF16),
                   jax.ShapeDtypeStruct((8, c), F32)],
        compiler_params=_cparams(("arbitrary",)))(y, z, dout, g)


def _split_dot(x, e, dims, terms):
    r, acc = x, None
    for i in range(terms):
        p = r.astype(BF16)
        part = lax.dot_general(p, e, dims, preferred_element_type=F32)
        acc = part if acc is None else acc + part
        if i + 1 < terms:
            r = r - p.astype(F32)
    return acc


def _split_dot_r(e, x, dims, terms):
    r, acc = x, None
    for i in range(terms):
        p = r.astype(BF16)
        part = lax.dot_general(e, p, dims, preferred_element_type=F32)
        acc = part if acc is None else acc + part
        if i + 1 < terms:
            r = r - p.astype(F32)
    return acc


def _ssd_prep(dtr_ref, bias_ref, alog_ref):
    q = SSM_CHUNK
    dtr = dtr_ref[...] + bias_ref[...]
    dt = jnp.maximum(dtr, 0.0) + jnp.log(1.0 + jnp.exp(-jnp.abs(dtr)))
    a_neg = -jnp.exp(alog_ref[...])
    dta = dt * a_neg
    row = lax.broadcasted_iota(jnp.int32, (q, q), 0)
    col = lax.broadcasted_iota(jnp.int32, (q, q), 1)
    causal = row >= col
    a = _split_dot_r(causal.astype(BF16), dta, NN, 3)
    aq_row = jnp.sum(dta, axis=0, keepdims=True)
    aq_hb = _split_dot(dta, jnp.ones((q, LANES), BF16), TN, 3)
    return dict(dtr=dtr, dt=dt, a_neg=a_neg, a=a, a_t=jnp.transpose(a), aq_row=aq_row, aq_hb=aq_hb,
                ea=jnp.exp(a), fa=jnp.exp(aq_row - a), causal=causal)


def _ssd_group_mats(g):
    hrow = lax.broadcasted_iota(jnp.int32, (LANES, SSM_GW), 0)
    jcol = lax.broadcasted_iota(jnp.int32, (LANES, SSM_GW), 1)
    eg = (hrow == 8 * g + jcol // HEAD_DIM).astype(BF16)
    jrow = lax.broadcasted_iota(jnp.int32, (SSM_GW, LANES), 0)
    hcol = lax.broadcasted_iota(jnp.int32, (SSM_GW, LANES), 1)
    eg_t = (hcol == 8 * g + jrow // HEAD_DIM).astype(BF16)
    hrow2 = lax.broadcasted_iota(jnp.int32, (LANES, 8 * LANES), 0)
    jcol2 = lax.broadcasted_iota(jnp.int32, (LANES, 8 * LANES), 1)
    sel = (hrow2 == 8 * g + jcol2 // LANES).astype(BF16)
    return eg, eg_t, sel


def _ssd_fwd(xbc, dt_raw, dt_bias, a_log, d_e, name):
    s = xbc.shape[0]
    q = SSM_CHUNK
    nc = s // q

    def body(x_ref, dtr_ref, bias_ref, alog_ref, de_ref, y_ref, hin_ref, state, at_ref):
        c = pl.program_id(0)

        @pl.when(c == 0)
        def _():
            state[...] = jnp.zeros_like(state)

        hin_ref[...] = state[...].astype(BF16)
        pr = _ssd_prep(dtr_ref, bias_ref, alog_ref)
        at_ref[...] = pr["a_t"]
        exp_aq_hb = jnp.exp(pr["aq_hb"])
        stack3 = jnp.concatenate([pr["dt"], pr["ea"], pr["fa"]], axis=0)
        mq = _head_masks(q)
        for g in range(SSM_GROUPS):
            eg, eg_t, sel = _ssd_group_mats(g)
            cols = slice(g * SSM_GW, (g + 1) * SSM_GW)
            xg = x_ref[:, cols]
            bg = x_ref[:, SSM_INNER + g * SSM_STATE:SSM_INNER + (g + 1) * SSM_STATE]
            cg = x_ref[:, SSM_INNER + SSM_GROUPS * SSM_STATE + g * SSM_STATE:
                       SSM_INNER + SSM_GROUPS * SSM_STATE + (g + 1) * SSM_STATE]
            e3 = _split_dot(stack3, eg, NN, 2)
            dt_e, ea_e, fa_e = e3[:q], e3[q:2 * q], e3[2 * q:]
            xf = xg.astype(F32)
            xdt = xf * dt_e
            xdt_b = xdt.astype(BF16)
            cb = lax.dot_general(cg, bg, NT, preferred_element_type=F32)
            colb = _split_dot(pr["a"], sel, NN, 3)
            hg = state[cols, :]
            y_g = lax.dot_general(cg, hg.astype(BF16), NT, preferred_element_type=F32) * ea_e + de_ref[:, cols] * xf
            pairs = []
            for pp in range(4):
                xp = xdt_b[:, pp * LANES:(pp + 1) * LANES]
                yp = jnp.zeros((q, LANES), F32)
                for hh in range(2):
                    hi = 2 * pp + hh
                    diff = colb[:, hi * LANES:(hi + 1) * LANES] - at_ref[8 * g + hi:8 * g + hi + 1, :]
                    lmat = jnp.exp(jnp.where(pr["causal"], diff, NEG_BIG))
                    m_b = (cb * lmat).astype(BF16)
                    yp = yp + jnp.dot(m_b, jnp.where(mq[hh], xp, jnp.zeros_like(xp)), preferred_element_type=F32)
                pairs.append(yp)
            y_ref[:, cols] = y_g + jnp.concatenate(pairs, axis=1)
            s_g = lax.dot_general((xdt * fa_e).astype(BF16), bg, TN, preferred_element_type=F32)
            dec_g = _split_dot_r(eg, exp_aq_hb, TN, 2)
            state[cols, :] = dec_g * hg + s_g

    row128 = pl.BlockSpec((1, LANES), lambda c: (0, 0))
    return pl.pallas_call(
        body, name=name, grid=(nc,),
        in_specs=[pl.BlockSpec((q, SSM_CONV_DIM), lambda c: (c, 0)), pl.BlockSpec((q, LANES), lambda c: (c, 0)),
                  row128, row128, pl.BlockSpec((1, SSM_INNER), lambda c: (0, 0))],
        out_specs=[pl.BlockSpec((q, SSM_INNER), lambda c: (c, 0)),
                   pl.BlockSpec((None, SSM_INNER, SSM_STATE), lambda c: (c, 0, 0))],
        out_shape=[jax.ShapeDtypeStruct((s, SSM_INNER), F32), jax.ShapeDtypeStruct((nc, SSM_INNER, SSM_STATE), BF16)],
        scratch_shapes=[pltpu.VMEM((SSM_INNER, SSM_STATE), F32), pltpu.VMEM((LANES, q), F32)],
        compiler_params=_cparams(("arbitrary",)))(xbc, dt_raw, dt_bias, a_log, d_e)


def _ssd_bwd(xbc, dt_raw, dt_bias, a_log, d_e, hin, dy, name):
    s = xbc.shape[0]
    q = SSM_CHUNK
    nc = s // q

    def body(x_ref, dtr_ref, bias_ref, alog_ref, de_ref, hin_ref, dy_ref,
             dx_ref, ddt_ref, da_acc, db_acc, dd_acc, gst, at_ref):
        i = pl.program_id(0)

        @pl.when(i == 0)
        def _():
            gst[...] = jnp.zeros_like(gst)
            da_acc[...] = jnp.zeros_like(da_acc)
            db_acc[...] = jnp.zeros_like(db_acc)
            dd_acc[...] = jnp.zeros_like(dd_acc)

        pr = _ssd_prep(dtr_ref, bias_ref, alog_ref)
        at_ref[...] = pr["a_t"]
        exp_aq_hb = jnp.exp(pr["aq_hb"])
        stack3 = jnp.concatenate([pr["dt"], pr["ea"], pr["fa"]], axis=0)
        mq = _head_masks(q)
        lane_h = lax.broadcasted_iota(jnp.int32, (q, LANES), 1)
        sub_h = lax.broadcasted_iota(jnp.int32, (LANES, q), 0)
        da = jnp.zeros((q, LANES), F32)
        da_tn = jnp.zeros((LANES, q), F32)
        daq = jnp.zeros((1, LANES), F32)
        ddt = jnp.zeros((q, LANES), F32)
        for g in range(SSM_GROUPS):
            eg, eg_t, sel = _ssd_group_mats(g)
            cols = slice(g * SSM_GW, (g + 1) * SSM_GW)
            bcols = slice(SSM_INNER + g * SSM_STATE, SSM_INNER + (g + 1) * SSM_STATE)
            ccols = slice(SSM_INNER + SSM_GROUPS * SSM_STATE + g * SSM_STATE,
                          SSM_INNER + SSM_GROUPS * SSM_STATE + (g + 1) * SSM_STATE)
            xg, bg, cg = x_ref[:, cols], x_ref[:, bcols], x_ref[:, ccols]
            e3 = _split_dot(stack3, eg, NN, 2)
            dt_e, ea_e, fa_e = e3[:q], e3[q:2 * q], e3[2 * q:]
            xf = xg.astype(F32)
            xdt = xf * dt_e
            xdt_b = xdt.astype(BF16)
            xdtf = xdt * fa_e
            xdtf_b = xdtf.astype(BF16)
            cb = lax.dot_general(cg, bg, NT, preferred_element_type=F32)
            colb = _split_dot(pr["a"], sel, NN, 3)
            dyg = dy_ref[:, cols]
            dd_acc[:, cols] += _colsum8(dyg * xf)
            hg_b = hin_ref[cols, :]
            gg = gst[cols, :]
            gg_b = gg.astype(BF16)
            dye_b = (dyg * ea_e).astype(BF16)
            dc_g = jnp.dot(dye_b, hg_b, preferred_element_type=F32)
            dh_g = lax.dot_general(dye_b, cg, TN, preferred_element_type=F32)
            yoff = lax.dot_general(cg, hg_b, NT, preferred_element_type=F32) * ea_e
            da = da + _split_dot(dyg * yoff, eg_t, NN, 2)
            db_g = jnp.dot(xdtf_b, gg_b, preferred_element_type=F32)
            tmat = lax.dot_general(bg, gg_b, NT, preferred_element_type=F32)
            dxdt = fa_e * tmat
            qmat = _split_dot(xdtf * tmat, eg_t, NN, 2)
            da = da - qmat
            daq = daq + jnp.sum(qmat, axis=0, keepdims=True)
            gh = _split_dot(gg * hg_b.astype(F32), eg_t, TN, 2)
            daq = daq + jnp.sum(gh, axis=0, keepdims=True) * jnp.exp(pr["aq_row"])
            dcb = jnp.zeros((q, q), F32)
            pairs = []
            for pp in range(4):
                xp = xdt_b[:, pp * LANES:(pp + 1) * LANES]
                dyp = dyg[:, pp * LANES:(pp + 1) * LANES]
                dxp = jnp.zeros((q, LANES), F32)
                for hh in range(2):
                    hi = 2 * pp + hh
                    h = 8 * g + hi
                    diff = colb[:, hi * LANES:(hi + 1) * LANES] - at_ref[h:h + 1, :]
                    lmat = jnp.exp(jnp.where(pr["causal"], diff, NEG_BIG))
                    mmat = cb * lmat
                    dyh = jnp.where(mq[hh], dyp, 0.0).astype(BF16)
                    dm = lax.dot_general(dyh, xp, NT, preferred_element_type=F32)
                    dxp = dxp + lax.dot_general(mmat.astype(BF16), dyh, TN, preferred_element_type=F32)
                    wmat = dm * mmat
                    da = da + jnp.where(lane_h == h, jnp.sum(wmat, axis=-1, keepdims=True), 0.0)
                    da_tn = da_tn + jnp.where(sub_h == h, jnp.sum(wmat, axis=0, keepdims=True), 0.0)
                    dcb = dcb + dm * lmat
                pairs.append(dxp)
            dxdt = dxdt + jnp.concatenate(pairs, axis=1)
            dcb_b = dcb.astype(BF16)
            dc_g = dc_g + jnp.dot(dcb_b, bg, preferred_element_type=F32)
            db_g = db_g + lax.dot_general(dcb_b, cg, TN, preferred_element_type=F32)
            ddt = ddt + _split_dot(dxdt * xf, eg_t, NN, 2)
            dx_ref[:, cols] = dxdt * dt_e + de_ref[:, cols] * dyg
            dx_ref[:, bcols] = db_g
            dx_ref[:, ccols] = dc_g
            dec_g = _split_dot_r(eg, exp_aq_hb, TN, 2)
            gst[cols, :] = dh_g + dec_g * gg
        row = lax.broadcasted_iota(jnp.int32, (q, LANES), 0)
        da_all = da - jnp.transpose(da_tn) + jnp.where(row == q - 1, daq, 0.0)
        upper = jnp.logical_not(pr["causal"]) | (lax.broadcasted_iota(jnp.int32, (q, q), 0)
                                                 == lax.broadcasted_iota(jnp.int32, (q, q), 1))
        rcs = _split_dot_r(upper.astype(BF16), da_all, NN, 3)
        ddt_all = ddt + pr["a_neg"] * rcs
        da_acc[...] += _colsum8(pr["dt"] * rcs)
        ddtr = jnp.where(lane_h < SSM_HEADS, ddt_all * _sigmoid(pr["dtr"]), 0.0)
        ddt_ref[...] = ddtr
        db_acc[...] += _colsum8(ddtr)

    rev = lambda i: (nc - 1 - i, 0)
    row128 = pl.BlockSpec((1, LANES), lambda i: (0, 0))
    acc128 = pl.BlockSpec((8, LANES), lambda i: (0, 0))
    return pl.pallas_call(
        body, name=name, grid=(nc,),
        in_specs=[pl.BlockSpec((q, SSM_CONV_DIM), rev), pl.BlockSpec((q, LANES), rev), row128, row128,
                  pl.BlockSpec((1, SSM_INNER), lambda i: (0, 0)),
                  pl.BlockSpec((None, SSM_INNER, SSM_STATE), lambda i: (nc - 1 - i, 0, 0)),
                  pl.BlockSpec((q, SSM_INNER), rev)],
        out_specs=[pl.BlockSpec((q, SSM_CONV_DIM), rev), pl.BlockSpec((q, LANES), rev), acc128, acc128,
                   pl.BlockSpec((8, SSM_INNER), lambda i: (0, 0))],
        out_shape=[jax.ShapeDtypeStruct((s, SSM_CONV_DIM), F32), jax.ShapeDtypeStruct((s, LANES), F32),
                   jax.ShapeDtypeStruct((8, LANES), F32), jax.ShapeDtypeStruct((8, LANES), F32),
                   jax.ShapeDtypeStruct((8, SSM_INNER), F32)],
        scratch_shapes=[pltpu.VMEM((SSM_INNER, SSM_STATE), F32), pltpu.VMEM((LANES, q), F32)],
        compiler_params=_cparams(("arbitrary",)))(xbc, dt_raw, dt_bias, a_log, d_e, hin, dy)


MM = (1024, 1024, 2048)
MM_TN = (1024, 2048, 1024)


def _relu2_epilogue(acc, ex, outs):
    r = jnp.maximum(acc, 0.0)
    outs[0][...] = (r * r).astype(BF16)
    outs[1][...] = r.astype(BF16)


def _relu2_bwd_epilogue(acc, ex, outs):
    outs[0][...] = (acc * (2.0 * ex[0][...].astype(F32))).astype(BF16)


def _add_epilogue(acc, ex, outs):
    outs[0][...] = acc + ex[0][...]


def _tile_spec(tm, tn):
    return pl.BlockSpec((tm, tn), lambda i, j, k: (i, j))


def _sum8(acc):
    return jnp.sum(acc, axis=0)


def _row(v):
    return v.reshape(1, -1)


def _ffn_fwd(xin, norm_g, sc, sh, gate, w1, w2, tag):
    h = _norm_mod_fwd(xin, norm_g, sc, sh, f"norm_ffn_fwd_{tag}")
    f, r = _mm_nn(f"mm_ffn1_{tag}", h, w1, *MM, out_dtypes=(BF16, BF16), epilogue=_relu2_epilogue)
    xout, yf = _mm_resid(f"mm_ffn2_{tag}", f, w2, xin, gate, *MM)
    return xout, (h, f, r, yf)


def _ffn_bwd(dxo, xin, saved, norm_g, sc, gate, w1, w2, tag):
    h, f, r, yf = saved
    tm, tn, tk = MM
    dyf, dgate = _gate_bwd(dxo, yf, gate, f"gate_bwd_ffn_{tag}")
    da = _mm_nt(f"mm_ffn2_bwd_{tag}", dyf, w2, *MM, extras=(r,), extra_specs=(_tile_spec(tm, tn),),
                epilogue=_relu2_bwd_epilogue)[0]
    dw2 = _mm_tn(f"mm_ffn2_dw_{tag}", f, dyf, *MM_TN)
    dh = _mm_nt(f"mm_ffn1_bwd_{tag}", da, w1, *MM, out_dtypes=(F32,))[0]
    dw1 = _mm_tn(f"mm_ffn1_dw_{tag}", h, da, *MM_TN)
    dxin, dsh, dsc, dng = _norm_mod_bwd(xin, dh, dxo, norm_g, sc, f"norm_ffn_bwd_{tag}")
    return dxin, dw1, dw2, (_sum8(dsh), _sum8(dsc), _sum8(dgate), _sum8(dng))


def _local_step(x, target, mod, w, p):
    s, d = x.shape
    tm, tn, tk = MM
    mods = [[_row(mod[l, j * d:(j + 1) * d]) for j in range(6)] for l in range(2)]
    g = {}

    sh1, sc1, g1, sh2, sc2, g2 = mods[0]
    nm0, nf0 = _row(p["norm_mix"][0]), _row(p["norm_ffn"][0])
    h0 = _norm_mod_fwd(x, nm0, sc1, sh1, "norm_mix_fwd_0")
    slabs = _mm_core(
        "mm_even_in", h0, w["even_in"], dims=NN, grid=(s // tm, N_SLABS, 1),
        a_spec=pl.BlockSpec((tm, d), lambda i, j, k: (i, 0)), b_spec=pl.BlockSpec((d, ATTN_W), lambda i, j, k: (0, j)),
        out_shapes=[jax.ShapeDtypeStruct((N_SLABS, s, ATTN_W), BF16)],
        out_specs=[pl.BlockSpec((None, tm, ATTN_W), lambda i, j, k: (j, i, 0))], acc_shape=None)[0]
    outs, lses = zip(*[_attn_fwd(slabs, gi, f"attn_fwd_{gi}") for gi in range(3)])
    attn, lse_tot = _attn_merge(outs, lses, "attn_merge")
    pool_scale = _row(p["pool_scale"])
    ap, pool_d = _pool_fwd(slabs, attn, p["pool_w"], pool_scale, "pool_fwd")
    x1, y0 = _mm_resid("mm_even_out", ap, w["even_out"], x, g1, *MM)
    x2, ffn0 = _ffn_fwd(x1, nf0, sc2, sh2, g2, w["w1"][0], w["w2"][0], "0")

    sh1b, sc1b, g1b, sh2b, sc2b, g2b = mods[1]
    nm1, nf1 = _row(p["norm_mix"][1]), _row(p["norm_ffn"][1])
    h1 = _norm_mod_fwd(x2, nm1, sc1b, sh1b, "norm_mix_fwd_1")
    z = _mm_nn("mm_ssm_z", h1, w["ssm_z"], *MM)[0]
    xbc_raw = _mm_nn("mm_ssm_xbc", h1, w["ssm_xbc"], *MM)[0]
    dt_raw = _mm_nn("mm_ssm_dt", h1, w["ssm_dt"], tm, LANES, tk, out_dtypes=(F32,))[0]
    xbc = _conv_fwd(xbc_raw, p["conv_w"], _row(p["conv_b"]), "conv_fwd")
    y_ssd, hin = _ssd_fwd(xbc, dt_raw, p["dt_bias"], p["a_log"], p["d_e"], "ssd_fwd")
    ssm_norm = _row(p["ssm_norm"])
    yn = _gated_norm_fwd(y_ssd, z, ssm_norm, "gated_norm_fwd")
    x3, y1 = _mm_resid("mm_ssm_out", yn, w["ssm_out"], x2, g1b, *MM)
    x4, ffn1 = _ffn_fwd(x3, nf1, sc2b, sh2b, g2b, w["w1"][1], w["w2"][1], "1")

    dx4, sq, dfn = _final_loss(x4, _row(p["final_norm"]), target, "final_loss")
    loss_share = (0.5 / d) * jnp.sum(sq)
    g["final_norm"] = _sum8(dfn)

    dx3, dw1_1, dw2_1, (dsh2b, dsc2b, dg2b, dnf1) = _ffn_bwd(dx4, x3, ffn1, nf1, sc2b, g2b, w["w1"][1], w["w2"][1], "1")
    dy1, dg1b = _gate_bwd(dx3, y1, g1b, "gate_bwd_ssm")
    dyn = _mm_nt("mm_ssm_out_bwd", dy1, w["ssm_out"], *MM, out_dtypes=(F32,))[0]
    g["ssm_out"] = _mm_tn("mm_ssm_out_dw", yn, dy1, *MM_TN)
    dy_ssd, dz, dgn = _gated_norm_bwd(y_ssd, z, dyn, ssm_norm, "gated_norm_bwd")
    g["ssm_norm"] = _sum8(dgn)
    dact, ddt, da_acc, db_acc, dd_acc = _ssd_bwd(xbc, dt_raw, p["dt_bias"], p["a_log"], p["d_e"], hin, dy_ssd, "ssd_bwd")
    g["dt_bias"] = _sum8(db_acc)[:SSM_HEADS]
    g["a_log"] = _sum8(da_acc)[:SSM_HEADS] * (-jnp.exp(p["a_log"][0, :SSM_HEADS]))
    g["ssm_d"] = jnp.sum(_sum8(dd_acc).reshape(SSM_HEADS, HEAD_DIM), axis=1)
    dxbc, dcw, dcb = _conv_bwd(xbc_raw, dact, p["conv_w"], _row(p["conv_b"]), "conv_bwd")
    g["conv_w"] = jnp.sum(dcw, axis=1)
    g["conv_b"] = _sum8(dcb)
    add_spec = (_tile_spec(tm, tn),)
    dh1 = _mm_nt("mm_ssm_z_bwd", dz, w["ssm_z"], *MM, out_dtypes=(F32,))[0]
    dh1 = _mm_nt("mm_ssm_xbc_bwd", dxbc, w["ssm_xbc"], *MM, out_dtypes=(F32,), extras=(dh1,), extra_specs=add_spec,
                 epilogue=_add_epilogue)[0]
    dh1 = _mm_nt("mm_ssm_dt_bwd", ddt, w["ssm_dt"], *MM, out_dtypes=(F32,), extras=(dh1,), extra_specs=add_spec,
                 epilogue=_add_epilogue)[0]
    g["ssm_in"] = jnp.concatenate(
        [_mm_tn("mm_ssm_z_dw", h1, dz, *MM_TN), _mm_tn("mm_ssm_xbc_dw", h1, dxbc, *MM_TN),
         _mm_tn("mm_ssm_dt_dw", h1, ddt, MM_TN[0], LANES, MM_TN[2])[:, :SSM_HEADS]], axis=1)
    dx2, dsh1b, dsc1b, dnm1 = _norm_mod_bwd(x2, dh1, dx3, nm1, sc1b, "norm_mix_bwd_1")
    dmod1 = jnp.concatenate([_sum8(dsh1b), _sum8(dsc1b), _sum8(dg1b), dsh2b, dsc2b, dg2b])

    dx1, dw1_0, dw2_0, (dsh2, dsc2, dg2, dnf0) = _ffn_bwd(dx2, x1, ffn0, nf0, sc2, g2, w["w1"][0], w["w2"][0], "0")
    dy0, dg1 = _gate_bwd(dx1, y0, g1, "gate_bwd_even")
    dap = _mm_nt("mm_even_out_bwd", dy0, w["even_out"], *MM)[0]
    g["even_out"] = _mm_tn("mm_even_out_dw", ap, dy0, *MM_TN)
    du, dpw, dpsc = _pool_bwd(dap, pool_d, p["pool_w"], pool_scale, "pool_bwd")
    g["pool_w"] = dpw
    g["pool_scale"] = _sum8(dpsc)
    dqkv = [_attn_bwd(slabs, dap, ap, lse_tot, gi, f"attn_bwd_{gi}") for gi in range(3)]
    dproj = jnp.concatenate([dqkv[gi][t] for t in range(3) for gi in range(3)] + [du], axis=1)
    dh0 = _mm_nt("mm_even_in_bwd", dproj, w["even_in"], tm, tn, N_SLABS * ATTN_W // 2, out_dtypes=(F32,))[0]
    g["even_in_t"] = _mm_tn("mm_even_in_dw", dproj, h0, *MM_TN)
    gx, dsh1, dsc1, dnm0 = _norm_mod_bwd(x, dh0, dx1, nm0, sc1, "norm_mix_bwd_0")
    dmod0 = jnp.concatenate([_sum8(dsh1), _sum8(dsc1), _sum8(dg1), dsh2, dsc2, dg2])

    g["w1"] = (dw1_0, dw1_1)
    g["w2"] = (dw2_0, dw2_1)
    g["norm_mix"] = jnp.stack([_sum8(dnm0), _sum8(dnm1)])
    g["norm_ffn"] = jnp.stack([dnf0, dnf1])
    return loss_share, gx, g, jnp.stack([dmod0, dmod1])


SMALL_COLS = 512
SMALL_ORDER = ("dmod", "norm_mix", "norm_ffn", "pool_w", "pool_scale", "conv_w", "conv_b", "dt_bias", "a_log",
               "ssm_d", "ssm_norm", "final_norm")


def _cols_to_full(gathered):
    n, k, ns = gathered.shape
    return jnp.transpose(gathered, (1, 0, 2)).reshape(k, n * ns)


def _full_to_cols(full):
    k, n4 = full.shape
    return jnp.transpose(full.reshape(k, N_CHIPS, n4 // N_CHIPS), (1, 0, 2))


def _pack(parts, cols):
    flat = jnp.concatenate([v.reshape(-1) for v in parts])
    rows = -(-flat.shape[0] // (cols * 8)) * 8
    return jnp.pad(flat, (0, rows * cols - flat.shape[0])).reshape(rows, cols)


def _unpack(packed, shapes):
    flat, out, at = packed.reshape(-1), [], 0
    for shp in shapes:
        n = math.prod(shp)
        out.append(flat[at:at + n].reshape(shp))
        at += n
    return out


def kernel(x, c, ada_w, ada_b, norm_mix, norm_ffn, ffn_w1, ffn_w2, even_w_in, pool_w, pool_scale, even_w_out, ssm_w_in, ssm_conv_w, ssm_conv_b, ssm_dt_bias, ssm_a_log, ssm_d, ssm_norm, ssm_w_out, final_norm, loss_target, m_ada_w, m_ada_b, m_norm_mix, m_norm_ffn, m_ffn_w1, m_ffn_w2, m_even_w_in, m_pool_w, m_pool_scale, m_even_w_out, m_ssm_w_in, m_ssm_conv_w, m_ssm_conv_b, m_ssm_dt_bias, m_ssm_a_log, m_ssm_d, m_ssm_norm, m_ssm_w_out, m_final_norm, v_ada_w, v_ada_b, v_norm_mix, v_norm_ffn, v_ffn_w1, v_ffn_w2, v_even_w_in, v_pool_w, v_pool_scale, v_even_w_out, v_ssm_w_in, v_ssm_conv_w, v_ssm_conv_b, v_ssm_dt_bias, v_ssm_a_log, v_ssm_d, v_ssm_norm, v_ssm_w_out, v_final_norm):
    names = ("ada_w", "ada_b", "norm_mix", "norm_ffn", "ffn_w1", "ffn_w2", "even_w_in", "pool_w", "pool_scale",
             "even_w_out", "ssm_w_in", "ssm_conv_w", "ssm_conv_b", "ssm_dt_bias", "ssm_a_log", "ssm_d", "ssm_norm",
             "ssm_w_out", "final_norm")
    wts = dict(zip(names, (ada_w, ada_b, norm_mix, norm_ffn, ffn_w1, ffn_w2, even_w_in, pool_w, pool_scale, even_w_out,
                           ssm_w_in, ssm_conv_w, ssm_conv_b, ssm_dt_bias, ssm_a_log, ssm_d, ssm_norm, ssm_w_out, final_norm)))
    m_in = dict(zip(names, (m_ada_w, m_ada_b, m_norm_mix, m_norm_ffn, m_ffn_w1, m_ffn_w2, m_even_w_in, m_pool_w, m_pool_scale,
                            m_even_w_out, m_ssm_w_in, m_ssm_conv_w, m_ssm_conv_b, m_ssm_dt_bias, m_ssm_a_log, m_ssm_d,
                            m_ssm_norm, m_ssm_w_out, m_final_norm)))
    v_in = dict(zip(names, (v_ada_w, v_ada_b, v_norm_mix, v_norm_ffn, v_ffn_w1, v_ffn_w2, v_even_w_in, v_pool_w, v_pool_scale,
                            v_even_w_out, v_ssm_w_in, v_ssm_conv_w, v_ssm_conv_b, v_ssm_dt_bias, v_ssm_a_log, v_ssm_d,
                            v_ssm_norm, v_ssm_w_out, v_final_norm)))
    d = D_MODEL
    s = x.shape[1]
    ix, iy, ic = _place()
    chip = 2 * ix + iy
    example = 4 * ix + 2 * iy + ic

    c_all = _allgather8(c.reshape(8, d // 8), "gather_c").reshape(N_DEV, d)
    cond = c_all * jax.nn.sigmoid(c_all)
    cond16 = jnp.pad(cond, ((0, 8), (0, 0)))
    ada_cols = ada_w.shape[2]
    bias_shard = lax.dynamic_slice_in_dim(ada_b, chip * ada_cols, ada_cols, axis=1)
    mod_parts = [
        _mm_nn(f"mm_ada_{l}", cond16, ada_w[l], 16, 512, d, out_dtypes=(F32,), extras=(_row(bias_shard[l]),),
               extra_specs=(pl.BlockSpec((1, 512), lambda i, j, k: (0, j)),), epilogue=_add_epilogue)[0][:8]
        for l in range(2)]
    mod_all = _allgather8(jnp.concatenate(mod_parts, axis=0), "gather_mod").reshape(N_CHIPS, 2, 2, 8, ada_cols)
    mod_mine = lax.dynamic_index_in_dim(mod_all[:, 0], example, axis=2, keepdims=False)
    mod = jnp.transpose(mod_mine, (1, 0, 2)).reshape(2, N_CHIPS * ada_cols)

    def gathered(wt, tag):
        lead = wt.shape[:-2]
        flat = wt.astype(BF16).reshape(-1, wt.shape[-1])
        return _chip_allgather(flat, f"gather_{tag}").reshape(N_CHIPS, *lead, *wt.shape[-2:])

    gw1 = gathered(ffn_w1, "ffn_w1")
    gw2 = gathered(ffn_w2, "ffn_w2")
    ssm_in_full = _cols_to_full(gathered(ssm_w_in, "ssm_in")[:, 0])
    n_xbc = SSM_INNER + SSM_CONV_DIM
    full = {
        "w1": [_cols_to_full(gw1[:, l]) for l in range(2)],
        "w2": [gw2[:, l].reshape(FFN_HIDDEN, d) for l in range(2)],
        "even_in": _cols_to_full(gathered(even_w_in, "even_in")[:, 0]),
        "even_out": _cols_to_full(gathered(even_w_out, "even_out")[:, 0]),
        "ssm_z": ssm_in_full[:, :SSM_INNER],
        "ssm_xbc": ssm_in_full[:, SSM_INNER:n_xbc],
        "ssm_dt": jnp.pad(ssm_in_full[:, n_xbc:], ((0, 0), (0, LANES - SSM_HEADS))),
        "ssm_out": gathered(ssm_w_out, "ssm_out")[:, 0].reshape(SSM_INNER, d),
    }
    pad_h = ((0, 0), (0, LANES - SSM_HEADS))
    cw, nw = ssm_conv_w.shape[2], ssm_norm.shape[1]
    sm = jnp.concatenate([ssm_conv_w[0].reshape(-1), ssm_conv_b.reshape(-1), jnp.pad(ssm_norm[0], (0, cw - nw)),
                          jnp.zeros((2 * cw,), F32)]).reshape(8, cw)
    sm_all = _allgather8(sm, "gather_ssm_small").reshape(N_CHIPS, 2, 8, cw)[:, 0]
    small = {
        "norm_mix": norm_mix, "norm_ffn": norm_ffn, "pool_w": pool_w[0], "pool_scale": pool_scale[0],
        "final_norm": final_norm,
        "conv_w": jnp.transpose(sm_all[:, :CONV_TAPS], (1, 0, 2)).reshape(CONV_TAPS, N_CHIPS * cw),
        "conv_b": sm_all[:, CONV_TAPS].reshape(N_CHIPS * cw),
        "ssm_norm": sm_all[:, CONV_TAPS + 1, :nw].reshape(N_CHIPS * nw),
        "dt_bias": jnp.pad(ssm_dt_bias, pad_h), "a_log": jnp.pad(ssm_a_log, pad_h),
        "d_e": jnp.repeat(ssm_d[0], HEAD_DIM).reshape(1, SSM_INNER),
    }

    loss_share, gx, g, dmod = _local_step(x[0], loss_target[0], mod, full, small)
    loss = lax.psum(loss_share, ("x", "y", "c"))

    g["dmod"] = dmod
    small_shapes = [g[k].shape for k in SMALL_ORDER]
    packed = _pack([g[k] for k in SMALL_ORDER], SMALL_COLS)
    every = _allgather8(packed, "gather_small").reshape(N_DEV, *packed.shape)
    summed = dict(zip(SMALL_ORDER, _unpack(_sum_devices(every, "sum_small"), small_shapes)))
    dmod_all = every.reshape(N_DEV, -1)[:, :2 * 6 * d].reshape(N_DEV, 2, 6 * d)

    grads = {}
    dmod_shard = lax.dynamic_slice_in_dim(dmod_all, chip * ada_cols, ada_cols, axis=2)
    grads["ada_w"] = jnp.stack([
        _mm_tn(f"mm_ada_dw_{l}", cond16, jnp.pad(dmod_shard[:, l], ((0, 8), (0, 0))), 1024, 512, 16, out_dtype=F32)
        for l in range(2)])
    grads["ada_b"] = summed["dmod"]
    grads["norm_mix"] = summed["norm_mix"]
    grads["norm_ffn"] = summed["norm_ffn"]
    grads["pool_w"] = summed["pool_w"][None]
    grads["pool_scale"] = summed["pool_scale"][None]
    grads["ssm_conv_w"] =lax.dynamic_slice_in_dim(summed["conv_w"], chip * cw, cw, axis=1)[None]
    grads["ssm_conv_b"] = lax.dynamic_slice_in_dim(summed["conv_b"], chip * cw, cw, axis=0)[None]
    grads["ssm_dt_bias"] = summed["dt_bias"][None]
    grads["ssm_a_log"] = summed["a_log"][None]
    grads["ssm_d"] = summed["ssm_d"][None]
    grads["ssm_norm"] =lax.dynamic_slice_in_dim(summed["ssm_norm"], chip * nw, nw, axis=0)[None]
    grads["final_norm"] = summed["final_norm"]

    def scattered(slots, tag, shape):
        n = slots.shape[0]
        return _reduce_scatter(slots.reshape(n, -1, slots.shape[-1]), tag).reshape(shape)

    grads["ffn_w1"] = scattered(jnp.stack([_full_to_cols(t) for t in g["w1"]], axis=1), "ffn_w1", ffn_w1.shape)
    grads["ffn_w2"] = scattered(jnp.stack([t.reshape(N_CHIPS, -1, d) for t in g["w2"]], axis=1), "ffn_w2", ffn_w2.shape)
    even_in_slots = jnp.transpose(g["even_in_t"].reshape(N_CHIPS, -1, d), (0, 2, 1))
    grads["even_w_in"] = scattered(even_in_slots, "even_in", even_w_in.shape)
    grads["even_w_out"] = scattered(_full_to_cols(g["even_out"]), "even_out", even_w_out.shape)
    grads["ssm_w_in"] = scattered(_full_to_cols(g["ssm_in"]), "ssm_in", ssm_w_in.shape)
    grads["ssm_w_out"] = scattered(g["ssm_out"].reshape(N_CHIPS, -1, d), "ssm_out", ssm_w_out.shape)

    big = ("ada_w", "ffn_w1", "ffn_w2", "even_w_in", "even_w_out", "ssm_w_in", "ssm_w_out")
    delta, new_m, new_v = {}, {}, {}
    for k in big:
        shp = wts[k].shape
        two_d = lambda t: t.reshape(-1, shp[-1])
        res = _adamw(two_d(wts[k]), two_d(grads[k]), two_d(m_in[k]), two_d(v_in[k]), f"adamw_{k}")
        delta[k], new_m[k], new_v[k] = [t.reshape(shp) for t in res]
    little = [k for k in names if k not in big]
    shapes = [wts[k].shape for k in little]
    res = _adamw(*[_pack([src[k] for k in little], LANES) for src in (wts, grads, m_in, v_in)], "adamw_small")
    for out, packed_out in zip((delta, new_m, new_v), res):
        out.update(zip(little, _unpack(packed_out, shapes)))

    return (loss, gx[None], *[grads[k] for k in names], *[delta[k] for k in names],
            *[new_m[k] for k in names], *[new_v[k] for k in names])
```

```python
import functools
import math

import jax
import jax.numpy as jnp
from jax import lax
from jax.experimental import pallas as pl
from jax.experimental.pallas import tpu as pltpu

F32 = jnp.float32
BF16 = jnp.bfloat16
MESH = pl.DeviceIdType.MESH
ANY = pl.BlockSpec(memory_space=pl.ANY)
VMEM_FULL = pl.BlockSpec(memory_space=pltpu.VMEM)

NORM_EPS = 1e-6
N_CHIPS = 4
N_DEV = 8
LANES = 128
VMEM_LIMIT = 56 << 20

D_MODEL = 2048
ATTN_GROUPS = ((128, 1), (512, 4), (2048, 16))
ATTN_BLOCK = 128
ATTN_W = 512
HEAD_DIM = 64
POOL_WINDOWS = (2, 4, 8, 16)
POOL_W = 512
N_SLABS = 10
SSM_INNER = 4096
SSM_HEADS = 64
SSM_GROUPS = 8
SSM_STATE = 128
SSM_CHUNK = 128
SSM_CONV_DIM = 6144
SSM_GW = SSM_INNER // SSM_GROUPS
FFN_HIDDEN = 8192

ADAM_LR, ADAM_B1, ADAM_B2, ADAM_EPS, ADAM_WD, ADAM_STEP = 0.001, 0.9, 0.999, 1e-08, 0.01, 10


def _cparams(sem=None):
    return pltpu.CompilerParams(dimension_semantics=sem, vmem_limit_bytes=VMEM_LIMIT)


def _place():
    return lax.axis_index("x"), lax.axis_index("y"), lax.axis_index("c")


def _chip_index():
    return 2 * lax.axis_index("x") + lax.axis_index("y")


def _allgather8(v, name):
    m_per, n = v.shape

    def body(x_ref, out_ref, send_sems, recv_sems, local_sem):
        x, y, c = _place()
        me, sibling = (x, y, c), (x, y, 1 - c)
        chips = [(1 - x, y), (x, 1 - y), (1 - x, 1 - y)]

        def rows(px, py, pc):
            return out_ref.at[pl.ds((4 * px + 2 * py + pc) * m_per, m_per), :]

        def copy(k, block, to, src=None):
            return pltpu.make_async_remote_copy(
                src_ref=rows(*block) if src is None else src, dst_ref=rows(*block),
                send_sem=send_sems.at[k], recv_sem=recv_sems.at[k], device_id=to, device_id_type=MESH)

        mine = pltpu.make_async_copy(x_ref, rows(*me), local_sem)
        mine.start()
        first = [copy(0, me, sibling, src=x_ref)]
        first += [copy(1 + j, me, (*chip, c), src=x_ref) for j, chip in enumerate(chips)]
        for cp in first:
            cp.start()
        passed = [copy(4 + j, (*chip, c), sibling) for j, chip in enumerate(chips)]
        for j, chip in enumerate(chips):
            copy(1 + j, (*chip, c), me).wait_recv()
            passed[j].start()
        copy(0, sibling, me).wait_recv()
        for j, chip in enumerate(chips):
            copy(4 + j, (*chip, 1 - c), me).wait_recv()
        for cp in first + passed:
            cp.wait_send()
        mine.wait()

    return pl.pallas_call(
        body, name=name,
        out_shape=jax.ShapeDtypeStruct((N_DEV * m_per, n), v.dtype),
        in_specs=[VMEM_FULL], out_specs=VMEM_FULL,
        scratch_shapes=[pltpu.SemaphoreType.DMA((7,)), pltpu.SemaphoreType.DMA((7,)), pltpu.SemaphoreType.DMA],
    )(v)


def _chip_allgather(w, name):
    r, ccols = w.shape
    h = r // 2

    def body(w_ref, out_ref, send_sems, recv_sems):
        x, y, c = _place()
        sibling = (x, y, 1 - c)
        chips = [(1 - x, y), (x, 1 - y), (1 - x, 1 - y)]

        def half(chip, cc):
            return out_ref.at[2 * chip[0] + chip[1], pl.ds(cc * h, h), :]

        def copy(k, dst, to, src):
            return pltpu.make_async_remote_copy(src_ref=src, dst_ref=dst, send_sem=send_sems.at[k],
                                                recv_sem=recv_sems.at[k], device_id=to, device_id_type=MESH)

        my_half = w_ref.at[pl.ds(c * h, h), :]
        first = [copy(j, half((x, y), c), (*chip, c), my_half) for j, chip in enumerate(chips)]
        for cp in first:
            cp.start()
        passed = [copy(3 + j, half(chip, c), sibling, half(chip, c)) for j, chip in enumerate(chips)]
        for j, chip in enumerate(chips):
            copy(j, half(chip, c), sibling, my_half).wait_recv()
            passed[j].start()
        for j, chip in enumerate(chips):
            copy(3 + j, half(chip, 1 - c), sibling, my_half).wait_recv()
        for cp in first + passed:
            cp.wait_send()

    out = pl.pallas_call(
        body, name=name,
        out_shape=jax.ShapeDtypeStruct((N_CHIPS, r, ccols), w.dtype),
        in_specs=[ANY], out_specs=ANY,
        scratch_shapes=[pltpu.SemaphoreType.DMA((6,)), pltpu.SemaphoreType.DMA((6,))],
    )(w)
    return lax.dynamic_update_index_in_dim(out, w, _chip_index(), 0)


def _sibling_send_other_half(g, name):
    n, r, ccols = g.shape
    h = r // 2

    def body(g_ref, out_ref, send_sem, recv_sem):
        x, y, c = _place()
        cp = pltpu.make_async_remote_copy(
            src_ref=g_ref.at[:, pl.ds((1 - c) * h, h), :], dst_ref=out_ref, send_sem=send_sem, recv_sem=recv_sem,
            device_id=(x, y, 1 - c), device_id_type=MESH)
        cp.start()
        cp.wait()

    return pl.pallas_call(
        body, name=name, out_shape=jax.ShapeDtypeStruct((n, h, ccols), g.dtype),
        in_specs=[ANY], out_specs=ANY,
        scratch_shapes=[pltpu.SemaphoreType.DMA, pltpu.SemaphoreType.DMA],
    )(g)


def _chip_alltoall(p, name):
    n, h, ccols = p.shape

    def body(p_ref, out_ref, send_sems, recv_sems):
        x, y, c = _place()
        k_me = 2 * x + y
        chips = [(1 - x, y), (x, 1 - y), (1 - x, 1 - y)]
        cps = []
        for j, chip in enumerate(chips):
            cps.append(pltpu.make_async_remote_copy(
                src_ref=p_ref.at[2 * chip[0] + chip[1]], dst_ref=out_ref.at[k_me],
                send_sem=send_sems.at[j], recv_sem=recv_sems.at[j], device_id=(*chip, c), device_id_type=MESH))
        for cp in cps:
            cp.start()
        for j, chip in enumerate(chips):
            pltpu.make_async_remote_copy(
                src_ref=p_ref.at[k_me], dst_ref=out_ref.at[2 * chip[0] + chip[1]],
                send_sem=send_sems.at[j], recv_sem=recv_sems.at[j], device_id=(*chip, c), device_id_type=MESH).wait_recv()
        for cp in cps:
            cp.wait_send()

    out = pl.pallas_call(
        body, name=name, out_shape=jax.ShapeDtypeStruct((n, h, ccols), p.dtype),
        in_specs=[ANY], out_specs=ANY,
        scratch_shapes=[pltpu.SemaphoreType.DMA((3,)), pltpu.SemaphoreType.DMA((3,))],
    )(p)
    k_me = _chip_index()
    return lax.dynamic_update_index_in_dim(out, lax.dynamic_index_in_dim(p, k_me, 0, keepdims=False), k_me, 0)


def _sibling_complete(f, name):
    h, ccols = f.shape

    def body(f_ref, out_ref, send_sem, recv_sem):
        x, y, c = _place()
        cp = pltpu.make_async_remote_copy(src_ref=f_ref, dst_ref=out_ref, send_sem=send_sem, recv_sem=recv_sem,
                                          device_id=(x, y, 1 - c), device_id_type=MESH)
        cp.start()
        cp.wait()

    other = pl.pallas_call(
        body, name=name, out_shape=jax.ShapeDtypeStruct((h, ccols), f.dtype),
        in_specs=[ANY], out_specs=ANY,
        scratch_shapes=[pltpu.SemaphoreType.DMA, pltpu.SemaphoreType.DMA],
    )(f)
    south = lax.axis_index("c") == 0
    return jnp.concatenate([jnp.where(south, f, other), jnp.where(south, other, f)], axis=0)


HBM_SPEC = pl.BlockSpec(memory_space=pltpu.HBM)
SEM_SPEC = pl.BlockSpec(memory_space=pltpu.SEMAPHORE)
N_PEERS = 3


def _split_params():
    return pltpu.CompilerParams(has_side_effects=pltpu.SideEffectType.DATAFLOW_SIDE_EFFECTING)


def _chip_exchange(mode, src_ref, land_ref, send_sems, recv_sems):
    x, y, c = _place()
    k_me = 2 * x + y
    pairs = []
    for j, chip in enumerate([(1 - x, y), (x, 1 - y), (1 - x, 1 - y)]):
        k_j = 2 * chip[0] + chip[1]
        if mode == "halves":
            h = src_ref.shape[0] // 2
            rows = pl.ds(c * h, h)
            src, there, here = src_ref.at[rows, :], land_ref.at[k_me, rows, :], land_ref.at[k_j, rows, :]
        else:
            src, there, here = src_ref.at[k_j], land_ref.at[k_me], land_ref.at[k_j]
        pairs.append(tuple(
            pltpu.make_async_remote_copy(src_ref=src, dst_ref=dst, send_sem=send_sems[j], recv_sem=recv_sems[j],
                                         device_id=(*chip, c), device_id_type=MESH) for dst in (there, here)))
    return pairs


def _exchange_start(src, land_shape, mode, name):
    def body(src_ref, land_ref, s0, s1, s2, r0, r1, r2, src_thru, land_thru, token):
        for start, _ in _chip_exchange(mode, src_ref, land_ref, (s0, s1, s2), (r0, r1, r2)):
            start.start()
        token[...] = jnp.zeros_like(token)

    sem = pltpu.SemaphoreType.DMA(())
    res = pl.pallas_call(
        body, name=name,
        out_shape=(sem,) * (2 * N_PEERS) + (pltpu.HBM(src.shape, src.dtype), pltpu.HBM(land_shape, src.dtype),
                                           jax.ShapeDtypeStruct((8, LANES), F32)),
        in_specs=(HBM_SPEC, HBM_SPEC), out_specs=(SEM_SPEC,) * (2 * N_PEERS) + (HBM_SPEC, HBM_SPEC, VMEM_FULL),
        input_output_aliases={0: 2 * N_PEERS, 1: 2 * N_PEERS + 1}, compiler_params=_split_params(),
    )(pltpu.with_memory_space_constraint(src, pltpu.HBM),
      pltpu.with_memory_space_constraint(lax.empty(land_shape, src.dtype), pltpu.HBM))
    return res[:2 * N_PEERS], res[2 * N_PEERS], res[2 * N_PEERS + 1], res[2 * N_PEERS + 2]


def _exchange_wait(sems, src_thru, land_thru, after, mode, name):
    def body(src_ref, land_ref, s0, s1, s2, r0, r1, r2, after_ref, src_out, land_out):
        waits = [w for _, w in _chip_exchange(mode, src_ref, land_ref, (s0, s1, s2), (r0, r1, r2))]
        for w in waits:
            w.wait_send()
        for w in waits:
            w.wait_recv()

    return pl.pallas_call(
        body, name=name,
        out_shape=(pltpu.HBM(src_thru.shape, src_thru.dtype), pltpu.HBM(land_thru.shape, land_thru.dtype)),
        in_specs=(HBM_SPEC, HBM_SPEC) + (SEM_SPEC,) * (2 * N_PEERS) + (ANY,), out_specs=(HBM_SPEC, HBM_SPEC),
        input_output_aliases={0: 0, 1: 1}, compiler_params=_split_params(),
    )(src_thru, land_thru, *sems, after)


def _sibling_fill(land, name):
    n, r, ccols = land.shape
    h = r // 2

    def body(land_ref, out_ref, send_sems, recv_sems):
        x, y, c = _place()
        slots = [2 * (1 - x) + y, 2 * x + (1 - y), 2 * (1 - x) + (1 - y)]

        def copy(j, rows):
            return pltpu.make_async_remote_copy(
                src_ref=land_ref.at[slots[j], pl.ds(c * h, h), :], dst_ref=out_ref.at[slots[j], rows, :],
                send_sem=send_sems.at[j], recv_sem=recv_sems.at[j], device_id=(x, y, 1 - c), device_id_type=MESH)

        sends = [copy(j, pl.ds(c * h, h)) for j in range(N_PEERS)]
        for cp in sends:
            cp.start()
        for j in range(N_PEERS):
            copy(j, pl.ds((1 - c) * h, h)).wait_recv()
        for cp in sends:
            cp.wait_send()

    return pl.pallas_call(
        body, name=name, out_shape=jax.ShapeDtypeStruct(land.shape, land.dtype), in_specs=[ANY], out_specs=ANY,
        input_output_aliases={0: 0},
        scratch_shapes=[pltpu.SemaphoreType.DMA((N_PEERS,)), pltpu.SemaphoreType.DMA((N_PEERS,))],
    )(land)


class _GatheredWeights:
    def __init__(self, shards):
        self._pending, self._done = {}, {}
        token = jnp.zeros((), BF16)
        for name, shard in shards.items():
            flat = shard.reshape(-1, shard.shape[-1]).astype(BF16) + token
            sems, thru, land, tok = _exchange_start(flat, (N_CHIPS, *flat.shape), "halves", f"gather_start_{name}")
            self._pending[name] = (sems, thru, land, shard.shape)
            token = tok[0, 0].astype(BF16)
        self.token = tok[0, 0]

    def take(self, name, after):
        if name not in self._done:
            sems, thru, land, shape = self._pending.pop(name)
            flat, land = _exchange_wait(sems, thru, land, after, "halves", f"gather_wait_{name}")
            land = _sibling_fill(land, f"gather_fill_{name}")
            land = lax.dynamic_update_index_in_dim(land, flat, _chip_index(), 0)
            self._done[name] = land.reshape(N_CHIPS, *shape)
        return self._done[name]


class _GradientExchange:
    def __init__(self):
        self._pending = []

    def put(self, name, slots):
        recv = _sibling_send_other_half(slots, f"rs_pair_{name}")
        part = _add_own_half(slots, recv, f"rs_add2_{name}")
        sems, thru, land, tok = _exchange_start(part, part.shape, "slots", f"rs_start_{name}")
        self._pending.append((name, sems, thru, land))
        return tok[0, 0]

    def finish(self, after):
        out = {}
        k_me = _chip_index()
        for name, sems, thru, land in self._pending:
            part, land = _exchange_wait(sems, thru, land, after, "slots", f"rs_wait_{name}")
            got = lax.dynamic_update_index_in_dim(land, lax.dynamic_index_in_dim(part, k_me, 0, keepdims=False), k_me, 0)
            out[name] = _sibling_complete(_sum_slots(got, f"rs_add4_{name}"), f"rs_fin_{name}")
        return out


def _row_tile(rows, cols, itemsize, budget=2 << 20):
    t = rows
    while t % 2 == 0 and t * cols * itemsize > budget and (t // 2) % 16 == 0:
        t //= 2
    return t


def _add_own_half(g, recv, name):
    n, r, ccols = g.shape
    h = r // 2
    t = _row_tile(h, ccols, 4)
    nt = h // t
    c_idx = lax.axis_index("c").astype(jnp.int32).reshape(1)

    def body(c_ref, g_ref, r_ref, o_ref):
        o_ref[...] = (g_ref[...].astype(F32) + r_ref[...].astype(F32)).astype(o_ref.dtype)

    grid_spec = pltpu.PrefetchScalarGridSpec(
        num_scalar_prefetch=1, grid=(n, nt),
        in_specs=[pl.BlockSpec((None, t, ccols), lambda j, i, c_ref: (j, c_ref[0] * nt + i, 0)),
                  pl.BlockSpec((None, t, ccols), lambda j, i, c_ref: (j, i, 0))],
        out_specs=pl.BlockSpec((None, t, ccols), lambda j, i, c_ref: (j, i, 0)))
    return pl.pallas_call(body, name=name, grid_spec=grid_spec,
                          out_shape=jax.ShapeDtypeStruct((n, h, ccols), BF16),
                          compiler_params=_cparams(("parallel", "parallel")))(c_idx, g, recv)


def _sum_slots(q, name):
    n, h, ccols = q.shape
    t = _row_tile(h, ccols, 4)

    def body(q_ref, o_ref):
        acc = q_ref[0].astype(F32)
        for j in range(1, n):
            acc = acc + q_ref[j].astype(F32)
        o_ref[...] = acc

    return pl.pallas_call(
        body, name=name, grid=(h // t,),
        in_specs=[pl.BlockSpec((n, t, ccols), lambda i: (0, i, 0))],
        out_specs=pl.BlockSpec((t, ccols), lambda i: (i, 0)),
        out_shape=jax.ShapeDtypeStruct((h, ccols), F32), compiler_params=_cparams(("parallel",)))(q)


def _sum_devices(v, name):
    n, r, ccols = v.shape
    t = 8
    while r % (t * 2) == 0 and t * 2 * ccols * 4 * n <= (8 << 20):
        t *= 2

    def body(v_ref, o_ref):
        acc = v_ref[0]
        for j in range(1, n):
            acc = acc + v_ref[j]
        o_ref[...] = acc

    return pl.pallas_call(
        body, name=name, grid=(r // t,),
        in_specs=[pl.BlockSpec((n, t, ccols), lambda i: (0, i, 0))],
        out_specs=pl.BlockSpec((t, ccols), lambda i: (i, 0)),
        out_shape=jax.ShapeDtypeStruct((r, ccols), F32), compiler_params=_cparams(("parallel",)))(v)


def _adamw(w, g, m, v, name):
    r, ccols = w.shape
    t = _row_tile(r, ccols, 4, budget=1 << 20)
    c1 = 1.0 / (1.0 - ADAM_B1 ** ADAM_STEP)
    c2 = 1.0 / (1.0 - ADAM_B2 ** ADAM_STEP)

    def body(w_ref, g_ref, m_ref, v_ref, d_ref, nm_ref, nv_ref):
        gg = g_ref[...]
        nm = ADAM_B1 * m_ref[...] + (1.0 - ADAM_B1) * gg
        nv = ADAM_B2 * v_ref[...] + (1.0 - ADAM_B2) * (gg * gg)
        d_ref[...] = -ADAM_LR * ((nm * c1) / (jnp.sqrt(nv * c2) + ADAM_EPS) + ADAM_WD * w_ref[...])
        nm_ref[...] = nm
        nv_ref[...] = nv

    spec = pl.BlockSpec((t, ccols), lambda i: (i, 0))
    sds = jax.ShapeDtypeStruct((r, ccols), F32)
    return pl.pallas_call(body, name=name, grid=(r // t,), in_specs=[spec] * 4, out_specs=[spec] * 3,
                          out_shape=[sds] * 3, compiler_params=_cparams(("parallel",)))(w, g, m, v)


NN = (((1,), (0,)), ((), ()))
NT = (((1,), (1,)), ((), ()))
TN = (((0,), (0,)), ((), ()))


def _mm_core(name, a, b, *, dims, grid, a_spec, b_spec, out_shapes, out_specs, acc_shape,
             extras=(), extra_specs=(), epilogue=None):
    nk = grid[2]
    n_ex, n_out = len(extras), len(out_shapes)
    if epilogue is None:
        def epilogue(acc, ex, outs):
            outs[0][...] = acc.astype(outs[0].dtype)

    def body(*refs):
        a_ref, b_ref = refs[0], refs[1]
        ex = refs[2:2 + n_ex]
        outs = refs[2 + n_ex:2 + n_ex + n_out]
        part = lax.dot_general(a_ref[...].astype(BF16), b_ref[...].astype(BF16), dims, preferred_element_type=F32)
        if nk == 1:
            epilogue(part, ex, outs)
        else:
            acc = refs[-1]
            k = pl.program_id(2)

            @pl.when(k == 0)
            def _():
                acc[...] = part

            @pl.when(k > 0)
            def _():
                acc[...] += part

            @pl.when(k == nk - 1)
            def _():
                epilogue(acc[...], ex, outs)

    scratch = [] if nk == 1 else [pltpu.VMEM(acc_shape, F32)]
    res = pl.pallas_call(
        body, name=name, grid=grid, in_specs=[a_spec, b_spec, *extra_specs], out_specs=list(out_specs),
        out_shape=list(out_shapes), scratch_shapes=scratch,
        compiler_params=_cparams(("parallel", "parallel", "arbitrary")))(a, b, *extras)
    return res


def _mm_nn(name, a, b, tm, tn, tk, out_dtypes=(BF16,), extras=(), extra_specs=(), epilogue=None):
    m, kk = a.shape
    n = b.shape[1]
    tm, tn, tk = min(tm, m), min(tn, n), min(tk, kk)
    grid = (m // tm, n // tn, kk // tk)
    return _mm_core(
        name, a, b, dims=NN, grid=grid,
        a_spec=pl.BlockSpec((tm, tk), lambda i, j, k: (i, k)), b_spec=pl.BlockSpec((tk, tn), lambda i, j, k: (k, j)),
        out_shapes=[jax.ShapeDtypeStruct((m, n), dt) for dt in out_dtypes],
        out_specs=[pl.BlockSpec((tm, tn), lambda i, j, k: (i, j)) for _ in out_dtypes],
        acc_shape=(tm, tn), extras=extras, extra_specs=extra_specs, epilogue=epilogue)


def _mm_nt(name, a, b, tm, tn, tk, out_dtypes=(BF16,), extras=(), extra_specs=(), epilogue=None):
    m, kk = a.shape
    n = b.shape[0]
    tm, tn, tk = min(tm, m), min(tn, n), min(tk, kk)
    grid = (m // tm, n // tn, kk // tk)
    return _mm_core(
        name, a, b, dims=NT, grid=grid,
        a_spec=pl.BlockSpec((tm, tk), lambda i, j, k: (i, k)), b_spec=pl.BlockSpec((tn, tk), lambda i, j, k: (j, k)),
        out_shapes=[jax.ShapeDtypeStruct((m, n), dt) for dt in out_dtypes],
        out_specs=[pl.BlockSpec((tm, tn), lambda i, j, k: (i, j)) for _ in out_dtypes],
        acc_shape=(tm, tn), extras=extras, extra_specs=extra_specs, epilogue=epilogue)


def _mm_tn(name, a, b, tm, tn, tk, out_dtype=BF16):
    kk, m = a.shape
    n = b.shape[1]
    tm, tn, tk = min(tm, m), min(tn, n), min(tk, kk)
    grid = (m // tm, n // tn, kk // tk)
    return _mm_core(
        name, a, b, dims=TN, grid=grid,
        a_spec=pl.BlockSpec((tk, tm), lambda i, j, k: (k, i)), b_spec=pl.BlockSpec((tk, tn), lambda i, j, k: (k, j)),
        out_shapes=[jax.ShapeDtypeStruct((m, n), out_dtype)],
        out_specs=[pl.BlockSpec((tm, tn), lambda i, j, k: (i, j))], acc_shape=(tm, tn))[0]


def _resid_gate_epilogue(acc, ex, outs):
    outs[0][...] = ex[0][...] + ex[1][...] * acc
    outs[1][...] = acc.astype(BF16)


def _mm_resid(name, a, b, resid, gate, tm, tn, tk):
    return _mm_nn(
        name, a, b, tm, tn, tk, out_dtypes=(F32, BF16), extras=(resid, gate),
        extra_specs=(pl.BlockSpec((tm, tn), lambda i, j, k: (i, j)), pl.BlockSpec((1, tn), lambda i, j, k: (0, j))),
        epilogue=_resid_gate_epilogue)


TOK_TILE = 512


def _colsum8(v):
    t, ccols = v.shape
    return jnp.sum(v.reshape(t // 8, 8, ccols), axis=0)


def _norm_mod_fwd(x, g, sc, sh, name):
    s, d = x.shape
    t = TOK_TILE

    def body(x_ref, g_ref, sc_ref, sh_ref, h_ref):
        xv = x_ref[...]
        n = xv * lax.rsqrt(jnp.mean(xv * xv, axis=-1, keepdims=True) + NORM_EPS)
        h_ref[...] = ((n * g_ref[...]) * (1.0 + sc_ref[...]) + sh_ref[...]).astype(BF16)

    row = pl.BlockSpec((1, d), lambda i: (0, 0))
    return pl.pallas_call(
        body, name=name, grid=(s // t,), in_specs=[pl.BlockSpec((t, d), lambda i: (i, 0)), row, row, row],
        out_specs=pl.BlockSpec((t, d), lambda i: (i, 0)), out_shape=jax.ShapeDtypeStruct((s, d), BF16),
        compiler_params=_cparams(("parallel",)))(x, g, sc, sh)


def _norm_mod_bwd(x, dh, resid, g, sc, name):
    s, d = x.shape
    t = TOK_TILE

    def body(x_ref, dh_ref, r_ref, g_ref, sc_ref, dx_ref, dsh_ref, dsc_ref, dg_ref):
        i = pl.program_id(0)
        xv = x_ref[...]
        rstd = lax.rsqrt(jnp.mean(xv * xv, axis=-1, keepdims=True) + NORM_EPS)
        n = xv * rstd
        dhv = dh_ref[...].astype(F32)
        gv = g_ref[...]
        dyn = dhv * (1.0 + sc_ref[...])
        dn = dyn * gv
        dx_ref[...] = r_ref[...] + rstd * (dn - n * jnp.mean(dn * n, axis=-1, keepdims=True))

        @pl.when(i == 0)
        def _():
            dsh_ref[...] = jnp.zeros_like(dsh_ref)
            dsc_ref[...] = jnp.zeros_like(dsc_ref)
            dg_ref[...] = jnp.zeros_like(dg_ref)

        dsh_ref[...] += _colsum8(dhv)
        dsc_ref[...] += _colsum8(dhv * (n * gv))
        dg_ref[...] += _colsum8(dyn * n)

    tile = pl.BlockSpec((t, d), lambda i: (i, 0))
    row = pl.BlockSpec((1, d), lambda i: (0, 0))
    acc = pl.BlockSpec((8, d), lambda i: (0, 0))
    acc_s = jax.ShapeDtypeStruct((8, d), F32)
    return pl.pallas_call(
        body, name=name, grid=(s // t,), in_specs=[tile, tile, tile, row, row],
        out_specs=[tile, acc, acc, acc], out_shape=[jax.ShapeDtypeStruct((s, d), F32), acc_s, acc_s, acc_s],
        compiler_params=_cparams(("arbitrary",)))(x, dh, resid, g, sc)


def _gate_bwd(dxo, y, gate, name):
    s, d = dxo.shape
    t = TOK_TILE

    def body(dx_ref, y_ref, g_ref, dy_ref, dg_ref):
        i = pl.program_id(0)
        dxv = dx_ref[...]
        dy_ref[...] = (dxv * g_ref[...]).astype(BF16)

        @pl.when(i == 0)
        def _():
            dg_ref[...] = jnp.zeros_like(dg_ref)

        dg_ref[...] += _colsum8(dxv * y_ref[...].astype(F32))

    tile = pl.BlockSpec((t, d), lambda i: (i, 0))
    return pl.pallas_call(
        body, name=name, grid=(s // t,), in_specs=[tile, tile, pl.BlockSpec((1, d), lambda i: (0, 0))],
        out_specs=[tile, pl.BlockSpec((8, d), lambda i: (0, 0))],
        out_shape=[jax.ShapeDtypeStruct((s, d), BF16), jax.ShapeDtypeStruct((8, d), F32)],
        compiler_params=_cparams(("arbitrary",)))(dxo, y, gate)


def _final_loss(x, g, target, name):
    s, d = x.shape
    t = TOK_TILE
    inv_d = 1.0 / d

    def body(x_ref, g_ref, t_ref, dx_ref, sq_ref, dg_ref):
        i = pl.program_id(0)
        xv = x_ref[...]
        rstd = lax.rsqrt(jnp.mean(xv * xv, axis=-1, keepdims=True) + NORM_EPS)
        n = xv * rstd
        gv = g_ref[...]
        err = n * gv - t_ref[...]
        dout = err * inv_d
        dn = dout * gv
        dx_ref[...] = rstd * (dn - n * jnp.mean(dn * n, axis=-1, keepdims=True))

        @pl.when(i == 0)
        def _():
            sq_ref[...] = jnp.zeros_like(sq_ref)
            dg_ref[...] = jnp.zeros_like(dg_ref)

        sq_ref[...] += _colsum8(err * err)
        dg_ref[...] += _colsum8(dout * n)

    tile = pl.BlockSpec((t, d), lambda i: (i, 0))
    acc = pl.BlockSpec((8, d), lambda i: (0, 0))
    acc_s = jax.ShapeDtypeStruct((8, d), F32)
    return pl.pallas_call(
        body, name=name, grid=(s // t,), in_specs=[tile, pl.BlockSpec((1, d), lambda i: (0, 0)), tile],
        out_specs=[tile, acc, acc], out_shape=[jax.ShapeDtypeStruct((s, d), F32), acc_s, acc_s],
        compiler_params=_cparams(("arbitrary",)))(x, g, target)


NEG_BIG = -1e30


def _head_masks(rows):
    lane = lax.broadcasted_iota(jnp.int32, (rows, LANES), 1)
    return [(lane // HEAD_DIM) == hh for hh in range(2)]


def _group_view(slabs, gi):
    dil = ATTN_GROUPS[gi][1]
    _, s, w = slabs.shape
    if dil == 1:
        return slabs, (gi, 3 + gi, 6 + gi)
    return slabs[gi:9:3].reshape(3, s // dil, dil * w), (0, 1, 2)


def _attn_fwd(slabs, gi, name):
    window, dil = ATTN_GROUPS[gi]
    n_back = window // dil
    _, s, w = slabs.shape
    big_l = s // dil
    nb = big_l // ATTN_BLOCK
    blk = ATTN_BLOCK
    view, (iq, ik, iv) = _group_view(slabs, gi)
    scale = HEAD_DIM ** -0.5

    def body(q_ref, kp_ref, kc_ref, vp_ref, vc_ref, o_ref, lse_ref):
        n = pl.program_id(1)
        qi = lax.broadcasted_iota(jnp.int32, (blk, 2 * blk), 0)
        kj = lax.broadcasted_iota(jnp.int32, (blk, 2 * blk), 1)
        dist = qi + blk - kj
        ok = (dist >= 0) & (dist <= n_back) & ((kj >= blk) | (n > 0))
        mq = _head_masks(blk)
        mk = _head_masks(2 * blk)
        for p in range(w // LANES):
            cols = slice(p * LANES, (p + 1) * LANES)
            qp = q_ref[:, cols]
            k2 = jnp.concatenate([kp_ref[:, cols], kc_ref[:, cols]], axis=0)
            v2 = jnp.concatenate([vp_ref[:, cols], vc_ref[:, cols]], axis=0)
            o_pair = jnp.zeros((blk, LANES), F32)
            lse_pair = jnp.zeros((blk, LANES), F32)
            for hh in range(2):
                qm = jnp.where(mq[hh], qp, jnp.zeros_like(qp))
                sc = lax.dot_general(qm, k2, NT, preferred_element_type=F32) * scale
                sc = jnp.where(ok, sc, NEG_BIG)
                mx = jnp.max(sc, axis=-1, keepdims=True)
                pe = jnp.exp(sc - mx)
                den = jnp.sum(pe, axis=-1, keepdims=True)
                pn = (pe / den).astype(BF16)
                vm = jnp.where(mk[hh], v2, jnp.zeros_like(v2))
                o_pair = o_pair + jnp.dot(pn, vm, preferred_element_type=F32)
                lse_pair = jnp.where(mq[hh], mx + jnp.log(den), lse_pair)
            o_ref[:, cols] = o_pair
            lse_ref[:, cols] = lse_pair

    def spec(slab, prev):
        if prev:
            return pl.BlockSpec((None, blk, w), lambda r, n: (slab, jnp.maximum(n - 1, 0), r))
        return pl.BlockSpec((None, blk, w), lambda r, n: (slab, n, r))

    out_spec = pl.BlockSpec((blk, w), lambda r, n: (n, r))
    sds = jax.ShapeDtypeStruct((big_l, dil * w), F32)
    o, lse = pl.pallas_call(
        body, name=name, grid=(dil, nb),
        in_specs=[spec(iq, False), spec(ik, True), spec(ik, False), spec(iv, True), spec(iv, False)],
        out_specs=[out_spec, out_spec], out_shape=[sds, sds],
        compiler_params=_cparams(("parallel", "arbitrary")))(view, view, view, view, view)
    return o.reshape(s, w), lse.reshape(s, w)


def _attn_merge(outs, lses, name):
    s, w = outs[0].shape
    t = TOK_TILE

    def body(o0, o1, o2, l0, l1, l2, a_ref, lt_ref):
        ls = [l0[...], l1[...], l2[...]]
        mx = jnp.maximum(jnp.maximum(ls[0], ls[1]), ls[2])
        es = [jnp.exp(l - mx) for l in ls]
        den = es[0] + es[1] + es[2]
        num = es[0] * o0[...] + es[1] * o1[...] + es[2] * o2[...]
        a_ref[...] = (num / den).astype(BF16)
        lt_ref[...] = mx + jnp.log(den)

    tile = pl.BlockSpec((t, w), lambda i: (i, 0))
    return pl.pallas_call(
        body, name=name, grid=(s // t,), in_specs=[tile] * 6, out_specs=[tile, tile],
        out_shape=[jax.ShapeDtypeStruct((s, w), BF16), jax.ShapeDtypeStruct((s, w), F32)],
        compiler_params=_cparams(("parallel",)))(*outs, *lses)


def _attn_bwd(slabs, dap, ap, lse_tot, gi, name):
    window, dil = ATTN_GROUPS[gi]
    n_back = window // dil
    _, s, w = slabs.shape
    big_l = s // dil
    nb = big_l // ATTN_BLOCK
    blk = ATTN_BLOCK
    view, (iq, ik, iv) = _group_view(slabs, gi)
    dap_v = dap.reshape(big_l, dil * 2 * w)
    ap_v = ap.reshape(big_l, dil * 2 * w)
    lse_v = lse_tot.reshape(big_l, dil * w)
    scale = HEAD_DIM ** -0.5

    def body(qc_ref, qn_ref, kp_ref, kc_ref, vp_ref, vc_ref, dc_ref, dn_ref, ac_ref, an_ref, lc_ref, ln_ref, out_ref):
        m = pl.program_id(1)
        qi = lax.broadcasted_iota(jnp.int32, (blk, 2 * blk), 0)
        kj = lax.broadcasted_iota(jnp.int32, (blk, 2 * blk), 1)
        dist = qi + blk - kj
        ok = (dist >= 0) & (dist <= n_back) & ((kj >= blk) | (m > 0))
        qi1 = lax.broadcasted_iota(jnp.int32, (blk, blk), 0)
        kj1 = lax.broadcasted_iota(jnp.int32, (blk, blk), 1)
        ok_next = (qi1 + blk - kj1 <= n_back) & (m + 1 < nb)
        mq = _head_masks(blk)
        mk = _head_masks(2 * blk)

        def per_head(q_t, d_t, a_t, l_t, hh):
            qm = jnp.where(mq[hh], q_t, jnp.zeros_like(q_t))
            dm = jnp.where(mq[hh], d_t, jnp.zeros_like(d_t))
            delta = jnp.sum(jnp.where(mq[hh], d_t.astype(F32) * a_t.astype(F32), 0.0), axis=-1, keepdims=True)
            lse_h = jnp.max(jnp.where(mq[hh], l_t, NEG_BIG), axis=-1, keepdims=True)
            return qm, dm, delta, lse_h

        for p in range(w // LANES):
            cols = slice(p * LANES, (p + 1) * LANES)
            k2 = jnp.concatenate([kp_ref[:, cols], kc_ref[:, cols]], axis=0)
            v2 = jnp.concatenate([vp_ref[:, cols], vc_ref[:, cols]], axis=0)
            kc, vc = kc_ref[:, cols], vc_ref[:, cols]
            dq_pair = jnp.zeros((blk, LANES), F32)
            dk_pair = jnp.zeros((blk, LANES), F32)
            dv_pair = jnp.zeros((blk, LANES), F32)
            for hh in range(2):
                qm, dm, delta, lse_h = per_head(qc_ref[:, cols], dc_ref[:, cols], ac_ref[:, cols], lc_ref[:, cols], hh)
                sc = lax.dot_general(qm, k2, NT, preferred_element_type=F32) * scale
                pr = jnp.exp(jnp.where(ok, sc, NEG_BIG) - lse_h)
                dp = lax.dot_general(dm, v2, NT, preferred_element_type=F32)
                ds = pr * (dp - delta)
                ds_b = ds.astype(BF16)
                km = jnp.where(mk[hh], k2, jnp.zeros_like(k2))
                dq_pair = dq_pair + jnp.dot(ds_b, km, preferred_element_type=F32)
                dk_pair = dk_pair + lax.dot_general(ds_b[:, blk:], qm, TN, preferred_element_type=F32)
                dv_pair = dv_pair + lax.dot_general(pr[:, blk:].astype(BF16), dm, TN, preferred_element_type=F32)
                qm, dm, delta, lse_h = per_head(qn_ref[:, cols], dn_ref[:, cols], an_ref[:, cols], ln_ref[:, cols], hh)
                sc = lax.dot_general(qm, kc, NT, preferred_element_type=F32) * scale
                pr = jnp.exp(jnp.where(ok_next, sc, NEG_BIG) - lse_h)
                dp = lax.dot_general(dm, vc, NT, preferred_element_type=F32)
                ds_b = (pr * (dp - delta)).astype(BF16)
                dk_pair = dk_pair + lax.dot_general(ds_b, qm, TN, preferred_element_type=F32)
                dv_pair = dv_pair + lax.dot_general(pr.astype(BF16), dm, TN, preferred_element_type=F32)
            out_ref[0, :, cols] = (dq_pair * scale).astype(BF16)
            out_ref[1, :, cols] = (dk_pair * scale).astype(BF16)
            out_ref[2, :, cols] = dv_pair.astype(BF16)

    def slab_spec(slab, shift):
        if shift < 0:
            return pl.BlockSpec((None, blk, w), lambda r, n: (slab, jnp.maximum(n - 1, 0), r))
        if shift > 0:
            return pl.BlockSpec((None, blk, w), lambda r, n: (slab, jnp.minimum(n + 1, nb - 1), r))
        return pl.BlockSpec((None, blk, w), lambda r, n: (slab, n, r))

    def tok_spec(stride, shift):
        if shift > 0:
            return pl.BlockSpec((blk, w), lambda r, n: (jnp.minimum(n + 1, nb - 1), stride * r))
        return pl.BlockSpec((blk, w), lambda r, n: (n, stride * r))

    out = pl.pallas_call(
        body, name=name, grid=(dil, nb),
        in_specs=[slab_spec(iq, 0), slab_spec(iq, 1), slab_spec(ik, -1), slab_spec(ik, 0),
                  slab_spec(iv, -1), slab_spec(iv, 0),
                  tok_spec(2, 0), tok_spec(2, 1), tok_spec(2, 0), tok_spec(2, 1), tok_spec(1, 0), tok_spec(1, 1)],
        out_specs=pl.BlockSpec((3, blk, w), lambda r, n: (0, n, r)),
        out_shape=jax.ShapeDtypeStruct((3, big_l, dil * w), BF16),
        compiler_params=_cparams(("parallel", "arbitrary")))(
            view, view, view, view, view, view, dap_v, dap_v, ap_v, ap_v, lse_v, lse_v)
    return out.reshape(3, s, w)


def _shift_down(cur, prev, j):
    row = lax.broadcasted_iota(jnp.int32, cur.shape, 0)
    return jnp.where(row >= j, pltpu.roll(cur, j, 0), pltpu.roll(prev, j, 0))


def _shift_up(cur, nxt, j):
    t = cur.shape[0]
    row = lax.broadcasted_iota(jnp.int32, cur.shape, 0)
    return jnp.where(row < t - j, pltpu.roll(cur, t - j, 0), pltpu.roll(nxt, t - j, 0))


def _window_sum_down(cur, prev, w):
    s, sp, step = cur, prev, 1
    while step < w:
        s_new = s + _shift_down(s, sp, step)
        sp = sp + pltpu.roll(sp, step, 0)
        s, step = s_new, step * 2
    return s


def _window_sum_up(cur, nxt, w):
    t = cur.shape[0]
    s, sn, step = cur, nxt, 1
    while step < w:
        s_new = s + _shift_up(s, sn, step)
        sn = sn + pltpu.roll(sn, t - step, 0)
        s, step = s_new, step * 2
    return s


def _pool_count(tile_idx, t, w):
    row = lax.broadcasted_iota(jnp.int32, (t, LANES), 0) + tile_idx * t
    return jnp.minimum(row + 1, w).astype(F32)


def _pool_fwd(slabs, attn, pool_w, pool_scale, name):
    _, s, w = slabs.shape
    t = TOK_TILE
    gw = POOL_W // len(POOL_WINDOWS)

    def body(u_ref, up_ref, a_ref, w_ref, sc_ref, ap_ref, d_ref):
        i = pl.program_id(0)
        ap_ref[:, :w] = a_ref[...]
        for gi, win in enumerate(POOL_WINDOWS):
            cols = slice(gi * gw, (gi + 1) * gw)
            u = u_ref[:, cols].astype(F32)
            up = jnp.where(i > 0, up_ref[:, cols].astype(F32), 0.0)
            d = (_window_sum_down(u, up, win) / _pool_count(i, t, win) - u).astype(BF16)
            d_ref[:, cols] = d
            y = jnp.dot(d, w_ref[gi].astype(BF16), preferred_element_type=F32)
            ap_ref[:, w + gi * gw:w + (gi + 1) * gw] = (y * sc_ref[:, cols]).astype(BF16)

    return pl.pallas_call(
        body, name=name, grid=(s // t,),
        in_specs=[pl.BlockSpec((None, t, w), lambda i: (N_SLABS - 1, i, 0)),
                  pl.BlockSpec((None, t, w), lambda i: (N_SLABS - 1, jnp.maximum(i - 1, 0), 0)),
                  pl.BlockSpec((t, w), lambda i: (i, 0)),
                  pl.BlockSpec((len(POOL_WINDOWS), gw, gw), lambda i: (0, 0, 0)),
                  pl.BlockSpec((1, w), lambda i: (0, 0))],
        out_specs=[pl.BlockSpec((t, 2 * w), lambda i: (i, 0)), pl.BlockSpec((t, w), lambda i: (i, 0))],
        out_shape=[jax.ShapeDtypeStruct((s, 2 * w), BF16), jax.ShapeDtypeStruct((s, w), BF16)],
        compiler_params=_cparams(("parallel",)))(slabs, slabs, attn, pool_w, pool_scale)


def _pool_bwd(dap, d, pool_w, pool_scale, name):
    s, w = d.shape
    t = TOK_TILE
    nt = s // t
    gw = POOL_W // len(POOL_WINDOWS)

    def body(dy_ref, dyn_ref, d_ref, w_ref, sc_ref, du_ref, dw_ref, dsc_ref):
        i = pl.program_id(0)

        @pl.when(i == 0)
        def _():
            dw_ref[...] = jnp.zeros_like(dw_ref)
            dsc_ref[...] = jnp.zeros_like(dsc_ref)

        for gi, win in enumerate(POOL_WINDOWS):
            cols = slice(gi * gw, (gi + 1) * gw)
            wb = w_ref[gi].astype(BF16)
            scale = sc_ref[:, cols]
            dv = d_ref[:, cols]
            dy = dy_ref[:, cols].astype(F32)
            y = jnp.dot(dv, wb, preferred_element_type=F32)
            dsc_ref[:, cols] += _colsum8(dy * y)
            dyp = (dy * scale).astype(BF16)
            dw_ref[gi] += lax.dot_general(dv, dyp, TN, preferred_element_type=F32)
            dd = lax.dot_general(dyp, wb, NT, preferred_element_type=F32)
            dypn = (dyn_ref[:, cols].astype(F32) * scale).astype(BF16)
            ddn = lax.dot_general(dypn, wb, NT, preferred_element_type=F32)
            e = dd / _pool_count(i, t, win)
            en = jnp.where(i + 1 < nt, ddn / _pool_count(i + 1, t, win), 0.0)
            du_ref[:, cols] = (_window_sum_up(e, en, win) - dd).astype(BF16)

    return pl.pallas_call(
        body, name=name, grid=(nt,),
        in_specs=[pl.BlockSpec((t, w), lambda i: (i, 1)),
                  pl.BlockSpec((t, w), lambda i: (jnp.minimum(i + 1, nt - 1), 1)),
                  pl.BlockSpec((t, w), lambda i: (i, 0)),
                  pl.BlockSpec((len(POOL_WINDOWS), gw, gw), lambda i: (0, 0, 0)),
                  pl.BlockSpec((1, w), lambda i: (0, 0))],
        out_specs=[pl.BlockSpec((t, w), lambda i: (i, 0)),
                   pl.BlockSpec((len(POOL_WINDOWS), gw, gw), lambda i: (0, 0, 0)),
                   pl.BlockSpec((8, w), lambda i: (0, 0))],
        out_shape=[jax.ShapeDtypeStruct((s, w), BF16), jax.ShapeDtypeStruct((len(POOL_WINDOWS), gw, gw), F32),
                   jax.ShapeDtypeStruct((8, w), F32)],
        compiler_params=_cparams(("arbitrary",)))(dap, dap, d, pool_w, pool_scale)


CONV_TAPS = 4
CONV_COLS = 512


def _sigmoid(v):
    return 1.0 / (1.0 + jnp.exp(-v))


def _conv_pre(x, xprev, w_ref, b_ref):
    pre = b_ref[...] + w_ref[CONV_TAPS - 1:CONV_TAPS, :] * x
    for k in range(CONV_TAPS - 1):
        pre = pre + w_ref[k:k + 1, :] * _shift_down(x, xprev, CONV_TAPS - 1 - k)
    return pre


def _conv_fwd(xbc, conv_w, conv_b, name):
    s, c = xbc.shape
    t, tc = TOK_TILE, CONV_COLS

    def body(x_ref, xp_ref, w_ref, b_ref, o_ref):
        i = pl.program_id(0)
        x = x_ref[...].astype(F32)
        xp = jnp.where(i > 0, xp_ref[...].astype(F32), 0.0)
        pre = _conv_pre(x, xp, w_ref, b_ref)
        o_ref[...] = (pre * _sigmoid(pre)).astype(BF16)

    return pl.pallas_call(
        body, name=name, grid=(s // t, c // tc),
        in_specs=[pl.BlockSpec((t, tc), lambda i, j: (i, j)),
                  pl.BlockSpec((t, tc), lambda i, j: (jnp.maximum(i - 1, 0), j)),
                  pl.BlockSpec((CONV_TAPS, tc), lambda i, j: (0, j)), pl.BlockSpec((1, tc), lambda i, j: (0, j))],
        out_specs=pl.BlockSpec((t, tc), lambda i, j: (i, j)), out_shape=jax.ShapeDtypeStruct((s, c), BF16),
        compiler_params=_cparams(("parallel", "parallel")))(xbc, xbc, conv_w, conv_b)


def _conv_bwd(xbc, dact, conv_w, conv_b, name):
    s, c = xbc.shape
    t, tc = TOK_TILE, CONV_COLS
    nt = s // t

    def body(xp_ref, x_ref, xn_ref, da_ref, dan_ref, w_ref, b_ref, dx_ref, dw_ref, db_ref):
        i = pl.program_id(1)

        @pl.when(i == 0)
        def _():
            dw_ref[...] = jnp.zeros_like(dw_ref)
            db_ref[...] = jnp.zeros_like(db_ref)

        x = x_ref[...].astype(F32)
        xp = jnp.where(i > 0, xp_ref[...].astype(F32), 0.0)
        xn = xn_ref[...].astype(F32)

        def dsilu(pre):
            sg = _sigmoid(pre)
            return sg * (1.0 + pre * (1.0 - sg))

        dpre = da_ref[...] * dsilu(_conv_pre(x, xp, w_ref, b_ref))
        dpre_n = jnp.where(i + 1 < nt, dan_ref[...] * dsilu(_conv_pre(xn, x, w_ref, b_ref)), 0.0)
        dx = w_ref[CONV_TAPS - 1:CONV_TAPS, :] * dpre
        dw_ref[CONV_TAPS - 1] += _colsum8(dpre * x)
        for k in range(CONV_TAPS - 1):
            j = CONV_TAPS - 1 - k
            dx = dx + w_ref[k:k + 1, :] * _shift_up(dpre, dpre_n, j)
            dw_ref[k] += _colsum8(dpre * _shift_down(x, xp, j))
        dx_ref[...] = dx.astype(BF16)
        db_ref[...] += _colsum8(dpre)

    def xspec(shift):
        if shift < 0:
            return pl.BlockSpec((t, tc), lambda j, i: (jnp.maximum(i - 1, 0), j))
        if shift > 0:
            return pl.BlockSpec((t, tc), lambda j, i: (jnp.minimum(i + 1, nt - 1), j))
        return pl.BlockSpec((t, tc), lambda j, i: (i, j))

    return pl.pallas_call(
        body, name=name, grid=(c // tc, nt),
        in_specs=[xspec(-1), xspec(0), xspec(1), xspec(0), xspec(1),
                  pl.BlockSpec((CONV_TAPS, tc), lambda j, i: (0, j)), pl.BlockSpec((1, tc), lambda j, i: (0, j))],
        out_specs=[xspec(0), pl.BlockSpec((CONV_TAPS, 8, tc), lambda j, i: (0, 0, j)),
                   pl.BlockSpec((8, tc), lambda j, i: (0, j))],
        out_shape=[jax.ShapeDtypeStruct((s, c), BF16), jax.ShapeDtypeStruct((CONV_TAPS, 8, c), F32),
                   jax.ShapeDtypeStruct((8, c), F32)],
        compiler_params=_cparams(("parallel", "arbitrary")))(xbc, xbc, xbc, dact, dact, conv_w, conv_b)


GN_TILE = 256


def _gated_norm_fwd(y, z, g, name):
    s, c = y.shape
    t = GN_TILE

    def body(y_ref, z_ref, g_ref, o_ref):
        for gi in range(SSM_GROUPS):
            cols = slice(gi * SSM_GW, (gi + 1) * SSM_GW)
            zv = z_ref[:, cols].astype(F32)
            yf = y_ref[:, cols] * (zv * _sigmoid(zv))
            r = lax.rsqrt(jnp.mean(yf * yf, axis=-1, keepdims=True) + NORM_EPS)
            o_ref[:, cols] = (yf * r * g_ref[:, cols]).astype(BF16)

    tile = pl.BlockSpec((t, c), lambda i: (i, 0))
    return pl.pallas_call(
        body, name=name, grid=(s // t,), in_specs=[tile, tile, pl.BlockSpec((1, c), lambda i: (0, 0))],
        out_specs=tile, out_shape=jax.ShapeDtypeStruct((s, c), BF16),
        compiler_params=_cparams(("parallel",)))(y, z, g)


def _gated_norm_bwd(y, z, dout, g, name):
    s, c = y.shape
    t = GN_TILE

    def body(y_ref, z_ref, do_ref, g_ref, dy_ref, dz_ref, dg_ref):
        i = pl.program_id(0)

        @pl.when(i == 0)
        def _():
            dg_ref[...] = jnp.zeros_like(dg_ref)

        for gi in range(SSM_GROUPS):
            cols = slice(gi * SSM_GW, (gi + 1) * SSM_GW)
            zv = z_ref[:, cols].astype(F32)
            yv = y_ref[:, cols]
            sg = _sigmoid(zv)
            sz = zv * sg
            yf = yv * sz
            r = lax.rsqrt(jnp.mean(yf * yf, axis=-1, keepdims=True) + NORM_EPS)
            n = yf * r
            dout = do_ref[:, cols]
            dn = dout * g_ref[:, cols]
            dg_ref[:, cols] += _colsum8(dout * n)
            dyf = r * (dn - n * jnp.mean(dn * n, axis=-1, keepdims=True))
            dy_ref[:, cols] = dyf * sz
            dz_ref[:, cols] = (dyf * yv * (sg * (1.0 + zv * (1.0 - sg)))).astype(BF16)

    tile = pl.BlockSpec((t, c), lambda i: (i, 0))
    return pl.pallas_call(
        body, name=name, grid=(s // t,), in_specs=[tile, tile, tile, pl.BlockSpec((1, c), lambda i: (0, 0))],
        out_specs=[tile, tile, pl.BlockSpec((8, c), lambda i: (0, 0))],
        out_shape=[jax.ShapeDtypeStruct((s, c), F32), jax.ShapeDtypeStruct((s, c), BF16),
                   jax.ShapeDtypeStruct((8, c), F32)],
        compiler_params=_cparams(("arbitrary",)))(y, z, dout, g)


def _split_dot(x, e, dims, terms):
    r, acc = x, None
    for i in range(terms):
        p = r.astype(BF16)
        part = lax.dot_general(p, e, dims, preferred_element_type=F32)
        acc = part if acc is None else acc + part
        if i + 1 < terms:
            r = r - p.astype(F32)
    return acc


def _split_dot_r(e, x, dims, terms):
    r, acc = x, None
    for i in range(terms):
        p = r.astype(BF16)
        part = lax.dot_general(e, p, dims, preferred_element_type=F32)
        acc = part if acc is None else acc + part
        if i + 1 < terms:
            r = r - p.astype(F32)
    return acc


def _ssd_prep(dtr_ref, bias_ref, alog_ref):
    q = SSM_CHUNK
    dtr = dtr_ref[...] + bias_ref[...]
    dt = jnp.maximum(dtr, 0.0) + jnp.log(1.0 + jnp.exp(-jnp.abs(dtr)))
    a_neg = -jnp.exp(alog_ref[...])
    dta = dt * a_neg
    row = lax.broadcasted_iota(jnp.int32, (q, q), 0)
    col = lax.broadcasted_iota(jnp.int32, (q, q), 1)
    causal = row >= col
    a = _split_dot_r(causal.astype(BF16), dta, NN, 3)
    aq_row = jnp.sum(dta, axis=0, keepdims=True)
    aq_hb = _split_dot(dta, jnp.ones((q, LANES), BF16), TN, 3)
    return dict(dtr=dtr, dt=dt, a_neg=a_neg, a=a, a_t=jnp.transpose(a), aq_row=aq_row, aq_hb=aq_hb,
                ea=jnp.exp(a), fa=jnp.exp(aq_row - a), causal=causal)


def _ssd_group_mats(g):
    hrow = lax.broadcasted_iota(jnp.int32, (LANES, SSM_GW), 0)
    jcol = lax.broadcasted_iota(jnp.int32, (LANES, SSM_GW), 1)
    eg = (hrow == 8 * g + jcol // HEAD_DIM).astype(BF16)
    jrow = lax.broadcasted_iota(jnp.int32, (SSM_GW, LANES), 0)
    hcol = lax.broadcasted_iota(jnp.int32, (SSM_GW, LANES), 1)
    eg_t = (hcol == 8 * g + jrow // HEAD_DIM).astype(BF16)
    hrow2 = lax.broadcasted_iota(jnp.int32, (LANES, 8 * LANES), 0)
    jcol2 = lax.broadcasted_iota(jnp.int32, (LANES, 8 * LANES), 1)
    sel = (hrow2 == 8 * g + jcol2 // LANES).astype(BF16)
    return eg, eg_t, sel


def _ssd_fwd(xbc, dt_raw, dt_bias, a_log, d_e, name):
    s = xbc.shape[0]
    q = SSM_CHUNK
    nc = s // q

    def body(x_ref, dtr_ref, bias_ref, alog_ref, de_ref, y_ref, hin_ref, state, at_ref):
        c = pl.program_id(0)

        @pl.when(c == 0)
        def _():
            state[...] = jnp.zeros_like(state)

        hin_ref[...] = state[...].astype(BF16)
        pr = _ssd_prep(dtr_ref, bias_ref, alog_ref)
        at_ref[...] = pr["a_t"]
        exp_aq_hb = jnp.exp(pr["aq_hb"])
        stack3 = jnp.concatenate([pr["dt"], pr["ea"], pr["fa"]], axis=0)
        mq = _head_masks(q)
        for g in range(SSM_GROUPS):
            eg, eg_t, sel = _ssd_group_mats(g)
            cols = slice(g * SSM_GW, (g + 1) * SSM_GW)
            xg = x_ref[:, cols]
            bg = x_ref[:, SSM_INNER + g * SSM_STATE:SSM_INNER + (g + 1) * SSM_STATE]
            cg = x_ref[:, SSM_INNER + SSM_GROUPS * SSM_STATE + g * SSM_STATE:
                       SSM_INNER + SSM_GROUPS * SSM_STATE + (g + 1) * SSM_STATE]
            e3 = _split_dot(stack3, eg, NN, 2)
            dt_e, ea_e, fa_e = e3[:q], e3[q:2 * q], e3[2 * q:]
            xf = xg.astype(F32)
            xdt = xf * dt_e
            xdt_b = xdt.astype(BF16)
            cb = lax.dot_general(cg, bg, NT, preferred_element_type=F32)
            colb = _split_dot(pr["a"], sel, NN, 3)
            hg = state[cols, :]
            y_g = lax.dot_general(cg, hg.astype(BF16), NT, preferred_element_type=F32) * ea_e + de_ref[:, cols] * xf
            pairs = []
            for pp in range(4):
                xp = xdt_b[:, pp * LANES:(pp + 1) * LANES]
                yp = jnp.zeros((q, LANES), F32)
                for hh in range(2):
                    hi = 2 * pp + hh
                    diff = colb[:, hi * LANES:(hi + 1) * LANES] - at_ref[8 * g + hi:8 * g + hi + 1, :]
                    lmat = jnp.exp(jnp.where(pr["causal"], diff, NEG_BIG))
                    m_b = (cb * lmat).astype(BF16)
                    yp = yp + jnp.dot(m_b, jnp.where(mq[hh], xp, jnp.zeros_like(xp)), preferred_element_type=F32)
                pairs.append(yp)
            y_ref[:, cols] = y_g + jnp.concatenate(pairs, axis=1)
            s_g = lax.dot_general((xdt * fa_e).astype(BF16), bg, TN, preferred_element_type=F32)
            dec_g = _split_dot_r(eg, exp_aq_hb, TN, 2)
            state[cols, :] = dec_g * hg + s_g

    row128 = pl.BlockSpec((1, LANES), lambda c: (0, 0))
    return pl.pallas_call(
        body, name=name, grid=(nc,),
        in_specs=[pl.BlockSpec((q, SSM_CONV_DIM), lambda c: (c, 0)), pl.BlockSpec((q, LANES), lambda c: (c, 0)),
                  row128, row128, pl.BlockSpec((1, SSM_INNER), lambda c: (0, 0))],
        out_specs=[pl.BlockSpec((q, SSM_INNER), lambda c: (c, 0)),
                   pl.BlockSpec((None, SSM_INNER, SSM_STATE), lambda c: (c, 0, 0))],
        out_shape=[jax.ShapeDtypeStruct((s, SSM_INNER), F32), jax.ShapeDtypeStruct((nc, SSM_INNER, SSM_STATE), BF16)],
        scratch_shapes=[pltpu.VMEM((SSM_INNER, SSM_STATE), F32), pltpu.VMEM((LANES, q), F32)],
        compiler_params=_cparams(("arbitrary",)))(xbc, dt_raw, dt_bias, a_log, d_e)


def _ssd_bwd(xbc, dt_raw, dt_bias, a_log, d_e, hin, dy, name):
    s = xbc.shape[0]
    q = SSM_CHUNK
    nc = s // q

    def body(x_ref, dtr_ref, bias_ref, alog_ref, de_ref, hin_ref, dy_ref,
             dx_ref, ddt_ref, da_acc, db_acc, dd_acc, gst, at_ref):
        i = pl.program_id(0)

        @pl.when(i == 0)
        def _():
            gst[...] = jnp.zeros_like(gst)
            da_acc[...] = jnp.zeros_like(da_acc)
            db_acc[...] = jnp.zeros_like(db_acc)
            dd_acc[...] = jnp.zeros_like(dd_acc)

        pr = _ssd_prep(dtr_ref, bias_ref, alog_ref)
        at_ref[...] = pr["a_t"]
        exp_aq_hb = jnp.exp(pr["aq_hb"])
        stack3 = jnp.concatenate([pr["dt"], pr["ea"], pr["fa"]], axis=0)
        mq = _head_masks(q)
        lane_h = lax.broadcasted_iota(jnp.int32, (q, LANES), 1)
        sub_h = lax.broadcasted_iota(jnp.int32, (LANES, q), 0)
        da = jnp.zeros((q, LANES), F32)
        da_tn = jnp.zeros((LANES, q), F32)
        daq = jnp.zeros((1, LANES), F32)
        ddt = jnp.zeros((q, LANES), F32)
        for g in range(SSM_GROUPS):
            eg, eg_t, sel = _ssd_group_mats(g)
            cols = slice(g * SSM_GW, (g + 1) * SSM_GW)
            bcols = slice(SSM_INNER + g * SSM_STATE, SSM_INNER + (g + 1) * SSM_STATE)
            ccols = slice(SSM_INNER + SSM_GROUPS * SSM_STATE + g * SSM_STATE,
                          SSM_INNER + SSM_GROUPS * SSM_STATE + (g + 1) * SSM_STATE)
            xg, bg, cg = x_ref[:, cols], x_ref[:, bcols], x_ref[:, ccols]
            e3 = _split_dot(stack3, eg, NN, 2)
            dt_e, ea_e, fa_e = e3[:q], e3[q:2 * q], e3[2 * q:]
            xf = xg.astype(F32)
            xdt = xf * dt_e
            xdt_b = xdt.astype(BF16)
            xdtf = xdt * fa_e
            xdtf_b = xdtf.astype(BF16)
            cb = lax.dot_general(cg, bg, NT, preferred_element_type=F32)
            colb = _split_dot(pr["a"], sel, NN, 3)
            dyg = dy_ref[:, cols]
            dd_acc[:, cols] += _colsum8(dyg * xf)
            hg_b = hin_ref[cols, :]
            gg = gst[cols, :]
            gg_b = gg.astype(BF16)
            dye_b = (dyg * ea_e).astype(BF16)
            dc_g = jnp.dot(dye_b, hg_b, preferred_element_type=F32)
            dh_g = lax.dot_general(dye_b, cg, TN, preferred_element_type=F32)
            yoff = lax.dot_general(cg, hg_b, NT, preferred_element_type=F32) * ea_e
            da = da + _split_dot(dyg * yoff, eg_t, NN, 2)
            db_g = jnp.dot(xdtf_b, gg_b, preferred_element_type=F32)
            tmat = lax.dot_general(bg, gg_b, NT, preferred_element_type=F32)
            dxdt = fa_e * tmat
            qmat = _split_dot(xdtf * tmat, eg_t, NN, 2)
            da = da - qmat
            daq = daq + jnp.sum(qmat, axis=0, keepdims=True)
            gh = _split_dot(gg * hg_b.astype(F32), eg_t, TN, 2)
            daq = daq + jnp.sum(gh, axis=0, keepdims=True) * jnp.exp(pr["aq_row"])
            dcb = jnp.zeros((q, q), F32)
            pairs = []
            for pp in range(4):
                xp = xdt_b[:, pp * LANES:(pp + 1) * LANES]
                dyp = dyg[:, pp * LANES:(pp + 1) * LANES]
                dxp = jnp.zeros((q, LANES), F32)
                for hh in range(2):
                    hi = 2 * pp + hh
                    h = 8 * g + hi
                    diff = colb[:, hi * LANES:(hi + 1) * LANES] - at_ref[h:h + 1, :]
                    lmat = jnp.exp(jnp.where(pr["causal"], diff, NEG_BIG))
                    mmat = cb * lmat
                    dyh = jnp.where(mq[hh], dyp, 0.0).astype(BF16)
                    dm = lax.dot_general(dyh, xp, NT, preferred_element_type=F32)
                    dxp = dxp + lax.dot_general(mmat.astype(BF16), dyh, TN, preferred_element_type=F32)
                    wmat = dm * mmat
                    da = da + jnp.where(lane_h == h, jnp.sum(wmat, axis=-1, keepdims=True), 0.0)
                    da_tn = da_tn + jnp.where(sub_h == h, jnp.sum(wmat, axis=0, keepdims=True), 0.0)
                    dcb = dcb + dm * lmat
                pairs.append(dxp)
            dxdt = dxdt + jnp.concatenate(pairs, axis=1)
            dcb_b = dcb.astype(BF16)
            dc_g = dc_g + jnp.dot(dcb_b, bg, preferred_element_type=F32)
            db_g = db_g + lax.dot_general(dcb_b, cg, TN, preferred_element_type=F32)
            ddt = ddt + _split_dot(dxdt * xf, eg_t, NN, 2)
            dx_ref[:, cols] = dxdt * dt_e + de_ref[:, cols] * dyg
            dx_ref[:, bcols] = db_g
            dx_ref[:, ccols] = dc_g
            dec_g = _split_dot_r(eg, exp_aq_hb, TN, 2)
            gst[cols, :] = dh_g + dec_g * gg
        row = lax.broadcasted_iota(jnp.int32, (q, LANES), 0)
        da_all = da - jnp.transpose(da_tn) + jnp.where(row == q - 1, daq, 0.0)
        upper = jnp.logical_not(pr["causal"]) | (lax.broadcasted_iota(jnp.int32, (q, q), 0)
                                                 == lax.broadcasted_iota(jnp.int32, (q, q), 1))
        rcs = _split_dot_r(upper.astype(BF16), da_all, NN, 3)
        ddt_all = ddt + pr["a_neg"] * rcs
        da_acc[...] += _colsum8(pr["dt"] * rcs)
        ddtr = jnp.where(lane_h < SSM_HEADS, ddt_all * _sigmoid(pr["dtr"]), 0.0)
        ddt_ref[...] = ddtr
        db_acc[...] += _colsum8(ddtr)

    rev = lambda i: (nc - 1 - i, 0)
    row128 = pl.BlockSpec((1, LANES), lambda i: (0, 0))
    acc128 = pl.BlockSpec((8, LANES), lambda i: (0, 0))
    return pl.pallas_call(
        body, name=name, grid=(nc,),
        in_specs=[pl.BlockSpec((q, SSM_CONV_DIM), rev), pl.BlockSpec((q, LANES), rev), row128, row128,
                  pl.BlockSpec((1, SSM_INNER), lambda i: (0, 0)),
                  pl.BlockSpec((None, SSM_INNER, SSM_STATE), lambda i: (nc - 1 - i, 0, 0)),
                  pl.BlockSpec((q, SSM_INNER), rev)],
        out_specs=[pl.BlockSpec((q, SSM_CONV_DIM), rev), pl.BlockSpec((q, LANES), rev), acc128, acc128,
                   pl.BlockSpec((8, SSM_INNER), lambda i: (0, 0))],
        out_shape=[jax.ShapeDtypeStruct((s, SSM_CONV_DIM), F32), jax.ShapeDtypeStruct((s, LANES), F32),
                   jax.ShapeDtypeStruct((8, LANES), F32), jax.ShapeDtypeStruct((8, LANES), F32),
                   jax.ShapeDtypeStruct((8, SSM_INNER), F32)],
        scratch_shapes=[pltpu.VMEM((SSM_INNER, SSM_STATE), F32), pltpu.VMEM((LANES, q), F32)],
        compiler_params=_cparams(("arbitrary",)))(xbc, dt_raw, dt_bias, a_log, d_e, hin, dy)


MM = (1024, 1024, 2048)
MM_TN = (1024, 2048, 1024)


def _relu2_epilogue(acc, ex, outs):
    r = jnp.maximum(acc, 0.0)
    outs[0][...] = (r * r).astype(BF16)
    outs[1][...] = r.astype(BF16)


def _relu2_bwd_epilogue(acc, ex, outs):
    outs[0][...] = (acc * (2.0 * ex[0][...].astype(F32))).astype(BF16)


def _add_epilogue(acc, ex, outs):
    outs[0][...] = acc + ex[0][...]


def _tile_spec(tm, tn):
    return pl.BlockSpec((tm, tn), lambda i, j, k: (i, j))


def _sum8(acc):
    return jnp.sum(acc, axis=0)


def _row(v):
    return v.reshape(1, -1)


def _ffn_fwd(xin, norm_g, sc, sh, gate, w, tag):
    h = _norm_mod_fwd(xin, norm_g, sc, sh, f"norm_ffn_fwd_{tag}")
    f, r = _mm_nn(f"mm_ffn1_{tag}", h, w(f"w1_{tag}", h), *MM, out_dtypes=(BF16, BF16), epilogue=_relu2_epilogue)
    xout, yf = _mm_resid(f"mm_ffn2_{tag}", f, w(f"w2_{tag}", f), xin, gate, *MM)
    return xout, (h, f, r, yf)


def _ffn_bwd(dxo, xin, saved, norm_g, sc, gate, w, sink, tag):
    h, f, r, yf = saved
    tm, tn, tk = MM
    dyf, dgate = _gate_bwd(dxo, yf, gate, f"gate_bwd_ffn_{tag}")
    da = _mm_nt(f"mm_ffn2_bwd_{tag}", dyf, w(f"w2_{tag}"), *MM, extras=(r,), extra_specs=(_tile_spec(tm, tn),),
                epilogue=_relu2_bwd_epilogue)[0]
    tok = sink(f"ffn_w2_{tag}", _mm_tn(f"mm_ffn2_dw_{tag}", f, dyf, *MM_TN), "rows")
    dh = _mm_nt(f"mm_ffn1_bwd_{tag}", da, w(f"w1_{tag}"), *MM, out_dtypes=(F32,))[0]
    tok = tok + sink(f"ffn_w1_{tag}", _mm_tn(f"mm_ffn1_dw_{tag}", h, da, *MM_TN), "cols")
    dxin, dsh, dsc, dng = _norm_mod_bwd(xin, dh, dxo, norm_g + tok, sc, f"norm_ffn_bwd_{tag}")
    return dxin, (_sum8(dsh), _sum8(dsc), _sum8(dgate), _sum8(dng))


def _local_step(x, target, mod, w, p, sink):
    s, d = x.shape
    tm, tn, tk = MM
    mods = [[_row(mod[l, j * d:(j + 1) * d]) for j in range(6)] for l in range(2)]
    g = {}

    sh1, sc1, g1, sh2, sc2, g2 = mods[0]
    nm0, nf0 = _row(p["norm_mix"][0]), _row(p["norm_ffn"][0])
    h0 = _norm_mod_fwd(x, nm0, sc1, sh1, "norm_mix_fwd_0")
    slabs = _mm_core(
        "mm_even_in", h0, w("even_in", h0), dims=NN, grid=(s // tm, N_SLABS, 1),
        a_spec=pl.BlockSpec((tm, d), lambda i, j, k: (i, 0)), b_spec=pl.BlockSpec((d, ATTN_W), lambda i, j, k: (0, j)),
        out_shapes=[jax.ShapeDtypeStruct((N_SLABS, s, ATTN_W), BF16)],
        out_specs=[pl.BlockSpec((None, tm, ATTN_W), lambda i, j, k: (j, i, 0))], acc_shape=None)[0]
    outs, lses = zip(*[_attn_fwd(slabs, gi, f"attn_fwd_{gi}") for gi in range(3)])
    attn, lse_tot = _attn_merge(outs, lses, "attn_merge")
    pool_scale = _row(p["pool_scale"])
    ap, pool_d = _pool_fwd(slabs, attn, p["pool_w"], pool_scale, "pool_fwd")
    x1, y0 = _mm_resid("mm_even_out", ap, w("even_out", ap), x, g1, *MM)
    x2, ffn0 = _ffn_fwd(x1, nf0, sc2, sh2, g2, w, "0")

    sh1b, sc1b, g1b, sh2b, sc2b, g2b = mods[1]
    nm1, nf1 = _row(p["norm_mix"][1]), _row(p["norm_ffn"][1])
    h1 = _norm_mod_fwd(x2, nm1, sc1b, sh1b, "norm_mix_fwd_1")
    z = _mm_nn("mm_ssm_z", h1, w("ssm_z", h1), *MM)[0]
    xbc_raw = _mm_nn("mm_ssm_xbc", h1, w("ssm_xbc", h1), *MM)[0]
    dt_raw = _mm_nn("mm_ssm_dt", h1, w("ssm_dt", h1), tm, LANES, tk, out_dtypes=(F32,))[0]
    xbc = _conv_fwd(xbc_raw, p["conv_w"], _row(p["conv_b"]), "conv_fwd")
    y_ssd, hin = _ssd_fwd(xbc, dt_raw, p["dt_bias"], p["a_log"], p["d_e"], "ssd_fwd")
    ssm_norm = _row(p["ssm_norm"])
    yn = _gated_norm_fwd(y_ssd, z, ssm_norm, "gated_norm_fwd")
    x3, y1 = _mm_resid("mm_ssm_out", yn, w("ssm_out", yn), x2, g1b, *MM)
    x4, ffn1 = _ffn_fwd(x3, nf1, sc2b, sh2b, g2b, w, "1")

    dx4, sq, dfn = _final_loss(x4, _row(p["final_norm"]), target, "final_loss")
    loss_share = (0.5 / d) * jnp.sum(sq)
    g["final_norm"] = _sum8(dfn)

    dx3, (dsh2b, dsc2b, dg2b, dnf1) = _ffn_bwd(dx4, x3, ffn1, nf1, sc2b, g2b, w, sink, "1")
    dy1, dg1b = _gate_bwd(dx3, y1, g1b, "gate_bwd_ssm")
    dyn = _mm_nt("mm_ssm_out_bwd", dy1, w("ssm_out"), *MM, out_dtypes=(F32,))[0]
    tok = sink("ssm_out", _mm_tn("mm_ssm_out_dw", yn, dy1, *MM_TN), "rows")
    dy_ssd, dz, dgn = _gated_norm_bwd(y_ssd, z, dyn, ssm_norm + tok, "gated_norm_bwd")
    g["ssm_norm"] = _sum8(dgn)
    dact, ddt, da_acc, db_acc, dd_acc = _ssd_bwd(xbc, dt_raw, p["dt_bias"], p["a_log"], p["d_e"], hin, dy_ssd, "ssd_bwd")
    g["dt_bias"] = _sum8(db_acc)[:SSM_HEADS]
    g["a_log"] = _sum8(da_acc)[:SSM_HEADS] * (-jnp.exp(p["a_log"][0, :SSM_HEADS]))
    g["ssm_d"] = jnp.sum(_sum8(dd_acc).reshape(SSM_HEADS, HEAD_DIM), axis=1)
    dxbc, dcw, dcb = _conv_bwd(xbc_raw, dact, p["conv_w"], _row(p["conv_b"]), "conv_bwd")
    g["conv_w"] = jnp.sum(dcw, axis=1)
    g["conv_b"] = _sum8(dcb)
    add_spec = (_tile_spec(tm, tn),)
    dh1 = _mm_nt("mm_ssm_z_bwd", dz, w("ssm_z"), *MM, out_dtypes=(F32,))[0]
    dh1 = _mm_nt("mm_ssm_xbc_bwd", dxbc, w("ssm_xbc"), *MM, out_dtypes=(F32,), extras=(dh1,), extra_specs=add_spec,
                 epilogue=_add_epilogue)[0]
    dh1 = _mm_nt("mm_ssm_dt_bwd", ddt, w("ssm_dt"), *MM, out_dtypes=(F32,), extras=(dh1,), extra_specs=add_spec,
                 epilogue=_add_epilogue)[0]
    tok = sink("ssm_in", jnp.concatenate(
        [_mm_tn("mm_ssm_z_dw", h1, dz, *MM_TN), _mm_tn("mm_ssm_xbc_dw", h1, dxbc, *MM_TN),
         _mm_tn("mm_ssm_dt_dw", h1, ddt, MM_TN[0], LANES, MM_TN[2])[:, :SSM_HEADS]], axis=1), "cols")
    dx2, dsh1b, dsc1b, dnm1 = _norm_mod_bwd(x2, dh1, dx3, nm1 + tok, sc1b, "norm_mix_bwd_1")
    dmod1 = jnp.concatenate([_sum8(dsh1b), _sum8(dsc1b), _sum8(dg1b), dsh2b, dsc2b, dg2b])

    dx1, (dsh2, dsc2, dg2, dnf0) = _ffn_bwd(dx2, x1, ffn0, nf0, sc2, g2, w, sink, "0")
    dy0, dg1 = _gate_bwd(dx1, y0, g1, "gate_bwd_even")
    dap = _mm_nt("mm_even_out_bwd", dy0, w("even_out"), *MM)[0]
    tok = sink("even_out", _mm_tn("mm_even_out_dw", ap, dy0, *MM_TN), "cols")
    du, dpw, dpsc = _pool_bwd(dap, pool_d, p["pool_w"], pool_scale + tok, "pool_bwd")
    g["pool_w"] = dpw
    g["pool_scale"] = _sum8(dpsc)
    dqkv = [_attn_bwd(slabs, dap, ap, lse_tot, gi, f"attn_bwd_{gi}") for gi in range(3)]
    dproj = jnp.concatenate([dqkv[gi][t] for t in range(3) for gi in range(3)] + [du], axis=1)
    dh0 = _mm_nt("mm_even_in_bwd", dproj, w("even_in"), tm, tn, N_SLABS * ATTN_W // 2, out_dtypes=(F32,))[0]
    tok = sink("even_in", _mm_tn("mm_even_in_dw", dproj, h0, *MM_TN), "cols_t")
    gx, dsh1, dsc1, dnm0 = _norm_mod_bwd(x, dh0, dx1, nm0 + tok, sc1, "norm_mix_bwd_0")
    dmod0 = jnp.concatenate([_sum8(dsh1), _sum8(dsc1), _sum8(dg1), dsh2, dsc2, dg2])

    g["norm_mix"] = jnp.stack([_sum8(dnm0), _sum8(dnm1)])
    g["norm_ffn"] = jnp.stack([dnf0, dnf1])
    return loss_share, gx, g, jnp.stack([dmod0, dmod1])


SMALL_COLS = 512
SMALL_ORDER = ("dmod", "norm_mix", "norm_ffn", "pool_w", "pool_scale", "conv_w", "conv_b", "dt_bias", "a_log",
               "ssm_d", "ssm_norm", "final_norm")


def _cols_to_full(gathered):
    n, k, ns = gathered.shape
    return jnp.transpose(gathered, (1, 0, 2)).reshape(k, n * ns)


def _full_to_cols(full):
    k, n4 = full.shape
    return jnp.transpose(full.reshape(k, N_CHIPS, n4 // N_CHIPS), (1, 0, 2))


def _pack(parts, cols):
    flat = jnp.concatenate([v.reshape(-1) for v in parts])
    rows = -(-flat.shape[0] // (cols * 8)) * 8
    return jnp.pad(flat, (0, rows * cols - flat.shape[0])).reshape(rows, cols)


def _unpack(packed, shapes):
    flat, out, at = packed.reshape(-1), [], 0
    for shp in shapes:
        n = math.prod(shp)
        out.append(flat[at:at + n].reshape(shp))
        at += n
    return out


def kernel(x, c, ada_w, ada_b, norm_mix, norm_ffn, ffn_w1, ffn_w2, even_w_in, pool_w, pool_scale, even_w_out, ssm_w_in, ssm_conv_w, ssm_conv_b, ssm_dt_bias, ssm_a_log, ssm_d, ssm_norm, ssm_w_out, final_norm, loss_target, m_ada_w, m_ada_b, m_norm_mix, m_norm_ffn, m_ffn_w1, m_ffn_w2, m_even_w_in, m_pool_w, m_pool_scale, m_even_w_out, m_ssm_w_in, m_ssm_conv_w, m_ssm_conv_b, m_ssm_dt_bias, m_ssm_a_log, m_ssm_d, m_ssm_norm, m_ssm_w_out, m_final_norm, v_ada_w, v_ada_b, v_norm_mix, v_norm_ffn, v_ffn_w1, v_ffn_w2, v_even_w_in, v_pool_w, v_pool_scale, v_even_w_out, v_ssm_w_in, v_ssm_conv_w, v_ssm_conv_b, v_ssm_dt_bias, v_ssm_a_log, v_ssm_d, v_ssm_norm, v_ssm_w_out, v_final_norm):
    names = ("ada_w", "ada_b", "norm_mix", "norm_ffn", "ffn_w1", "ffn_w2", "even_w_in", "pool_w", "pool_scale",
             "even_w_out", "ssm_w_in", "ssm_conv_w", "ssm_conv_b", "ssm_dt_bias", "ssm_a_log", "ssm_d", "ssm_norm",
             "ssm_w_out", "final_norm")
    wts = dict(zip(names, (ada_w, ada_b, norm_mix, norm_ffn, ffn_w1, ffn_w2, even_w_in, pool_w, pool_scale, even_w_out,
                           ssm_w_in, ssm_conv_w, ssm_conv_b, ssm_dt_bias, ssm_a_log, ssm_d, ssm_norm, ssm_w_out, final_norm)))
    m_in = dict(zip(names, (m_ada_w, m_ada_b, m_norm_mix, m_norm_ffn, m_ffn_w1, m_ffn_w2, m_even_w_in, m_pool_w, m_pool_scale,
                            m_even_w_out, m_ssm_w_in, m_ssm_conv_w, m_ssm_conv_b, m_ssm_dt_bias, m_ssm_a_log, m_ssm_d,
                            m_ssm_norm, m_ssm_w_out, m_final_norm)))
    v_in = dict(zip(names, (v_ada_w, v_ada_b, v_norm_mix, v_norm_ffn, v_ffn_w1, v_ffn_w2, v_even_w_in, v_pool_w, v_pool_scale,
                            v_even_w_out, v_ssm_w_in, v_ssm_conv_w, v_ssm_conv_b, v_ssm_dt_bias, v_ssm_a_log, v_ssm_d,
                            v_ssm_norm, v_ssm_w_out, v_final_norm)))
    d = D_MODEL
    s = x.shape[1]
    ix, iy, ic = _place()
    chip = 2 * ix + iy
    example = 4 * ix + 2 * iy + ic

    c_all = _allgather8(c.reshape(8, d // 8), "gather_c").reshape(N_DEV, d)
    cond = c_all * jax.nn.sigmoid(c_all)
    cond16 = jnp.pad(cond, ((0, 8), (0, 0)))
    ada_cols = ada_w.shape[2]
    bias_shard = lax.dynamic_slice_in_dim(ada_b, chip * ada_cols, ada_cols, axis=1)
    mod_parts = [
        _mm_nn(f"mm_ada_{l}", cond16, ada_w[l], 16, 512, d, out_dtypes=(F32,), extras=(_row(bias_shard[l]),),
               extra_specs=(pl.BlockSpec((1, 512), lambda i, j, k: (0, j)),), epilogue=_add_epilogue)[0][:8]
        for l in range(2)]
    mod_all = _allgather8(jnp.concatenate(mod_parts, axis=0), "gather_mod").reshape(N_CHIPS, 2, 2, 8, ada_cols)
    mod_mine = lax.dynamic_index_in_dim(mod_all[:, 0], example, axis=2, keepdims=False)
    mod = jnp.transpose(mod_mine, (1, 0, 2)).reshape(2, N_CHIPS * ada_cols)

    gathered = _GatheredWeights({"even_in": even_w_in, "even_out": even_w_out, "ffn_w1": ffn_w1, "ffn_w2": ffn_w2,
                                 "ssm_in": ssm_w_in, "ssm_out": ssm_w_out})
    mod = mod + gathered.token
    n_xbc = SSM_INNER + SSM_CONV_DIM
    cache = {}

    def full_weight(key, after=None):
        if key in cache:
            return cache[key]
        if key.startswith("w1_"):
            cache[key] = _cols_to_full(gathered.take("ffn_w1", after)[:, int(key[3:])])
        elif key.startswith("w2_"):
            cache[key] = gathered.take("ffn_w2", after)[:, int(key[3:])].reshape(FFN_HIDDEN, d)
        elif key in ("even_in", "even_out"):
            cache[key] = _cols_to_full(gathered.take(key, after)[:, 0])
        elif key == "ssm_out":
            cache[key] = gathered.take(key, after)[:, 0].reshape(SSM_INNER, d)
        else:
            whole = _cols_to_full(gathered.take("ssm_in", after)[:, 0])
            cache["ssm_z"] = whole[:, :SSM_INNER]
            cache["ssm_xbc"] = whole[:, SSM_INNER:n_xbc]
            cache["ssm_dt"] = jnp.pad(whole[:, n_xbc:], ((0, 0), (0, LANES - SSM_HEADS)))
        return cache[key]

    exchange = _GradientExchange()

    def sink(name, grad, layout):
        if layout == "rows":
            slots = grad.reshape(N_CHIPS, -1, grad.shape[-1])
        elif layout == "cols":
            slots = _full_to_cols(grad)
        else:
            slots = jnp.transpose(grad.reshape(N_CHIPS, -1, grad.shape[-1]), (0, 2, 1))
        return exchange.put(name, slots)

    pad_h = ((0, 0), (0, LANES - SSM_HEADS))
    cw, nw = ssm_conv_w.shape[2], ssm_norm.shape[1]
    sm = jnp.concatenate([ssm_conv_w[0].reshape(-1), ssm_conv_b.reshape(-1), jnp.pad(ssm_norm[0], (0, cw - nw)),
                          jnp.zeros((2 * cw,), F32)]).reshape(8, cw)
    sm_all = _allgather8(sm, "gather_ssm_small").reshape(N_CHIPS, 2, 8, cw)[:, 0]
    small = {
        "norm_mix": norm_mix, "norm_ffn": norm_ffn, "pool_w": pool_w[0], "pool_scale": pool_scale[0],
        "final_norm": final_norm,
        "conv_w": jnp.transpose(sm_all[:, :CONV_TAPS], (1, 0, 2)).reshape(CONV_TAPS, N_CHIPS * cw),
        "conv_b": sm_all[:, CONV_TAPS].reshape(N_CHIPS * cw),
        "ssm_norm": sm_all[:, CONV_TAPS + 1, :nw].reshape(N_CHIPS * nw),
        "dt_bias": jnp.pad(ssm_dt_bias, pad_h), "a_log": jnp.pad(ssm_a_log, pad_h),
        "d_e": jnp.repeat(ssm_d[0], HEAD_DIM).reshape(1, SSM_INNER),
    }

    loss_share, gx, g, dmod = _local_step(x[0], loss_target[0], mod, full_weight, small, sink)
    loss = lax.psum(loss_share, ("x", "y", "c"))
    reduced = exchange.finish(gx)

    g["dmod"] = dmod
    small_shapes = [g[k].shape for k in SMALL_ORDER]
    packed = _pack([g[k] for k in SMALL_ORDER], SMALL_COLS)
    every = _allgather8(packed, "gather_small").reshape(N_DEV, *packed.shape)
    summed = dict(zip(SMALL_ORDER, _unpack(_sum_devices(every, "sum_small"), small_shapes)))
    dmod_all = every.reshape(N_DEV, -1)[:, :2 * 6 * d].reshape(N_DEV, 2, 6 * d)

    grads = {}
    dmod_shard = lax.dynamic_slice_in_dim(dmod_all, chip * ada_cols, ada_cols, axis=2)
    grads["ada_w"] = jnp.stack([
        _mm_tn(f"mm_ada_dw_{l}", cond16, jnp.pad(dmod_shard[:, l], ((0, 8), (0, 0))), 1024, 512, 16, out_dtype=F32)
        for l in range(2)])
    grads["ada_b"] = summed["dmod"]
    grads["norm_mix"] = summed["norm_mix"]
    grads["norm_ffn"] = summed["norm_ffn"]
    grads["pool_w"] = summed["pool_w"][None]
    grads["pool_scale"] = summed["pool_scale"][None]
    grads["ssm_conv_w"] =lax.dynamic_slice_in_dim(summed["conv_w"], chip * cw, cw, axis=1)[None]
    grads["ssm_conv_b"] = lax.dynamic_slice_in_dim(summed["conv_b"], chip * cw, cw, axis=0)[None]
    grads["ssm_dt_bias"] = summed["dt_bias"][None]
    grads["ssm_a_log"] = summed["a_log"][None]
    grads["ssm_d"] = summed["ssm_d"][None]
    grads["ssm_norm"] =lax.dynamic_slice_in_dim(summed["ssm_norm"], chip * nw, nw, axis=0)[None]
    grads["final_norm"] = summed["final_norm"]

    grads["ffn_w1"] = jnp.stack([reduced["ffn_w1_0"], reduced["ffn_w1_1"]])
    grads["ffn_w2"] = jnp.stack([reduced["ffn_w2_0"], reduced["ffn_w2_1"]])
    grads["even_w_in"] = reduced["even_in"][None]
    grads["even_w_out"] = reduced["even_out"][None]
    grads["ssm_w_in"] = reduced["ssm_in"][None]
    grads["ssm_w_out"] = reduced["ssm_out"][None]

    big = ("ada_w", "ffn_w1", "ffn_w2", "even_w_in", "even_w_out", "ssm_w_in", "ssm_w_out")
    delta, new_m, new_v = {}, {}, {}
    for k in big:
        shp = wts[k].shape
        two_d = lambda t: t.reshape(-1, shp[-1])
        res = _adamw(two_d(wts[k]), two_d(grads[k]), two_d(m_in[k]), two_d(v_in[k]), f"adamw_{k}")
        delta[k], new_m[k], new_v[k] = [t.reshape(shp) for t in res]
    little = [k for k in names if k not in big]
    shapes = [wts[k].shape for k in little]
    res = _adamw(*[_pack([src[k] for k in little], LANES) for src in (wts, grads, m_in, v_in)], "adamw_small")
    for out, packed_out in zip((delta, new_m, new_v), res):
        out.update(zip(little, _unpack(packed_out, shapes)))

    return (loss, gx[None], *[grads[k] for k in names], *[delta[k] for k in names],
            *[new_m[k] for k in names], *[new_v[k] for k in names])
```

```python
import functools
import math

import jax
import jax.numpy as jnp
from jax import lax
from jax.experimental import pallas as pl
from jax.experimental.pallas import tpu as pltpu

F32 = jnp.float32
BF16 = jnp.bfloat16
MESH = pl.DeviceIdType.MESH
ANY = pl.BlockSpec(memory_space=pl.ANY)
VMEM_FULL = pl.BlockSpec(memory_space=pltpu.VMEM)

NORM_EPS = 1e-6
N_CHIPS = 4
N_DEV = 8
LANES = 128
VMEM_LIMIT = 56 << 20

D_MODEL = 2048
ATTN_GROUPS = ((128, 1), (512, 4), (2048, 16))
ATTN_BLOCK = 128
ATTN_W = 512
HEAD_DIM = 64
POOL_WINDOWS = (2, 4, 8, 16)
POOL_W = 512
N_SLABS = 10
SSM_INNER = 4096
SSM_HEADS = 64
SSM_GROUPS = 8
SSM_STATE = 128
SSM_CHUNK = 128
SSM_CONV_DIM = 6144
SSM_GW = SSM_INNER // SSM_GROUPS
FFN_HIDDEN = 8192

ADAM_LR, ADAM_B1, ADAM_B2, ADAM_EPS, ADAM_WD, ADAM_STEP = 0.001, 0.9, 0.999, 1e-08, 0.01, 10


def _cparams(sem=None):
    return pltpu.CompilerParams(dimension_semantics=sem, vmem_limit_bytes=VMEM_LIMIT)


def _place():
    return lax.axis_index("x"), lax.axis_index("y"), lax.axis_index("c")


def _chip_index():
    return 2 * lax.axis_index("x") + lax.axis_index("y")


def _allgather8(v, name):
    m_per, n = v.shape

    def body(x_ref, out_ref, send_sems, recv_sems, local_sem):
        x, y, c = _place()
        me, sibling = (x, y, c), (x, y, 1 - c)
        chips = [(1 - x, y), (x, 1 - y), (1 - x, 1 - y)]

        def rows(px, py, pc):
            return out_ref.at[pl.ds((4 * px + 2 * py + pc) * m_per, m_per), :]

        def copy(k, block, to, src=None):
            return pltpu.make_async_remote_copy(
                src_ref=rows(*block) if src is None else src, dst_ref=rows(*block),
                send_sem=send_sems.at[k], recv_sem=recv_sems.at[k], device_id=to, device_id_type=MESH)

        mine = pltpu.make_async_copy(x_ref, rows(*me), local_sem)
        mine.start()
        first = [copy(0, me, sibling, src=x_ref)]
        first += [copy(1 + j, me, (*chip, c), src=x_ref) for j, chip in enumerate(chips)]
        for cp in first:
            cp.start()
        passed = [copy(4 + j, (*chip, c), sibling) for j, chip in enumerate(chips)]
        for j, chip in enumerate(chips):
            copy(1 + j, (*chip, c), me).wait_recv()
            passed[j].start()
        copy(0, sibling, me).wait_recv()
        for j, chip in enumerate(chips):
            copy(4 + j, (*chip, 1 - c), me).wait_recv()
        for cp in first + passed:
            cp.wait_send()
        mine.wait()

    return pl.pallas_call(
        body, name=name,
        out_shape=jax.ShapeDtypeStruct((N_DEV * m_per, n), v.dtype),
        in_specs=[VMEM_FULL], out_specs=VMEM_FULL,
        scratch_shapes=[pltpu.SemaphoreType.DMA((7,)), pltpu.SemaphoreType.DMA((7,)), pltpu.SemaphoreType.DMA],
    )(v)


def _sibling_send_other_half(g, name):
    n, r, ccols = g.shape
    h = r // 2

    def body(g_ref, out_ref, send_sem, recv_sem):
        x, y, c = _place()
        cp = pltpu.make_async_remote_copy(
            src_ref=g_ref.at[:, pl.ds((1 - c) * h, h), :], dst_ref=out_ref, send_sem=send_sem, recv_sem=recv_sem,
            device_id=(x, y, 1 - c), device_id_type=MESH)
        cp.start()
        cp.wait()

    return pl.pallas_call(
        body, name=name, out_shape=jax.ShapeDtypeStruct((n, h, ccols), g.dtype),
        in_specs=[ANY], out_specs=ANY,
        scratch_shapes=[pltpu.SemaphoreType.DMA, pltpu.SemaphoreType.DMA],
    )(g)


def _sibling_complete(f, name):
    h, ccols = f.shape

    def body(f_ref, out_ref, send_sem, recv_sem):
        x, y, c = _place()
        cp = pltpu.make_async_remote_copy(src_ref=f_ref, dst_ref=out_ref, send_sem=send_sem, recv_sem=recv_sem,
                                          device_id=(x, y, 1 - c), device_id_type=MESH)
        cp.start()
        cp.wait()

    other = pl.pallas_call(
        body, name=name, out_shape=jax.ShapeDtypeStruct((h, ccols), f.dtype),
        in_specs=[ANY], out_specs=ANY,
        scratch_shapes=[pltpu.SemaphoreType.DMA, pltpu.SemaphoreType.DMA],
    )(f)
    south = lax.axis_index("c") == 0
    return jnp.concatenate([jnp.where(south, f, other), jnp.where(south, other, f)], axis=0)


HBM_SPEC = pl.BlockSpec(memory_space=pltpu.HBM)
SEM_SPEC = pl.BlockSpec(memory_space=pltpu.SEMAPHORE)
N_PEERS = 3


def _split_params():
    return pltpu.CompilerParams(has_side_effects=pltpu.SideEffectType.DATAFLOW_SIDE_EFFECTING)


def _chip_exchange(mode, src_ref, land_ref, send_sems, recv_sems):
    x, y, c = _place()
    k_me = 2 * x + y
    pairs = []
    for j, chip in enumerate([(1 - x, y), (x, 1 - y), (1 - x, 1 - y)]):
        k_j = 2 * chip[0] + chip[1]
        if mode == "halves":
            h = src_ref.shape[0] // 2
            rows = pl.ds(c * h, h)
            src, there, here = src_ref.at[rows, :], land_ref.at[k_me, rows, :], land_ref.at[k_j, rows, :]
        else:
            src, there, here = src_ref.at[k_j], land_ref.at[k_me], land_ref.at[k_j]
        pairs.append(tuple(
            pltpu.make_async_remote_copy(src_ref=src, dst_ref=dst, send_sem=send_sems[j], recv_sem=recv_sems[j],
                                         device_id=(*chip, c), device_id_type=MESH) for dst in (there, here)))
    return pairs


def _exchange_start(src, land_shape, mode, name):
    def body(src_ref, land_ref, s0, s1, s2, r0, r1, r2, src_thru, land_thru, token):
        for start, _ in _chip_exchange(mode, src_ref, land_ref, (s0, s1, s2), (r0, r1, r2)):
            start.start()
        token[...] = jnp.zeros_like(token)

    sem = pltpu.SemaphoreType.DMA(())
    res = pl.pallas_call(
        body, name=name,
        out_shape=(sem,) * (2 * N_PEERS) + (pltpu.HBM(src.shape, src.dtype), pltpu.HBM(land_shape, src.dtype),
                                           jax.ShapeDtypeStruct((8, LANES), F32)),
        in_specs=(HBM_SPEC, HBM_SPEC), out_specs=(SEM_SPEC,) * (2 * N_PEERS) + (HBM_SPEC, HBM_SPEC, VMEM_FULL),
        input_output_aliases={0: 2 * N_PEERS, 1: 2 * N_PEERS + 1}, compiler_params=_split_params(),
    )(pltpu.with_memory_space_constraint(src, pltpu.HBM),
      pltpu.with_memory_space_constraint(lax.empty(land_shape, src.dtype), pltpu.HBM))
    return res[:2 * N_PEERS], res[2 * N_PEERS], res[2 * N_PEERS + 1], res[2 * N_PEERS + 2]


def _exchange_wait(sems, src_thru, land_thru, after, mode, name):
    def body(src_ref, land_ref, s0, s1, s2, r0, r1, r2, after_ref, src_out, land_out):
        waits = [w for _, w in _chip_exchange(mode, src_ref, land_ref, (s0, s1, s2), (r0, r1, r2))]
        for w in waits:
            w.wait_send()
        for w in waits:
            w.wait_recv()

    return pl.pallas_call(
        body, name=name,
        out_shape=(pltpu.HBM(src_thru.shape, src_thru.dtype), pltpu.HBM(land_thru.shape, land_thru.dtype)),
        in_specs=(HBM_SPEC, HBM_SPEC) + (SEM_SPEC,) * (2 * N_PEERS) + (ANY,), out_specs=(HBM_SPEC, HBM_SPEC),
        input_output_aliases={0: 0, 1: 1}, compiler_params=_split_params(),
    )(src_thru, land_thru, *sems, after)


def _sibling_fill(land, name):
    n, r, ccols = land.shape
    h = r // 2

    def body(land_ref, out_ref, send_sems, recv_sems):
        x, y, c = _place()
        slots = [2 * (1 - x) + y, 2 * x + (1 - y), 2 * (1 - x) + (1 - y)]

        def copy(j, rows):
            return pltpu.make_async_remote_copy(
                src_ref=land_ref.at[slots[j], pl.ds(c * h, h), :], dst_ref=out_ref.at[slots[j], rows, :],
                send_sem=send_sems.at[j], recv_sem=recv_sems.at[j], device_id=(x, y, 1 - c), device_id_type=MESH)

        sends = [copy(j, pl.ds(c * h, h)) for j in range(N_PEERS)]
        for cp in sends:
            cp.start()
        for j in range(N_PEERS):
            copy(j, pl.ds((1 - c) * h, h)).wait_recv()
        for cp in sends:
            cp.wait_send()

    return pl.pallas_call(
        body, name=name, out_shape=jax.ShapeDtypeStruct(land.shape, land.dtype), in_specs=[ANY], out_specs=ANY,
        input_output_aliases={0: 0},
        scratch_shapes=[pltpu.SemaphoreType.DMA((N_PEERS,)), pltpu.SemaphoreType.DMA((N_PEERS,))],
    )(land)


class _GatheredWeights:
    def __init__(self, shards):
        self._pending, self._done = {}, {}
        token = jnp.zeros((), BF16)
        for name, shard in shards.items():
            flat = shard.reshape(-1, shard.shape[-1]).astype(BF16) + token
            sems, thru, land, tok = _exchange_start(flat, (N_CHIPS, *flat.shape), "halves", f"gather_start_{name}")
            self._pending[name] = (sems, thru, land, shard.shape)
            token = tok[0, 0].astype(BF16)
        self.token = tok[0, 0]

    def take(self, name, after):
        if name not in self._done:
            sems, thru, land, shape = self._pending.pop(name)
            flat, land = _exchange_wait(sems, thru, land, after, "halves", f"gather_wait_{name}")
            land = _sibling_fill(land, f"gather_fill_{name}")
            land = lax.dynamic_update_index_in_dim(land, flat, _chip_index(), 0)
            self._done[name] = land.reshape(N_CHIPS, *shape)
        return self._done[name]


class _GradientExchange:
    def __init__(self):
        self._pending = []

    def put(self, name, slots):
        recv = _sibling_send_other_half(slots, f"rs_pair_{name}")
        part = _add_own_half(slots, recv, f"rs_add2_{name}")
        sems, thru, land, tok = _exchange_start(part, part.shape, "slots", f"rs_start_{name}")
        self._pending.append((name, sems, thru, land))
        return tok[0, 0]

    def finish(self, after):
        out = {}
        k_me = _chip_index()
        for name, sems, thru, land in self._pending:
            part, land = _exchange_wait(sems, thru, land, after, "slots", f"rs_wait_{name}")
            got = lax.dynamic_update_index_in_dim(land, lax.dynamic_index_in_dim(part, k_me, 0, keepdims=False), k_me, 0)
            out[name] = _sibling_complete(_sum_slots(got, f"rs_add4_{name}"), f"rs_fin_{name}")
        return out


def _row_tile(rows, cols, itemsize, budget=2 << 20):
    t = rows
    while t % 2 == 0 and t * cols * itemsize > budget and (t // 2) % 16 == 0:
        t //= 2
    return t


def _add_own_half(g, recv, name):
    n, r, ccols = g.shape
    h = r // 2
    t = _row_tile(h, ccols, 4)
    nt = h // t
    c_idx = lax.axis_index("c").astype(jnp.int32).reshape(1)

    def body(c_ref, g_ref, r_ref, o_ref):
        o_ref[...] = (g_ref[...].astype(F32) + r_ref[...].astype(F32)).astype(o_ref.dtype)

    grid_spec = pltpu.PrefetchScalarGridSpec(
        num_scalar_prefetch=1, grid=(n, nt),
        in_specs=[pl.BlockSpec((None, t, ccols), lambda j, i, c_ref: (j, c_ref[0] * nt + i, 0)),
                  pl.BlockSpec((None, t, ccols), lambda j, i, c_ref: (j, i, 0))],
        out_specs=pl.BlockSpec((None, t, ccols), lambda j, i, c_ref: (j, i, 0)))
    return pl.pallas_call(body, name=name, grid_spec=grid_spec,
                          out_shape=jax.ShapeDtypeStruct((n, h, ccols), BF16),
                          compiler_params=_cparams(("parallel", "parallel")))(c_idx, g, recv)


def _sum_slots(q, name):
    n, h, ccols = q.shape
    t = _row_tile(h, ccols, 4)

    def body(q_ref, o_ref):
        acc = q_ref[0].astype(F32)
        for j in range(1, n):
            acc = acc + q_ref[j].astype(F32)
        o_ref[...] = acc

    return pl.pallas_call(
        body, name=name, grid=(h // t,),
        in_specs=[pl.BlockSpec((n, t, ccols), lambda i: (0, i, 0))],
        out_specs=pl.BlockSpec((t, ccols), lambda i: (i, 0)),
        out_shape=jax.ShapeDtypeStruct((h, ccols), F32), compiler_params=_cparams(("parallel",)))(q)


def _sum_devices(v, name):
    n, r, ccols = v.shape
    t = 8
    while r % (t * 2) == 0 and t * 2 * ccols * 4 * n <= (8 << 20):
        t *= 2

    def body(v_ref, o_ref):
        acc = v_ref[0]
        for j in range(1, n):
            acc = acc + v_ref[j]
        o_ref[...] = acc

    return pl.pallas_call(
        body, name=name, grid=(r // t,),
        in_specs=[pl.BlockSpec((n, t, ccols), lambda i: (0, i, 0))],
        out_specs=pl.BlockSpec((t, ccols), lambda i: (i, 0)),
        out_shape=jax.ShapeDtypeStruct((r, ccols), F32), compiler_params=_cparams(("parallel",)))(v)


def _adamw(w, g, m, v, name):
    r, ccols = w.shape
    t = _row_tile(r, ccols, 4, budget=1 << 20)
    c1 = 1.0 / (1.0 - ADAM_B1 ** ADAM_STEP)
    c2 = 1.0 / (1.0 - ADAM_B2 ** ADAM_STEP)

    def body(w_ref, g_ref, m_ref, v_ref, d_ref, nm_ref, nv_ref):
        gg = g_ref[...]
        nm = ADAM_B1 * m_ref[...] + (1.0 - ADAM_B1) * gg
        nv = ADAM_B2 * v_ref[...] + (1.0 - ADAM_B2) * (gg * gg)
        d_ref[...] = -ADAM_LR * ((nm * c1) / (jnp.sqrt(nv * c2) + ADAM_EPS) + ADAM_WD * w_ref[...])
        nm_ref[...] = nm
        nv_ref[...] = nv

    spec = pl.BlockSpec((t, ccols), lambda i: (i, 0))
    sds = jax.ShapeDtypeStruct((r, ccols), F32)
    return pl.pallas_call(body, name=name, grid=(r // t,), in_specs=[spec] * 4, out_specs=[spec] * 3,
                          out_shape=[sds] * 3, compiler_params=_cparams(("parallel",)))(w, g, m, v)


NN = (((1,), (0,)), ((), ()))
NT = (((1,), (1,)), ((), ()))
TN = (((0,), (0,)), ((), ()))


def _mm_core(name, a, b, *, dims, grid, a_spec, b_spec, out_shapes, out_specs, acc_shape,
             extras=(), extra_specs=(), epilogue=None):
    nk = grid[2]
    n_ex, n_out = len(extras), len(out_shapes)
    if epilogue is None:
        def epilogue(acc, ex, outs):
            outs[0][...] = acc.astype(outs[0].dtype)

    def body(*refs):
        a_ref, b_ref = refs[0], refs[1]
        ex = refs[2:2 + n_ex]
        outs = refs[2 + n_ex:2 + n_ex + n_out]
        part = lax.dot_general(a_ref[...].astype(BF16), b_ref[...].astype(BF16), dims, preferred_element_type=F32)
        if nk == 1:
            epilogue(part, ex, outs)
        else:
            acc = refs[-1]
            k = pl.program_id(2)

            @pl.when(k == 0)
            def _():
                acc[...] = part

            @pl.when(k > 0)
            def _():
                acc[...] += part

            @pl.when(k == nk - 1)
            def _():
                epilogue(acc[...], ex, outs)

    scratch = [] if nk == 1 else [pltpu.VMEM(acc_shape, F32)]
    res = pl.pallas_call(
        body, name=name, grid=grid, in_specs=[a_spec, b_spec, *extra_specs], out_specs=list(out_specs),
        out_shape=list(out_shapes), scratch_shapes=scratch,
        compiler_params=_cparams(("parallel", "parallel", "arbitrary")))(a, b, *extras)
    return res


def _mm_nn(name, a, b, tm, tn, tk, out_dtypes=(BF16,), extras=(), extra_specs=(), epilogue=None, b_tile=None, n=None):
    m, kk = a.shape
    n = b.shape[1] if b_tile is None else n
    tm, tn, tk = min(tm, m), min(tn, n), min(tk, kk)
    grid = (m // tm, n // tn, kk // tk)
    if b_tile is None:
        b_spec = pl.BlockSpec((tk, tn), lambda i, j, k: (k, j))
    else:
        b_spec = pl.BlockSpec((None,) * (b.ndim - 2) + (tk, tn), lambda i, j, k: b_tile(j, k))
    return _mm_core(
        name, a, b, dims=NN, grid=grid,
        a_spec=pl.BlockSpec((tm, tk), lambda i, j, k: (i, k)), b_spec=b_spec,
        out_shapes=[jax.ShapeDtypeStruct((m, n), dt) for dt in out_dtypes],
        out_specs=[pl.BlockSpec((tm, tn), lambda i, j, k: (i, j)) for _ in out_dtypes],
        acc_shape=(tm, tn), extras=extras, extra_specs=extra_specs, epilogue=epilogue)


def _mm_nt(name, a, b, tm, tn, tk, out_dtypes=(BF16,), extras=(), extra_specs=(), epilogue=None, b_tile=None, n=None):
    m, kk = a.shape
    n = b.shape[0] if b_tile is None else n
    tm, tn, tk = min(tm, m), min(tn, n), min(tk, kk)
    grid = (m // tm, n // tn, kk // tk)
    if b_tile is None:
        b_spec = pl.BlockSpec((tn, tk), lambda i, j, k: (j, k))
    else:
        b_spec = pl.BlockSpec((None,) * (b.ndim - 2) + (tn, tk), lambda i, j, k: b_tile(j, k))
    return _mm_core(
        name, a, b, dims=NT, grid=grid,
        a_spec=pl.BlockSpec((tm, tk), lambda i, j, k: (i, k)), b_spec=b_spec,
        out_shapes=[jax.ShapeDtypeStruct((m, n), dt) for dt in out_dtypes],
        out_specs=[pl.BlockSpec((tm, tn), lambda i, j, k: (i, j)) for _ in out_dtypes],
        acc_shape=(tm, tn), extras=extras, extra_specs=extra_specs, epilogue=epilogue)


def _mm_tn(name, a, b, tm, tn, tk, out_dtype=BF16, col_chunks=1):
    kk, m = a.shape
    n = b.shape[1]
    tm, tn, tk = min(tm, m), min(tn, n), min(tk, kk)
    grid = (m // tm, n // tn, kk // tk)
    if col_chunks == 1:
        out_shape, out_spec = (m, n), pl.BlockSpec((tm, tn), lambda i, j, k: (i, j))
    else:
        assert tn * col_chunks == n
        out_shape, out_spec = (col_chunks, m, tn), pl.BlockSpec((None, tm, tn), lambda i, j, k: (j, i, 0))
    return _mm_core(
        name, a, b, dims=TN, grid=grid,
        a_spec=pl.BlockSpec((tk, tm), lambda i, j, k: (k, i)), b_spec=pl.BlockSpec((tk, tn), lambda i, j, k: (k, j)),
        out_shapes=[jax.ShapeDtypeStruct(out_shape, out_dtype)], out_specs=[out_spec], acc_shape=(tm, tn))[0]


def _resid_gate_epilogue(acc, ex, outs):
    outs[0][...] = ex[0][...] + ex[1][...] * acc
    outs[1][...] = acc.astype(BF16)


def _mm_resid(name, a, b, resid, gate, tm, tn, tk, b_tile=None, n=None):
    return _mm_nn(
        name, a, b, tm, tn, tk, out_dtypes=(F32, BF16), extras=(resid, gate),
        extra_specs=(pl.BlockSpec((tm, tn), lambda i, j, k: (i, j)), pl.BlockSpec((1, tn), lambda i, j, k: (0, j))),
        epilogue=_resid_gate_epilogue, b_tile=b_tile, n=n)


TOK_TILE = 512


def _colsum8(v):
    t, ccols = v.shape
    return jnp.sum(v.reshape(t // 8, 8, ccols), axis=0)


def _norm_mod_fwd(x, g, sc, sh, name):
    s, d = x.shape
    t = TOK_TILE

    def body(x_ref, g_ref, sc_ref, sh_ref, h_ref):
        xv = x_ref[...]
        n = xv * lax.rsqrt(jnp.mean(xv * xv, axis=-1, keepdims=True) + NORM_EPS)
        h_ref[...] = ((n * g_ref[...]) * (1.0 + sc_ref[...]) + sh_ref[...]).astype(BF16)

    row = pl.BlockSpec((1, d), lambda i: (0, 0))
    return pl.pallas_call(
        body, name=name, grid=(s // t,), in_specs=[pl.BlockSpec((t, d), lambda i: (i, 0)), row, row, row],
        out_specs=pl.BlockSpec((t, d), lambda i: (i, 0)), out_shape=jax.ShapeDtypeStruct((s, d), BF16),
        compiler_params=_cparams(("parallel",)))(x, g, sc, sh)


def _norm_mod_bwd(x, dh, resid, g, sc, name):
    s, d = x.shape
    t = TOK_TILE

    def body(x_ref, dh_ref, r_ref, g_ref, sc_ref, dx_ref, dsh_ref, dsc_ref, dg_ref):
        i = pl.program_id(0)
        xv = x_ref[...]
        rstd = lax.rsqrt(jnp.mean(xv * xv, axis=-1, keepdims=True) + NORM_EPS)
        n = xv * rstd
        dhv = dh_ref[...].astype(F32)
        gv = g_ref[...]
        dyn = dhv * (1.0 + sc_ref[...])
        dn = dyn * gv
        dx_ref[...] = r_ref[...] + rstd * (dn - n * jnp.mean(dn * n, axis=-1, keepdims=True))

        @pl.when(i == 0)
        def _():
            dsh_ref[...] = jnp.zeros_like(dsh_ref)
            dsc_ref[...] = jnp.zeros_like(dsc_ref)
            dg_ref[...] = jnp.zeros_like(dg_ref)

        dsh_ref[...] += _colsum8(dhv)
        dsc_ref[...] += _colsum8(dhv * (n * gv))
        dg_ref[...] += _colsum8(dyn * n)

    tile = pl.BlockSpec((t, d), lambda i: (i, 0))
    row = pl.BlockSpec((1, d), lambda i: (0, 0))
    acc = pl.BlockSpec((8, d), lambda i: (0, 0))
    acc_s = jax.ShapeDtypeStruct((8, d), F32)
    return pl.pallas_call(
        body, name=name, grid=(s // t,), in_specs=[tile, tile, tile, row, row],
        out_specs=[tile, acc, acc, acc], out_shape=[jax.ShapeDtypeStruct((s, d), F32), acc_s, acc_s, acc_s],
        compiler_params=_cparams(("arbitrary",)))(x, dh, resid, g, sc)


def _gate_bwd(dxo, y, gate, name):
    s, d = dxo.shape
    t = TOK_TILE

    def body(dx_ref, y_ref, g_ref, dy_ref, dg_ref):
        i = pl.program_id(0)
        dxv = dx_ref[...]
        dy_ref[...] = (dxv * g_ref[...]).astype(BF16)

        @pl.when(i == 0)
        def _():
            dg_ref[...] = jnp.zeros_like(dg_ref)

        dg_ref[...] += _colsum8(dxv * y_ref[...].astype(F32))

    tile = pl.BlockSpec((t, d), lambda i: (i, 0))
    return pl.pallas_call(
        body, name=name, grid=(s // t,), in_specs=[tile, tile, pl.BlockSpec((1, d), lambda i: (0, 0))],
        out_specs=[tile, pl.BlockSpec((8, d), lambda i: (0, 0))],
        out_shape=[jax.ShapeDtypeStruct((s, d), BF16), jax.ShapeDtypeStruct((8, d), F32)],
        compiler_params=_cparams(("arbitrary",)))(dxo, y, gate)


def _final_loss(x, g, target, name):
    s, d = x.shape
    t = TOK_TILE
    inv_d = 1.0 / d

    def body(x_ref, g_ref, t_ref, dx_ref, sq_ref, dg_ref):
        i = pl.program_id(0)
        xv = x_ref[...]
        rstd = lax.rsqrt(jnp.mean(xv * xv, axis=-1, keepdims=True) + NORM_EPS)
        n = xv * rstd
        gv = g_ref[...]
        err = n * gv - t_ref[...]
        dout = err * inv_d
        dn = dout * gv
        dx_ref[...] = rstd * (dn - n * jnp.mean(dn * n, axis=-1, keepdims=True))

        @pl.when(i == 0)
        def _():
            sq_ref[...] = jnp.zeros_like(sq_ref)
            dg_ref[...] = jnp.zeros_like(dg_ref)

        sq_ref[...] += _colsum8(err * err)
        dg_ref[...] += _colsum8(dout * n)

    tile = pl.BlockSpec((t, d), lambda i: (i, 0))
    acc = pl.BlockSpec((8, d), lambda i: (0, 0))
    acc_s = jax.ShapeDtypeStruct((8, d), F32)
    return pl.pallas_call(
        body, name=name, grid=(s // t,), in_specs=[tile, pl.BlockSpec((1, d), lambda i: (0, 0)), tile],
        out_specs=[tile, acc, acc], out_shape=[jax.ShapeDtypeStruct((s, d), F32), acc_s, acc_s],
        compiler_params=_cparams(("arbitrary",)))(x, g, target)


NEG_BIG = -1e30


def _head_masks(rows):
    lane = lax.broadcasted_iota(jnp.int32, (rows, LANES), 1)
    return [(lane // HEAD_DIM) == hh for hh in range(2)]


def _group_view(slabs, gi):
    dil = ATTN_GROUPS[gi][1]
    _, s, w = slabs.shape
    if dil == 1:
        return slabs, (gi, 3 + gi, 6 + gi)
    return slabs[gi:9:3].reshape(3, s // dil, dil * w), (0, 1, 2)


def _attn_fwd(slabs, gi, name):
    window, dil = ATTN_GROUPS[gi]
    n_back = window // dil
    _, s, w = slabs.shape
    big_l = s // dil
    nb = big_l // ATTN_BLOCK
    blk = ATTN_BLOCK
    view, (iq, ik, iv) = _group_view(slabs, gi)
    scale = HEAD_DIM ** -0.5

    def body(q_ref, kp_ref, kc_ref, vp_ref, vc_ref, o_ref, lse_ref):
        n = pl.program_id(1)
        qi = lax.broadcasted_iota(jnp.int32, (blk, 2 * blk), 0)
        kj = lax.broadcasted_iota(jnp.int32, (blk, 2 * blk), 1)
        dist = qi + blk - kj
        ok = (dist >= 0) & (dist <= n_back) & ((kj >= blk) | (n > 0))
        mq = _head_masks(blk)
        mk = _head_masks(2 * blk)
        for p in range(w // LANES):
            cols = slice(p * LANES, (p + 1) * LANES)
            qp = q_ref[:, cols]
            k2 = jnp.concatenate([kp_ref[:, cols], kc_ref[:, cols]], axis=0)
            v2 = jnp.concatenate([vp_ref[:, cols], vc_ref[:, cols]], axis=0)
            o_pair = jnp.zeros((blk, LANES), F32)
            lse_pair = jnp.zeros((blk, LANES), F32)
            for hh in range(2):
                qm = jnp.where(mq[hh], qp, jnp.zeros_like(qp))
                sc = lax.dot_general(qm, k2, NT, preferred_element_type=F32) * scale
                sc = jnp.where(ok, sc, NEG_BIG)
                mx = jnp.max(sc, axis=-1, keepdims=True)
                pe = jnp.exp(sc - mx)
                den = jnp.sum(pe, axis=-1, keepdims=True)
                pn = (pe / den).astype(BF16)
                vm = jnp.where(mk[hh], v2, jnp.zeros_like(v2))
                o_pair = o_pair + jnp.dot(pn, vm, preferred_element_type=F32)
                lse_pair = jnp.where(mq[hh], mx + jnp.log(den), lse_pair)
            o_ref[:, cols] = o_pair
            lse_ref[:, cols] = lse_pair

    def spec(slab, prev):
        if prev:
            return pl.BlockSpec((None, blk, w), lambda r, n: (slab, jnp.maximum(n - 1, 0), r))
        return pl.BlockSpec((None, blk, w), lambda r, n: (slab, n, r))

    out_spec = pl.BlockSpec((blk, w), lambda r, n: (n, r))
    sds = jax.ShapeDtypeStruct((big_l, dil * w), F32)
    o, lse = pl.pallas_call(
        body, name=name, grid=(dil, nb),
        in_specs=[spec(iq, False), spec(ik, True), spec(ik, False), spec(iv, True), spec(iv, False)],
        out_specs=[out_spec, out_spec], out_shape=[sds, sds],
        compiler_params=_cparams(("parallel", "arbitrary")))(view, view, view, view, view)
    return o.reshape(s, w), lse.reshape(s, w)


def _attn_merge(outs, lses, name):
    s, w = outs[0].shape
    t = TOK_TILE

    def body(o0, o1, o2, l0, l1, l2, a_ref, lt_ref):
        ls = [l0[...], l1[...], l2[...]]
        mx = jnp.maximum(jnp.maximum(ls[0], ls[1]), ls[2])
        es = [jnp.exp(l - mx) for l in ls]
        den = es[0] + es[1] + es[2]
        num = es[0] * o0[...] + es[1] * o1[...] + es[2] * o2[...]
        a_ref[...] = (num / den).astype(BF16)
        lt_ref[...] = mx + jnp.log(den)

    tile = pl.BlockSpec((t, w), lambda i: (i, 0))
    return pl.pallas_call(
        body, name=name, grid=(s // t,), in_specs=[tile] * 6, out_specs=[tile, tile],
        out_shape=[jax.ShapeDtypeStruct((s, w), BF16), jax.ShapeDtypeStruct((s, w), F32)],
        compiler_params=_cparams(("parallel",)))(*outs, *lses)


def _attn_bwd(slabs, dap, ap, lse_tot, gi, name):
    window, dil = ATTN_GROUPS[gi]
    n_back = window // dil
    _, s, w = slabs.shape
    big_l = s // dil
    nb = big_l // ATTN_BLOCK
    blk = ATTN_BLOCK
    view, (iq, ik, iv) = _group_view(slabs, gi)
    dap_v = dap.reshape(big_l, dil * 2 * w)
    ap_v = ap.reshape(big_l, dil * 2 * w)
    lse_v = lse_tot.reshape(big_l, dil * w)
    scale = HEAD_DIM ** -0.5

    def body(qc_ref, qn_ref, kp_ref, kc_ref, vp_ref, vc_ref, dc_ref, dn_ref, ac_ref, an_ref, lc_ref, ln_ref, out_ref):
        m = pl.program_id(1)
        qi = lax.broadcasted_iota(jnp.int32, (blk, 2 * blk), 0)
        kj = lax.broadcasted_iota(jnp.int32, (blk, 2 * blk), 1)
        dist = qi + blk - kj
        ok = (dist >= 0) & (dist <= n_back) & ((kj >= blk) | (m > 0))
        qi1 = lax.broadcasted_iota(jnp.int32, (blk, blk), 0)
        kj1 = lax.broadcasted_iota(jnp.int32, (blk, blk), 1)
        ok_next = (qi1 + blk - kj1 <= n_back) & (m + 1 < nb)
        mq = _head_masks(blk)
        mk = _head_masks(2 * blk)

        def per_head(q_t, d_t, a_t, l_t, hh):
            qm = jnp.where(mq[hh], q_t, jnp.zeros_like(q_t))
            dm = jnp.where(mq[hh], d_t, jnp.zeros_like(d_t))
            delta = jnp.sum(jnp.where(mq[hh], d_t.astype(F32) * a_t.astype(F32), 0.0), axis=-1, keepdims=True)
            lse_h = jnp.max(jnp.where(mq[hh], l_t, NEG_BIG), axis=-1, keepdims=True)
            return qm, dm, delta, lse_h

        for p in range(w // LANES):
            cols = slice(p * LANES, (p + 1) * LANES)
            k2 = jnp.concatenate([kp_ref[:, cols], kc_ref[:, cols]], axis=0)
            v2 = jnp.concatenate([vp_ref[:, cols], vc_ref[:, cols]], axis=0)
            kc, vc = kc_ref[:, cols], vc_ref[:, cols]
            dq_pair = jnp.zeros((blk, LANES), F32)
            dk_pair = jnp.zeros((blk, LANES), F32)
            dv_pair = jnp.zeros((blk, LANES), F32)
            for hh in range(2):
                qm, dm, delta, lse_h = per_head(qc_ref[:, cols], dc_ref[:, cols], ac_ref[:, cols], lc_ref[:, cols], hh)
                sc = lax.dot_general(qm, k2, NT, preferred_element_type=F32) * scale
                pr = jnp.exp(jnp.where(ok, sc, NEG_BIG) - lse_h)
                dp = lax.dot_general(dm, v2, NT, preferred_element_type=F32)
                ds = pr * (dp - delta)
                ds_b = ds.astype(BF16)
                km = jnp.where(mk[hh], k2, jnp.zeros_like(k2))
                dq_pair = dq_pair + jnp.dot(ds_b, km, preferred_element_type=F32)
                dk_pair = dk_pair + lax.dot_general(ds_b[:, blk:], qm, TN, preferred_element_type=F32)
                dv_pair = dv_pair + lax.dot_general(pr[:, blk:].astype(BF16), dm, TN, preferred_element_type=F32)
                qm, dm, delta, lse_h = per_head(qn_ref[:, cols], dn_ref[:, cols], an_ref[:, cols], ln_ref[:, cols], hh)
                sc = lax.dot_general(qm, kc, NT, preferred_element_type=F32) * scale
                pr = jnp.exp(jnp.where(ok_next, sc, NEG_BIG) - lse_h)
                dp = lax.dot_general(dm, vc, NT, preferred_element_type=F32)
                ds_b = (pr * (dp - delta)).astype(BF16)
                dk_pair = dk_pair + lax.dot_general(ds_b, qm, TN, preferred_element_type=F32)
                dv_pair = dv_pair + lax.dot_general(pr.astype(BF16), dm, TN, preferred_element_type=F32)
            out_ref[0, :, cols] = (dq_pair * scale).astype(BF16)
            out_ref[1, :, cols] = (dk_pair * scale).astype(BF16)
            out_ref[2, :, cols] = dv_pair.astype(BF16)

    def slab_spec(slab, shift):
        if shift < 0:
            return pl.BlockSpec((None, blk, w), lambda r, n: (slab, jnp.maximum(n - 1, 0), r))
        if shift > 0:
            return pl.BlockSpec((None, blk, w), lambda r, n: (slab, jnp.minimum(n + 1, nb - 1), r))
        return pl.BlockSpec((None, blk, w), lambda r, n: (slab, n, r))

    def tok_spec(stride, shift):
        if shift > 0:
            return pl.BlockSpec((blk, w), lambda r, n: (jnp.minimum(n + 1, nb - 1), stride * r))
        return pl.BlockSpec((blk, w), lambda r, n: (n, stride * r))

    out = pl.pallas_call(
        body, name=name, grid=(dil, nb),
        in_specs=[slab_spec(iq, 0), slab_spec(iq, 1), slab_spec(ik, -1), slab_spec(ik, 0),
                  slab_spec(iv, -1), slab_spec(iv, 0),
                  tok_spec(2, 0), tok_spec(2, 1), tok_spec(2, 0), tok_spec(2, 1), tok_spec(1, 0), tok_spec(1, 1)],
        out_specs=pl.BlockSpec((3, blk, w), lambda r, n: (0, n, r)),
        out_shape=jax.ShapeDtypeStruct((3, big_l, dil * w), BF16),
        compiler_params=_cparams(("parallel", "arbitrary")))(
            view, view, view, view, view, view, dap_v, dap_v, ap_v, ap_v, lse_v, lse_v)
    return out.reshape(3, s, w)


def _shift_down(cur, prev, j):
    row = lax.broadcasted_iota(jnp.int32, cur.shape, 0)
    return jnp.where(row >= j, pltpu.roll(cur, j, 0), pltpu.roll(prev, j, 0))


def _shift_up(cur, nxt, j):
    t = cur.shape[0]
    row = lax.broadcasted_iota(jnp.int32, cur.shape, 0)
    return jnp.where(row < t - j, pltpu.roll(cur, t - j, 0), pltpu.roll(nxt, t - j, 0))


def _window_sum_down(cur, prev, w):
    s, sp, step = cur, prev, 1
    while step < w:
        s_new = s + _shift_down(s, sp, step)
        sp = sp + pltpu.roll(sp, step, 0)
        s, step = s_new, step * 2
    return s


def _window_sum_up(cur, nxt, w):
    t = cur.shape[0]
    s, sn, step = cur, nxt, 1
    while step < w:
        s_new = s + _shift_up(s, sn, step)
        sn = sn + pltpu.roll(sn, t - step, 0)
        s, step = s_new, step * 2
    return s


def _pool_count(tile_idx, t, w):
    row = lax.broadcasted_iota(jnp.int32, (t, LANES), 0) + tile_idx * t
    return jnp.minimum(row + 1, w).astype(F32)


def _pool_fwd(slabs, attn, pool_w, pool_scale, name):
    _, s, w = slabs.shape
    t = TOK_TILE
    gw = POOL_W // len(POOL_WINDOWS)

    def body(u_ref, up_ref, a_ref, w_ref, sc_ref, ap_ref, d_ref):
        i = pl.program_id(0)
        ap_ref[:, :w] = a_ref[...]
        for gi, win in enumerate(POOL_WINDOWS):
            cols = slice(gi * gw, (gi + 1) * gw)
            u = u_ref[:, cols].astype(F32)
            up = jnp.where(i > 0, up_ref[:, cols].astype(F32), 0.0)
            d = (_window_sum_down(u, up, win) / _pool_count(i, t, win) - u).astype(BF16)
            d_ref[:, cols] = d
            y = jnp.dot(d, w_ref[gi].astype(BF16), preferred_element_type=F32)
            ap_ref[:, w + gi * gw:w + (gi + 1) * gw] = (y * sc_ref[:, cols]).astype(BF16)

    return pl.pallas_call(
        body, name=name, grid=(s // t,),
        in_specs=[pl.BlockSpec((None, t, w), lambda i: (N_SLABS - 1, i, 0)),
                  pl.BlockSpec((None, t, w), lambda i: (N_SLABS - 1, jnp.maximum(i - 1, 0), 0)),
                  pl.BlockSpec((t, w), lambda i: (i, 0)),
                  pl.BlockSpec((len(POOL_WINDOWS), gw, gw), lambda i: (0, 0, 0)),
                  pl.BlockSpec((1, w), lambda i: (0, 0))],
        out_specs=[pl.BlockSpec((t, 2 * w), lambda i: (i, 0)), pl.BlockSpec((t, w), lambda i: (i, 0))],
        out_shape=[jax.ShapeDtypeStruct((s, 2 * w), BF16), jax.ShapeDtypeStruct((s, w), BF16)],
        compiler_params=_cparams(("parallel",)))(slabs, slabs, attn, pool_w, pool_scale)


def _pool_bwd(dap, d, pool_w, pool_scale, name):
    s, w = d.shape
    t = TOK_TILE
    nt = s // t
    gw = POOL_W // len(POOL_WINDOWS)

    def body(dy_ref, dyn_ref, d_ref, w_ref, sc_ref, du_ref, dw_ref, dsc_ref):
        i = pl.program_id(0)

        @pl.when(i == 0)
        def _():
            dw_ref[...] = jnp.zeros_like(dw_ref)
            dsc_ref[...] = jnp.zeros_like(dsc_ref)

        for gi, win in enumerate(POOL_WINDOWS):
            cols = slice(gi * gw, (gi + 1) * gw)
            wb = w_ref[gi].astype(BF16)
            scale = sc_ref[:, cols]
            dv = d_ref[:, cols]
            dy = dy_ref[:, cols].astype(F32)
            y = jnp.dot(dv, wb, preferred_element_type=F32)
            dsc_ref[:, cols] += _colsum8(dy * y)
            dyp = (dy * scale).astype(BF16)
            dw_ref[gi] += lax.dot_general(dv, dyp, TN, preferred_element_type=F32)
            dd = lax.dot_general(dyp, wb, NT, preferred_element_type=F32)
            dypn = (dyn_ref[:, cols].astype(F32) * scale).astype(BF16)
            ddn = lax.dot_general(dypn, wb, NT, preferred_element_type=F32)
            e = dd / _pool_count(i, t, win)
            en = jnp.where(i + 1 < nt, ddn / _pool_count(i + 1, t, win), 0.0)
            du_ref[:, cols] = (_window_sum_up(e, en, win) - dd).astype(BF16)

    return pl.pallas_call(
        body, name=name, grid=(nt,),
        in_specs=[pl.BlockSpec((t, w), lambda i: (i, 1)),
                  pl.BlockSpec((t, w), lambda i: (jnp.minimum(i + 1, nt - 1), 1)),
                  pl.BlockSpec((t, w), lambda i: (i, 0)),
                  pl.BlockSpec((len(POOL_WINDOWS), gw, gw), lambda i: (0, 0, 0)),
                  pl.BlockSpec((1, w), lambda i: (0, 0))],
        out_specs=[pl.BlockSpec((t, w), lambda i: (i, 0)),
                   pl.BlockSpec((len(POOL_WINDOWS), gw, gw), lambda i: (0, 0, 0)),
                   pl.BlockSpec((8, w), lambda i: (0, 0))],
        out_shape=[jax.ShapeDtypeStruct((s, w), BF16), jax.ShapeDtypeStruct((len(POOL_WINDOWS), gw, gw), F32),
                   jax.ShapeDtypeStruct((8, w), F32)],
        compiler_params=_cparams(("arbitrary",)))(dap, dap, d, pool_w, pool_scale)


CONV_TAPS = 4
CONV_COLS = 512


def _sigmoid(v):
    return 1.0 / (1.0 + jnp.exp(-v))


CONV_HALO = 16


def _conv_pre(xs_ref, rows, w_ref, b_ref):
    pre = b_ref[...]
    for k in range(CONV_TAPS):
        pre = pre + w_ref[k:k + 1, :] * xs_ref[pl.ds(CONV_HALO - (CONV_TAPS - 1) + k, rows), :]
    return pre


def _conv_fwd(xbc, conv_w, conv_b, name):
    s, c = xbc.shape
    t, tc, hl = TOK_TILE, CONV_COLS, CONV_HALO

    def body(x_ref, xp_ref, w_ref, b_ref, o_ref, xs):
        i = pl.program_id(0)
        xs[:hl, :] = jnp.where(i > 0, xp_ref[...].astype(F32), 0.0)
        xs[hl:, :] = x_ref[...].astype(F32)
        pre = _conv_pre(xs, t, w_ref, b_ref)
        o_ref[...] = (pre * _sigmoid(pre)).astype(BF16)

    return pl.pallas_call(
        body, name=name, grid=(s // t, c // tc),
        in_specs=[pl.BlockSpec((t, tc), lambda i, j: (i, j)),
                  pl.BlockSpec((hl, tc), lambda i, j: (jnp.maximum(i * (t // hl) - 1, 0), j)),
                  pl.BlockSpec((CONV_TAPS, tc), lambda i, j: (0, j)), pl.BlockSpec((1, tc), lambda i, j: (0, j))],
        out_specs=pl.BlockSpec((t, tc), lambda i, j: (i, j)), out_shape=jax.ShapeDtypeStruct((s, c), BF16),
        scratch_shapes=[pltpu.VMEM((hl + t, tc), F32)],
        compiler_params=_cparams(("parallel", "parallel")))(xbc, xbc, conv_w, conv_b)


def _conv_bwd(xbc, dact, conv_w, conv_b, name):
    s, c = xbc.shape
    t, tc, hl = TOK_TILE, CONV_COLS, CONV_HALO
    nt = s // t

    def body(xp_ref, x_ref, xn_ref, da_ref, dan_ref, w_ref, b_ref, dx_ref, dw_ref, db_ref, xs, ds):
        i = pl.program_id(1)

        @pl.when(i == 0)
        def _():
            dw_ref[...] = jnp.zeros_like(dw_ref)
            db_ref[...] = jnp.zeros_like(db_ref)

        xs[:hl, :] = jnp.where(i > 0, xp_ref[...].astype(F32), 0.0)
        xs[hl:hl + t, :] = x_ref[...].astype(F32)
        xs[hl + t:, :] = xn_ref[...].astype(F32)
        pre = _conv_pre(xs, t + hl, w_ref, b_ref)
        sg = _sigmoid(pre)
        dsilu = sg * (1.0 + pre * (1.0 - sg))
        ds[:t, :] = da_ref[...] * dsilu[:t]
        ds[t:, :] = jnp.where(i + 1 < nt, dan_ref[...] * dsilu[t:], 0.0)
        dpre = ds[:t, :]
        dx = jnp.zeros((t, tc), F32)
        for k in range(CONV_TAPS):
            j = CONV_TAPS - 1 - k
            dx = dx + w_ref[k:k + 1, :] * ds[pl.ds(j, t), :]
            dw_ref[k] += _colsum8(dpre * xs[pl.ds(hl - j, t), :])
        dx_ref[...] = dx.astype(BF16)
        db_ref[...] += _colsum8(dpre)

    per = t // hl
    tile = pl.BlockSpec((t, tc), lambda j, i: (i, j))
    before = pl.BlockSpec((hl, tc), lambda j, i: (jnp.maximum(i * per - 1, 0), j))
    after = pl.BlockSpec((hl, tc), lambda j, i: (jnp.minimum((i + 1) * per, nt * per - 1), j))
    return pl.pallas_call(
        body, name=name, grid=(c // tc, nt),
        in_specs=[before, tile, after, tile, after,
                  pl.BlockSpec((CONV_TAPS, tc), lambda j, i: (0, j)), pl.BlockSpec((1, tc), lambda j, i: (0, j))],
        out_specs=[tile, pl.BlockSpec((CONV_TAPS, 8, tc), lambda j, i: (0, 0, j)),
                   pl.BlockSpec((8, tc), lambda j, i: (0, j))],
        out_shape=[jax.ShapeDtypeStruct((s, c), BF16), jax.ShapeDtypeStruct((CONV_TAPS, 8, c), F32),
                   jax.ShapeDtypeStruct((8, c), F32)],
        scratch_shapes=[pltpu.VMEM((hl + t + hl, tc), F32), pltpu.VMEM((t + hl, tc), F32)],
        compiler_params=_cparams(("parallel", "arbitrary")))(xbc, xbc, xbc, dact, dact, conv_w, conv_b)


GN_TILE = 256


def _gated_norm_fwd(y, z, g, name):
    s, c = y.shape
    t = GN_TILE

    def body(y_ref, z_ref, g_ref, o_ref):
        for gi in range(SSM_GROUPS):
            cols = slice(gi * SSM_GW, (gi + 1) * SSM_GW)
            zv = z_ref[:, cols].astype(F32)
            yf = y_ref[:, cols] * (zv * _sigmoid(zv))
            r = lax.rsqrt(jnp.mean(yf * yf, axis=-1, keepdims=True) + NORM_EPS)
            o_ref[:, cols] = (yf * r * g_ref[:, cols]).astype(BF16)

    tile = pl.BlockSpec((t, c), lambda i: (i, 0))
    return pl.pallas_call(
        body, name=name, grid=(s // t,), in_specs=[tile, tile, pl.BlockSpec((1, c), lambda i: (0, 0))],
        out_specs=tile, out_shape=jax.ShapeDtypeStruct((s, c), BF16),
        compiler_params=_cparams(("parallel",)))(y, z, g)


def _gated_norm_bwd(y, z, dout, g, name):
    s, c = y.shape
    t = GN_TILE

    def body(y_ref, z_ref, do_ref, g_ref, dy_ref, dz_ref, dg_ref):
        i = pl.program_id(0)

        @pl.when(i == 0)
        def _():
            dg_ref[...] = jnp.zeros_like(dg_ref)

        for gi in range(SSM_GROUPS):
            cols = slice(gi * SSM_GW, (gi + 1) * SSM_GW)
            zv = z_ref[:, cols].astype(F32)
            yv = y_ref[:, cols]
            sg = _sigmoid(zv)
            sz = zv * sg
            yf = yv * sz
            r = lax.rsqrt(jnp.mean(yf * yf, axis=-1, keepdims=True) + NORM_EPS)
            n = yf * r
            dout = do_ref[:, cols]
            dn = dout * g_ref[:, cols]
            dg_ref[:, cols] += _colsum8(dout * n)
            dyf = r * (dn - n * jnp.mean(dn * n, axis=-1, keepdims=True))
            dy_ref[:, cols] = dyf * sz
            dz_ref[:, cols] = (dyf * yv * (sg * (1.0 + zv * (1.0 - sg)))).astype(BF16)

    tile = pl.BlockSpec((t, c), lambda i: (i, 0))
    return pl.pallas_call(
        body, name=name, grid=(s // t,), in_specs=[tile, tile, tile, pl.BlockSpec((1, c), lambda i: (0, 0))],
        out_specs=[tile, tile, pl.BlockSpec((8, c), lambda i: (0, 0))],
        out_shape=[jax.ShapeDtypeStruct((s, c), F32), jax.ShapeDtypeStruct((s, c), BF16),
                   jax.ShapeDtypeStruct((8, c), F32)],
        compiler_params=_cparams(("arbitrary",)))(y, z, dout, g)


def _split_dot(x, e, dims, terms):
    r, acc = x, None
    for i in range(terms):
        p = r.astype(BF16)
        part = lax.dot_general(p, e, dims, preferred_element_type=F32)
        acc = part if acc is None else acc + part
        if i + 1 < terms:
            r = r - p.astype(F32)
    return acc


def _split_dot_r(e, x, dims, terms):
    r, acc = x, None
    for i in range(terms):
        p = r.astype(BF16)
        part = lax.dot_general(e, p, dims, preferred_element_type=F32)
        acc = part if acc is None else acc + part
        if i + 1 < terms:
            r = r - p.astype(F32)
    return acc


def _ssd_prep(dtr_ref, bias_ref, alog_ref):
    q = SSM_CHUNK
    dtr = dtr_ref[...] + bias_ref[...]
    dt = jnp.maximum(dtr, 0.0) + jnp.log(1.0 + jnp.exp(-jnp.abs(dtr)))
    a_neg = -jnp.exp(alog_ref[...])
    dta = dt * a_neg
    row = lax.broadcasted_iota(jnp.int32, (q, q), 0)
    col = lax.broadcasted_iota(jnp.int32, (q, q), 1)
    causal = row >= col
    a = _split_dot_r(causal.astype(BF16), dta, NN, 3)
    aq_row = jnp.sum(dta, axis=0, keepdims=True)
    aq_hb = _split_dot(dta, jnp.ones((q, LANES), BF16), TN, 3)
    return dict(dtr=dtr, dt=dt, a_neg=a_neg, a=a, a_t=jnp.transpose(a), aq_row=aq_row, aq_hb=aq_hb,
                ea=jnp.exp(a), fa=jnp.exp(aq_row - a), causal=causal)


def _ssd_group_mats(g):
    hrow = lax.broadcasted_iota(jnp.int32, (LANES, SSM_GW), 0)
    jcol = lax.broadcasted_iota(jnp.int32, (LANES, SSM_GW), 1)
    eg = (hrow == 8 * g + jcol // HEAD_DIM).astype(BF16)
    jrow = lax.broadcasted_iota(jnp.int32, (SSM_GW, LANES), 0)
    hcol = lax.broadcasted_iota(jnp.int32, (SSM_GW, LANES), 1)
    eg_t = (hcol == 8 * g + jrow // HEAD_DIM).astype(BF16)
    hrow2 = lax.broadcasted_iota(jnp.int32, (LANES, 8 * LANES), 0)
    jcol2 = lax.broadcasted_iota(jnp.int32, (LANES, 8 * LANES), 1)
    sel = (hrow2 == 8 * g + jcol2 // LANES).astype(BF16)
    return eg, eg_t, sel


def _ssd_fwd(xbc, dt_raw, dt_bias, a_log, d_e, name):
    s = xbc.shape[0]
    q = SSM_CHUNK
    nc = s // q

    def body(x_ref, dtr_ref, bias_ref, alog_ref, de_ref, y_ref, hin_ref, state, at_ref):
        c = pl.program_id(0)

        @pl.when(c == 0)
        def _():
            state[...] = jnp.zeros_like(state)

        hin_ref[...] = state[...].astype(BF16)
        pr = _ssd_prep(dtr_ref, bias_ref, alog_ref)
        at_ref[...] = pr["a_t"]
        exp_aq_hb = jnp.exp(pr["aq_hb"])
        stack3 = jnp.concatenate([pr["dt"], pr["ea"], pr["fa"]], axis=0)
        mq = _head_masks(q)
        for g in range(SSM_GROUPS):
            eg, eg_t, sel = _ssd_group_mats(g)
            cols = slice(g * SSM_GW, (g + 1) * SSM_GW)
            xg = x_ref[:, cols]
            bg = x_ref[:, SSM_INNER + g * SSM_STATE:SSM_INNER + (g + 1) * SSM_STATE]
            cg = x_ref[:, SSM_INNER + SSM_GROUPS * SSM_STATE + g * SSM_STATE:
                       SSM_INNER + SSM_GROUPS * SSM_STATE + (g + 1) * SSM_STATE]
            e3 = _split_dot(stack3, eg, NN, 2)
            dt_e, ea_e, fa_e = e3[:q], e3[q:2 * q], e3[2 * q:]
            xf = xg.astype(F32)
            xdt = xf * dt_e
            xdt_b = xdt.astype(BF16)
            cb = lax.dot_general(cg, bg, NT, preferred_element_type=F32)
            colb = _split_dot(pr["a"], sel, NN, 3)
            hg = state[cols, :]
            y_g = lax.dot_general(cg, hg.astype(BF16), NT, preferred_element_type=F32) * ea_e + de_ref[:, cols] * xf
            pairs = []
            for pp in range(4):
                xp = xdt_b[:, pp * LANES:(pp + 1) * LANES]
                yp = jnp.zeros((q, LANES), F32)
                for hh in range(2):
                    hi = 2 * pp + hh
                    diff = colb[:, hi * LANES:(hi + 1) * LANES] - at_ref[8 * g + hi:8 * g + hi + 1, :]
                    lmat = jnp.exp(jnp.where(pr["causal"], diff, NEG_BIG))
                    m_b = (cb * lmat).astype(BF16)
                    yp = yp + jnp.dot(m_b, jnp.where(mq[hh], xp, jnp.zeros_like(xp)), preferred_element_type=F32)
                pairs.append(yp)
            y_ref[:, cols] = y_g + jnp.concatenate(pairs, axis=1)
            s_g = lax.dot_general((xdt * fa_e).astype(BF16), bg, TN, preferred_element_type=F32)
            dec_g = _split_dot_r(eg, exp_aq_hb, TN, 2)
            state[cols, :] = dec_g * hg + s_g

    row128 = pl.BlockSpec((1, LANES), lambda c: (0, 0))
    return pl.pallas_call(
        body, name=name, grid=(nc,),
        in_specs=[pl.BlockSpec((q, SSM_CONV_DIM), lambda c: (c, 0)), pl.BlockSpec((q, LANES), lambda c: (c, 0)),
                  row128, row128, pl.BlockSpec((1, SSM_INNER), lambda c: (0, 0))],
        out_specs=[pl.BlockSpec((q, SSM_INNER), lambda c: (c, 0)),
                   pl.BlockSpec((None, SSM_INNER, SSM_STATE), lambda c: (c, 0, 0))],
        out_shape=[jax.ShapeDtypeStruct((s, SSM_INNER), F32), jax.ShapeDtypeStruct((nc, SSM_INNER, SSM_STATE), BF16)],
        scratch_shapes=[pltpu.VMEM((SSM_INNER, SSM_STATE), F32), pltpu.VMEM((LANES, q), F32)],
        compiler_params=_cparams(("arbitrary",)))(xbc, dt_raw, dt_bias, a_log, d_e)


def _ssd_bwd(xbc, dt_raw, dt_bias, a_log, d_e, hin, dy, name):
    s = xbc.shape[0]
    q = SSM_CHUNK
    nc = s // q

    def body(x_ref, dtr_ref, bias_ref, alog_ref, de_ref, hin_ref, dy_ref,
             dx_ref, ddt_ref, da_acc, db_acc, dd_acc, gst, at_ref):
        i = pl.program_id(0)

        @pl.when(i == 0)
        def _():
            gst[...] = jnp.zeros_like(gst)
            da_acc[...] = jnp.zeros_like(da_acc)
            db_acc[...] = jnp.zeros_like(db_acc)
            dd_acc[...] = jnp.zeros_like(dd_acc)

        pr = _ssd_prep(dtr_ref, bias_ref, alog_ref)
        at_ref[...] = pr["a_t"]
        exp_aq_hb = jnp.exp(pr["aq_hb"])
        stack3 = jnp.concatenate([pr["dt"], pr["ea"], pr["fa"]], axis=0)
        mq = _head_masks(q)
        lane_h = lax.broadcasted_iota(jnp.int32, (q, LANES), 1)
        sub_h = lax.broadcasted_iota(jnp.int32, (LANES, q), 0)
        da = jnp.zeros((q, LANES), F32)
        da_tn = jnp.zeros((LANES, q), F32)
        daq = jnp.zeros((1, LANES), F32)
        ddt = jnp.zeros((q, LANES), F32)
        for g in range(SSM_GROUPS):
            eg, eg_t, sel = _ssd_group_mats(g)
            cols = slice(g * SSM_GW, (g + 1) * SSM_GW)
            bcols = slice(SSM_INNER + g * SSM_STATE, SSM_INNER + (g + 1) * SSM_STATE)
            ccols = slice(SSM_INNER + SSM_GROUPS * SSM_STATE + g * SSM_STATE,
                          SSM_INNER + SSM_GROUPS * SSM_STATE + (g + 1) * SSM_STATE)
            xg, bg, cg = x_ref[:, cols], x_ref[:, bcols], x_ref[:, ccols]
            e3 = _split_dot(stack3, eg, NN, 2)
            dt_e, ea_e, fa_e = e3[:q], e3[q:2 * q], e3[2 * q:]
            xf = xg.astype(F32)
            xdt = xf * dt_e
            xdt_b = xdt.astype(BF16)
            xdtf = xdt * fa_e
            xdtf_b = xdtf.astype(BF16)
            cb = lax.dot_general(cg, bg, NT, preferred_element_type=F32)
            colb = _split_dot(pr["a"], sel, NN, 3)
            dyg = dy_ref[:, cols]
            dd_acc[:, cols] += _colsum8(dyg * xf)
            hg_b = hin_ref[cols, :]
            gg = gst[cols, :]
            gg_b = gg.astype(BF16)
            dye_b = (dyg * ea_e).astype(BF16)
            dc_g = jnp.dot(dye_b, hg_b, preferred_element_type=F32)
            dh_g = lax.dot_general(dye_b, cg, TN, preferred_element_type=F32)
            yoff = lax.dot_general(cg, hg_b, NT, preferred_element_type=F32) * ea_e
            da = da + _split_dot(dyg * yoff, eg_t, NN, 2)
            db_g = jnp.dot(xdtf_b, gg_b, preferred_element_type=F32)
            tmat = lax.dot_general(bg, gg_b, NT, preferred_element_type=F32)
            dxdt = fa_e * tmat
            qmat = _split_dot(xdtf * tmat, eg_t, NN, 2)
            da = da - qmat
            daq = daq + jnp.sum(qmat, axis=0, keepdims=True)
            gh = _split_dot(gg * hg_b.astype(F32), eg_t, TN, 2)
            daq = daq + jnp.sum(gh, axis=0, keepdims=True) * jnp.exp(pr["aq_row"])
            dcb = jnp.zeros((q, q), F32)
            pairs = []
            for pp in range(4):
                xp = xdt_b[:, pp * LANES:(pp + 1) * LANES]
                dyp = dyg[:, pp * LANES:(pp + 1) * LANES]
                dxp = jnp.zeros((q, LANES), F32)
                for hh in range(2):
                    hi = 2 * pp + hh
                    h = 8 * g + hi
                    diff = colb[:, hi * LANES:(hi + 1) * LANES] - at_ref[h:h + 1, :]
                    lmat = jnp.exp(jnp.where(pr["causal"], diff, NEG_BIG))
                    mmat = cb * lmat
                    dyh = jnp.where(mq[hh], dyp, 0.0).astype(BF16)
                    dm = lax.dot_general(dyh, xp, NT, preferred_element_type=F32)
                    dxp = dxp + lax.dot_general(mmat.astype(BF16), dyh, TN, preferred_element_type=F32)
                    wmat = dm * mmat
                    da = da + jnp.where(lane_h == h, jnp.sum(wmat, axis=-1, keepdims=True), 0.0)
                    da_tn = da_tn + jnp.where(sub_h == h, jnp.sum(wmat, axis=0, keepdims=True), 0.0)
                    dcb = dcb + dm * lmat
                pairs.append(dxp)
            dxdt = dxdt + jnp.concatenate(pairs, axis=1)
            dcb_b = dcb.astype(BF16)
            dc_g = dc_g + jnp.dot(dcb_b, bg, preferred_element_type=F32)
            db_g = db_g + lax.dot_general(dcb_b, cg, TN, preferred_element_type=F32)
            ddt = ddt + _split_dot(dxdt * xf, eg_t, NN, 2)
            dx_ref[:, cols] = dxdt * dt_e + de_ref[:, cols] * dyg
            dx_ref[:, bcols] = db_g
            dx_ref[:, ccols] = dc_g
            dec_g = _split_dot_r(eg, exp_aq_hb, TN, 2)
            gst[cols, :] = dh_g + dec_g * gg
        row = lax.broadcasted_iota(jnp.int32, (q, LANES), 0)
        da_all = da - jnp.transpose(da_tn) + jnp.where(row == q - 1, daq, 0.0)
        upper = jnp.logical_not(pr["causal"]) | (lax.broadcasted_iota(jnp.int32, (q, q), 0)
                                                 == lax.broadcasted_iota(jnp.int32, (q, q), 1))
        rcs = _split_dot_r(upper.astype(BF16), da_all, NN, 3)
        ddt_all = ddt + pr["a_neg"] * rcs
        da_acc[...] += _colsum8(pr["dt"] * rcs)
        ddtr = jnp.where(lane_h < SSM_HEADS, ddt_all * _sigmoid(pr["dtr"]), 0.0)
        ddt_ref[...] = ddtr
        db_acc[...] += _colsum8(ddtr)

    rev = lambda i: (nc - 1 - i, 0)
    row128 = pl.BlockSpec((1, LANES), lambda i: (0, 0))
    acc128 = pl.BlockSpec((8, LANES), lambda i: (0, 0))
    return pl.pallas_call(
        body, name=name, grid=(nc,),
        in_specs=[pl.BlockSpec((q, SSM_CONV_DIM), rev), pl.BlockSpec((q, LANES), rev), row128, row128,
                  pl.BlockSpec((1, SSM_INNER), lambda i: (0, 0)),
                  pl.BlockSpec((None, SSM_INNER, SSM_STATE), lambda i: (nc - 1 - i, 0, 0)),
                  pl.BlockSpec((q, SSM_INNER), rev)],
        out_specs=[pl.BlockSpec((q, SSM_CONV_DIM), rev), pl.BlockSpec((q, LANES), rev), acc128, acc128,
                   pl.BlockSpec((8, SSM_INNER), lambda i: (0, 0))],
        out_shape=[jax.ShapeDtypeStruct((s, SSM_CONV_DIM), F32), jax.ShapeDtypeStruct((s, LANES), F32),
                   jax.ShapeDtypeStruct((8, LANES), F32), jax.ShapeDtypeStruct((8, LANES), F32),
                   jax.ShapeDtypeStruct((8, SSM_INNER), F32)],
        scratch_shapes=[pltpu.VMEM((SSM_INNER, SSM_STATE), F32), pltpu.VMEM((LANES, q), F32)],
        compiler_params=_cparams(("arbitrary",)))(xbc, dt_raw, dt_bias, a_log, d_e, hin, dy)


MM = (1024, 1024, 2048)
MM_TN = (1024, 2048, 1024)


def _relu2_epilogue(acc, ex, outs):
    r = jnp.maximum(acc, 0.0)
    outs[0][...] = (r * r).astype(BF16)
    outs[1][...] = r.astype(BF16)


def _relu2_bwd_epilogue(acc, ex, outs):
    outs[0][...] = (acc * (2.0 * ex[0][...].astype(F32))).astype(BF16)


def _add_epilogue(acc, ex, outs):
    outs[0][...] = acc + ex[0][...]


def _tile_spec(tm, tn):
    return pl.BlockSpec((tm, tn), lambda i, j, k: (i, j))


def _sum8(acc):
    return jnp.sum(acc, axis=0)


def _row(v):
    return v.reshape(1, -1)


def _ffn_fwd(xin, norm_g, sc, sh, gate, w, l):
    tm, tn, tk = MM
    h = _norm_mod_fwd(xin, norm_g, sc, sh, f"norm_ffn_fwd_{l}")
    w1 = w("ffn_w1", h)
    per = w1.shape[-1] // tn
    f, r = _mm_nn(f"mm_ffn1_{l}", h, w1, *MM, out_dtypes=(BF16, BF16), epilogue=_relu2_epilogue,
                  b_tile=lambda j, k: (j // per, l, 0, j % per), n=FFN_HIDDEN)
    xout, yf = _mm_resid(f"mm_ffn2_{l}", f, w("ffn_w2", f), xin, gate, *MM, b_tile=lambda j, k: (k, l, 0, j), n=D_MODEL)
    return xout, (h, f, r, yf)


def _ffn_bwd(dxo, xin, saved, norm_g, sc, gate, w, sink, l):
    h, f, r, yf = saved
    tm, tn, tk = MM
    w1, w2 = w("ffn_w1"), w("ffn_w2")
    per = w2.shape[-2] // tn
    dyf, dgate = _gate_bwd(dxo, yf, gate, f"gate_bwd_ffn_{l}")
    da = _mm_nt(f"mm_ffn2_bwd_{l}", dyf, w2, *MM, extras=(r,), extra_specs=(_tile_spec(tm, tn),),
                epilogue=_relu2_bwd_epilogue, b_tile=lambda j, k: (j // per, l, j % per, 0), n=FFN_HIDDEN)[0]
    tok = sink(f"ffn_w2_{l}", _mm_tn(f"mm_ffn2_dw_{l}", f, dyf, *MM_TN), "rows")
    dh = _mm_nt(f"mm_ffn1_bwd_{l}", da, w1, *MM, out_dtypes=(F32,), b_tile=lambda j, k: (k, l, j, 0), n=D_MODEL)[0]
    tok = tok + sink(f"ffn_w1_{l}", _mm_tn(f"mm_ffn1_dw_{l}", h, da, *MM_TN, col_chunks=N_CHIPS), "slots")
    dxin, dsh, dsc, dng = _norm_mod_bwd(xin, dh, dxo, norm_g + tok, sc, f"norm_ffn_bwd_{l}")
    return dxin, (_sum8(dsh), _sum8(dsc), _sum8(dgate), _sum8(dng))


def _local_step(x, target, mod, w, p, sink):
    s, d = x.shape
    tm, tn, tk = MM
    mods = [[_row(mod[l, j * d:(j + 1) * d]) for j in range(6)] for l in range(2)]
    g = {}

    sh1, sc1, g1, sh2, sc2, g2 = mods[0]
    nm0, nf0 = _row(p["norm_mix"][0]), _row(p["norm_ffn"][0])
    h0 = _norm_mod_fwd(x, nm0, sc1, sh1, "norm_mix_fwd_0")
    slabs = _mm_core(
        "mm_even_in", h0, w("even_in", h0), dims=NN, grid=(s // tm, N_SLABS, 1),
        a_spec=pl.BlockSpec((tm, d), lambda i, j, k: (i, 0)), b_spec=pl.BlockSpec((d, ATTN_W), lambda i, j, k: (0, j)),
        out_shapes=[jax.ShapeDtypeStruct((N_SLABS, s, ATTN_W), BF16)],
        out_specs=[pl.BlockSpec((None, tm, ATTN_W), lambda i, j, k: (j, i, 0))], acc_shape=None)[0]
    outs, lses = zip(*[_attn_fwd(slabs, gi, f"attn_fwd_{gi}") for gi in range(3)])
    attn, lse_tot = _attn_merge(outs, lses, "attn_merge")
    pool_scale = _row(p["pool_scale"])
    ap, pool_d = _pool_fwd(slabs, attn, p["pool_w"], pool_scale, "pool_fwd")
    x1, y0 = _mm_resid("mm_even_out", ap, w("even_out", ap), x, g1, *MM)
    x2, ffn0 = _ffn_fwd(x1, nf0, sc2, sh2, g2, w, 0)

    sh1b, sc1b, g1b, sh2b, sc2b, g2b = mods[1]
    nm1, nf1 = _row(p["norm_mix"][1]), _row(p["norm_ffn"][1])
    h1 = _norm_mod_fwd(x2, nm1, sc1b, sh1b, "norm_mix_fwd_1")
    z = _mm_nn("mm_ssm_z", h1, w("ssm_z", h1), *MM)[0]
    xbc_raw = _mm_nn("mm_ssm_xbc", h1, w("ssm_xbc", h1), *MM)[0]
    dt_raw = _mm_nn("mm_ssm_dt", h1, w("ssm_dt", h1), tm, LANES, tk, out_dtypes=(F32,))[0]
    xbc = _conv_fwd(xbc_raw, p["conv_w"], _row(p["conv_b"]), "conv_fwd")
    y_ssd, hin = _ssd_fwd(xbc, dt_raw, p["dt_bias"], p["a_log"], p["d_e"], "ssd_fwd")
    ssm_norm = _row(p["ssm_norm"])
    yn = _gated_norm_fwd(y_ssd, z, ssm_norm, "gated_norm_fwd")
    x3, y1 = _mm_resid("mm_ssm_out", yn, w("ssm_out", yn), x2, g1b, *MM)
    x4, ffn1 = _ffn_fwd(x3, nf1, sc2b, sh2b, g2b, w, 1)

    dx4, sq, dfn = _final_loss(x4, _row(p["final_norm"]), target, "final_loss")
    loss_share = (0.5 / d) * jnp.sum(sq)
    g["final_norm"] = _sum8(dfn)

    dx3, (dsh2b, dsc2b, dg2b, dnf1) = _ffn_bwd(dx4, x3, ffn1, nf1, sc2b, g2b, w, sink, 1)
    dy1, dg1b = _gate_bwd(dx3, y1, g1b, "gate_bwd_ssm")
    dyn = _mm_nt("mm_ssm_out_bwd", dy1, w("ssm_out"), *MM, out_dtypes=(F32,))[0]
    tok = sink("ssm_out", _mm_tn("mm_ssm_out_dw", yn, dy1, *MM_TN), "rows")
    dy_ssd, dz, dgn = _gated_norm_bwd(y_ssd, z, dyn, ssm_norm + tok, "gated_norm_bwd")
    g["ssm_norm"] = _sum8(dgn)
    dact, ddt, da_acc, db_acc, dd_acc = _ssd_bwd(xbc, dt_raw, p["dt_bias"], p["a_log"], p["d_e"], hin, dy_ssd, "ssd_bwd")
    g["dt_bias"] = _sum8(db_acc)[:SSM_HEADS]
    g["a_log"] = _sum8(da_acc)[:SSM_HEADS] * (-jnp.exp(p["a_log"][0, :SSM_HEADS]))
    g["ssm_d"] = jnp.sum(_sum8(dd_acc).reshape(SSM_HEADS, HEAD_DIM), axis=1)
    dxbc, dcw, dcb = _conv_bwd(xbc_raw, dact, p["conv_w"], _row(p["conv_b"]), "conv_bwd")
    g["conv_w"] = jnp.sum(dcw, axis=1)
    g["conv_b"] = _sum8(dcb)
    add_spec = (_tile_spec(tm, tn),)
    dh1 = _mm_nt("mm_ssm_z_bwd", dz, w("ssm_z"), *MM, out_dtypes=(F32,))[0]
    dh1 = _mm_nt("mm_ssm_xbc_bwd", dxbc, w("ssm_xbc"), *MM, out_dtypes=(F32,), extras=(dh1,), extra_specs=add_spec,
                 epilogue=_add_epilogue)[0]
    dh1 = _mm_nt("mm_ssm_dt_bwd", ddt, w("ssm_dt"), *MM, out_dtypes=(F32,), extras=(dh1,), extra_specs=add_spec,
                 epilogue=_add_epilogue)[0]
    tok = sink("ssm_in", jnp.concatenate(
        [_mm_tn("mm_ssm_z_dw", h1, dz, *MM_TN), _mm_tn("mm_ssm_xbc_dw", h1, dxbc, *MM_TN),
         _mm_tn("mm_ssm_dt_dw", h1, ddt, MM_TN[0], LANES, MM_TN[2])[:, :SSM_HEADS]], axis=1), "cols")
    dx2, dsh1b, dsc1b, dnm1 = _norm_mod_bwd(x2, dh1, dx3, nm1 + tok, sc1b, "norm_mix_bwd_1")
    dmod1 = jnp.concatenate([_sum8(dsh1b), _sum8(dsc1b), _sum8(dg1b), dsh2b, dsc2b, dg2b])

    dx1, (dsh2, dsc2, dg2, dnf0) = _ffn_bwd(dx2, x1, ffn0, nf0, sc2, g2, w, sink, 0)
    dy0, dg1 = _gate_bwd(dx1, y0, g1, "gate_bwd_even")
    dap = _mm_nt("mm_even_out_bwd", dy0, w("even_out"), *MM)[0]
    tok = sink("even_out", _mm_tn("mm_even_out_dw", ap, dy0, *MM_TN), "cols")
    du, dpw, dpsc = _pool_bwd(dap, pool_d, p["pool_w"], pool_scale + tok, "pool_bwd")
    g["pool_w"] = dpw
    g["pool_scale"] = _sum8(dpsc)
    dqkv = [_attn_bwd(slabs, dap, ap, lse_tot, gi, f"attn_bwd_{gi}") for gi in range(3)]
    dproj = jnp.concatenate([dqkv[gi][t] for t in range(3) for gi in range(3)] + [du], axis=1)
    dh0 = _mm_nt("mm_even_in_bwd", dproj, w("even_in"), tm, tn, N_SLABS * ATTN_W // 2, out_dtypes=(F32,))[0]
    tok = sink("even_in", _mm_tn("mm_even_in_dw", dproj, h0, *MM_TN), "cols_t")
    gx, dsh1, dsc1, dnm0 = _norm_mod_bwd(x, dh0, dx1, nm0 + tok, sc1, "norm_mix_bwd_0")
    dmod0 = jnp.concatenate([_sum8(dsh1), _sum8(dsc1), _sum8(dg1), dsh2, dsc2, dg2])

    g["norm_mix"] = jnp.stack([_sum8(dnm0), _sum8(dnm1)])
    g["norm_ffn"] = jnp.stack([dnf0, dnf1])
    return loss_share, gx, g, jnp.stack([dmod0, dmod1])


SMALL_COLS = 512
SMALL_ORDER = ("dmod", "norm_mix", "norm_ffn", "pool_w", "pool_scale", "conv_w", "conv_b", "dt_bias", "a_log",
               "ssm_d", "ssm_norm", "final_norm")


def _cols_to_full(gathered):
    n, k, ns = gathered.shape
    return jnp.transpose(gathered, (1, 0, 2)).reshape(k, n * ns)


def _full_to_cols(full):
    k, n4 = full.shape
    return jnp.transpose(full.reshape(k, N_CHIPS, n4 // N_CHIPS), (1, 0, 2))


def _pack(parts, cols):
    flat = jnp.concatenate([v.reshape(-1) for v in parts])
    rows = -(-flat.shape[0] // (cols * 8)) * 8
    return jnp.pad(flat, (0, rows * cols - flat.shape[0])).reshape(rows, cols)


def _unpack(packed, shapes):
    flat, out, at = packed.reshape(-1), [], 0
    for shp in shapes:
        n = math.prod(shp)
        out.append(flat[at:at + n].reshape(shp))
        at += n
    return out


def kernel(x, c, ada_w, ada_b, norm_mix, norm_ffn, ffn_w1, ffn_w2, even_w_in, pool_w, pool_scale, even_w_out, ssm_w_in, ssm_conv_w, ssm_conv_b, ssm_dt_bias, ssm_a_log, ssm_d, ssm_norm, ssm_w_out, final_norm, loss_target, m_ada_w, m_ada_b, m_norm_mix, m_norm_ffn, m_ffn_w1, m_ffn_w2, m_even_w_in, m_pool_w, m_pool_scale, m_even_w_out, m_ssm_w_in, m_ssm_conv_w, m_ssm_conv_b, m_ssm_dt_bias, m_ssm_a_log, m_ssm_d, m_ssm_norm, m_ssm_w_out, m_final_norm, v_ada_w, v_ada_b, v_norm_mix, v_norm_ffn, v_ffn_w1, v_ffn_w2, v_even_w_in, v_pool_w, v_pool_scale, v_even_w_out, v_ssm_w_in, v_ssm_conv_w, v_ssm_conv_b, v_ssm_dt_bias, v_ssm_a_log, v_ssm_d, v_ssm_norm, v_ssm_w_out, v_final_norm):
    names = ("ada_w", "ada_b", "norm_mix", "norm_ffn", "ffn_w1", "ffn_w2", "even_w_in", "pool_w", "pool_scale",
             "even_w_out", "ssm_w_in", "ssm_conv_w", "ssm_conv_b", "ssm_dt_bias", "ssm_a_log", "ssm_d", "ssm_norm",
             "ssm_w_out", "final_norm")
    wts = dict(zip(names, (ada_w, ada_b, norm_mix, norm_ffn, ffn_w1, ffn_w2, even_w_in, pool_w, pool_scale, even_w_out,
                           ssm_w_in, ssm_conv_w, ssm_conv_b, ssm_dt_bias, ssm_a_log, ssm_d, ssm_norm, ssm_w_out, final_norm)))
    m_in = dict(zip(names, (m_ada_w, m_ada_b, m_norm_mix, m_norm_ffn, m_ffn_w1, m_ffn_w2, m_even_w_in, m_pool_w, m_pool_scale,
                            m_even_w_out, m_ssm_w_in, m_ssm_conv_w, m_ssm_conv_b, m_ssm_dt_bias, m_ssm_a_log, m_ssm_d,
                            m_ssm_norm, m_ssm_w_out, m_final_norm)))
    v_in = dict(zip(names, (v_ada_w, v_ada_b, v_norm_mix, v_norm_ffn, v_ffn_w1, v_ffn_w2, v_even_w_in, v_pool_w, v_pool_scale,
                            v_even_w_out, v_ssm_w_in, v_ssm_conv_w, v_ssm_conv_b, v_ssm_dt_bias, v_ssm_a_log, v_ssm_d,
                            v_ssm_norm, v_ssm_w_out, v_final_norm)))
    d = D_MODEL
    s = x.shape[1]
    ix, iy, ic = _place()
    chip = 2 * ix + iy
    example = 4 * ix + 2 * iy + ic

    c_all = _allgather8(c.reshape(8, d // 8), "gather_c").reshape(N_DEV, d)
    cond = c_all * jax.nn.sigmoid(c_all)
    cond16 = jnp.pad(cond, ((0, 8), (0, 0)))
    ada_cols = ada_w.shape[2]
    bias_shard = lax.dynamic_slice_in_dim(ada_b, chip * ada_cols, ada_cols, axis=1)
    mod_parts = [
        _mm_nn(f"mm_ada_{l}", cond16, ada_w, 16, 512, d, out_dtypes=(F32,), extras=(_row(bias_shard[l]),),
               extra_specs=(pl.BlockSpec((1, 512), lambda i, j, k: (0, j)),), epilogue=_add_epilogue,
               b_tile=lambda j, k, l=l: (l, k, j), n=ada_cols)[0][:8]
        for l in range(2)]
    mod_all = _allgather8(jnp.concatenate(mod_parts, axis=0), "gather_mod").reshape(N_CHIPS, 2, 2, 8, ada_cols)
    mod_mine = lax.dynamic_index_in_dim(mod_all[:, 0], example, axis=2, keepdims=False)
    mod = jnp.transpose(mod_mine, (1, 0, 2)).reshape(2, N_CHIPS * ada_cols)

    gathered = _GatheredWeights({"even_in": even_w_in, "even_out": even_w_out, "ffn_w1": ffn_w1, "ffn_w2": ffn_w2,
                                 "ssm_in": ssm_w_in, "ssm_out": ssm_w_out})
    mod = mod + gathered.token
    n_xbc = SSM_INNER + SSM_CONV_DIM
    cache = {}

    def full_weight(key, after=None):
        if key in cache:
            return cache[key]
        if key in ("ffn_w1", "ffn_w2"):
            cache[key] = gathered.take(key, after)
        elif key in ("even_in", "even_out"):
            cache[key] = _cols_to_full(gathered.take(key, after)[:, 0])
        elif key == "ssm_out":
            cache[key] = gathered.take(key, after)[:, 0].reshape(SSM_INNER, d)
        else:
            whole = _cols_to_full(gathered.take("ssm_in", after)[:, 0])
            cache["ssm_z"] = whole[:, :SSM_INNER]
            cache["ssm_xbc"] = whole[:, SSM_INNER:n_xbc]
            cache["ssm_dt"] = jnp.pad(whole[:, n_xbc:], ((0, 0), (0, LANES - SSM_HEADS)))
        return cache[key]

    exchange = _GradientExchange()

    def sink(name, grad, layout):
        if layout == "slots":
            slots = grad
        elif layout == "rows":
            slots = grad.reshape(N_CHIPS, -1, grad.shape[-1])
        elif layout == "cols":
            slots = _full_to_cols(grad)
        else:
            slots = jnp.transpose(grad.reshape(N_CHIPS, -1, grad.shape[-1]), (0, 2, 1))
        return exchange.put(name, slots)

    pad_h = ((0, 0), (0, LANES - SSM_HEADS))
    cw, nw = ssm_conv_w.shape[2], ssm_norm.shape[1]
    sm = jnp.concatenate([ssm_conv_w[0].reshape(-1), ssm_conv_b.reshape(-1), jnp.pad(ssm_norm[0], (0, cw - nw)),
                          jnp.zeros((2 * cw,), F32)]).reshape(8, cw)
    sm_all = _allgather8(sm, "gather_ssm_small").reshape(N_CHIPS, 2, 8, cw)[:, 0]
    small = {
        "norm_mix": norm_mix, "norm_ffn": norm_ffn, "pool_w": pool_w[0], "pool_scale": pool_scale[0],
        "final_norm": final_norm,
        "conv_w": jnp.transpose(sm_all[:, :CONV_TAPS], (1, 0, 2)).reshape(CONV_TAPS, N_CHIPS * cw),
        "conv_b": sm_all[:, CONV_TAPS].reshape(N_CHIPS * cw),
        "ssm_norm": sm_all[:, CONV_TAPS + 1, :nw].reshape(N_CHIPS * nw),
        "dt_bias": jnp.pad(ssm_dt_bias, pad_h), "a_log": jnp.pad(ssm_a_log, pad_h),
        "d_e": jnp.repeat(ssm_d[0], HEAD_DIM).reshape(1, SSM_INNER),
    }

    loss_share, gx, g, dmod = _local_step(x[0], loss_target[0], mod, full_weight, small, sink)
    loss = lax.psum(loss_share, ("x", "y", "c"))
    reduced = exchange.finish(gx)

    g["dmod"] = dmod
    small_shapes = [g[k].shape for k in SMALL_ORDER]
    packed = _pack([g[k] for k in SMALL_ORDER], SMALL_COLS)
    every = _allgather8(packed, "gather_small").reshape(N_DEV, *packed.shape)
    summed = dict(zip(SMALL_ORDER, _unpack(_sum_devices(every, "sum_small"), small_shapes)))
    dmod_all = every.reshape(N_DEV, -1)[:, :2 * 6 * d].reshape(N_DEV, 2, 6 * d)

    grads = {}
    dmod_shard = lax.dynamic_slice_in_dim(dmod_all, chip * ada_cols, ada_cols, axis=2)
    grads["ada_w"] = jnp.stack([
        _mm_tn(f"mm_ada_dw_{l}", cond16, jnp.pad(dmod_shard[:, l], ((0, 8), (0, 0))), 1024, 512, 16, out_dtype=F32)
        for l in range(2)])
    grads["ada_b"] = summed["dmod"]
    grads["norm_mix"] = summed["norm_mix"]
    grads["norm_ffn"] = summed["norm_ffn"]
    grads["pool_w"] = summed["pool_w"][None]
    grads["pool_scale"] = summed["pool_scale"][None]
    grads["ssm_conv_w"] =lax.dynamic_slice_in_dim(summed["conv_w"], chip * cw, cw, axis=1)[None]
    grads["ssm_conv_b"] = lax.dynamic_slice_in_dim(summed["conv_b"], chip * cw, cw, axis=0)[None]
    grads["ssm_dt_bias"] = summed["dt_bias"][None]
    grads["ssm_a_log"] = summed["a_log"][None]
    grads["ssm_d"] = summed["ssm_d"][None]
    grads["ssm_norm"] =lax.dynamic_slice_in_dim(summed["ssm_norm"], chip * nw, nw, axis=0)[None]
    grads["final_norm"] = summed["final_norm"]

    grads["ffn_w1"] = jnp.stack([reduced["ffn_w1_0"], reduced["ffn_w1_1"]])
    grads["ffn_w2"] = jnp.stack([reduced["ffn_w2_0"], reduced["ffn_w2_1"]])
    grads["even_w_in"] = reduced["even_in"][None]
    grads["even_w_out"] = reduced["even_out"][None]
    grads["ssm_w_in"] = reduced["ssm_in"][None]
    grads["ssm_w_out"] = reduced["ssm_out"][None]

    big = ("ada_w", "ffn_w1", "ffn_w2", "even_w_in", "even_w_out", "ssm_w_in", "ssm_w_out")
    delta, new_m, new_v = {}, {}, {}
    for k in big:
        shp = wts[k].shape
        two_d = lambda t: t.reshape(-1, shp[-1])
        res = _adamw(two_d(wts[k]), two_d(grads[k]), two_d(m_in[k]), two_d(v_in[k]), f"adamw_{k}")
        delta[k], new_m[k], new_v[k] = [t.reshape(shp) for t in res]
    little = [k for k in names if k not in big]
    shapes = [wts[k].shape for k in little]
    res = _adamw(*[_pack([src[k] for k in little], LANES) for src in (wts, grads, m_in, v_in)], "adamw_small")
    for out, packed_out in zip((delta, new_m, new_v), res):
        out.update(zip(little, _unpack(packed_out, shapes)))

    return (loss, gx[None], *[grads[k] for k in names], *[delta[k] for k in names],
            *[new_m[k] for k in names], *[new_v[k] for k in names])
```

```python
import functools
import math

import jax
import jax.numpy as jnp
from jax import lax
from jax.experimental import pallas as pl
from jax.experimental.pallas import tpu as pltpu

F32 = jnp.float32
BF16 = jnp.bfloat16
MESH = pl.DeviceIdType.MESH
ANY = pl.BlockSpec(memory_space=pl.ANY)
VMEM_FULL = pl.BlockSpec(memory_space=pltpu.VMEM)

NORM_EPS = 1e-6
N_CHIPS = 4
N_DEV = 8
LANES = 128
VMEM_LIMIT = 56 << 20

D_MODEL = 2048
ATTN_GROUPS = ((128, 1), (512, 4), (2048, 16))
ATTN_BLOCK = 128
ATTN_W = 512
HEAD_DIM = 64
POOL_WINDOWS = (2, 4, 8, 16)
POOL_W = 512
N_SLABS = 10
SSM_INNER = 4096
SSM_HEADS = 64
SSM_GROUPS = 8
SSM_STATE = 128
SSM_CHUNK = 128
SSM_CONV_DIM = 6144
SSM_GW = SSM_INNER // SSM_GROUPS
FFN_HIDDEN = 8192

ADAM_LR, ADAM_B1, ADAM_B2, ADAM_EPS, ADAM_WD, ADAM_STEP = 0.001, 0.9, 0.999, 1e-08, 0.01, 10


def _cparams(sem=None):
    return pltpu.CompilerParams(dimension_semantics=sem, vmem_limit_bytes=VMEM_LIMIT)


def _place():
    return lax.axis_index("x"), lax.axis_index("y"), lax.axis_index("c")


def _chip_index():
    return 2 * lax.axis_index("x") + lax.axis_index("y")


def _allgather8(v, name):
    m_per, n = v.shape

    def body(x_ref, out_ref, send_sems, recv_sems, local_sem):
        x, y, c = _place()
        me, sibling = (x, y, c), (x, y, 1 - c)
        chips = [(1 - x, y), (x, 1 - y), (1 - x, 1 - y)]

        def rows(px, py, pc):
            return out_ref.at[pl.ds((4 * px + 2 * py + pc) * m_per, m_per), :]

        def copy(k, block, to, src=None):
            return pltpu.make_async_remote_copy(
                src_ref=rows(*block) if src is None else src, dst_ref=rows(*block),
                send_sem=send_sems.at[k], recv_sem=recv_sems.at[k], device_id=to, device_id_type=MESH)

        mine = pltpu.make_async_copy(x_ref, rows(*me), local_sem)
        mine.start()
        first = [copy(0, me, sibling, src=x_ref)]
        first += [copy(1 + j, me, (*chip, c), src=x_ref) for j, chip in enumerate(chips)]
        for cp in first:
            cp.start()
        passed = [copy(4 + j, (*chip, c), sibling) for j, chip in enumerate(chips)]
        for j, chip in enumerate(chips):
            copy(1 + j, (*chip, c), me).wait_recv()
            passed[j].start()
        copy(0, sibling, me).wait_recv()
        for j, chip in enumerate(chips):
            copy(4 + j, (*chip, 1 - c), me).wait_recv()
        for cp in first + passed:
            cp.wait_send()
        mine.wait()

    return pl.pallas_call(
        body, name=name,
        out_shape=jax.ShapeDtypeStruct((N_DEV * m_per, n), v.dtype),
        in_specs=[VMEM_FULL], out_specs=VMEM_FULL,
        scratch_shapes=[pltpu.SemaphoreType.DMA((7,)), pltpu.SemaphoreType.DMA((7,)), pltpu.SemaphoreType.DMA],
    )(v)


def _sibling_send_other_half(g, name):
    n, r, ccols = g.shape
    h = r // 2

    def body(g_ref, out_ref, send_sem, recv_sem):
        x, y, c = _place()
        cp = pltpu.make_async_remote_copy(
            src_ref=g_ref.at[:, pl.ds((1 - c) * h, h), :], dst_ref=out_ref, send_sem=send_sem, recv_sem=recv_sem,
            device_id=(x, y, 1 - c), device_id_type=MESH)
        cp.start()
        cp.wait()

    return pl.pallas_call(
        body, name=name, out_shape=jax.ShapeDtypeStruct((n, h, ccols), g.dtype),
        in_specs=[ANY], out_specs=ANY,
        scratch_shapes=[pltpu.SemaphoreType.DMA, pltpu.SemaphoreType.DMA],
    )(g)


def _sibling_complete(f, name):
    h, ccols = f.shape

    def body(f_ref, out_ref, send_sem, recv_sem):
        x, y, c = _place()
        cp = pltpu.make_async_remote_copy(src_ref=f_ref, dst_ref=out_ref, send_sem=send_sem, recv_sem=recv_sem,
                                          device_id=(x, y, 1 - c), device_id_type=MESH)
        cp.start()
        cp.wait()

    other = pl.pallas_call(
        body, name=name, out_shape=jax.ShapeDtypeStruct((h, ccols), f.dtype),
        in_specs=[ANY], out_specs=ANY,
        scratch_shapes=[pltpu.SemaphoreType.DMA, pltpu.SemaphoreType.DMA],
    )(f)
    south = lax.axis_index("c") == 0
    return jnp.concatenate([jnp.where(south, f, other), jnp.where(south, other, f)], axis=0)


HBM_SPEC = pl.BlockSpec(memory_space=pltpu.HBM)
SEM_SPEC = pl.BlockSpec(memory_space=pltpu.SEMAPHORE)
N_PEERS = 3


def _split_params():
    return pltpu.CompilerParams(has_side_effects=pltpu.SideEffectType.DATAFLOW_SIDE_EFFECTING)


def _chip_exchange(mode, src_ref, land_ref, send_sems, recv_sems):
    x, y, c = _place()
    k_me = 2 * x + y
    pairs = []
    for j, chip in enumerate([(1 - x, y), (x, 1 - y), (1 - x, 1 - y)]):
        k_j = 2 * chip[0] + chip[1]
        if mode == "halves":
            h = src_ref.shape[0] // 2
            rows = pl.ds(c * h, h)
            src, there, here = src_ref.at[rows, :], land_ref.at[k_me, rows, :], land_ref.at[k_j, rows, :]
        else:
            src, there, here = src_ref.at[k_j], land_ref.at[k_me], land_ref.at[k_j]
        pairs.append(tuple(
            pltpu.make_async_remote_copy(src_ref=src, dst_ref=dst, send_sem=send_sems[j], recv_sem=recv_sems[j],
                                         device_id=(*chip, c), device_id_type=MESH) for dst in (there, here)))
    return pairs


def _exchange_start(src, land_shape, mode, name):
    def body(src_ref, land_ref, s0, s1, s2, r0, r1, r2, src_thru, land_thru, token):
        for start, _ in _chip_exchange(mode, src_ref, land_ref, (s0, s1, s2), (r0, r1, r2)):
            start.start()
        token[...] = jnp.zeros_like(token)

    sem = pltpu.SemaphoreType.DMA(())
    res = pl.pallas_call(
        body, name=name,
        out_shape=(sem,) * (2 * N_PEERS) + (pltpu.HBM(src.shape, src.dtype), pltpu.HBM(land_shape, src.dtype),
                                           jax.ShapeDtypeStruct((8, LANES), F32)),
        in_specs=(HBM_SPEC, HBM_SPEC), out_specs=(SEM_SPEC,) * (2 * N_PEERS) + (HBM_SPEC, HBM_SPEC, VMEM_FULL),
        input_output_aliases={0: 2 * N_PEERS, 1: 2 * N_PEERS + 1}, compiler_params=_split_params(),
    )(pltpu.with_memory_space_constraint(src, pltpu.HBM),
      pltpu.with_memory_space_constraint(lax.empty(land_shape, src.dtype), pltpu.HBM))
    return res[:2 * N_PEERS], res[2 * N_PEERS], res[2 * N_PEERS + 1], res[2 * N_PEERS + 2]


def _exchange_wait(sems, src_thru, land_thru, after, mode, name):
    def body(src_ref, land_ref, s0, s1, s2, r0, r1, r2, after_ref, src_out, land_out):
        waits = [w for _, w in _chip_exchange(mode, src_ref, land_ref, (s0, s1, s2), (r0, r1, r2))]
        for w in waits:
            w.wait_send()
        for w in waits:
            w.wait_recv()

    return pl.pallas_call(
        body, name=name,
        out_shape=(pltpu.HBM(src_thru.shape, src_thru.dtype), pltpu.HBM(land_thru.shape, land_thru.dtype)),
        in_specs=(HBM_SPEC, HBM_SPEC) + (SEM_SPEC,) * (2 * N_PEERS) + (ANY,), out_specs=(HBM_SPEC, HBM_SPEC),
        input_output_aliases={0: 0, 1: 1}, compiler_params=_split_params(),
    )(src_thru, land_thru, *sems, after)


def _sibling_fill(land, name):
    n, r, ccols = land.shape
    h = r // 2

    def body(land_ref, out_ref, send_sems, recv_sems):
        x, y, c = _place()
        slots = [2 * (1 - x) + y, 2 * x + (1 - y), 2 * (1 - x) + (1 - y)]

        def copy(j, rows):
            return pltpu.make_async_remote_copy(
                src_ref=land_ref.at[slots[j], pl.ds(c * h, h), :], dst_ref=out_ref.at[slots[j], rows, :],
                send_sem=send_sems.at[j], recv_sem=recv_sems.at[j], device_id=(x, y, 1 - c), device_id_type=MESH)

        sends = [copy(j, pl.ds(c * h, h)) for j in range(N_PEERS)]
        for cp in sends:
            cp.start()
        for j in range(N_PEERS):
            copy(j, pl.ds((1 - c) * h, h)).wait_recv()
        for cp in sends:
            cp.wait_send()

    return pl.pallas_call(
        body, name=name, out_shape=jax.ShapeDtypeStruct(land.shape, land.dtype), in_specs=[ANY], out_specs=ANY,
        input_output_aliases={0: 0},
        scratch_shapes=[pltpu.SemaphoreType.DMA((N_PEERS,)), pltpu.SemaphoreType.DMA((N_PEERS,))],
    )(land)


class _GatheredWeights:
    def __init__(self, shards):
        self._pending, self._done = {}, {}
        token = jnp.zeros((), BF16)
        for name, shard in shards.items():
            flat = shard.reshape(-1, shard.shape[-1]).astype(BF16) + token
            sems, thru, land, tok = _exchange_start(flat, (N_CHIPS, *flat.shape), "halves", f"gather_start_{name}")
            self._pending[name] = (sems, thru, land, shard.shape)
            token = tok[0, 0].astype(BF16)
        self.token = tok[0, 0]

    def take(self, name, after):
        if name not in self._done:
            sems, thru, land, shape = self._pending.pop(name)
            flat, land = _exchange_wait(sems, thru, land, after, "halves", f"gather_wait_{name}")
            land = _sibling_fill(land, f"gather_fill_{name}")
            land = lax.dynamic_update_index_in_dim(land, flat, _chip_index(), 0)
            self._done[name] = land.reshape(N_CHIPS, *shape)
        return self._done[name]


class _GradientExchange:
    def __init__(self):
        self._pending = []

    def put(self, name, slots):
        recv = _sibling_send_other_half(slots, f"rs_pair_{name}")
        part = _add_own_half(slots, recv, f"rs_add2_{name}")
        sems, thru, land, tok = _exchange_start(part, part.shape, "slots", f"rs_start_{name}")
        self._pending.append((name, sems, thru, land))
        return tok[0, 0]

    def finish(self, after):
        out = {}
        k_me = _chip_index()
        for name, sems, thru, land in self._pending:
            part, land = _exchange_wait(sems, thru, land, after, "slots", f"rs_wait_{name}")
            got = lax.dynamic_update_index_in_dim(land, lax.dynamic_index_in_dim(part, k_me, 0, keepdims=False), k_me, 0)
            out[name] = _sibling_complete(_sum_slots(got, f"rs_add4_{name}"), f"rs_fin_{name}")
        return out


def _row_tile(rows, cols, itemsize, budget=2 << 20):
    t = rows
    while t % 2 == 0 and t * cols * itemsize > budget and (t // 2) % 16 == 0:
        t //= 2
    return t


def _add_own_half(g, recv, name):
    n, r, ccols = g.shape
    h = r // 2
    t = _row_tile(h, ccols, 4)
    nt = h // t
    c_idx = lax.axis_index("c").astype(jnp.int32).reshape(1)

    def body(c_ref, g_ref, r_ref, o_ref):
        o_ref[...] = (g_ref[...].astype(F32) + r_ref[...].astype(F32)).astype(o_ref.dtype)

    grid_spec = pltpu.PrefetchScalarGridSpec(
        num_scalar_prefetch=1, grid=(n, nt),
        in_specs=[pl.BlockSpec((None, t, ccols), lambda j, i, c_ref: (j, c_ref[0] * nt + i, 0)),
                  pl.BlockSpec((None, t, ccols), lambda j, i, c_ref: (j, i, 0))],
        out_specs=pl.BlockSpec((None, t, ccols), lambda j, i, c_ref: (j, i, 0)))
    return pl.pallas_call(body, name=name, grid_spec=grid_spec,
                          out_shape=jax.ShapeDtypeStruct((n, h, ccols), BF16),
                          compiler_params=_cparams(("parallel", "parallel")))(c_idx, g, recv)


def _sum_slots(q, name):
    n, h, ccols = q.shape
    t = _row_tile(h, ccols, 4)

    def body(q_ref, o_ref):
        acc = q_ref[0].astype(F32)
        for j in range(1, n):
            acc = acc + q_ref[j].astype(F32)
        o_ref[...] = acc

    return pl.pallas_call(
        body, name=name, grid=(h // t,),
        in_specs=[pl.BlockSpec((n, t, ccols), lambda i: (0, i, 0))],
        out_specs=pl.BlockSpec((t, ccols), lambda i: (i, 0)),
        out_shape=jax.ShapeDtypeStruct((h, ccols), F32), compiler_params=_cparams(("parallel",)))(q)


def _sum_devices(v, name):
    n, r, ccols = v.shape
    t = 8
    while r % (t * 2) == 0 and t * 2 * ccols * 4 * n <= (8 << 20):
        t *= 2

    def body(v_ref, o_ref):
        acc = v_ref[0]
        for j in range(1, n):
            acc = acc + v_ref[j]
        o_ref[...] = acc

    return pl.pallas_call(
        body, name=name, grid=(r // t,),
        in_specs=[pl.BlockSpec((n, t, ccols), lambda i: (0, i, 0))],
        out_specs=pl.BlockSpec((t, ccols), lambda i: (i, 0)),
        out_shape=jax.ShapeDtypeStruct((r, ccols), F32), compiler_params=_cparams(("parallel",)))(v)


def _adamw(w, g, m, v, name):
    r, ccols = w.shape
    t = _row_tile(r, ccols, 4, budget=1 << 20)
    c1 = 1.0 / (1.0 - ADAM_B1 ** ADAM_STEP)
    c2 = 1.0 / (1.0 - ADAM_B2 ** ADAM_STEP)

    def body(w_ref, g_ref, m_ref, v_ref, d_ref, nm_ref, nv_ref):
        gg = g_ref[...]
        nm = ADAM_B1 * m_ref[...] + (1.0 - ADAM_B1) * gg
        nv = ADAM_B2 * v_ref[...] + (1.0 - ADAM_B2) * (gg * gg)
        d_ref[...] = -ADAM_LR * ((nm * c1) / (jnp.sqrt(nv * c2) + ADAM_EPS) + ADAM_WD * w_ref[...])
        nm_ref[...] = nm
        nv_ref[...] = nv

    spec = pl.BlockSpec((t, ccols), lambda i: (i, 0))
    sds = jax.ShapeDtypeStruct((r, ccols), F32)
    return pl.pallas_call(body, name=name, grid=(r // t,), in_specs=[spec] * 4, out_specs=[spec] * 3,
                          out_shape=[sds] * 3, compiler_params=_cparams(("parallel",)))(w, g, m, v)


NN = (((1,), (0,)), ((), ()))
NT = (((1,), (1,)), ((), ()))
TN = (((0,), (0,)), ((), ()))


def _mm_core(name, a, b, *, dims, grid, a_spec, b_spec, out_shapes, out_specs, acc_shape,
             extras=(), extra_specs=(), epilogue=None):
    nk = grid[2]
    n_ex, n_out = len(extras), len(out_shapes)
    if epilogue is None:
        def epilogue(acc, ex, outs):
            outs[0][...] = acc.astype(outs[0].dtype)

    def body(*refs):
        a_ref, b_ref = refs[0], refs[1]
        ex = refs[2:2 + n_ex]
        outs = refs[2 + n_ex:2 + n_ex + n_out]
        part = lax.dot_general(a_ref[...].astype(BF16), b_ref[...].astype(BF16), dims, preferred_element_type=F32)
        if nk == 1:
            epilogue(part, ex, outs)
        else:
            acc = refs[-1]
            k = pl.program_id(2)

            @pl.when(k == 0)
            def _():
                acc[...] = part

            @pl.when(k > 0)
            def _():
                acc[...] += part

            @pl.when(k == nk - 1)
            def _():
                epilogue(acc[...], ex, outs)

    scratch = [] if nk == 1 else [pltpu.VMEM(acc_shape, F32)]
    res = pl.pallas_call(
        body, name=name, grid=grid, in_specs=[a_spec, b_spec, *extra_specs], out_specs=list(out_specs),
        out_shape=list(out_shapes), scratch_shapes=scratch,
        compiler_params=_cparams(("parallel", "parallel", "arbitrary")))(a, b, *extras)
    return res


def _mm_nn(name, a, b, tm, tn, tk, out_dtypes=(BF16,), extras=(), extra_specs=(), epilogue=None, b_tile=None, n=None):
    m, kk = a.shape
    n = b.shape[1] if b_tile is None else n
    tm, tn, tk = min(tm, m), min(tn, n), min(tk, kk)
    grid = (m // tm, n // tn, kk // tk)
    if b_tile is None:
        b_spec = pl.BlockSpec((tk, tn), lambda i, j, k: (k, j))
    else:
        b_spec = pl.BlockSpec((None,) * (b.ndim - 2) + (tk, tn), lambda i, j, k: b_tile(j, k))
    return _mm_core(
        name, a, b, dims=NN, grid=grid,
        a_spec=pl.BlockSpec((tm, tk), lambda i, j, k: (i, k)), b_spec=b_spec,
        out_shapes=[jax.ShapeDtypeStruct((m, n), dt) for dt in out_dtypes],
        out_specs=[pl.BlockSpec((tm, tn), lambda i, j, k: (i, j)) for _ in out_dtypes],
        acc_shape=(tm, tn), extras=extras, extra_specs=extra_specs, epilogue=epilogue)


def _mm_nt(name, a, b, tm, tn, tk, out_dtypes=(BF16,), extras=(), extra_specs=(), epilogue=None, b_tile=None, n=None):
    m, kk = a.shape
    n = b.shape[0] if b_tile is None else n
    tm, tn, tk = min(tm, m), min(tn, n), min(tk, kk)
    grid = (m // tm, n // tn, kk // tk)
    if b_tile is None:
        b_spec = pl.BlockSpec((tn, tk), lambda i, j, k: (j, k))
    else:
        b_spec = pl.BlockSpec((None,) * (b.ndim - 2) + (tn, tk), lambda i, j, k: b_tile(j, k))
    return _mm_core(
        name, a, b, dims=NT, grid=grid,
        a_spec=pl.BlockSpec((tm, tk), lambda i, j, k: (i, k)), b_spec=b_spec,
        out_shapes=[jax.ShapeDtypeStruct((m, n), dt) for dt in out_dtypes],
        out_specs=[pl.BlockSpec((tm, tn), lambda i, j, k: (i, j)) for _ in out_dtypes],
        acc_shape=(tm, tn), extras=extras, extra_specs=extra_specs, epilogue=epilogue)


def _mm_tn(name, a, b, tm, tn, tk, out_dtype=BF16, col_chunks=1):
    kk, m = a.shape
    n = b.shape[1]
    tm, tn, tk = min(tm, m), min(tn, n), min(tk, kk)
    grid = (m // tm, n // tn, kk // tk)
    if col_chunks == 1:
        out_shape, out_spec = (m, n), pl.BlockSpec((tm, tn), lambda i, j, k: (i, j))
    else:
        per = n // col_chunks // tn
        out_shape = (col_chunks, m, n // col_chunks)
        out_spec = pl.BlockSpec((None, tm, tn), lambda i, j, k: (j // per, i, j % per))
    return _mm_core(
        name, a, b, dims=TN, grid=grid,
        a_spec=pl.BlockSpec((tk, tm), lambda i, j, k: (k, i)), b_spec=pl.BlockSpec((tk, tn), lambda i, j, k: (k, j)),
        out_shapes=[jax.ShapeDtypeStruct(out_shape, out_dtype)], out_specs=[out_spec], acc_shape=(tm, tn))[0]


def _resid_gate_epilogue(acc, ex, outs):
    outs[0][...] = ex[0][...] + ex[1][...] * acc
    outs[1][...] = acc.astype(BF16)


def _mm_resid(name, a, b, resid, gate, tm, tn, tk, b_tile=None, n=None):
    return _mm_nn(
        name, a, b, tm, tn, tk, out_dtypes=(F32, BF16), extras=(resid, gate),
        extra_specs=(pl.BlockSpec((tm, tn), lambda i, j, k: (i, j)), pl.BlockSpec((1, tn), lambda i, j, k: (0, j))),
        epilogue=_resid_gate_epilogue, b_tile=b_tile, n=n)


TOK_TILE = 512


def _colsum8(v):
    t, ccols = v.shape
    return jnp.sum(v.reshape(t // 8, 8, ccols), axis=0)


def _norm_mod_fwd(x, g, sc, sh, name):
    s, d = x.shape
    t = TOK_TILE

    def body(x_ref, g_ref, sc_ref, sh_ref, h_ref):
        xv = x_ref[...]
        n = xv * lax.rsqrt(jnp.mean(xv * xv, axis=-1, keepdims=True) + NORM_EPS)
        h_ref[...] = ((n * g_ref[...]) * (1.0 + sc_ref[...]) + sh_ref[...]).astype(BF16)

    row = pl.BlockSpec((1, d), lambda i: (0, 0))
    return pl.pallas_call(
        body, name=name, grid=(s // t,), in_specs=[pl.BlockSpec((t, d), lambda i: (i, 0)), row, row, row],
        out_specs=pl.BlockSpec((t, d), lambda i: (i, 0)), out_shape=jax.ShapeDtypeStruct((s, d), BF16),
        compiler_params=_cparams(("parallel",)))(x, g, sc, sh)


def _norm_mod_bwd(x, dh, resid, g, sc, name):
    s, d = x.shape
    t = TOK_TILE

    def body(x_ref, dh_ref, r_ref, g_ref, sc_ref, dx_ref, dsh_ref, dsc_ref, dg_ref):
        i = pl.program_id(0)
        xv = x_ref[...]
        rstd = lax.rsqrt(jnp.mean(xv * xv, axis=-1, keepdims=True) + NORM_EPS)
        n = xv * rstd
        dhv = dh_ref[...].astype(F32)
        gv = g_ref[...]
        dyn = dhv * (1.0 + sc_ref[...])
        dn = dyn * gv
        dx_ref[...] = r_ref[...] + rstd * (dn - n * jnp.mean(dn * n, axis=-1, keepdims=True))

        @pl.when(i == 0)
        def _():
            dsh_ref[...] = jnp.zeros_like(dsh_ref)
            dsc_ref[...] = jnp.zeros_like(dsc_ref)
            dg_ref[...] = jnp.zeros_like(dg_ref)

        dsh_ref[...] += _colsum8(dhv)
        dsc_ref[...] += _colsum8(dhv * (n * gv))
        dg_ref[...] += _colsum8(dyn * n)

    tile = pl.BlockSpec((t, d), lambda i: (i, 0))
    row = pl.BlockSpec((1, d), lambda i: (0, 0))
    acc = pl.BlockSpec((8, d), lambda i: (0, 0))
    acc_s = jax.ShapeDtypeStruct((8, d), F32)
    return pl.pallas_call(
        body, name=name, grid=(s // t,), in_specs=[tile, tile, tile, row, row],
        out_specs=[tile, acc, acc, acc], out_shape=[jax.ShapeDtypeStruct((s, d), F32), acc_s, acc_s, acc_s],
        compiler_params=_cparams(("arbitrary",)))(x, dh, resid, g, sc)


def _gate_bwd(dxo, y, gate, name):
    s, d = dxo.shape
    t = TOK_TILE

    def body(dx_ref, y_ref, g_ref, dy_ref, dg_ref):
        i = pl.program_id(0)
        dxv = dx_ref[...]
        dy_ref[...] = (dxv * g_ref[...]).astype(BF16)

        @pl.when(i == 0)
        def _():
            dg_ref[...] = jnp.zeros_like(dg_ref)

        dg_ref[...] += _colsum8(dxv * y_ref[...].astype(F32))

    tile = pl.BlockSpec((t, d), lambda i: (i, 0))
    return pl.pallas_call(
        body, name=name, grid=(s // t,), in_specs=[tile, tile, pl.BlockSpec((1, d), lambda i: (0, 0))],
        out_specs=[tile, pl.BlockSpec((8, d), lambda i: (0, 0))],
        out_shape=[jax.ShapeDtypeStruct((s, d), BF16), jax.ShapeDtypeStruct((8, d), F32)],
        compiler_params=_cparams(("arbitrary",)))(dxo, y, gate)


def _final_loss(x, g, target, name):
    s, d = x.shape
    t = TOK_TILE
    inv_d = 1.0 / d

    def body(x_ref, g_ref, t_ref, dx_ref, sq_ref, dg_ref):
        i = pl.program_id(0)
        xv = x_ref[...]
        rstd = lax.rsqrt(jnp.mean(xv * xv, axis=-1, keepdims=True) + NORM_EPS)
        n = xv * rstd
        gv = g_ref[...]
        err = n * gv - t_ref[...]
        dout = err * inv_d
        dn = dout * gv
        dx_ref[...] = rstd * (dn - n * jnp.mean(dn * n, axis=-1, keepdims=True))

        @pl.when(i == 0)
        def _():
            sq_ref[...] = jnp.zeros_like(sq_ref)
            dg_ref[...] = jnp.zeros_like(dg_ref)

        sq_ref[...] += _colsum8(err * err)
        dg_ref[...] += _colsum8(dout * n)

    tile = pl.BlockSpec((t, d), lambda i: (i, 0))
    acc = pl.BlockSpec((8, d), lambda i: (0, 0))
    acc_s = jax.ShapeDtypeStruct((8, d), F32)
    return pl.pallas_call(
        body, name=name, grid=(s // t,), in_specs=[tile, pl.BlockSpec((1, d), lambda i: (0, 0)), tile],
        out_specs=[tile, acc, acc], out_shape=[jax.ShapeDtypeStruct((s, d), F32), acc_s, acc_s],
        compiler_params=_cparams(("arbitrary",)))(x, g, target)


NEG_BIG = -1e30


def _head_masks(rows):
    lane = lax.broadcasted_iota(jnp.int32, (rows, LANES), 1)
    return [(lane // HEAD_DIM) == hh for hh in range(2)]


def _group_view(slabs, gi):
    dil = ATTN_GROUPS[gi][1]
    _, s, w = slabs.shape
    if dil == 1:
        return slabs, (gi, 3 + gi, 6 + gi)
    return slabs[gi:9:3].reshape(3, s // dil, dil * w), (0, 1, 2)


def _attn_fwd(slabs, gi, name):
    window, dil = ATTN_GROUPS[gi]
    n_back = window // dil
    _, s, w = slabs.shape
    big_l = s // dil
    nb = big_l // ATTN_BLOCK
    blk = ATTN_BLOCK
    view, (iq, ik, iv) = _group_view(slabs, gi)
    scale = HEAD_DIM ** -0.5

    def body(q_ref, kp_ref, kc_ref, vp_ref, vc_ref, o_ref, lse_ref):
        n = pl.program_id(1)
        qi = lax.broadcasted_iota(jnp.int32, (blk, 2 * blk), 0)
        kj = lax.broadcasted_iota(jnp.int32, (blk, 2 * blk), 1)
        dist = qi + blk - kj
        ok = (dist >= 0) & (dist <= n_back) & ((kj >= blk) | (n > 0))
        mq = _head_masks(blk)
        mk = _head_masks(2 * blk)
        for p in range(w // LANES):
            cols = slice(p * LANES, (p + 1) * LANES)
            qp = q_ref[:, cols]
            k2 = jnp.concatenate([kp_ref[:, cols], kc_ref[:, cols]], axis=0)
            v2 = jnp.concatenate([vp_ref[:, cols], vc_ref[:, cols]], axis=0)
            o_pair = jnp.zeros((blk, LANES), F32)
            lse_pair = jnp.zeros((blk, LANES), F32)
            for hh in range(2):
                qm = jnp.where(mq[hh], qp, jnp.zeros_like(qp))
                sc = lax.dot_general(qm, k2, NT, preferred_element_type=F32) * scale
                sc = jnp.where(ok, sc, NEG_BIG)
                mx = jnp.max(sc, axis=-1, keepdims=True)
                pe = jnp.exp(sc - mx)
                den = jnp.sum(pe, axis=-1, keepdims=True)
                pn = (pe / den).astype(BF16)
                vm = jnp.where(mk[hh], v2, jnp.zeros_like(v2))
                o_pair = o_pair + jnp.dot(pn, vm, preferred_element_type=F32)
                lse_pair = jnp.where(mq[hh], mx + jnp.log(den), lse_pair)
            o_ref[:, cols] = o_pair
            lse_ref[:, cols] = lse_pair

    def spec(slab, prev):
        if prev:
            return pl.BlockSpec((None, blk, w), lambda r, n: (slab, jnp.maximum(n - 1, 0), r))
        return pl.BlockSpec((None, blk, w), lambda r, n: (slab, n, r))

    out_spec = pl.BlockSpec((blk, w), lambda r, n: (n, r))
    sds = jax.ShapeDtypeStruct((big_l, dil * w), F32)
    o, lse = pl.pallas_call(
        body, name=name, grid=(dil, nb),
        in_specs=[spec(iq, False), spec(ik, True), spec(ik, False), spec(iv, True), spec(iv, False)],
        out_specs=[out_spec, out_spec], out_shape=[sds, sds],
        compiler_params=_cparams(("parallel", "arbitrary")))(view, view, view, view, view)
    return o.reshape(s, w), lse.reshape(s, w)


def _attn_merge(outs, lses, name):
    s, w = outs[0].shape
    t = TOK_TILE

    def body(o0, o1, o2, l0, l1, l2, a_ref, lt_ref):
        ls = [l0[...], l1[...], l2[...]]
        mx = jnp.maximum(jnp.maximum(ls[0], ls[1]), ls[2])
        es = [jnp.exp(l - mx) for l in ls]
        den = es[0] + es[1] + es[2]
        num = es[0] * o0[...] + es[1] * o1[...] + es[2] * o2[...]
        a_ref[...] = (num / den).astype(BF16)
        lt_ref[...] = mx + jnp.log(den)

    tile = pl.BlockSpec((t, w), lambda i: (i, 0))
    return pl.pallas_call(
        body, name=name, grid=(s // t,), in_specs=[tile] * 6, out_specs=[tile, tile],
        out_shape=[jax.ShapeDtypeStruct((s, w), BF16), jax.ShapeDtypeStruct((s, w), F32)],
        compiler_params=_cparams(("parallel",)))(*outs, *lses)


def _attn_bwd(slabs, dap, ap, lse_tot, gi, name):
    window, dil = ATTN_GROUPS[gi]
    n_back = window // dil
    _, s, w = slabs.shape
    big_l = s // dil
    nb = big_l // ATTN_BLOCK
    blk = ATTN_BLOCK
    view, (iq, ik, iv) = _group_view(slabs, gi)
    dap_v = dap.reshape(big_l, dil * 2 * w)
    ap_v = ap.reshape(big_l, dil * 2 * w)
    lse_v = lse_tot.reshape(big_l, dil * w)
    scale = HEAD_DIM ** -0.5

    def body(qc_ref, qn_ref, kp_ref, kc_ref, vp_ref, vc_ref, dc_ref, dn_ref, ac_ref, an_ref, lc_ref, ln_ref, out_ref):
        m = pl.program_id(1)
        qi = lax.broadcasted_iota(jnp.int32, (blk, 2 * blk), 0)
        kj = lax.broadcasted_iota(jnp.int32, (blk, 2 * blk), 1)
        dist = qi + blk - kj
        ok = (dist >= 0) & (dist <= n_back) & ((kj >= blk) | (m > 0))
        qi1 = lax.broadcasted_iota(jnp.int32, (blk, blk), 0)
        kj1 = lax.broadcasted_iota(jnp.int32, (blk, blk), 1)
        ok_next = (qi1 + blk - kj1 <= n_back) & (m + 1 < nb)
        mq = _head_masks(blk)
        mk = _head_masks(2 * blk)

        def per_head(q_t, d_t, a_t, l_t, hh):
            qm = jnp.where(mq[hh], q_t, jnp.zeros_like(q_t))
            dm = jnp.where(mq[hh], d_t, jnp.zeros_like(d_t))
            delta = jnp.sum(jnp.where(mq[hh], d_t.astype(F32) * a_t.astype(F32), 0.0), axis=-1, keepdims=True)
            lse_h = jnp.max(jnp.where(mq[hh], l_t, NEG_BIG), axis=-1, keepdims=True)
            return qm, dm, delta, lse_h

        for p in range(w // LANES):
            cols = slice(p * LANES, (p + 1) * LANES)
            k2 = jnp.concatenate([kp_ref[:, cols], kc_ref[:, cols]], axis=0)
            v2 = jnp.concatenate([vp_ref[:, cols], vc_ref[:, cols]], axis=0)
            kc, vc = kc_ref[:, cols], vc_ref[:, cols]
            dq_pair = jnp.zeros((blk, LANES), F32)
            dk_pair = jnp.zeros((blk, LANES), F32)
            dv_pair = jnp.zeros((blk, LANES), F32)
            for hh in range(2):
                qm, dm, delta, lse_h = per_head(qc_ref[:, cols], dc_ref[:, cols], ac_ref[:, cols], lc_ref[:, cols], hh)
                sc = lax.dot_general(qm, k2, NT, preferred_element_type=F32) * scale
                pr = jnp.exp(jnp.where(ok, sc, NEG_BIG) - lse_h)
                dp = lax.dot_general(dm, v2, NT, preferred_element_type=F32)
                ds = pr * (dp - delta)
                ds_b = ds.astype(BF16)
                km = jnp.where(mk[hh], k2, jnp.zeros_like(k2))
                dq_pair = dq_pair + jnp.dot(ds_b, km, preferred_element_type=F32)
                dk_pair = dk_pair + lax.dot_general(ds_b[:, blk:], qm, TN, preferred_element_type=F32)
                dv_pair = dv_pair + lax.dot_general(pr[:, blk:].astype(BF16), dm, TN, preferred_element_type=F32)
                qm, dm, delta, lse_h = per_head(qn_ref[:, cols], dn_ref[:, cols], an_ref[:, cols], ln_ref[:, cols], hh)
                sc = lax.dot_general(qm, kc, NT, preferred_element_type=F32) * scale
                pr = jnp.exp(jnp.where(ok_next, sc, NEG_BIG) - lse_h)
                dp = lax.dot_general(dm, vc, NT, preferred_element_type=F32)
                ds_b = (pr * (dp - delta)).astype(BF16)
                dk_pair = dk_pair + lax.dot_general(ds_b, qm, TN, preferred_element_type=F32)
                dv_pair = dv_pair + lax.dot_general(pr.astype(BF16), dm, TN, preferred_element_type=F32)
            out_ref[0, :, cols] = (dq_pair * scale).astype(BF16)
            out_ref[1, :, cols] = (dk_pair * scale).astype(BF16)
            out_ref[2, :, cols] = dv_pair.astype(BF16)

    def slab_spec(slab, shift):
        if shift < 0:
            return pl.BlockSpec((None, blk, w), lambda r, n: (slab, jnp.maximum(n - 1, 0), r))
        if shift > 0:
            return pl.BlockSpec((None, blk, w), lambda r, n: (slab, jnp.minimum(n + 1, nb - 1), r))
        return pl.BlockSpec((None, blk, w), lambda r, n: (slab, n, r))

    def tok_spec(stride, shift):
        if shift > 0:
            return pl.BlockSpec((blk, w), lambda r, n: (jnp.minimum(n + 1, nb - 1), stride * r))
        return pl.BlockSpec((blk, w), lambda r, n: (n, stride * r))

    out = pl.pallas_call(
        body, name=name, grid=(dil, nb),
        in_specs=[slab_spec(iq, 0), slab_spec(iq, 1), slab_spec(ik, -1), slab_spec(ik, 0),
                  slab_spec(iv, -1), slab_spec(iv, 0),
                  tok_spec(2, 0), tok_spec(2, 1), tok_spec(2, 0), tok_spec(2, 1), tok_spec(1, 0), tok_spec(1, 1)],
        out_specs=pl.BlockSpec((3, blk, w), lambda r, n: (0, n, r)),
        out_shape=jax.ShapeDtypeStruct((3, big_l, dil * w), BF16),
        compiler_params=_cparams(("parallel", "arbitrary")))(
            view, view, view, view, view, view, dap_v, dap_v, ap_v, ap_v, lse_v, lse_v)
    return out.reshape(3, s, w)


def _shift_down(cur, prev, j):
    row = lax.broadcasted_iota(jnp.int32, cur.shape, 0)
    return jnp.where(row >= j, pltpu.roll(cur, j, 0), pltpu.roll(prev, j, 0))


def _shift_up(cur, nxt, j):
    t = cur.shape[0]
    row = lax.broadcasted_iota(jnp.int32, cur.shape, 0)
    return jnp.where(row < t - j, pltpu.roll(cur, t - j, 0), pltpu.roll(nxt, t - j, 0))


def _window_sum_down(cur, prev, w):
    s, sp, step = cur, prev, 1
    while step < w:
        s_new = s + _shift_down(s, sp, step)
        sp = sp + pltpu.roll(sp, step, 0)
        s, step = s_new, step * 2
    return s


def _window_sum_up(cur, nxt, w):
    t = cur.shape[0]
    s, sn, step = cur, nxt, 1
    while step < w:
        s_new = s + _shift_up(s, sn, step)
        sn = sn + pltpu.roll(sn, t - step, 0)
        s, step = s_new, step * 2
    return s


def _pool_count(tile_idx, t, w):
    row = lax.broadcasted_iota(jnp.int32, (t, LANES), 0) + tile_idx * t
    return jnp.minimum(row + 1, w).astype(F32)


def _pool_fwd(slabs, attn, pool_w, pool_scale, name):
    _, s, w = slabs.shape
    t = TOK_TILE
    gw = POOL_W // len(POOL_WINDOWS)

    def body(u_ref, up_ref, a_ref, w_ref, sc_ref, ap_ref, d_ref):
        i = pl.program_id(0)
        ap_ref[:, :w] = a_ref[...]
        for gi, win in enumerate(POOL_WINDOWS):
            cols = slice(gi * gw, (gi + 1) * gw)
            u = u_ref[:, cols].astype(F32)
            up = jnp.where(i > 0, up_ref[:, cols].astype(F32), 0.0)
            d = (_window_sum_down(u, up, win) / _pool_count(i, t, win) - u).astype(BF16)
            d_ref[:, cols] = d
            y = jnp.dot(d, w_ref[gi].astype(BF16), preferred_element_type=F32)
            ap_ref[:, w + gi * gw:w + (gi + 1) * gw] = (y * sc_ref[:, cols]).astype(BF16)

    return pl.pallas_call(
        body, name=name, grid=(s // t,),
        in_specs=[pl.BlockSpec((None, t, w), lambda i: (N_SLABS - 1, i, 0)),
                  pl.BlockSpec((None, t, w), lambda i: (N_SLABS - 1, jnp.maximum(i - 1, 0), 0)),
                  pl.BlockSpec((t, w), lambda i: (i, 0)),
                  pl.BlockSpec((len(POOL_WINDOWS), gw, gw), lambda i: (0, 0, 0)),
                  pl.BlockSpec((1, w), lambda i: (0, 0))],
        out_specs=[pl.BlockSpec((t, 2 * w), lambda i: (i, 0)), pl.BlockSpec((t, w), lambda i: (i, 0))],
        out_shape=[jax.ShapeDtypeStruct((s, 2 * w), BF16), jax.ShapeDtypeStruct((s, w), BF16)],
        compiler_params=_cparams(("parallel",)))(slabs, slabs, attn, pool_w, pool_scale)


def _pool_bwd(dap, d, pool_w, pool_scale, name):
    s, w = d.shape
    t = TOK_TILE
    nt = s // t
    gw = POOL_W // len(POOL_WINDOWS)

    def body(dy_ref, dyn_ref, d_ref, w_ref, sc_ref, du_ref, dw_ref, dsc_ref):
        i = pl.program_id(0)

        @pl.when(i == 0)
        def _():
            dw_ref[...] = jnp.zeros_like(dw_ref)
            dsc_ref[...] = jnp.zeros_like(dsc_ref)

        for gi, win in enumerate(POOL_WINDOWS):
            cols = slice(gi * gw, (gi + 1) * gw)
            wb = w_ref[gi].astype(BF16)
            scale = sc_ref[:, cols]
            dv = d_ref[:, cols]
            dy = dy_ref[:, cols].astype(F32)
            y = jnp.dot(dv, wb, preferred_element_type=F32)
            dsc_ref[:, cols] += _colsum8(dy * y)
            dyp = (dy * scale).astype(BF16)
            dw_ref[gi] += lax.dot_general(dv, dyp, TN, preferred_element_type=F32)
            dd = lax.dot_general(dyp, wb, NT, preferred_element_type=F32)
            dypn = (dyn_ref[:, cols].astype(F32) * scale).astype(BF16)
            ddn = lax.dot_general(dypn, wb, NT, preferred_element_type=F32)
            e = dd / _pool_count(i, t, win)
            en = jnp.where(i + 1 < nt, ddn / _pool_count(i + 1, t, win), 0.0)
            du_ref[:, cols] = (_window_sum_up(e, en, win) - dd).astype(BF16)

    return pl.pallas_call(
        body, name=name, grid=(nt,),
        in_specs=[pl.BlockSpec((t, w), lambda i: (i, 1)),
                  pl.BlockSpec((t, w), lambda i: (jnp.minimum(i + 1, nt - 1), 1)),
                  pl.BlockSpec((t, w), lambda i: (i, 0)),
                  pl.BlockSpec((len(POOL_WINDOWS), gw, gw), lambda i: (0, 0, 0)),
                  pl.BlockSpec((1, w), lambda i: (0, 0))],
        out_specs=[pl.BlockSpec((t, w), lambda i: (i, 0)),
                   pl.BlockSpec((len(POOL_WINDOWS), gw, gw), lambda i: (0, 0, 0)),
                   pl.BlockSpec((8, w), lambda i: (0, 0))],
        out_shape=[jax.ShapeDtypeStruct((s, w), BF16), jax.ShapeDtypeStruct((len(POOL_WINDOWS), gw, gw), F32),
                   jax.ShapeDtypeStruct((8, w), F32)],
        compiler_params=_cparams(("arbitrary",)))(dap, dap, d, pool_w, pool_scale)


CONV_TAPS = 4
CONV_COLS = 512


def _sigmoid(v):
    return 1.0 / (1.0 + jnp.exp(-v))


def _conv_pre(x, xprev, w_ref, b_ref):
    pre = b_ref[...] + w_ref[CONV_TAPS - 1:CONV_TAPS, :] * x
    for k in range(CONV_TAPS - 1):
        pre = pre + w_ref[k:k + 1, :] * _shift_down(x, xprev, CONV_TAPS - 1 - k)
    return pre


def _conv_fwd(xbc, conv_w, conv_b, name):
    s, c = xbc.shape
    t, tc = TOK_TILE, CONV_COLS

    def body(x_ref, xp_ref, w_ref, b_ref, o_ref):
        i = pl.program_id(0)
        x = x_ref[...].astype(F32)
        xp = jnp.where(i > 0, xp_ref[...].astype(F32), 0.0)
        pre = _conv_pre(x, xp, w_ref, b_ref)
        o_ref[...] = (pre * _sigmoid(pre)).astype(BF16)

    return pl.pallas_call(
        body, name=name, grid=(s // t, c // tc),
        in_specs=[pl.BlockSpec((t, tc), lambda i, j: (i, j)),
                  pl.BlockSpec((t, tc), lambda i, j: (jnp.maximum(i - 1, 0), j)),
                  pl.BlockSpec((CONV_TAPS, tc), lambda i, j: (0, j)), pl.BlockSpec((1, tc), lambda i, j: (0, j))],
        out_specs=pl.BlockSpec((t, tc), lambda i, j: (i, j)), out_shape=jax.ShapeDtypeStruct((s, c), BF16),
        compiler_params=_cparams(("parallel", "parallel")))(xbc, xbc, conv_w, conv_b)


def _conv_bwd(xbc, dact, conv_w, conv_b, name):
    s, c = xbc.shape
    t, tc = TOK_TILE, CONV_COLS
    nt = s // t

    def body(xp_ref, x_ref, xn_ref, da_ref, dan_ref, w_ref, b_ref, dx_ref, dw_ref, db_ref):
        i = pl.program_id(1)

        @pl.when(i == 0)
        def _():
            dw_ref[...] = jnp.zeros_like(dw_ref)
            db_ref[...] = jnp.zeros_like(db_ref)

        x = x_ref[...].astype(F32)
        xp = jnp.where(i > 0, xp_ref[...].astype(F32), 0.0)
        xn = xn_ref[...].astype(F32)

        def dsilu(pre):
            sg = _sigmoid(pre)
            return sg * (1.0 + pre * (1.0 - sg))

        dpre = da_ref[...] * dsilu(_conv_pre(x, xp, w_ref, b_ref))
        dpre_n = jnp.where(i + 1 < nt, dan_ref[...] * dsilu(_conv_pre(xn, x, w_ref, b_ref)), 0.0)
        dx = w_ref[CONV_TAPS - 1:CONV_TAPS, :] * dpre
        dw_ref[CONV_TAPS - 1] += _colsum8(dpre * x)
        for k in range(CONV_TAPS - 1):
            j = CONV_TAPS - 1 - k
            dx = dx + w_ref[k:k + 1, :] * _shift_up(dpre, dpre_n, j)
            dw_ref[k] += _colsum8(dpre * _shift_down(x, xp, j))
        dx_ref[...] = dx.astype(BF16)
        db_ref[...] += _colsum8(dpre)

    def xspec(shift):
        if shift < 0:
            return pl.BlockSpec((t, tc), lambda j, i: (jnp.maximum(i - 1, 0), j))
        if shift > 0:
            return pl.BlockSpec((t, tc), lambda j, i: (jnp.minimum(i + 1, nt - 1), j))
        return pl.BlockSpec((t, tc), lambda j, i: (i, j))

    return pl.pallas_call(
        body, name=name, grid=(c // tc, nt),
        in_specs=[xspec(-1), xspec(0), xspec(1), xspec(0), xspec(1),
                  pl.BlockSpec((CONV_TAPS, tc), lambda j, i: (0, j)), pl.BlockSpec((1, tc), lambda j, i: (0, j))],
        out_specs=[xspec(0), pl.BlockSpec((CONV_TAPS, 8, tc), lambda j, i: (0, 0, j)),
                   pl.BlockSpec((8, tc), lambda j, i: (0, j))],
        out_shape=[jax.ShapeDtypeStruct((s, c), BF16), jax.ShapeDtypeStruct((CONV_TAPS, 8, c), F32),
                   jax.ShapeDtypeStruct((8, c), F32)],
        compiler_params=_cparams(("parallel", "arbitrary")))(xbc, xbc, xbc, dact, dact, conv_w, conv_b)


GN_TILE = 256


def _gated_norm_fwd(y, z, g, name):
    s, c = y.shape
    t = GN_TILE

    def body(y_ref, z_ref, g_ref, o_ref):
        for gi in range(SSM_GROUPS):
            cols = slice(gi * SSM_GW, (gi + 1) * SSM_GW)
            zv = z_ref[:, cols].astype(F32)
            yf = y_ref[:, cols] * (zv * _sigmoid(zv))
            r = lax.rsqrt(jnp.mean(yf * yf, axis=-1, keepdims=True) + NORM_EPS)
            o_ref[:, cols] = (yf * r * g_ref[:, cols]).astype(BF16)

    tile = pl.BlockSpec((t, c), lambda i: (i, 0))
    return pl.pallas_call(
        body, name=name, grid=(s // t,), in_specs=[tile, tile, pl.BlockSpec((1, c), lambda i: (0, 0))],
        out_specs=tile, out_shape=jax.ShapeDtypeStruct((s, c), BF16),
        compiler_params=_cparams(("parallel",)))(y, z, g)


def _gated_norm_bwd(y, z, dout, g, name):
    s, c = y.shape
    t = GN_TILE

    def body(y_ref, z_ref, do_ref, g_ref, dy_ref, dz_ref, dg_ref):
        i = pl.program_id(0)

        @pl.when(i == 0)
        def _():
            dg_ref[...] = jnp.zeros_like(dg_ref)

        for gi in range(SSM_GROUPS):
            cols = slice(gi * SSM_GW, (gi + 1) * SSM_GW)
            zv = z_ref[:, cols].astype(F32)
            yv = y_ref[:, cols]
            sg = _sigmoid(zv)
            sz = zv * sg
            yf = yv * sz
            r = lax.rsqrt(jnp.mean(yf * yf, axis=-1, keepdims=True) + NORM_EPS)
            n = yf * r
            dout = do_ref[:, cols]
            dn = dout * g_ref[:, cols]
            dg_ref[:, cols] += _colsum8(dout * n)
            dyf = r * (dn - n * jnp.mean(dn * n, axis=-1, keepdims=True))
            dy_ref[:, cols] = dyf * sz
            dz_ref[:, cols] = (dyf * yv * (sg * (1.0 + zv * (1.0 - sg)))).astype(BF16)

    tile = pl.BlockSpec((t, c), lambda i: (i, 0))
    return pl.pallas_call(
        body, name=name, grid=(s // t,), in_specs=[tile, tile, tile, pl.BlockSpec((1, c), lambda i: (0, 0))],
        out_specs=[tile, tile, pl.BlockSpec((8, c), lambda i: (0, 0))],
        out_shape=[jax.ShapeDtypeStruct((s, c), F32), jax.ShapeDtypeStruct((s, c), BF16),
                   jax.ShapeDtypeStruct((8, c), F32)],
        compiler_params=_cparams(("arbitrary",)))(y, z, dout, g)


def _split_dot(x, e, dims, terms):
    r, acc = x, None
    for i in range(terms):
        p = r.astype(BF16)
        part = lax.dot_general(p, e, dims, preferred_element_type=F32)
        acc = part if acc is None else acc + part
        if i + 1 < terms:
            r = r - p.astype(F32)
    return acc


def _split_dot_r(e, x, dims, terms):
    r, acc = x, None
    for i in range(terms):
        p = r.astype(BF16)
        part = lax.dot_general(e, p, dims, preferred_element_type=F32)
        acc = part if acc is None else acc + part
        if i + 1 < terms:
            r = r - p.astype(F32)
    return acc


def _ssd_prep(dtr_ref, bias_ref, alog_ref):
    q = SSM_CHUNK
    dtr = dtr_ref[...] + bias_ref[...]
    dt = jnp.maximum(dtr, 0.0) + jnp.log(1.0 + jnp.exp(-jnp.abs(dtr)))
    a_neg = -jnp.exp(alog_ref[...])
    dta = dt * a_neg
    row = lax.broadcasted_iota(jnp.int32, (q, q), 0)
    col = lax.broadcasted_iota(jnp.int32, (q, q), 1)
    causal = row >= col
    a = _split_dot_r(causal.astype(BF16), dta, NN, 3)
    aq_row = jnp.sum(dta, axis=0, keepdims=True)
    aq_hb = _split_dot(dta, jnp.ones((q, LANES), BF16), TN, 3)
    return dict(dtr=dtr, dt=dt, a_neg=a_neg, a=a, a_t=jnp.transpose(a), aq_row=aq_row, aq_hb=aq_hb,
                ea=jnp.exp(a), fa=jnp.exp(aq_row - a), causal=causal)


def _ssd_group_mats():
    g = jnp.arange(SSM_GROUPS)[:, None, None]
    head = jnp.arange(LANES)[None, :, None]
    eg = (head == 8 * g + jnp.arange(SSM_GW)[None, None, :] // HEAD_DIM).astype(BF16)
    sel = (head == 8 * g + jnp.arange(8 * LANES)[None, None, :] // LANES).astype(BF16)
    mats = (eg, jnp.transpose(eg, (0, 2, 1)), sel)
    return mats, [pl.BlockSpec(m.shape, lambda c: (0, 0, 0)) for m in mats]


def _ssd_fwd(xbc, dt_raw, dt_bias, a_log, d_e, name):
    s = xbc.shape[0]
    q = SSM_CHUNK
    nc = s // q

    def body(x_ref, dtr_ref, bias_ref, alog_ref, de_ref, eg_ref, egt_ref, sel_ref, y_ref, hin_ref, state, at_ref):
        c = pl.program_id(0)

        @pl.when(c == 0)
        def _():
            state[...] = jnp.zeros_like(state)

        hin_ref[...] = state[...].astype(BF16)
        pr = _ssd_prep(dtr_ref, bias_ref, alog_ref)
        at_ref[...] = pr["a_t"]
        exp_aq_hb = jnp.exp(pr["aq_hb"])
        stack3 = jnp.concatenate([pr["dt"], pr["ea"], pr["fa"]], axis=0)
        mq = _head_masks(q)
        for g in range(SSM_GROUPS):
            eg, sel = eg_ref[g], sel_ref[g]
            cols = slice(g * SSM_GW, (g + 1) * SSM_GW)
            xg = x_ref[:, cols]
            bg = x_ref[:, SSM_INNER + g * SSM_STATE:SSM_INNER + (g + 1) * SSM_STATE]
            cg = x_ref[:, SSM_INNER + SSM_GROUPS * SSM_STATE + g * SSM_STATE:
                       SSM_INNER + SSM_GROUPS * SSM_STATE + (g + 1) * SSM_STATE]
            e3 = _split_dot(stack3, eg, NN, 1)
            dt_e, ea_e, fa_e = e3[:q], e3[q:2 * q], e3[2 * q:]
            xf = xg.astype(F32)
            xdt = xf * dt_e
            xdt_b = xdt.astype(BF16)
            cb = lax.dot_general(cg, bg, NT, preferred_element_type=F32)
            colb = _split_dot(pr["a"], sel, NN, 2)
            hg = state[cols, :]
            y_g = lax.dot_general(cg, hg.astype(BF16), NT, preferred_element_type=F32) * ea_e + de_ref[:, cols] * xf
            pairs = []
            for pp in range(4):
                xp = xdt_b[:, pp * LANES:(pp + 1) * LANES]
                yp = jnp.zeros((q, LANES), F32)
                for hh in range(2):
                    hi = 2 * pp + hh
                    diff = colb[:, hi * LANES:(hi + 1) * LANES] - at_ref[8 * g + hi:8 * g + hi + 1, :]
                    lmat = jnp.exp(jnp.where(pr["causal"], diff, NEG_BIG))
                    m_b = (cb * lmat).astype(BF16)
                    yp = yp + jnp.dot(m_b, jnp.where(mq[hh], xp, jnp.zeros_like(xp)), preferred_element_type=F32)
                pairs.append(yp)
            y_ref[:, cols] = y_g + jnp.concatenate(pairs, axis=1)
            s_g = lax.dot_general((xdt * fa_e).astype(BF16), bg, TN, preferred_element_type=F32)
            dec_g = _split_dot_r(eg, exp_aq_hb, TN, 2)
            state[cols, :] = dec_g * hg + s_g

    row128 = pl.BlockSpec((1, LANES), lambda c: (0, 0))
    mats, mat_specs = _ssd_group_mats()
    return pl.pallas_call(
        body, name=name, grid=(nc,),
        in_specs=[pl.BlockSpec((q, SSM_CONV_DIM), lambda c: (c, 0)), pl.BlockSpec((q, LANES), lambda c: (c, 0)),
                  row128, row128, pl.BlockSpec((1, SSM_INNER), lambda c: (0, 0)), *mat_specs],
        out_specs=[pl.BlockSpec((q, SSM_INNER), lambda c: (c, 0)),
                   pl.BlockSpec((None, SSM_INNER, SSM_STATE), lambda c: (c, 0, 0))],
        out_shape=[jax.ShapeDtypeStruct((s, SSM_INNER), F32), jax.ShapeDtypeStruct((nc, SSM_INNER, SSM_STATE), BF16)],
        scratch_shapes=[pltpu.VMEM((SSM_INNER, SSM_STATE), F32), pltpu.VMEM((LANES, q), F32)],
        compiler_params=_cparams(("arbitrary",)))(xbc, dt_raw, dt_bias, a_log, d_e, *mats)


def _ssd_bwd(xbc, dt_raw, dt_bias, a_log, d_e, hin, dy, name):
    s = xbc.shape[0]
    q = SSM_CHUNK
    nc = s // q

    def body(x_ref, dtr_ref, bias_ref, alog_ref, de_ref, hin_ref, dy_ref, eg_ref, egt_ref, sel_ref,
             dx_ref, ddt_ref, da_acc, db_acc, dd_acc, gst, at_ref):
        i = pl.program_id(0)

        @pl.when(i == 0)
        def _():
            gst[...] = jnp.zeros_like(gst)
            da_acc[...] = jnp.zeros_like(da_acc)
            db_acc[...] = jnp.zeros_like(db_acc)
            dd_acc[...] = jnp.zeros_like(dd_acc)

        pr = _ssd_prep(dtr_ref, bias_ref, alog_ref)
        at_ref[...] = pr["a_t"]
        exp_aq_hb = jnp.exp(pr["aq_hb"])
        stack3 = jnp.concatenate([pr["dt"], pr["ea"], pr["fa"]], axis=0)
        mq = _head_masks(q)
        lane_h = lax.broadcasted_iota(jnp.int32, (q, LANES), 1)
        sub_h = lax.broadcasted_iota(jnp.int32, (LANES, q), 0)
        da = jnp.zeros((q, LANES), F32)
        da_tn = jnp.zeros((LANES, q), F32)
        daq = jnp.zeros((1, LANES), F32)
        ddt = jnp.zeros((q, LANES), F32)
        for g in range(SSM_GROUPS):
            eg, eg_t, sel = eg_ref[g], egt_ref[g], sel_ref[g]
            cols = slice(g * SSM_GW, (g + 1) * SSM_GW)
            bcols = slice(SSM_INNER + g * SSM_STATE, SSM_INNER + (g + 1) * SSM_STATE)
            ccols = slice(SSM_INNER + SSM_GROUPS * SSM_STATE + g * SSM_STATE,
                          SSM_INNER + SSM_GROUPS * SSM_STATE + (g + 1) * SSM_STATE)
            xg, bg, cg = x_ref[:, cols], x_ref[:, bcols], x_ref[:, ccols]
            e3 = _split_dot(stack3, eg, NN, 1)
            dt_e, ea_e, fa_e = e3[:q], e3[q:2 * q], e3[2 * q:]
            xf = xg.astype(F32)
            xdt = xf * dt_e
            xdt_b = xdt.astype(BF16)
            xdtf = xdt * fa_e
            xdtf_b = xdtf.astype(BF16)
            cb = lax.dot_general(cg, bg, NT, preferred_element_type=F32)
            colb = _split_dot(pr["a"], sel, NN, 2)
            dyg = dy_ref[:, cols]
            dd_acc[:, cols] += _colsum8(dyg * xf)
            hg_b = hin_ref[cols, :]
            gg = gst[cols, :]
            gg_b = gg.astype(BF16)
            dye_b = (dyg * ea_e).astype(BF16)
            dc_g = jnp.dot(dye_b, hg_b, preferred_element_type=F32)
            dh_g = lax.dot_general(dye_b, cg, TN, preferred_element_type=F32)
            yoff = lax.dot_general(cg, hg_b, NT, preferred_element_type=F32) * ea_e
            da = da + _split_dot(dyg * yoff, eg_t, NN, 1)
            db_g = jnp.dot(xdtf_b, gg_b, preferred_element_type=F32)
            tmat = lax.dot_general(bg, gg_b, NT, preferred_element_type=F32)
            dxdt = fa_e * tmat
            qmat = _split_dot(xdtf * tmat, eg_t, NN, 1)
            da = da - qmat
            daq = daq + jnp.sum(qmat, axis=0, keepdims=True)
            gh = _split_dot(gg * hg_b.astype(F32), eg_t, TN, 1)
            daq = daq + jnp.sum(gh, axis=0, keepdims=True) * jnp.exp(pr["aq_row"])
            dcb = jnp.zeros((q, q), F32)
            pairs = []
            for pp in range(4):
                xp = xdt_b[:, pp * LANES:(pp + 1) * LANES]
                dyp = dyg[:, pp * LANES:(pp + 1) * LANES]
                dxp = jnp.zeros((q, LANES), F32)
                for hh in range(2):
                    hi = 2 * pp + hh
                    h = 8 * g + hi
                    diff = colb[:, hi * LANES:(hi + 1) * LANES] - at_ref[h:h + 1, :]
                    lmat = jnp.exp(jnp.where(pr["causal"], diff, NEG_BIG))
                    mmat = cb * lmat
                    dyh = jnp.where(mq[hh], dyp, 0.0).astype(BF16)
                    dm = lax.dot_general(dyh, xp, NT, preferred_element_type=F32)
                    dxp = dxp + lax.dot_general(mmat.astype(BF16), dyh, TN, preferred_element_type=F32)
                    wmat = dm * mmat
                    da = da + jnp.where(lane_h == h, jnp.sum(wmat, axis=-1, keepdims=True), 0.0)
                    da_tn = da_tn + jnp.where(sub_h == h, jnp.sum(wmat, axis=0, keepdims=True), 0.0)
                    dcb = dcb + dm * lmat
                pairs.append(dxp)
            dxdt = dxdt + jnp.concatenate(pairs, axis=1)
            dcb_b = dcb.astype(BF16)
            dc_g = dc_g + jnp.dot(dcb_b, bg, preferred_element_type=F32)
            db_g = db_g + lax.dot_general(dcb_b, cg, TN, preferred_element_type=F32)
            ddt = ddt + _split_dot(dxdt * xf, eg_t, NN, 1)
            dx_ref[:, cols] = dxdt * dt_e + de_ref[:, cols] * dyg
            dx_ref[:, bcols] = db_g
            dx_ref[:, ccols] = dc_g
            dec_g = _split_dot_r(eg, exp_aq_hb, TN, 2)
            gst[cols, :] = dh_g + dec_g * gg
        row = lax.broadcasted_iota(jnp.int32, (q, LANES), 0)
        da_all = da - jnp.transpose(da_tn) + jnp.where(row == q - 1, daq, 0.0)
        upper = jnp.logical_not(pr["causal"]) | (lax.broadcasted_iota(jnp.int32, (q, q), 0)
                                                 == lax.broadcasted_iota(jnp.int32, (q, q), 1))
        rcs = _split_dot_r(upper.astype(BF16), da_all, NN, 3)
        ddt_all = ddt + pr["a_neg"] * rcs
        da_acc[...] += _colsum8(pr["dt"] * rcs)
        ddtr = jnp.where(lane_h < SSM_HEADS, ddt_all * _sigmoid(pr["dtr"]), 0.0)
        ddt_ref[...] = ddtr
        db_acc[...] += _colsum8(ddtr)

    rev = lambda i: (nc - 1 - i, 0)
    row128 = pl.BlockSpec((1, LANES), lambda i: (0, 0))
    acc128 = pl.BlockSpec((8, LANES), lambda i: (0, 0))
    mats, mat_specs = _ssd_group_mats()
    return pl.pallas_call(
        body, name=name, grid=(nc,),
        in_specs=[pl.BlockSpec((q, SSM_CONV_DIM), rev), pl.BlockSpec((q, LANES), rev), row128, row128,
                  pl.BlockSpec((1, SSM_INNER), lambda i: (0, 0)),
                  pl.BlockSpec((None, SSM_INNER, SSM_STATE), lambda i: (nc - 1 - i, 0, 0)),
                  pl.BlockSpec((q, SSM_INNER), rev), *mat_specs],
        out_specs=[pl.BlockSpec((q, SSM_CONV_DIM), rev), pl.BlockSpec((q, LANES), rev), acc128, acc128,
                   pl.BlockSpec((8, SSM_INNER), lambda i: (0, 0))],
        out_shape=[jax.ShapeDtypeStruct((s, SSM_CONV_DIM), F32), jax.ShapeDtypeStruct((s, LANES), F32),
                   jax.ShapeDtypeStruct((8, LANES), F32), jax.ShapeDtypeStruct((8, LANES), F32),
                   jax.ShapeDtypeStruct((8, SSM_INNER), F32)],
        scratch_shapes=[pltpu.VMEM((SSM_INNER, SSM_STATE), F32), pltpu.VMEM((LANES, q), F32)],
        compiler_params=_cparams(("arbitrary",)))(xbc, dt_raw, dt_bias, a_log, d_e, hin, dy, *mats)


MM = (1024, 1024, 2048)
MM_TN = (1024, 1024, 4096)


def _relu2_epilogue(acc, ex, outs):
    r = jnp.maximum(acc, 0.0)
    outs[0][...] = (r * r).astype(BF16)
    outs[1][...] = r.astype(BF16)


def _relu2_bwd_epilogue(acc, ex, outs):
    outs[0][...] = (acc * (2.0 * ex[0][...].astype(F32))).astype(BF16)


def _add_epilogue(acc, ex, outs):
    outs[0][...] = acc + ex[0][...]


def _tile_spec(tm, tn):
    return pl.BlockSpec((tm, tn), lambda i, j, k: (i, j))


def _sum8(acc):
    return jnp.sum(acc, axis=0)


def _row(v):
    return v.reshape(1, -1)


def _ffn_fwd(xin, norm_g, sc, sh, gate, w, l):
    tm, tn, tk = MM
    h = _norm_mod_fwd(xin, norm_g, sc, sh, f"norm_ffn_fwd_{l}")
    w1 = w("ffn_w1", h)
    per = w1.shape[-1] // tn
    f, r = _mm_nn(f"mm_ffn1_{l}", h, w1, *MM, out_dtypes=(BF16, BF16), epilogue=_relu2_epilogue,
                  b_tile=lambda j, k: (j // per, l, 0, j % per), n=FFN_HIDDEN)
    xout, yf = _mm_resid(f"mm_ffn2_{l}", f, w("ffn_w2", f), xin, gate, *MM, b_tile=lambda j, k: (k, l, 0, j), n=D_MODEL)
    return xout, (h, f, r, yf)


def _ffn_bwd(dxo, xin, saved, norm_g, sc, gate, w, sink, l):
    h, f, r, yf = saved
    tm, tn, tk = MM
    w1, w2 = w("ffn_w1"), w("ffn_w2")
    per = w2.shape[-2] // tn
    dyf, dgate = _gate_bwd(dxo, yf, gate, f"gate_bwd_ffn_{l}")
    da = _mm_nt(f"mm_ffn2_bwd_{l}", dyf, w2, *MM, extras=(r,), extra_specs=(_tile_spec(tm, tn),),
                epilogue=_relu2_bwd_epilogue, b_tile=lambda j, k: (j // per, l, j % per, 0), n=FFN_HIDDEN)[0]
    tok = sink(f"ffn_w2_{l}", _mm_tn(f"mm_ffn2_dw_{l}", f, dyf, *MM_TN), "rows")
    dh = _mm_nt(f"mm_ffn1_bwd_{l}", da, w1, *MM, out_dtypes=(F32,), b_tile=lambda j, k: (k, l, j, 0), n=D_MODEL)[0]
    tok = tok + sink(f"ffn_w1_{l}", _mm_tn(f"mm_ffn1_dw_{l}", h, da, *MM_TN, col_chunks=N_CHIPS), "slots")
    dxin, dsh, dsc, dng = _norm_mod_bwd(xin, dh, dxo, norm_g + tok, sc, f"norm_ffn_bwd_{l}")
    return dxin, (_sum8(dsh), _sum8(dsc), _sum8(dgate), _sum8(dng))


def _local_step(x, target, mod, w, p, sink):
    s, d = x.shape
    tm, tn, tk = MM
    mods = [[_row(mod[l, j * d:(j + 1) * d]) for j in range(6)] for l in range(2)]
    g = {}

    sh1, sc1, g1, sh2, sc2, g2 = mods[0]
    nm0, nf0 = _row(p["norm_mix"][0]), _row(p["norm_ffn"][0])
    h0 = _norm_mod_fwd(x, nm0, sc1, sh1, "norm_mix_fwd_0")
    slabs = _mm_core(
        "mm_even_in", h0, w("even_in", h0), dims=NN, grid=(s // tm, N_SLABS, 1),
        a_spec=pl.BlockSpec((tm, d), lambda i, j, k: (i, 0)), b_spec=pl.BlockSpec((d, ATTN_W), lambda i, j, k: (0, j)),
        out_shapes=[jax.ShapeDtypeStruct((N_SLABS, s, ATTN_W), BF16)],
        out_specs=[pl.BlockSpec((None, tm, ATTN_W), lambda i, j, k: (j, i, 0))], acc_shape=None)[0]
    outs, lses = zip(*[_attn_fwd(slabs, gi, f"attn_fwd_{gi}") for gi in range(3)])
    attn, lse_tot = _attn_merge(outs, lses, "attn_merge")
    pool_scale = _row(p["pool_scale"])
    ap, pool_d = _pool_fwd(slabs, attn, p["pool_w"], pool_scale, "pool_fwd")
    x1, y0 = _mm_resid("mm_even_out", ap, w("even_out", ap), x, g1, *MM)
    x2, ffn0 = _ffn_fwd(x1, nf0, sc2, sh2, g2, w, 0)

    sh1b, sc1b, g1b, sh2b, sc2b, g2b = mods[1]
    nm1, nf1 = _row(p["norm_mix"][1]), _row(p["norm_ffn"][1])
    h1 = _norm_mod_fwd(x2, nm1, sc1b, sh1b, "norm_mix_fwd_1")
    z = _mm_nn("mm_ssm_z", h1, w("ssm_z", h1), *MM)[0]
    xbc_raw = _mm_nn("mm_ssm_xbc", h1, w("ssm_xbc", h1), *MM)[0]
    dt_raw = _mm_nn("mm_ssm_dt", h1, w("ssm_dt", h1), tm, LANES, tk, out_dtypes=(F32,))[0]
    xbc = _conv_fwd(xbc_raw, p["conv_w"], _row(p["conv_b"]), "conv_fwd")
    y_ssd, hin = _ssd_fwd(xbc, dt_raw, p["dt_bias"], p["a_log"], p["d_e"], "ssd_fwd")
    ssm_norm = _row(p["ssm_norm"])
    yn = _gated_norm_fwd(y_ssd, z, ssm_norm, "gated_norm_fwd")
    x3, y1 = _mm_resid("mm_ssm_out", yn, w("ssm_out", yn), x2, g1b, *MM)
    x4, ffn1 = _ffn_fwd(x3, nf1, sc2b, sh2b, g2b, w, 1)

    dx4, sq, dfn = _final_loss(x4, _row(p["final_norm"]), target, "final_loss")
    loss_share = (0.5 / d) * jnp.sum(sq)
    g["final_norm"] = _sum8(dfn)

    dx3, (dsh2b, dsc2b, dg2b, dnf1) = _ffn_bwd(dx4, x3, ffn1, nf1, sc2b, g2b, w, sink, 1)
    dy1, dg1b = _gate_bwd(dx3, y1, g1b, "gate_bwd_ssm")
    dyn = _mm_nt("mm_ssm_out_bwd", dy1, w("ssm_out"), *MM, out_dtypes=(F32,))[0]
    tok = sink("ssm_out", _mm_tn("mm_ssm_out_dw", yn, dy1, *MM_TN), "rows")
    dy_ssd, dz, dgn = _gated_norm_bwd(y_ssd, z, dyn, ssm_norm + tok, "gated_norm_bwd")
    g["ssm_norm"] = _sum8(dgn)
    dact, ddt, da_acc, db_acc, dd_acc = _ssd_bwd(xbc, dt_raw, p["dt_bias"], p["a_log"], p["d_e"], hin, dy_ssd, "ssd_bwd")
    g["dt_bias"] = _sum8(db_acc)[:SSM_HEADS]
    g["a_log"] = _sum8(da_acc)[:SSM_HEADS] * (-jnp.exp(p["a_log"][0, :SSM_HEADS]))
    g["ssm_d"] = jnp.sum(_sum8(dd_acc).reshape(SSM_HEADS, HEAD_DIM), axis=1)
    dxbc, dcw, dcb = _conv_bwd(xbc_raw, dact, p["conv_w"], _row(p["conv_b"]), "conv_bwd")
    g["conv_w"] = jnp.sum(dcw, axis=1)
    g["conv_b"] = _sum8(dcb)
    add_spec = (_tile_spec(tm, tn),)
    dh1 = _mm_nt("mm_ssm_z_bwd", dz, w("ssm_z"), *MM, out_dtypes=(F32,))[0]
    dh1 = _mm_nt("mm_ssm_xbc_bwd", dxbc, w("ssm_xbc"), *MM, out_dtypes=(F32,), extras=(dh1,), extra_specs=add_spec,
                 epilogue=_add_epilogue)[0]
    dh1 = _mm_nt("mm_ssm_dt_bwd", ddt, w("ssm_dt"), *MM, out_dtypes=(F32,), extras=(dh1,), extra_specs=add_spec,
                 epilogue=_add_epilogue)[0]
    tok = sink("ssm_in", jnp.concatenate(
        [_mm_tn("mm_ssm_z_dw", h1, dz, *MM_TN), _mm_tn("mm_ssm_xbc_dw", h1, dxbc, *MM_TN),
         _mm_tn("mm_ssm_dt_dw", h1, ddt, MM_TN[0], LANES, MM_TN[2])[:, :SSM_HEADS]], axis=1), "cols")
    dx2, dsh1b, dsc1b, dnm1 = _norm_mod_bwd(x2, dh1, dx3, nm1 + tok, sc1b, "norm_mix_bwd_1")
    dmod1 = jnp.concatenate([_sum8(dsh1b), _sum8(dsc1b), _sum8(dg1b), dsh2b, dsc2b, dg2b])

    dx1, (dsh2, dsc2, dg2, dnf0) = _ffn_bwd(dx2, x1, ffn0, nf0, sc2, g2, w, sink, 0)
    dy0, dg1 = _gate_bwd(dx1, y0, g1, "gate_bwd_even")
    dap = _mm_nt("mm_even_out_bwd", dy0, w("even_out"), *MM)[0]
    tok = sink("even_out", _mm_tn("mm_even_out_dw", ap, dy0, *MM_TN), "cols")
    du, dpw, dpsc = _pool_bwd(dap, pool_d, p["pool_w"], pool_scale + tok, "pool_bwd")
    g["pool_w"] = dpw
    g["pool_scale"] = _sum8(dpsc)
    dqkv = [_attn_bwd(slabs, dap, ap, lse_tot, gi, f"attn_bwd_{gi}") for gi in range(3)]
    dproj = jnp.concatenate([dqkv[gi][t] for t in range(3) for gi in range(3)] + [du], axis=1)
    dh0 = _mm_nt("mm_even_in_bwd", dproj, w("even_in"), tm, tn, N_SLABS * ATTN_W // 2, out_dtypes=(F32,))[0]
    tok = sink("even_in", _mm_tn("mm_even_in_dw", dproj, h0, *MM_TN), "cols_t")
    gx, dsh1, dsc1, dnm0 = _norm_mod_bwd(x, dh0, dx1, nm0 + tok, sc1, "norm_mix_bwd_0")
    dmod0 = jnp.concatenate([_sum8(dsh1), _sum8(dsc1), _sum8(dg1), dsh2, dsc2, dg2])

    g["norm_mix"] = jnp.stack([_sum8(dnm0), _sum8(dnm1)])
    g["norm_ffn"] = jnp.stack([dnf0, dnf1])
    return loss_share, gx, g, jnp.stack([dmod0, dmod1])


SMALL_COLS = 512
SMALL_ORDER = ("dmod", "norm_mix", "norm_ffn", "pool_w", "pool_scale", "conv_w", "conv_b", "dt_bias", "a_log",
               "ssm_d", "ssm_norm", "final_norm")


def _cols_to_full(gathered):
    n, k, ns = gathered.shape
    return jnp.transpose(gathered, (1, 0, 2)).reshape(k, n * ns)


def _full_to_cols(full):
    k, n4 = full.shape
    return jnp.transpose(full.reshape(k, N_CHIPS, n4 // N_CHIPS), (1, 0, 2))


def _pack(parts, cols):
    flat = jnp.concatenate([v.reshape(-1) for v in parts])
    rows = -(-flat.shape[0] // (cols * 8)) * 8
    return jnp.pad(flat, (0, rows * cols - flat.shape[0])).reshape(rows, cols)


def _unpack(packed, shapes):
    flat, out, at = packed.reshape(-1), [], 0
    for shp in shapes:
        n = math.prod(shp)
        out.append(flat[at:at + n].reshape(shp))
        at += n
    return out


def kernel(x, c, ada_w, ada_b, norm_mix, norm_ffn, ffn_w1, ffn_w2, even_w_in, pool_w, pool_scale, even_w_out, ssm_w_in, ssm_conv_w, ssm_conv_b, ssm_dt_bias, ssm_a_log, ssm_d, ssm_norm, ssm_w_out, final_norm, loss_target, m_ada_w, m_ada_b, m_norm_mix, m_norm_ffn, m_ffn_w1, m_ffn_w2, m_even_w_in, m_pool_w, m_pool_scale, m_even_w_out, m_ssm_w_in, m_ssm_conv_w, m_ssm_conv_b, m_ssm_dt_bias, m_ssm_a_log, m_ssm_d, m_ssm_norm, m_ssm_w_out, m_final_norm, v_ada_w, v_ada_b, v_norm_mix, v_norm_ffn, v_ffn_w1, v_ffn_w2, v_even_w_in, v_pool_w, v_pool_scale, v_even_w_out, v_ssm_w_in, v_ssm_conv_w, v_ssm_conv_b, v_ssm_dt_bias, v_ssm_a_log, v_ssm_d, v_ssm_norm, v_ssm_w_out, v_final_norm):
    names = ("ada_w", "ada_b", "norm_mix", "norm_ffn", "ffn_w1", "ffn_w2", "even_w_in", "pool_w", "pool_scale",
             "even_w_out", "ssm_w_in", "ssm_conv_w", "ssm_conv_b", "ssm_dt_bias", "ssm_a_log", "ssm_d", "ssm_norm",
             "ssm_w_out", "final_norm")
    wts = dict(zip(names, (ada_w, ada_b, norm_mix, norm_ffn, ffn_w1, ffn_w2, even_w_in, pool_w, pool_scale, even_w_out,
                           ssm_w_in, ssm_conv_w, ssm_conv_b, ssm_dt_bias, ssm_a_log, ssm_d, ssm_norm, ssm_w_out, final_norm)))
    m_in = dict(zip(names, (m_ada_w, m_ada_b, m_norm_mix, m_norm_ffn, m_ffn_w1, m_ffn_w2, m_even_w_in, m_pool_w, m_pool_scale,
                            m_even_w_out, m_ssm_w_in, m_ssm_conv_w, m_ssm_conv_b, m_ssm_dt_bias, m_ssm_a_log, m_ssm_d,
                            m_ssm_norm, m_ssm_w_out, m_final_norm)))
    v_in = dict(zip(names, (v_ada_w, v_ada_b, v_norm_mix, v_norm_ffn, v_ffn_w1, v_ffn_w2, v_even_w_in, v_pool_w, v_pool_scale,
                            v_even_w_out, v_ssm_w_in, v_ssm_conv_w, v_ssm_conv_b, v_ssm_dt_bias, v_ssm_a_log, v_ssm_d,
                            v_ssm_norm, v_ssm_w_out, v_final_norm)))
    d = D_MODEL
    s = x.shape[1]
    ix, iy, ic = _place()
    chip = 2 * ix + iy
    example = 4 * ix + 2 * iy + ic

    c_all = _allgather8(c.reshape(8, d // 8), "gather_c").reshape(N_DEV, d)
    cond = c_all * jax.nn.sigmoid(c_all)
    cond16 = jnp.pad(cond, ((0, 8), (0, 0)))
    ada_cols = ada_w.shape[2]
    bias_shard = lax.dynamic_slice_in_dim(ada_b, chip * ada_cols, ada_cols, axis=1)
    mod_parts = [
        _mm_nn(f"mm_ada_{l}", cond16, ada_w, 16, 512, d, out_dtypes=(F32,), extras=(_row(bias_shard[l]),),
               extra_specs=(pl.BlockSpec((1, 512), lambda i, j, k: (0, j)),), epilogue=_add_epilogue,
               b_tile=lambda j, k, l=l: (l, k, j), n=ada_cols)[0][:8]
        for l in range(2)]
    mod_all = _allgather8(jnp.concatenate(mod_parts, axis=0), "gather_mod").reshape(N_CHIPS, 2, 2, 8, ada_cols)
    mod_mine = lax.dynamic_index_in_dim(mod_all[:, 0], example, axis=2, keepdims=False)
    mod = jnp.transpose(mod_mine, (1, 0, 2)).reshape(2, N_CHIPS * ada_cols)

    gathered = _GatheredWeights({"even_in": even_w_in, "even_out": even_w_out, "ffn_w1": ffn_w1, "ffn_w2": ffn_w2,
                                 "ssm_in": ssm_w_in, "ssm_out": ssm_w_out})
    mod = mod + gathered.token
    n_xbc = SSM_INNER + SSM_CONV_DIM
    cache = {}

    def full_weight(key, after=None):
        if key in cache:
            return cache[key]
        if key in ("ffn_w1", "ffn_w2"):
            cache[key] = gathered.take(key, after)
        elif key in ("even_in", "even_out"):
            cache[key] = _cols_to_full(gathered.take(key, after)[:, 0])
        elif key == "ssm_out":
            cache[key] = gathered.take(key, after)[:, 0].reshape(SSM_INNER, d)
        else:
            whole = _cols_to_full(gathered.take("ssm_in", after)[:, 0])
            cache["ssm_z"] = whole[:, :SSM_INNER]
            cache["ssm_xbc"] = whole[:, SSM_INNER:n_xbc]
            cache["ssm_dt"] = jnp.pad(whole[:, n_xbc:], ((0, 0), (0, LANES - SSM_HEADS)))
        return cache[key]

    exchange = _GradientExchange()

    def sink(name, grad, layout):
        if layout == "slots":
            slots = grad
        elif layout == "rows":
            slots = grad.reshape(N_CHIPS, -1, grad.shape[-1])
        elif layout == "cols":
            slots = _full_to_cols(grad)
        else:
            slots = jnp.transpose(grad.reshape(N_CHIPS, -1, grad.shape[-1]), (0, 2, 1))
        return exchange.put(name, slots)

    pad_h = ((0, 0), (0, LANES - SSM_HEADS))
    cw, nw = ssm_conv_w.shape[2], ssm_norm.shape[1]
    sm = jnp.concatenate([ssm_conv_w[0].reshape(-1), ssm_conv_b.reshape(-1), jnp.pad(ssm_norm[0], (0, cw - nw)),
                          jnp.zeros((2 * cw,), F32)]).reshape(8, cw)
    sm_all = _allgather8(sm, "gather_ssm_small").reshape(N_CHIPS, 2, 8, cw)[:, 0]
    small = {
        "norm_mix": norm_mix, "norm_ffn": norm_ffn, "pool_w": pool_w[0], "pool_scale": pool_scale[0],
        "final_norm": final_norm,
        "conv_w": jnp.transpose(sm_all[:, :CONV_TAPS], (1, 0, 2)).reshape(CONV_TAPS, N_CHIPS * cw),
        "conv_b": sm_all[:, CONV_TAPS].reshape(N_CHIPS * cw),
        "ssm_norm": sm_all[:, CONV_TAPS + 1, :nw].reshape(N_CHIPS * nw),
        "dt_bias": jnp.pad(ssm_dt_bias, pad_h), "a_log": jnp.pad(ssm_a_log, pad_h),
        "d_e": jnp.repeat(ssm_d[0], HEAD_DIM).reshape(1, SSM_INNER),
    }

    loss_share, gx, g, dmod = _local_step(x[0], loss_target[0], mod, full_weight, small, sink)
    loss = lax.psum(loss_share, ("x", "y", "c"))
    reduced = exchange.finish(gx)

    g["dmod"] = dmod
    small_shapes = [g[k].shape for k in SMALL_ORDER]
    packed = _pack([g[k] for k in SMALL_ORDER], SMALL_COLS)
    every = _allgather8(packed, "gather_small").reshape(N_DEV, *packed.shape)
    summed = dict(zip(SMALL_ORDER, _unpack(_sum_devices(every, "sum_small"), small_shapes)))
    dmod_all = every.reshape(N_DEV, -1)[:, :2 * 6 * d].reshape(N_DEV, 2, 6 * d)

    grads = {}
    dmod_shard = lax.dynamic_slice_in_dim(dmod_all, chip * ada_cols, ada_cols, axis=2)
    grads["ada_w"] = jnp.stack([
        _mm_tn(f"mm_ada_dw_{l}", cond16, jnp.pad(dmod_shard[:, l], ((0, 8), (0, 0))), 1024, 512, 16, out_dtype=F32)
        for l in range(2)])
    grads["ada_b"] = summed["dmod"]
    grads["norm_mix"] = summed["norm_mix"]
    grads["norm_ffn"] = summed["norm_ffn"]
    grads["pool_w"] = summed["pool_w"][None]
    grads["pool_scale"] = summed["pool_scale"][None]
    grads["ssm_conv_w"] =lax.dynamic_slice_in_dim(summed["conv_w"], chip * cw, cw, axis=1)[None]
    grads["ssm_conv_b"] = lax.dynamic_slice_in_dim(summed["conv_b"], chip * cw, cw, axis=0)[None]
    grads["ssm_dt_bias"] = summed["dt_bias"][None]
    grads["ssm_a_log"] = summed["a_log"][None]
    grads["ssm_d"] = summed["ssm_d"][None]
    grads["ssm_norm"] =lax.dynamic_slice_in_dim(summed["ssm_norm"], chip * nw, nw, axis=0)[None]
    grads["final_norm"] = summed["final_norm"]

    grads["ffn_w1"] = jnp.stack([reduced["ffn_w1_0"], reduced["ffn_w1_1"]])
    grads["ffn_w2"] = jnp.stack([reduced["ffn_w2_0"], reduced["ffn_w2_1"]])
    grads["even_w_in"] = reduced["even_in"][None]
    grads["even_w_out"] = reduced["even_out"][None]
    grads["ssm_w_in"] = reduced["ssm_in"][None]
    grads["ssm_w_out"] = reduced["ssm_out"][None]

    big = ("ada_w", "ffn_w1", "ffn_w2", "even_w_in", "even_w_out", "ssm_w_in", "ssm_w_out")
    delta, new_m, new_v = {}, {}, {}
    for k in big:
        shp = wts[k].shape
        two_d = lambda t: t.reshape(-1, shp[-1])
        res = _adamw(two_d(wts[k]), two_d(grads[k]), two_d(m_in[k]), two_d(v_in[k]), f"adamw_{k}")
        delta[k], new_m[k], new_v[k] = [t.reshape(shp) for t in res]
    little = [k for k in names if k not in big]
    shapes = [wts[k].shape for k in little]
    res = _adamw(*[_pack([src[k] for k in little], LANES) for src in (wts, grads, m_in, v_in)], "adamw_small")
    for out, packed_out in zip((delta, new_m, new_v), res):
        out.update(zip(little, _unpack(packed_out, shapes)))

    return (loss, gx[None], *[grads[k] for k in names], *[delta[k] for k in names],
            *[new_m[k] for k in names], *[new_v[k] for k in names])
```

```python
import functools
import math

import jax
import jax.numpy as jnp
from jax import lax
from jax.experimental import pallas as pl
from jax.experimental.pallas import tpu as pltpu

F32 = jnp.float32
BF16 = jnp.bfloat16
MESH = pl.DeviceIdType.MESH
ANY = pl.BlockSpec(memory_space=pl.ANY)
VMEM_FULL = pl.BlockSpec(memory_space=pltpu.VMEM)

NORM_EPS = 1e-6
N_CHIPS = 4
N_DEV = 8
LANES = 128
VMEM_LIMIT = 56 << 20

D_MODEL = 2048
ATTN_GROUPS = ((128, 1), (512, 4), (2048, 16))
ATTN_BLOCK = 128
ATTN_W = 512
HEAD_DIM = 64
POOL_WINDOWS = (2, 4, 8, 16)
POOL_W = 512
N_SLABS = 10
SSM_INNER = 4096
SSM_HEADS = 64
SSM_GROUPS = 8
SSM_STATE = 128
SSM_CHUNK = 128
SSM_CONV_DIM = 6144
SSM_GW = SSM_INNER // SSM_GROUPS
FFN_HIDDEN = 8192

ADAM_LR, ADAM_B1, ADAM_B2, ADAM_EPS, ADAM_WD, ADAM_STEP = 0.001, 0.9, 0.999, 1e-08, 0.01, 10


def _cparams(sem=None):
    return pltpu.CompilerParams(dimension_semantics=sem, vmem_limit_bytes=VMEM_LIMIT)


def _place():
    return lax.axis_index("x"), lax.axis_index("y"), lax.axis_index("c")


def _chip_index():
    return 2 * lax.axis_index("x") + lax.axis_index("y")


def _allgather8(v, name):
    m_per, n = v.shape

    def body(x_ref, out_ref, send_sems, recv_sems, local_sem):
        x, y, c = _place()
        me, sibling = (x, y, c), (x, y, 1 - c)
        chips = [(1 - x, y), (x, 1 - y), (1 - x, 1 - y)]

        def rows(px, py, pc):
            return out_ref.at[pl.ds((4 * px + 2 * py + pc) * m_per, m_per), :]

        def copy(k, block, to, src=None):
            return pltpu.make_async_remote_copy(
                src_ref=rows(*block) if src is None else src, dst_ref=rows(*block),
                send_sem=send_sems.at[k], recv_sem=recv_sems.at[k], device_id=to, device_id_type=MESH)

        mine = pltpu.make_async_copy(x_ref, rows(*me), local_sem)
        mine.start()
        first = [copy(0, me, sibling, src=x_ref)]
        first += [copy(1 + j, me, (*chip, c), src=x_ref) for j, chip in enumerate(chips)]
        for cp in first:
            cp.start()
        passed = [copy(4 + j, (*chip, c), sibling) for j, chip in enumerate(chips)]
        for j, chip in enumerate(chips):
            copy(1 + j, (*chip, c), me).wait_recv()
            passed[j].start()
        copy(0, sibling, me).wait_recv()
        for j, chip in enumerate(chips):
            copy(4 + j, (*chip, 1 - c), me).wait_recv()
        for cp in first + passed:
            cp.wait_send()
        mine.wait()

    return pl.pallas_call(
        body, name=name,
        out_shape=jax.ShapeDtypeStruct((N_DEV * m_per, n), v.dtype),
        in_specs=[VMEM_FULL], out_specs=VMEM_FULL,
        scratch_shapes=[pltpu.SemaphoreType.DMA((7,)), pltpu.SemaphoreType.DMA((7,)), pltpu.SemaphoreType.DMA],
    )(v)


def _sibling_send_other_half(g, name):
    n, r, ccols = g.shape
    h = r // 2

    def body(g_ref, out_ref, send_sem, recv_sem):
        x, y, c = _place()
        cp = pltpu.make_async_remote_copy(
            src_ref=g_ref.at[:, pl.ds((1 - c) * h, h), :], dst_ref=out_ref, send_sem=send_sem, recv_sem=recv_sem,
            device_id=(x, y, 1 - c), device_id_type=MESH)
        cp.start()
        cp.wait()

    return pl.pallas_call(
        body, name=name, out_shape=jax.ShapeDtypeStruct((n, h, ccols), g.dtype),
        in_specs=[ANY], out_specs=ANY,
        scratch_shapes=[pltpu.SemaphoreType.DMA, pltpu.SemaphoreType.DMA],
    )(g)


def _sibling_complete(full, name):
    r, ccols = full.shape
    h = r // 2

    def body(in_ref, out_ref, send_sem, recv_sem):
        x, y, c = _place()

        def copy(rows):
            return pltpu.make_async_remote_copy(
                src_ref=in_ref.at[pl.ds(c * h, h), :], dst_ref=out_ref.at[rows, :], send_sem=send_sem,
                recv_sem=recv_sem, device_id=(x, y, 1 - c), device_id_type=MESH)

        mine = copy(pl.ds(c * h, h))
        mine.start()
        copy(pl.ds((1 - c) * h, h)).wait_recv()
        mine.wait_send()

    return pl.pallas_call(
        body, name=name, out_shape=jax.ShapeDtypeStruct(full.shape, full.dtype), in_specs=[ANY], out_specs=ANY,
        input_output_aliases={0: 0}, scratch_shapes=[pltpu.SemaphoreType.DMA, pltpu.SemaphoreType.DMA],
    )(full)


HBM_SPEC = pl.BlockSpec(memory_space=pltpu.HBM)
SEM_SPEC = pl.BlockSpec(memory_space=pltpu.SEMAPHORE)
N_PEERS = 3


def _split_params():
    return pltpu.CompilerParams(has_side_effects=pltpu.SideEffectType.DATAFLOW_SIDE_EFFECTING)


def _chip_exchange(mode, src_ref, land_ref, send_sems, recv_sems):
    x, y, c = _place()
    k_me = 2 * x + y
    pairs = []
    for j, chip in enumerate([(1 - x, y), (x, 1 - y), (1 - x, 1 - y)]):
        k_j = 2 * chip[0] + chip[1]
        if mode == "halves":
            h = src_ref.shape[0] // 2
            rows = pl.ds(c * h, h)
            src, there, here = src_ref.at[rows, :], land_ref.at[k_me, rows, :], land_ref.at[k_j, rows, :]
        else:
            src, there, here = src_ref.at[k_j], land_ref.at[k_me], land_ref.at[k_j]
        pairs.append(tuple(
            pltpu.make_async_remote_copy(src_ref=src, dst_ref=dst, send_sem=send_sems[j], recv_sem=recv_sems[j],
                                         device_id=(*chip, c), device_id_type=MESH) for dst in (there, here)))
    return pairs


def _exchange_start(src, land_shape, mode, name):
    def body(src_ref, land_ref, s0, s1, s2, r0, r1, r2, src_thru, land_thru, token):
        for start, _ in _chip_exchange(mode, src_ref, land_ref, (s0, s1, s2), (r0, r1, r2)):
            start.start()
        token[...] = jnp.zeros_like(token)

    sem = pltpu.SemaphoreType.DMA(())
    res = pl.pallas_call(
        body, name=name,
        out_shape=(sem,) * (2 * N_PEERS) + (pltpu.HBM(src.shape, src.dtype), pltpu.HBM(land_shape, src.dtype),
                                           jax.ShapeDtypeStruct((8, LANES), F32)),
        in_specs=(HBM_SPEC, HBM_SPEC), out_specs=(SEM_SPEC,) * (2 * N_PEERS) + (HBM_SPEC, HBM_SPEC, VMEM_FULL),
        input_output_aliases={0: 2 * N_PEERS, 1: 2 * N_PEERS + 1}, compiler_params=_split_params(),
    )(pltpu.with_memory_space_constraint(src, pltpu.HBM),
      pltpu.with_memory_space_constraint(lax.empty(land_shape, src.dtype), pltpu.HBM))
    return res[:2 * N_PEERS], res[2 * N_PEERS], res[2 * N_PEERS + 1], res[2 * N_PEERS + 2]


def _exchange_wait(sems, src_thru, land_thru, after, mode, name):
    def body(src_ref, land_ref, s0, s1, s2, r0, r1, r2, after_ref, src_out, land_out):
        waits = [w for _, w in _chip_exchange(mode, src_ref, land_ref, (s0, s1, s2), (r0, r1, r2))]
        for w in waits:
            w.wait_send()
        for w in waits:
            w.wait_recv()

    return pl.pallas_call(
        body, name=name,
        out_shape=(pltpu.HBM(src_thru.shape, src_thru.dtype), pltpu.HBM(land_thru.shape, land_thru.dtype)),
        in_specs=(HBM_SPEC, HBM_SPEC) + (SEM_SPEC,) * (2 * N_PEERS) + (ANY,), out_specs=(HBM_SPEC, HBM_SPEC),
        input_output_aliases={0: 0, 1: 1}, compiler_params=_split_params(),
    )(src_thru, land_thru, *sems, after)


def _sibling_fill(land, name):
    n, r, ccols = land.shape
    h = r // 2

    def body(land_ref, out_ref, send_sems, recv_sems):
        x, y, c = _place()
        slots = [2 * (1 - x) + y, 2 * x + (1 - y), 2 * (1 - x) + (1 - y)]

        def copy(j, rows):
            return pltpu.make_async_remote_copy(
                src_ref=land_ref.at[slots[j], pl.ds(c * h, h), :], dst_ref=out_ref.at[slots[j], rows, :],
                send_sem=send_sems.at[j], recv_sem=recv_sems.at[j], device_id=(x, y, 1 - c), device_id_type=MESH)

        sends = [copy(j, pl.ds(c * h, h)) for j in range(N_PEERS)]
        for cp in sends:
            cp.start()
        for j in range(N_PEERS):
            copy(j, pl.ds((1 - c) * h, h)).wait_recv()
        for cp in sends:
            cp.wait_send()

    return pl.pallas_call(
        body, name=name, out_shape=jax.ShapeDtypeStruct(land.shape, land.dtype), in_specs=[ANY], out_specs=ANY,
        input_output_aliases={0: 0},
        scratch_shapes=[pltpu.SemaphoreType.DMA((N_PEERS,)), pltpu.SemaphoreType.DMA((N_PEERS,))],
    )(land)


class _GatheredWeights:
    def __init__(self, shards):
        self._pending, self._done = {}, {}
        token = jnp.zeros((), BF16)
        for name, shard in shards.items():
            flat = shard.reshape(-1, shard.shape[-1]).astype(BF16) + token
            sems, thru, land, tok = _exchange_start(flat, (N_CHIPS, *flat.shape), "halves", f"gather_start_{name}")
            self._pending[name] = (sems, thru, land, shard.shape)
            token = tok[0, 0].astype(BF16)
        self.token = tok[0, 0]

    def take(self, name, after):
        if name not in self._done:
            sems, thru, land, shape = self._pending.pop(name)
            flat, land = _exchange_wait(sems, thru, land, after, "halves", f"gather_wait_{name}")
            land = _sibling_fill(land, f"gather_fill_{name}")
            land = lax.dynamic_update_index_in_dim(land, flat, _chip_index(), 0)
            self._done[name] = land.reshape(N_CHIPS, *shape)
        return self._done[name]


class _GradientExchange:
    def __init__(self):
        self._pending = []

    def put(self, name, slots):
        recv = _sibling_send_other_half(slots, f"rs_pair_{name}")
        part = _add_own_half(slots, recv, f"rs_add2_{name}")
        sems, thru, land, tok = _exchange_start(part, part.shape, "slots", f"rs_start_{name}")
        self._pending.append((name, sems, thru, land))
        return tok[0, 0]

    def finish(self, after):
        out = {}
        k_me = _chip_index()
        for name, sems, thru, land in self._pending:
            part, land = _exchange_wait(sems, thru, land, after, "slots", f"rs_wait_{name}")
            got = lax.dynamic_update_index_in_dim(land, lax.dynamic_index_in_dim(part, k_me, 0, keepdims=False), k_me, 0)
            out[name] = _sibling_complete(_sum_slots(got, f"rs_add4_{name}"), f"rs_fin_{name}")
        return out


def _row_tile(rows, cols, itemsize, budget=2 << 20):
    t = rows
    while t % 2 == 0 and t * cols * itemsize > budget and (t // 2) % 16 == 0:
        t //= 2
    return t


def _add_own_half(g, recv, name):
    n, r, ccols = g.shape
    h = r // 2
    t = _row_tile(h, ccols, 4)
    nt = h // t
    c_idx = lax.axis_index("c").astype(jnp.int32).reshape(1)

    def body(c_ref, g_ref, r_ref, o_ref):
        o_ref[...] = (g_ref[...].astype(F32) + r_ref[...].astype(F32)).astype(o_ref.dtype)

    grid_spec = pltpu.PrefetchScalarGridSpec(
        num_scalar_prefetch=1, grid=(n, nt),
        in_specs=[pl.BlockSpec((None, t, ccols), lambda j, i, c_ref: (j, c_ref[0] * nt + i, 0)),
                  pl.BlockSpec((None, t, ccols), lambda j, i, c_ref: (j, i, 0))],
        out_specs=pl.BlockSpec((None, t, ccols), lambda j, i, c_ref: (j, i, 0)))
    return pl.pallas_call(body, name=name, grid_spec=grid_spec,
                          out_shape=jax.ShapeDtypeStruct((n, h, ccols), BF16),
                          compiler_params=_cparams(("parallel", "parallel")))(c_idx, g, recv)


def _sum_slots(q, name):
    n, h, ccols = q.shape
    t = _row_tile(h, ccols, 4)
    nt = h // t
    c_idx = lax.axis_index("c").astype(jnp.int32).reshape(1)

    def body(c_ref, q_ref, o_ref):
        acc = q_ref[0].astype(F32)
        for j in range(1, n):
            acc = acc + q_ref[j].astype(F32)
        o_ref[...] = acc

    grid_spec = pltpu.PrefetchScalarGridSpec(
        num_scalar_prefetch=1, grid=(nt,),
        in_specs=[pl.BlockSpec((n, t, ccols), lambda i, c_ref: (0, i, 0))],
        out_specs=pl.BlockSpec((t, ccols), lambda i, c_ref: (c_ref[0] * nt + i, 0)))
    return pl.pallas_call(body, name=name, grid_spec=grid_spec, out_shape=jax.ShapeDtypeStruct((2 * h, ccols), F32),
                          compiler_params=_cparams(("parallel",)))(c_idx, q)


def _sum_devices(v, name):
    n, r, ccols = v.shape
    t = 8
    while r % (t * 2) == 0 and t * 2 * ccols * 4 * n <= (8 << 20):
        t *= 2

    def body(v_ref, o_ref):
        acc = v_ref[0]
        for j in range(1, n):
            acc = acc + v_ref[j]
        o_ref[...] = acc

    return pl.pallas_call(
        body, name=name, grid=(r // t,),
        in_specs=[pl.BlockSpec((n, t, ccols), lambda i: (0, i, 0))],
        out_specs=pl.BlockSpec((t, ccols), lambda i: (i, 0)),
        out_shape=jax.ShapeDtypeStruct((r, ccols), F32), compiler_params=_cparams(("parallel",)))(v)


def _adamw(w, g, m, v, name):
    r, ccols = w.shape
    t = _row_tile(r, ccols, 4, budget=1 << 20)
    c1 = 1.0 / (1.0 - ADAM_B1 ** ADAM_STEP)
    c2 = 1.0 / (1.0 - ADAM_B2 ** ADAM_STEP)

    def body(w_ref, g_ref, m_ref, v_ref, d_ref, nm_ref, nv_ref):
        gg = g_ref[...]
        nm = ADAM_B1 * m_ref[...] + (1.0 - ADAM_B1) * gg
        nv = ADAM_B2 * v_ref[...] + (1.0 - ADAM_B2) * (gg * gg)
        d_ref[...] = -ADAM_LR * ((nm * c1) / (jnp.sqrt(nv * c2) + ADAM_EPS) + ADAM_WD * w_ref[...])
        nm_ref[...] = nm
        nv_ref[...] = nv

    spec = pl.BlockSpec((t, ccols), lambda i: (i, 0))
    sds = jax.ShapeDtypeStruct((r, ccols), F32)
    return pl.pallas_call(body, name=name, grid=(r // t,), in_specs=[spec] * 4, out_specs=[spec] * 3,
                          out_shape=[sds] * 3, compiler_params=_cparams(("parallel",)))(w, g, m, v)


NN = (((1,), (0,)), ((), ()))
NT = (((1,), (1,)), ((), ()))
TN = (((0,), (0,)), ((), ()))


def _mm_core(name, a, b, *, dims, grid, a_spec, b_spec, out_shapes, out_specs, acc_shape,
             extras=(), extra_specs=(), epilogue=None):
    nk = grid[2]
    n_ex, n_out = len(extras), len(out_shapes)
    if epilogue is None:
        def epilogue(acc, ex, outs):
            outs[0][...] = acc.astype(outs[0].dtype)

    def body(*refs):
        a_ref, b_ref = refs[0], refs[1]
        ex = refs[2:2 + n_ex]
        outs = refs[2 + n_ex:2 + n_ex + n_out]
        part = lax.dot_general(a_ref[...].astype(BF16), b_ref[...].astype(BF16), dims, preferred_element_type=F32)
        if nk == 1:
            epilogue(part, ex, outs)
        else:
            acc = refs[-1]
            k = pl.program_id(2)

            @pl.when(k == 0)
            def _():
                acc[...] = part

            @pl.when(k > 0)
            def _():
                acc[...] += part

            @pl.when(k == nk - 1)
            def _():
                epilogue(acc[...], ex, outs)

    scratch = [] if nk == 1 else [pltpu.VMEM(acc_shape, F32)]
    res = pl.pallas_call(
        body, name=name, grid=grid, in_specs=[a_spec, b_spec, *extra_specs], out_specs=list(out_specs),
        out_shape=list(out_shapes), scratch_shapes=scratch,
        compiler_params=_cparams(("parallel", "parallel", "arbitrary")))(a, b, *extras)
    return res


def _mm_nn(name, a, b, tm, tn, tk, out_dtypes=(BF16,), extras=(), extra_specs=(), epilogue=None, b_tile=None, n=None):
    m, kk = a.shape
    n = b.shape[1] if b_tile is None else n
    tm, tn, tk = min(tm, m), min(tn, n), min(tk, kk)
    grid = (m // tm, n // tn, kk // tk)
    if b_tile is None:
        b_spec = pl.BlockSpec((tk, tn), lambda i, j, k: (k, j))
    else:
        b_spec = pl.BlockSpec((None,) * (b.ndim - 2) + (tk, tn), lambda i, j, k: b_tile(j, k))
    return _mm_core(
        name, a, b, dims=NN, grid=grid,
        a_spec=pl.BlockSpec((tm, tk), lambda i, j, k: (i, k)), b_spec=b_spec,
        out_shapes=[jax.ShapeDtypeStruct((m, n), dt) for dt in out_dtypes],
        out_specs=[pl.BlockSpec((tm, tn), lambda i, j, k: (i, j)) for _ in out_dtypes],
        acc_shape=(tm, tn), extras=extras, extra_specs=extra_specs, epilogue=epilogue)


def _mm_nt(name, a, b, tm, tn, tk, out_dtypes=(BF16,), extras=(), extra_specs=(), epilogue=None, b_tile=None, n=None):
    m, kk = a.shape
    n = b.shape[0] if b_tile is None else n
    tm, tn, tk = min(tm, m), min(tn, n), min(tk, kk)
    grid = (m // tm, n // tn, kk // tk)
    if b_tile is None:
        b_spec = pl.BlockSpec((tn, tk), lambda i, j, k: (j, k))
    else:
        b_spec = pl.BlockSpec((None,) * (b.ndim - 2) + (tn, tk), lambda i, j, k: b_tile(j, k))
    return _mm_core(
        name, a, b, dims=NT, grid=grid,
        a_spec=pl.BlockSpec((tm, tk), lambda i, j, k: (i, k)), b_spec=b_spec,
        out_shapes=[jax.ShapeDtypeStruct((m, n), dt) for dt in out_dtypes],
        out_specs=[pl.BlockSpec((tm, tn), lambda i, j, k: (i, j)) for _ in out_dtypes],
        acc_shape=(tm, tn), extras=extras, extra_specs=extra_specs, epilogue=epilogue)


def _mm_tn(name, a, b, tm, tn, tk, out_dtype=BF16, col_chunks=1):
    kk, m = a.shape
    n = b.shape[1]
    tm, tn, tk = min(tm, m), min(tn, n), min(tk, kk)
    grid = (m // tm, n // tn, kk // tk)
    if col_chunks == 1:
        out_shape, out_spec = (m, n), pl.BlockSpec((tm, tn), lambda i, j, k: (i, j))
    else:
        per = n // col_chunks // tn
        out_shape = (col_chunks, m, n // col_chunks)
        out_spec = pl.BlockSpec((None, tm, tn), lambda i, j, k: (j // per, i, j % per))
    return _mm_core(
        name, a, b, dims=TN, grid=grid,
        a_spec=pl.BlockSpec((tk, tm), lambda i, j, k: (k, i)), b_spec=pl.BlockSpec((tk, tn), lambda i, j, k: (k, j)),
        out_shapes=[jax.ShapeDtypeStruct(out_shape, out_dtype)], out_specs=[out_spec], acc_shape=(tm, tn))[0]


def _resid_gate_epilogue(acc, ex, outs):
    outs[0][...] = ex[0][...] + ex[1][...] * acc
    outs[1][...] = acc.astype(BF16)


def _mm_resid(name, a, b, resid, gate, tm, tn, tk, b_tile=None, n=None):
    return _mm_nn(
        name, a, b, tm, tn, tk, out_dtypes=(F32, BF16), extras=(resid, gate),
        extra_specs=(pl.BlockSpec((tm, tn), lambda i, j, k: (i, j)), pl.BlockSpec((1, tn), lambda i, j, k: (0, j))),
        epilogue=_resid_gate_epilogue, b_tile=b_tile, n=n)


TOK_TILE = 512


def _colsum8(v):
    t, ccols = v.shape
    return jnp.sum(v.reshape(t // 8, 8, ccols), axis=0)


def _norm_mod_fwd(x, g, sc, sh, name):
    s, d = x.shape
    t = TOK_TILE

    def body(x_ref, g_ref, sc_ref, sh_ref, h_ref):
        xv = x_ref[...]
        n = xv * lax.rsqrt(jnp.mean(xv * xv, axis=-1, keepdims=True) + NORM_EPS)
        h_ref[...] = ((n * g_ref[...]) * (1.0 + sc_ref[...]) + sh_ref[...]).astype(BF16)

    row = pl.BlockSpec((1, d), lambda i: (0, 0))
    return pl.pallas_call(
        body, name=name, grid=(s // t,), in_specs=[pl.BlockSpec((t, d), lambda i: (i, 0)), row, row, row],
        out_specs=pl.BlockSpec((t, d), lambda i: (i, 0)), out_shape=jax.ShapeDtypeStruct((s, d), BF16),
        compiler_params=_cparams(("parallel",)))(x, g, sc, sh)


def _norm_mod_bwd(x, dh, resid, g, sc, name):
    s, d = x.shape
    t = TOK_TILE

    def body(x_ref, dh_ref, r_ref, g_ref, sc_ref, dx_ref, dsh_ref, dsc_ref, dg_ref):
        i = pl.program_id(0)
        xv = x_ref[...]
        rstd = lax.rsqrt(jnp.mean(xv * xv, axis=-1, keepdims=True) + NORM_EPS)
        n = xv * rstd
        dhv = dh_ref[...].astype(F32)
        gv = g_ref[...]
        dyn = dhv * (1.0 + sc_ref[...])
        dn = dyn * gv
        dx_ref[...] = r_ref[...] + rstd * (dn - n * jnp.mean(dn * n, axis=-1, keepdims=True))

        @pl.when(i == 0)
        def _():
            dsh_ref[...] = jnp.zeros_like(dsh_ref)
            dsc_ref[...] = jnp.zeros_like(dsc_ref)
            dg_ref[...] = jnp.zeros_like(dg_ref)

        dsh_ref[...] += _colsum8(dhv)
        dsc_ref[...] += _colsum8(dhv * (n * gv))
        dg_ref[...] += _colsum8(dyn * n)

    tile = pl.BlockSpec((t, d), lambda i: (i, 0))
    row = pl.BlockSpec((1, d), lambda i: (0, 0))
    acc = pl.BlockSpec((8, d), lambda i: (0, 0))
    acc_s = jax.ShapeDtypeStruct((8, d), F32)
    return pl.pallas_call(
        body, name=name, grid=(s // t,), in_specs=[tile, tile, tile, row, row],
        out_specs=[tile, acc, acc, acc], out_shape=[jax.ShapeDtypeStruct((s, d), F32), acc_s, acc_s, acc_s],
        compiler_params=_cparams(("arbitrary",)))(x, dh, resid, g, sc)


def _gate_bwd(dxo, y, gate, name):
    s, d = dxo.shape
    t = TOK_TILE

    def body(dx_ref, y_ref, g_ref, dy_ref, dg_ref):
        i = pl.program_id(0)
        dxv = dx_ref[...]
        dy_ref[...] = (dxv * g_ref[...]).astype(BF16)

        @pl.when(i == 0)
        def _():
            dg_ref[...] = jnp.zeros_like(dg_ref)

        dg_ref[...] += _colsum8(dxv * y_ref[...].astype(F32))

    tile = pl.BlockSpec((t, d), lambda i: (i, 0))
    return pl.pallas_call(
        body, name=name, grid=(s // t,), in_specs=[tile, tile, pl.BlockSpec((1, d), lambda i: (0, 0))],
        out_specs=[tile, pl.BlockSpec((8, d), lambda i: (0, 0))],
        out_shape=[jax.ShapeDtypeStruct((s, d), BF16), jax.ShapeDtypeStruct((8, d), F32)],
        compiler_params=_cparams(("arbitrary",)))(dxo, y, gate)


def _final_loss(x, g, target, name):
    s, d = x.shape
    t = TOK_TILE
    inv_d = 1.0 / d

    def body(x_ref, g_ref, t_ref, dx_ref, sq_ref, dg_ref):
        i = pl.program_id(0)
        xv = x_ref[...]
        rstd = lax.rsqrt(jnp.mean(xv * xv, axis=-1, keepdims=True) + NORM_EPS)
        n = xv * rstd
        gv = g_ref[...]
        err = n * gv - t_ref[...]
        dout = err * inv_d
        dn = dout * gv
        dx_ref[...] = rstd * (dn - n * jnp.mean(dn * n, axis=-1, keepdims=True))

        @pl.when(i == 0)
        def _():
            sq_ref[...] = jnp.zeros_like(sq_ref)
            dg_ref[...] = jnp.zeros_like(dg_ref)

        sq_ref[...] += _colsum8(err * err)
        dg_ref[...] += _colsum8(dout * n)

    tile = pl.BlockSpec((t, d), lambda i: (i, 0))
    acc = pl.BlockSpec((8, d), lambda i: (0, 0))
    acc_s = jax.ShapeDtypeStruct((8, d), F32)
    return pl.pallas_call(
        body, name=name, grid=(s // t,), in_specs=[tile, pl.BlockSpec((1, d), lambda i: (0, 0)), tile],
        out_specs=[tile, acc, acc], out_shape=[jax.ShapeDtypeStruct((s, d), F32), acc_s, acc_s],
        compiler_params=_cparams(("arbitrary",)))(x, g, target)


NEG_BIG = -1e30


def _head_masks(rows):
    lane = lax.broadcasted_iota(jnp.int32, (rows, LANES), 1)
    return [(lane // HEAD_DIM) == hh for hh in range(2)]


def _group_view(slabs, gi):
    dil = ATTN_GROUPS[gi][1]
    _, s, w = slabs.shape
    if dil == 1:
        return slabs, (gi, 3 + gi, 6 + gi)
    return slabs[gi:9:3].reshape(3, s // dil, dil * w), (0, 1, 2)


def _attn_fwd(slabs, gi, name):
    window, dil = ATTN_GROUPS[gi]
    n_back = window // dil
    _, s, w = slabs.shape
    big_l = s // dil
    nb = big_l // ATTN_BLOCK
    blk = ATTN_BLOCK
    view, (iq, ik, iv) = _group_view(slabs, gi)
    scale = HEAD_DIM ** -0.5

    def body(q_ref, kp_ref, kc_ref, vp_ref, vc_ref, o_ref, lse_ref):
        n = pl.program_id(1)
        qi = lax.broadcasted_iota(jnp.int32, (blk, 2 * blk), 0)
        kj = lax.broadcasted_iota(jnp.int32, (blk, 2 * blk), 1)
        dist = qi + blk - kj
        ok = (dist >= 0) & (dist <= n_back) & ((kj >= blk) | (n > 0))
        mq = _head_masks(blk)
        mk = _head_masks(2 * blk)
        for p in range(w // LANES):
            cols = slice(p * LANES, (p + 1) * LANES)
            qp = q_ref[:, cols]
            k2 = jnp.concatenate([kp_ref[:, cols], kc_ref[:, cols]], axis=0)
            v2 = jnp.concatenate([vp_ref[:, cols], vc_ref[:, cols]], axis=0)
            o_pair = jnp.zeros((blk, LANES), F32)
            lse_pair = jnp.zeros((blk, LANES), F32)
            for hh in range(2):
                qm = jnp.where(mq[hh], qp, jnp.zeros_like(qp))
                sc = lax.dot_general(qm, k2, NT, preferred_element_type=F32) * scale
                sc = jnp.where(ok, sc, NEG_BIG)
                mx = jnp.max(sc, axis=-1, keepdims=True)
                pe = jnp.exp(sc - mx)
                den = jnp.sum(pe, axis=-1, keepdims=True)
                pn = (pe / den).astype(BF16)
                vm = jnp.where(mk[hh], v2, jnp.zeros_like(v2))
                o_pair = o_pair + jnp.dot(pn, vm, preferred_element_type=F32)
                lse_pair = jnp.where(mq[hh], mx + jnp.log(den), lse_pair)
            o_ref[:, cols] = o_pair
            lse_ref[:, cols] = lse_pair

    def spec(slab, prev):
        if prev:
            return pl.BlockSpec((None, blk, w), lambda r, n: (slab, jnp.maximum(n - 1, 0), r))
        return pl.BlockSpec((None, blk, w), lambda r, n: (slab, n, r))

    out_spec = pl.BlockSpec((blk, w), lambda r, n: (n, r))
    sds = jax.ShapeDtypeStruct((big_l, dil * w), F32)
    o, lse = pl.pallas_call(
        body, name=name, grid=(dil, nb),
        in_specs=[spec(iq, False), spec(ik, True), spec(ik, False), spec(iv, True), spec(iv, False)],
        out_specs=[out_spec, out_spec], out_shape=[sds, sds],
        compiler_params=_cparams(("parallel", "arbitrary")))(view, view, view, view, view)
    return o.reshape(s, w), lse.reshape(s, w)


def _attn_merge(outs, lses, name):
    s, w = outs[0].shape
    t = TOK_TILE

    def body(o0, o1, o2, l0, l1, l2, a_ref, lt_ref):
        ls = [l0[...], l1[...], l2[...]]
        mx = jnp.maximum(jnp.maximum(ls[0], ls[1]), ls[2])
        es = [jnp.exp(l - mx) for l in ls]
        den = es[0] + es[1] + es[2]
        num = es[0] * o0[...] + es[1] * o1[...] + es[2] * o2[...]
        a_ref[...] = (num / den).astype(BF16)
        lt_ref[...] = mx + jnp.log(den)

    tile = pl.BlockSpec((t, w), lambda i: (i, 0))
    return pl.pallas_call(
        body, name=name, grid=(s // t,), in_specs=[tile] * 6, out_specs=[tile, tile],
        out_shape=[jax.ShapeDtypeStruct((s, w), BF16), jax.ShapeDtypeStruct((s, w), F32)],
        compiler_params=_cparams(("parallel",)))(*outs, *lses)


def _attn_bwd(slabs, dap, ap, lse_tot, gi, name):
    window, dil = ATTN_GROUPS[gi]
    n_back = window // dil
    _, s, w = slabs.shape
    big_l = s // dil
    nb = big_l // ATTN_BLOCK
    blk = ATTN_BLOCK
    view, (iq, ik, iv) = _group_view(slabs, gi)
    dap_v = dap.reshape(big_l, dil * 2 * w)
    ap_v = ap.reshape(big_l, dil * 2 * w)
    lse_v = lse_tot.reshape(big_l, dil * w)
    scale = HEAD_DIM ** -0.5

    def body(qc_ref, qn_ref, kp_ref, kc_ref, vp_ref, vc_ref, dc_ref, dn_ref, ac_ref, an_ref, lc_ref, ln_ref, out_ref):
        m = pl.program_id(1)
        qi = lax.broadcasted_iota(jnp.int32, (blk, 2 * blk), 0)
        kj = lax.broadcasted_iota(jnp.int32, (blk, 2 * blk), 1)
        dist = qi + blk - kj
        ok = (dist >= 0) & (dist <= n_back) & ((kj >= blk) | (m > 0))
        qi1 = lax.broadcasted_iota(jnp.int32, (blk, blk), 0)
        kj1 = lax.broadcasted_iota(jnp.int32, (blk, blk), 1)
        ok_next = (qi1 + blk - kj1 <= n_back) & (m + 1 < nb)
        mq = _head_masks(blk)
        mk = _head_masks(2 * blk)

        def per_head(q_t, d_t, a_t, l_t, hh):
            qm = jnp.where(mq[hh], q_t, jnp.zeros_like(q_t))
            dm = jnp.where(mq[hh], d_t, jnp.zeros_like(d_t))
            delta = jnp.sum(jnp.where(mq[hh], d_t.astype(F32) * a_t.astype(F32), 0.0), axis=-1, keepdims=True)
            lse_h = jnp.max(jnp.where(mq[hh], l_t, NEG_BIG), axis=-1, keepdims=True)
            return qm, dm, delta, lse_h

        for p in range(w // LANES):
            cols = slice(p * LANES, (p + 1) * LANES)
            k2 = jnp.concatenate([kp_ref[:, cols], kc_ref[:, cols]], axis=0)
            v2 = jnp.concatenate([vp_ref[:, cols], vc_ref[:, cols]], axis=0)
            kc, vc = kc_ref[:, cols], vc_ref[:, cols]
            dq_pair = jnp.zeros((blk, LANES), F32)
            dk_pair = jnp.zeros((blk, LANES), F32)
            dv_pair = jnp.zeros((blk, LANES), F32)
            for hh in range(2):
                qm, dm, delta, lse_h = per_head(qc_ref[:, cols], dc_ref[:, cols], ac_ref[:, cols], lc_ref[:, cols], hh)
                sc = lax.dot_general(qm, k2, NT, preferred_element_type=F32) * scale
                pr = jnp.exp(jnp.where(ok, sc, NEG_BIG) - lse_h)
                dp = lax.dot_general(dm, v2, NT, preferred_element_type=F32)
                ds = pr * (dp - delta)
                ds_b = ds.astype(BF16)
                km = jnp.where(mk[hh], k2, jnp.zeros_like(k2))
                dq_pair = dq_pair + jnp.dot(ds_b, km, preferred_element_type=F32)
                dk_pair = dk_pair + lax.dot_general(ds_b[:, blk:], qm, TN, preferred_element_type=F32)
                dv_pair = dv_pair + lax.dot_general(pr[:, blk:].astype(BF16), dm, TN, preferred_element_type=F32)
                qm, dm, delta, lse_h = per_head(qn_ref[:, cols], dn_ref[:, cols], an_ref[:, cols], ln_ref[:, cols], hh)
                sc = lax.dot_general(qm, kc, NT, preferred_element_type=F32) * scale
                pr = jnp.exp(jnp.where(ok_next, sc, NEG_BIG) - lse_h)
                dp = lax.dot_general(dm, vc, NT, preferred_element_type=F32)
                ds_b = (pr * (dp - delta)).astype(BF16)
                dk_pair = dk_pair + lax.dot_general(ds_b, qm, TN, preferred_element_type=F32)
                dv_pair = dv_pair + lax.dot_general(pr.astype(BF16), dm, TN, preferred_element_type=F32)
            out_ref[0, :, cols] = (dq_pair * scale).astype(BF16)
            out_ref[1, :, cols] = (dk_pair * scale).astype(BF16)
            out_ref[2, :, cols] = dv_pair.astype(BF16)

    def slab_spec(slab, shift):
        if shift < 0:
            return pl.BlockSpec((None, blk, w), lambda r, n: (slab, jnp.maximum(n - 1, 0), r))
        if shift > 0:
            return pl.BlockSpec((None, blk, w), lambda r, n: (slab, jnp.minimum(n + 1, nb - 1), r))
        return pl.BlockSpec((None, blk, w), lambda r, n: (slab, n, r))

    def tok_spec(stride, shift):
        if shift > 0:
            return pl.BlockSpec((blk, w), lambda r, n: (jnp.minimum(n + 1, nb - 1), stride * r))
        return pl.BlockSpec((blk, w), lambda r, n: (n, stride * r))

    out = pl.pallas_call(
        body, name=name, grid=(dil, nb),
        in_specs=[slab_spec(iq, 0), slab_spec(iq, 1), slab_spec(ik, -1), slab_spec(ik, 0),
                  slab_spec(iv, -1), slab_spec(iv, 0),
                  tok_spec(2, 0), tok_spec(2, 1), tok_spec(2, 0), tok_spec(2, 1), tok_spec(1, 0), tok_spec(1, 1)],
        out_specs=pl.BlockSpec((3, blk, w), lambda r, n: (0, n, r)),
        out_shape=jax.ShapeDtypeStruct((3, big_l, dil * w), BF16),
        compiler_params=_cparams(("parallel", "arbitrary")))(
            view, view, view, view, view, view, dap_v, dap_v, ap_v, ap_v, lse_v, lse_v)
    return out.reshape(3, s, w)


def _shift_down(cur, prev, j):
    row = lax.broadcasted_iota(jnp.int32, cur.shape, 0)
    return jnp.where(row >= j, pltpu.roll(cur, j, 0), pltpu.roll(prev, j, 0))


def _shift_up(cur, nxt, j):
    t = cur.shape[0]
    row = lax.broadcasted_iota(jnp.int32, cur.shape, 0)
    return jnp.where(row < t - j, pltpu.roll(cur, t - j, 0), pltpu.roll(nxt, t - j, 0))


def _window_sum_down(cur, prev, w):
    s, sp, step = cur, prev, 1
    while step < w:
        s_new = s + _shift_down(s, sp, step)
        sp = sp + pltpu.roll(sp, step, 0)
        s, step = s_new, step * 2
    return s


def _window_sum_up(cur, nxt, w):
    t = cur.shape[0]
    s, sn, step = cur, nxt, 1
    while step < w:
        s_new = s + _shift_up(s, sn, step)
        sn = sn + pltpu.roll(sn, t - step, 0)
        s, step = s_new, step * 2
    return s


def _pool_count(tile_idx, t, w):
    row = lax.broadcasted_iota(jnp.int32, (t, LANES), 0) + tile_idx * t
    return jnp.minimum(row + 1, w).astype(F32)


def _pool_fwd(slabs, attn, pool_w, pool_scale, name):
    _, s, w = slabs.shape
    t = TOK_TILE
    gw = POOL_W // len(POOL_WINDOWS)

    def body(u_ref, up_ref, a_ref, w_ref, sc_ref, ap_ref, d_ref):
        i = pl.program_id(0)
        ap_ref[:, :w] = a_ref[...]
        for gi, win in enumerate(POOL_WINDOWS):
            cols = slice(gi * gw, (gi + 1) * gw)
            u = u_ref[:, cols].astype(F32)
            up = jnp.where(i > 0, up_ref[:, cols].astype(F32), 0.0)
            d = (_window_sum_down(u, up, win) / _pool_count(i, t, win) - u).astype(BF16)
            d_ref[:, cols] = d
            y = jnp.dot(d, w_ref[gi].astype(BF16), preferred_element_type=F32)
            ap_ref[:, w + gi * gw:w + (gi + 1) * gw] = (y * sc_ref[:, cols]).astype(BF16)

    return pl.pallas_call(
        body, name=name, grid=(s // t,),
        in_specs=[pl.BlockSpec((None, t, w), lambda i: (N_SLABS - 1, i, 0)),
                  pl.BlockSpec((None, t, w), lambda i: (N_SLABS - 1, jnp.maximum(i - 1, 0), 0)),
                  pl.BlockSpec((t, w), lambda i: (i, 0)),
                  pl.BlockSpec((len(POOL_WINDOWS), gw, gw), lambda i: (0, 0, 0)),
                  pl.BlockSpec((1, w), lambda i: (0, 0))],
        out_specs=[pl.BlockSpec((t, 2 * w), lambda i: (i, 0)), pl.BlockSpec((t, w), lambda i: (i, 0))],
        out_shape=[jax.ShapeDtypeStruct((s, 2 * w), BF16), jax.ShapeDtypeStruct((s, w), BF16)],
        compiler_params=_cparams(("parallel",)))(slabs, slabs, attn, pool_w, pool_scale)


def _pool_bwd(dap, d, pool_w, pool_scale, name):
    s, w = d.shape
    t = TOK_TILE
    nt = s // t
    gw = POOL_W // len(POOL_WINDOWS)

    def body(dy_ref, dyn_ref, d_ref, w_ref, sc_ref, du_ref, dw_ref, dsc_ref):
        i = pl.program_id(0)

        @pl.when(i == 0)
        def _():
            dw_ref[...] = jnp.zeros_like(dw_ref)
            dsc_ref[...] = jnp.zeros_like(dsc_ref)

        for gi, win in enumerate(POOL_WINDOWS):
            cols = slice(gi * gw, (gi + 1) * gw)
            wb = w_ref[gi].astype(BF16)
            scale = sc_ref[:, cols]
            dv = d_ref[:, cols]
            dy = dy_ref[:, cols].astype(F32)
            y = jnp.dot(dv, wb, preferred_element_type=F32)
            dsc_ref[:, cols] += _colsum8(dy * y)
            dyp = (dy * scale).astype(BF16)
            dw_ref[gi] += lax.dot_general(dv, dyp, TN, preferred_element_type=F32)
            dd = lax.dot_general(dyp, wb, NT, preferred_element_type=F32)
            dypn = (dyn_ref[:, cols].astype(F32) * scale).astype(BF16)
            ddn = lax.dot_general(dypn, wb, NT, preferred_element_type=F32)
            e = dd / _pool_count(i, t, win)
            en = jnp.where(i + 1 < nt, ddn / _pool_count(i + 1, t, win), 0.0)
            du_ref[:, cols] = (_window_sum_up(e, en, win) - dd).astype(BF16)

    return pl.pallas_call(
        body, name=name, grid=(nt,),
        in_specs=[pl.BlockSpec((t, w), lambda i: (i, 1)),
                  pl.BlockSpec((t, w), lambda i: (jnp.minimum(i + 1, nt - 1), 1)),
                  pl.BlockSpec((t, w), lambda i: (i, 0)),
                  pl.BlockSpec((len(POOL_WINDOWS), gw, gw), lambda i: (0, 0, 0)),
                  pl.BlockSpec((1, w), lambda i: (0, 0))],
        out_specs=[pl.BlockSpec((t, w), lambda i: (i, 0)),
                   pl.BlockSpec((len(POOL_WINDOWS), gw, gw), lambda i: (0, 0, 0)),
                   pl.BlockSpec((8, w), lambda i: (0, 0))],
        out_shape=[jax.ShapeDtypeStruct((s, w), BF16), jax.ShapeDtypeStruct((len(POOL_WINDOWS), gw, gw), F32),
                   jax.ShapeDtypeStruct((8, w), F32)],
        compiler_params=_cparams(("arbitrary",)))(dap, dap, d, pool_w, pool_scale)


CONV_TAPS = 4
CONV_COLS = 512


def _sigmoid(v):
    return 1.0 / (1.0 + jnp.exp(-v))


def _conv_pre(x, xprev, w_ref, b_ref):
    pre = b_ref[...] + w_ref[CONV_TAPS - 1:CONV_TAPS, :] * x
    for k in range(CONV_TAPS - 1):
        pre = pre + w_ref[k:k + 1, :] * _shift_down(x, xprev, CONV_TAPS - 1 - k)
    return pre


def _conv_fwd(xbc, conv_w, conv_b, name):
    s, c = xbc.shape
    t, tc = TOK_TILE, CONV_COLS

    def body(x_ref, xp_ref, w_ref, b_ref, o_ref):
        i = pl.program_id(0)
        x = x_ref[...].astype(F32)
        xp = jnp.where(i > 0, xp_ref[...].astype(F32), 0.0)
        pre = _conv_pre(x, xp, w_ref, b_ref)
        o_ref[...] = (pre * _sigmoid(pre)).astype(BF16)

    return pl.pallas_call(
        body, name=name, grid=(s // t, c // tc),
        in_specs=[pl.BlockSpec((t, tc), lambda i, j: (i, j)),
                  pl.BlockSpec((t, tc), lambda i, j: (jnp.maximum(i - 1, 0), j)),
                  pl.BlockSpec((CONV_TAPS, tc), lambda i, j: (0, j)), pl.BlockSpec((1, tc), lambda i, j: (0, j))],
        out_specs=pl.BlockSpec((t, tc), lambda i, j: (i, j)), out_shape=jax.ShapeDtypeStruct((s, c), BF16),
        compiler_params=_cparams(("parallel", "parallel")))(xbc, xbc, conv_w, conv_b)


def _conv_bwd(xbc, dact, conv_w, conv_b, name):
    s, c = xbc.shape
    t, tc = TOK_TILE, CONV_COLS
    nt = s // t

    def body(xp_ref, x_ref, xn_ref, da_ref, dan_ref, w_ref, b_ref, dx_ref, dw_ref, db_ref):
        i = pl.program_id(1)

        @pl.when(i == 0)
        def _():
            dw_ref[...] = jnp.zeros_like(dw_ref)
            db_ref[...] = jnp.zeros_like(db_ref)

        x = x_ref[...].astype(F32)
        xp = jnp.where(i > 0, xp_ref[...].astype(F32), 0.0)
        xn = xn_ref[...].astype(F32)

        def dsilu(pre):
            sg = _sigmoid(pre)
            return sg * (1.0 + pre * (1.0 - sg))

        dpre = da_ref[...] * dsilu(_conv_pre(x, xp, w_ref, b_ref))
        dpre_n = jnp.where(i + 1 < nt, dan_ref[...] * dsilu(_conv_pre(xn, x, w_ref, b_ref)), 0.0)
        dx = w_ref[CONV_TAPS - 1:CONV_TAPS, :] * dpre
        dw_ref[CONV_TAPS - 1] += _colsum8(dpre * x)
        for k in range(CONV_TAPS - 1):
            j = CONV_TAPS - 1 - k
            dx = dx + w_ref[k:k + 1, :] * _shift_up(dpre, dpre_n, j)
            dw_ref[k] += _colsum8(dpre * _shift_down(x, xp, j))
        dx_ref[...] = dx.astype(BF16)
        db_ref[...] += _colsum8(dpre)

    def xspec(shift):
        if shift < 0:
            return pl.BlockSpec((t, tc), lambda j, i: (jnp.maximum(i - 1, 0), j))
        if shift > 0:
            return pl.BlockSpec((t, tc), lambda j, i: (jnp.minimum(i + 1, nt - 1), j))
        return pl.BlockSpec((t, tc), lambda j, i: (i, j))

    return pl.pallas_call(
        body, name=name, grid=(c // tc, nt),
        in_specs=[xspec(-1), xspec(0), xspec(1), xspec(0), xspec(1),
                  pl.BlockSpec((CONV_TAPS, tc), lambda j, i: (0, j)), pl.BlockSpec((1, tc), lambda j, i: (0, j))],
        out_specs=[xspec(0), pl.BlockSpec((CONV_TAPS, 8, tc), lambda j, i: (0, 0, j)),
                   pl.BlockSpec((8, tc), lambda j, i: (0, j))],
        out_shape=[jax.ShapeDtypeStruct((s, c), BF16), jax.ShapeDtypeStruct((CONV_TAPS, 8, c), F32),
                   jax.ShapeDtypeStruct((8, c), F32)],
        compiler_params=_cparams(("parallel", "arbitrary")))(xbc, xbc, xbc, dact, dact, conv_w, conv_b)


GN_TILE = 256


def _gated_norm_fwd(y, z, g, name):
    s, c = y.shape
    t = GN_TILE

    def body(y_ref, z_ref, g_ref, o_ref):
        for gi in range(SSM_GROUPS):
            cols = slice(gi * SSM_GW, (gi + 1) * SSM_GW)
            zv = z_ref[:, cols].astype(F32)
            yf = y_ref[:, cols] * (zv * _sigmoid(zv))
            r = lax.rsqrt(jnp.mean(yf * yf, axis=-1, keepdims=True) + NORM_EPS)
            o_ref[:, cols] = (yf * r * g_ref[:, cols]).astype(BF16)

    tile = pl.BlockSpec((t, c), lambda i: (i, 0))
    return pl.pallas_call(
        body, name=name, grid=(s // t,), in_specs=[tile, tile, pl.BlockSpec((1, c), lambda i: (0, 0))],
        out_specs=tile, out_shape=jax.ShapeDtypeStruct((s, c), BF16),
        compiler_params=_cparams(("parallel",)))(y, z, g)


def _gated_norm_bwd(y, z, dout, g, name):
    s, c = y.shape
    t = GN_TILE

    def body(y_ref, z_ref, do_ref, g_ref, dy_ref, dz_ref, dg_ref):
        i = pl.program_id(0)

        @pl.when(i == 0)
        def _():
            dg_ref[...] = jnp.zeros_like(dg_ref)

        for gi in range(SSM_GROUPS):
            cols = slice(gi * SSM_GW, (gi + 1) * SSM_GW)
            zv = z_ref[:, cols].astype(F32)
            yv = y_ref[:, cols]
            sg = _sigmoid(zv)
            sz = zv * sg
            yf = yv * sz
            r = lax.rsqrt(jnp.mean(yf * yf, axis=-1, keepdims=True) + NORM_EPS)
            n = yf * r
            dout = do_ref[:, cols]
            dn = dout * g_ref[:, cols]
            dg_ref[:, cols] += _colsum8(dout * n)
            dyf = r * (dn - n * jnp.mean(dn * n, axis=-1, keepdims=True))
            dy_ref[:, cols] = dyf * sz
            dz_ref[:, cols] = (dyf * yv * (sg * (1.0 + zv * (1.0 - sg)))).astype(BF16)

    tile = pl.BlockSpec((t, c), lambda i: (i, 0))
    return pl.pallas_call(
        body, name=name, grid=(s // t,), in_specs=[tile, tile, tile, pl.BlockSpec((1, c), lambda i: (0, 0))],
        out_specs=[tile, tile, pl.BlockSpec((8, c), lambda i: (0, 0))],
        out_shape=[jax.ShapeDtypeStruct((s, c), F32), jax.ShapeDtypeStruct((s, c), BF16),
                   jax.ShapeDtypeStruct((8, c), F32)],
        compiler_params=_cparams(("arbitrary",)))(y, z, dout, g)


def _split_dot(x, e, dims, terms):
    r, acc = x, None
    for i in range(terms):
        p = r.astype(BF16)
        part = lax.dot_general(p, e, dims, preferred_element_type=F32)
        acc = part if acc is None else acc + part
        if i + 1 < terms:
            r = r - p.astype(F32)
    return acc


def _split_dot_r(e, x, dims, terms):
    r, acc = x, None
    for i in range(terms):
        p = r.astype(BF16)
        part = lax.dot_general(e, p, dims, preferred_element_type=F32)
        acc = part if acc is None else acc + part
        if i + 1 < terms:
            r = r - p.astype(F32)
    return acc


def _ssd_prep(dtr_ref, bias_ref, alog_ref):
    q = SSM_CHUNK
    dtr = dtr_ref[...] + bias_ref[...]
    dt = jnp.maximum(dtr, 0.0) + jnp.log(1.0 + jnp.exp(-jnp.abs(dtr)))
    a_neg = -jnp.exp(alog_ref[...])
    dta = dt * a_neg
    row = lax.broadcasted_iota(jnp.int32, (q, q), 0)
    col = lax.broadcasted_iota(jnp.int32, (q, q), 1)
    causal = row >= col
    a = _split_dot_r(causal.astype(BF16), dta, NN, 3)
    aq_row = jnp.sum(dta, axis=0, keepdims=True)
    aq_hb = _split_dot(dta, jnp.ones((q, LANES), BF16), TN, 3)
    return dict(dtr=dtr, dt=dt, a_neg=a_neg, a=a, a_t=jnp.transpose(a), aq_row=aq_row, aq_hb=aq_hb,
                ea=jnp.exp(a), fa=jnp.exp(aq_row - a), causal=causal)


def _ssd_group_mats():
    g = jnp.arange(SSM_GROUPS)[:, None, None]
    head = jnp.arange(LANES)[None, :, None]
    eg = (head == 8 * g + jnp.arange(SSM_GW)[None, None, :] // HEAD_DIM).astype(BF16)
    sel = (head == 8 * g + jnp.arange(8 * LANES)[None, None, :] // LANES).astype(BF16)
    mats = (eg, jnp.transpose(eg, (0, 2, 1)), sel)
    return mats, [pl.BlockSpec(m.shape, lambda c: (0, 0, 0)) for m in mats]


def _ssd_fwd(xbc, dt_raw, dt_bias, a_log, d_e, name):
    s = xbc.shape[0]
    q = SSM_CHUNK
    nc = s // q

    def body(x_ref, dtr_ref, bias_ref, alog_ref, de_ref, eg_ref, egt_ref, sel_ref, y_ref, hin_ref, state, at_ref):
        c = pl.program_id(0)

        @pl.when(c == 0)
        def _():
            state[...] = jnp.zeros_like(state)

        hin_ref[...] = state[...].astype(BF16)
        pr = _ssd_prep(dtr_ref, bias_ref, alog_ref)
        at_ref[...] = pr["a_t"]
        exp_aq_hb = jnp.exp(pr["aq_hb"])
        stack3 = jnp.concatenate([pr["dt"], pr["ea"], pr["fa"]], axis=0)
        mq = _head_masks(q)
        for g in range(SSM_GROUPS):
            eg, sel = eg_ref[g], sel_ref[g]
            cols = slice(g * SSM_GW, (g + 1) * SSM_GW)
            xg = x_ref[:, cols]
            bg = x_ref[:, SSM_INNER + g * SSM_STATE:SSM_INNER + (g + 1) * SSM_STATE]
            cg = x_ref[:, SSM_INNER + SSM_GROUPS * SSM_STATE + g * SSM_STATE:
                       SSM_INNER + SSM_GROUPS * SSM_STATE + (g + 1) * SSM_STATE]
            e3 = _split_dot(stack3, eg, NN, 1)
            dt_e, ea_e, fa_e = e3[:q], e3[q:2 * q], e3[2 * q:]
            xf = xg.astype(F32)
            xdt = xf * dt_e
            xdt_b = xdt.astype(BF16)
            cb = lax.dot_general(cg, bg, NT, preferred_element_type=F32)
            colb = _split_dot(pr["a"], sel, NN, 2)
            hg = state[cols, :]
            y_g = lax.dot_general(cg, hg.astype(BF16), NT, preferred_element_type=F32) * ea_e + de_ref[:, cols] * xf
            pairs = []
            for pp in range(4):
                xp = xdt_b[:, pp * LANES:(pp + 1) * LANES]
                yp = jnp.zeros((q, LANES), F32)
                for hh in range(2):
                    hi = 2 * pp + hh
                    diff = colb[:, hi * LANES:(hi + 1) * LANES] - at_ref[8 * g + hi:8 * g + hi + 1, :]
                    lmat = jnp.exp(jnp.where(pr["causal"], diff, NEG_BIG))
                    m_b = (cb * lmat).astype(BF16)
                    yp = yp + jnp.dot(m_b, jnp.where(mq[hh], xp, jnp.zeros_like(xp)), preferred_element_type=F32)
                pairs.append(yp)
            y_ref[:, cols] = y_g + jnp.concatenate(pairs, axis=1)
            s_g = lax.dot_general((xdt * fa_e).astype(BF16), bg, TN, preferred_element_type=F32)
            dec_g = _split_dot_r(eg, exp_aq_hb, TN, 2)
            state[cols, :] = dec_g * hg + s_g

    row128 = pl.BlockSpec((1, LANES), lambda c: (0, 0))
    mats, mat_specs = _ssd_group_mats()
    return pl.pallas_call(
        body, name=name, grid=(nc,),
        in_specs=[pl.BlockSpec((q, SSM_CONV_DIM), lambda c: (c, 0)), pl.BlockSpec((q, LANES), lambda c: (c, 0)),
                  row128, row128, pl.BlockSpec((1, SSM_INNER), lambda c: (0, 0)), *mat_specs],
        out_specs=[pl.BlockSpec((q, SSM_INNER), lambda c: (c, 0)),
                   pl.BlockSpec((None, SSM_INNER, SSM_STATE), lambda c: (c, 0, 0))],
        out_shape=[jax.ShapeDtypeStruct((s, SSM_INNER), F32), jax.ShapeDtypeStruct((nc, SSM_INNER, SSM_STATE), BF16)],
        scratch_shapes=[pltpu.VMEM((SSM_INNER, SSM_STATE), F32), pltpu.VMEM((LANES, q), F32)],
        compiler_params=_cparams(("arbitrary",)))(xbc, dt_raw, dt_bias, a_log, d_e, *mats)


def _ssd_bwd(xbc, dt_raw, dt_bias, a_log, d_e, hin, dy, name):
    s = xbc.shape[0]
    q = SSM_CHUNK
    nc = s // q

    def body(x_ref, dtr_ref, bias_ref, alog_ref, de_ref, hin_ref, dy_ref, eg_ref, egt_ref, sel_ref,
             dx_ref, ddt_ref, da_acc, db_acc, dd_acc, gst, at_ref):
        i = pl.program_id(0)

        @pl.when(i == 0)
        def _():
            gst[...] = jnp.zeros_like(gst)
            da_acc[...] = jnp.zeros_like(da_acc)
            db_acc[...] = jnp.zeros_like(db_acc)
            dd_acc[...] = jnp.zeros_like(dd_acc)

        pr = _ssd_prep(dtr_ref, bias_ref, alog_ref)
        at_ref[...] = pr["a_t"]
        exp_aq_hb = jnp.exp(pr["aq_hb"])
        stack3 = jnp.concatenate([pr["dt"], pr["ea"], pr["fa"]], axis=0)
        mq = _head_masks(q)
        lane_h = lax.broadcasted_iota(jnp.int32, (q, LANES), 1)
        sub_h = lax.broadcasted_iota(jnp.int32, (LANES, q), 0)
        da = jnp.zeros((q, LANES), F32)
        da_tn = jnp.zeros((LANES, q), F32)
        daq = jnp.zeros((1, LANES), F32)
        ddt = jnp.zeros((q, LANES), F32)
        for g in range(SSM_GROUPS):
            eg, eg_t, sel = eg_ref[g], egt_ref[g], sel_ref[g]
            cols = slice(g * SSM_GW, (g + 1) * SSM_GW)
            bcols = slice(SSM_INNER + g * SSM_STATE, SSM_INNER + (g + 1) * SSM_STATE)
            ccols = slice(SSM_INNER + SSM_GROUPS * SSM_STATE + g * SSM_STATE,
                          SSM_INNER + SSM_GROUPS * SSM_STATE + (g + 1) * SSM_STATE)
            xg, bg, cg = x_ref[:, cols], x_ref[:, bcols], x_ref[:, ccols]
            e3 = _split_dot(stack3, eg, NN, 1)
            dt_e, ea_e, fa_e = e3[:q], e3[q:2 * q], e3[2 * q:]
            xf = xg.astype(F32)
            xdt = xf * dt_e
            xdt_b = xdt.astype(BF16)
            xdtf = xdt * fa_e
            xdtf_b = xdtf.astype(BF16)
            cb = lax.dot_general(cg, bg, NT, preferred_element_type=F32)
            colb = _split_dot(pr["a"], sel, NN, 2)
            dyg = dy_ref[:, cols]
            dd_acc[:, cols] += _colsum8(dyg * xf)
            hg_b = hin_ref[cols, :]
            gg = gst[cols, :]
            gg_b = gg.astype(BF16)
            dye_b = (dyg * ea_e).astype(BF16)
            dc_g = jnp.dot(dye_b, hg_b, preferred_element_type=F32)
            dh_g = lax.dot_general(dye_b, cg, TN, preferred_element_type=F32)
            yoff = lax.dot_general(cg, hg_b, NT, preferred_element_type=F32) * ea_e
            da = da + _split_dot(dyg * yoff, eg_t, NN, 1)
            db_g = jnp.dot(xdtf_b, gg_b, preferred_element_type=F32)
            tmat = lax.dot_general(bg, gg_b, NT, preferred_element_type=F32)
            dxdt = fa_e * tmat
            qmat = _split_dot(xdtf * tmat, eg_t, NN, 1)
            da = da - qmat
            daq = daq + jnp.sum(qmat, axis=0, keepdims=True)
            gh = _split_dot(gg * hg_b.astype(F32), eg_t, TN, 1)
            daq = daq + jnp.sum(gh, axis=0, keepdims=True) * jnp.exp(pr["aq_row"])
            dcb = jnp.zeros((q, q), F32)
            pairs = []
            for pp in range(4):
                xp = xdt_b[:, pp * LANES:(pp + 1) * LANES]
                dyp = dyg[:, pp * LANES:(pp + 1) * LANES]
                dxp = jnp.zeros((q, LANES), F32)
                for hh in range(2):
                    hi = 2 * pp + hh
                    h = 8 * g + hi
                    diff = colb[:, hi * LANES:(hi + 1) * LANES] - at_ref[h:h + 1, :]
                    lmat = jnp.exp(jnp.where(pr["causal"], diff, NEG_BIG))
                    mmat = cb * lmat
                    dyh = jnp.where(mq[hh], dyp, 0.0).astype(BF16)
                    dm = lax.dot_general(dyh, xp, NT, preferred_element_type=F32)
                    dxp = dxp + lax.dot_general(mmat.astype(BF16), dyh, TN, preferred_element_type=F32)
                    wmat = dm * mmat
                    da = da + jnp.where(lane_h == h, jnp.sum(wmat, axis=-1, keepdims=True), 0.0)
                    da_tn = da_tn + jnp.where(sub_h == h, jnp.sum(wmat, axis=0, keepdims=True), 0.0)
                    dcb = dcb + dm * lmat
                pairs.append(dxp)
            dxdt = dxdt + jnp.concatenate(pairs, axis=1)
            dcb_b = dcb.astype(BF16)
            dc_g = dc_g + jnp.dot(dcb_b, bg, preferred_element_type=F32)
            db_g = db_g + lax.dot_general(dcb_b, cg, TN, preferred_element_type=F32)
            ddt = ddt + _split_dot(dxdt * xf, eg_t, NN, 1)
            dx_ref[:, cols] = dxdt * dt_e + de_ref[:, cols] * dyg
            dx_ref[:, bcols] = db_g
            dx_ref[:, ccols] = dc_g
            dec_g = _split_dot_r(eg, exp_aq_hb, TN, 2)
            gst[cols, :] = dh_g + dec_g * gg
        row = lax.broadcasted_iota(jnp.int32, (q, LANES), 0)
        da_all = da - jnp.transpose(da_tn) + jnp.where(row == q - 1, daq, 0.0)
        upper = jnp.logical_not(pr["causal"]) | (lax.broadcasted_iota(jnp.int32, (q, q), 0)
                                                 == lax.broadcasted_iota(jnp.int32, (q, q), 1))
        rcs = _split_dot_r(upper.astype(BF16), da_all, NN, 3)
        ddt_all = ddt + pr["a_neg"] * rcs
        da_acc[...] += _colsum8(pr["dt"] * rcs)
        ddtr = jnp.where(lane_h < SSM_HEADS, ddt_all * _sigmoid(pr["dtr"]), 0.0)
        ddt_ref[...] = ddtr
        db_acc[...] += _colsum8(ddtr)

    rev = lambda i: (nc - 1 - i, 0)
    row128 = pl.BlockSpec((1, LANES), lambda i: (0, 0))
    acc128 = pl.BlockSpec((8, LANES), lambda i: (0, 0))
    mats, mat_specs = _ssd_group_mats()
    return pl.pallas_call(
        body, name=name, grid=(nc,),
        in_specs=[pl.BlockSpec((q, SSM_CONV_DIM), rev), pl.BlockSpec((q, LANES), rev), row128, row128,
                  pl.BlockSpec((1, SSM_INNER), lambda i: (0, 0)),
                  pl.BlockSpec((None, SSM_INNER, SSM_STATE), lambda i: (nc - 1 - i, 0, 0)),
                  pl.BlockSpec((q, SSM_INNER), rev), *mat_specs],
        out_specs=[pl.BlockSpec((q, SSM_CONV_DIM), rev), pl.BlockSpec((q, LANES), rev), acc128, acc128,
                   pl.BlockSpec((8, SSM_INNER), lambda i: (0, 0))],
        out_shape=[jax.ShapeDtypeStruct((s, SSM_CONV_DIM), F32), jax.ShapeDtypeStruct((s, LANES), F32),
                   jax.ShapeDtypeStruct((8, LANES), F32), jax.ShapeDtypeStruct((8, LANES), F32),
                   jax.ShapeDtypeStruct((8, SSM_INNER), F32)],
        scratch_shapes=[pltpu.VMEM((SSM_INNER, SSM_STATE), F32), pltpu.VMEM((LANES, q), F32)],
        compiler_params=_cparams(("arbitrary",)))(xbc, dt_raw, dt_bias, a_log, d_e, hin, dy, *mats)


MM = (1024, 1024, 2048)
MM_TN = (1024, 1024, 4096)


def _relu2_epilogue(acc, ex, outs):
    r = jnp.maximum(acc, 0.0)
    outs[0][...] = (r * r).astype(BF16)
    outs[1][...] = r.astype(BF16)


def _relu2_bwd_epilogue(acc, ex, outs):
    outs[0][...] = (acc * (2.0 * ex[0][...].astype(F32))).astype(BF16)


def _add_epilogue(acc, ex, outs):
    outs[0][...] = acc + ex[0][...]


def _tile_spec(tm, tn):
    return pl.BlockSpec((tm, tn), lambda i, j, k: (i, j))


def _sum8(acc):
    return jnp.sum(acc, axis=0)


def _row(v):
    return v.reshape(1, -1)


def _ffn_fwd(xin, norm_g, sc, sh, gate, w, l):
    tm, tn, tk = MM
    h = _norm_mod_fwd(xin, norm_g, sc, sh, f"norm_ffn_fwd_{l}")
    w1 = w("ffn_w1", h)
    per = w1.shape[-1] // tn
    f, r = _mm_nn(f"mm_ffn1_{l}", h, w1, *MM, out_dtypes=(BF16, BF16), epilogue=_relu2_epilogue,
                  b_tile=lambda j, k: (j // per, l, 0, j % per), n=FFN_HIDDEN)
    xout, yf = _mm_resid(f"mm_ffn2_{l}", f, w("ffn_w2", f), xin, gate, *MM, b_tile=lambda j, k: (k, l, 0, j), n=D_MODEL)
    return xout, (h, f, r, yf)


def _ffn_bwd(dxo, xin, saved, norm_g, sc, gate, w, sink, l):
    h, f, r, yf = saved
    tm, tn, tk = MM
    w1, w2 = w("ffn_w1"), w("ffn_w2")
    per = w2.shape[-2] // tn
    dyf, dgate = _gate_bwd(dxo, yf, gate, f"gate_bwd_ffn_{l}")
    da = _mm_nt(f"mm_ffn2_bwd_{l}", dyf, w2, *MM, extras=(r,), extra_specs=(_tile_spec(tm, tn),),
                epilogue=_relu2_bwd_epilogue, b_tile=lambda j, k: (j // per, l, j % per, 0), n=FFN_HIDDEN)[0]
    tok = sink(f"ffn_w2_{l}", _mm_tn(f"mm_ffn2_dw_{l}", f, dyf, *MM_TN), "rows")
    dh = _mm_nt(f"mm_ffn1_bwd_{l}", da, w1, *MM, out_dtypes=(F32,), b_tile=lambda j, k: (k, l, j, 0), n=D_MODEL)[0]
    tok = tok + sink(f"ffn_w1_{l}", _mm_tn(f"mm_ffn1_dw_{l}", h, da, *MM_TN, col_chunks=N_CHIPS), "slots")
    dxin, dsh, dsc, dng = _norm_mod_bwd(xin, dh, dxo, norm_g + tok, sc, f"norm_ffn_bwd_{l}")
    return dxin, (_sum8(dsh), _sum8(dsc), _sum8(dgate), _sum8(dng))


def _local_step(x, target, mod, w, p, sink):
    s, d = x.shape
    tm, tn, tk = MM
    mods = [[_row(mod[l, j * d:(j + 1) * d]) for j in range(6)] for l in range(2)]
    g = {}

    sh1, sc1, g1, sh2, sc2, g2 = mods[0]
    nm0, nf0 = _row(p["norm_mix"][0]), _row(p["norm_ffn"][0])
    h0 = _norm_mod_fwd(x, nm0, sc1, sh1, "norm_mix_fwd_0")
    slabs = _mm_core(
        "mm_even_in", h0, w("even_in", h0), dims=NN, grid=(s // tm, N_SLABS, 1),
        a_spec=pl.BlockSpec((tm, d), lambda i, j, k: (i, 0)), b_spec=pl.BlockSpec((d, ATTN_W), lambda i, j, k: (0, j)),
        out_shapes=[jax.ShapeDtypeStruct((N_SLABS, s, ATTN_W), BF16)],
        out_specs=[pl.BlockSpec((None, tm, ATTN_W), lambda i, j, k: (j, i, 0))], acc_shape=None)[0]
    outs, lses = zip(*[_attn_fwd(slabs, gi, f"attn_fwd_{gi}") for gi in range(3)])
    attn, lse_tot = _attn_merge(outs, lses, "attn_merge")
    pool_scale = _row(p["pool_scale"])
    ap, pool_d = _pool_fwd(slabs, attn, p["pool_w"], pool_scale, "pool_fwd")
    x1, y0 = _mm_resid("mm_even_out", ap, w("even_out", ap), x, g1, *MM)
    x2, ffn0 = _ffn_fwd(x1, nf0, sc2, sh2, g2, w, 0)

    sh1b, sc1b, g1b, sh2b, sc2b, g2b = mods[1]
    nm1, nf1 = _row(p["norm_mix"][1]), _row(p["norm_ffn"][1])
    h1 = _norm_mod_fwd(x2, nm1, sc1b, sh1b, "norm_mix_fwd_1")
    z = _mm_nn("mm_ssm_z", h1, w("ssm_z", h1), *MM)[0]
    xbc_raw = _mm_nn("mm_ssm_xbc", h1, w("ssm_xbc", h1), *MM)[0]
    dt_raw = _mm_nn("mm_ssm_dt", h1, w("ssm_dt", h1), tm, LANES, tk, out_dtypes=(F32,))[0]
    xbc = _conv_fwd(xbc_raw, p["conv_w"], _row(p["conv_b"]), "conv_fwd")
    y_ssd, hin = _ssd_fwd(xbc, dt_raw, p["dt_bias"], p["a_log"], p["d_e"], "ssd_fwd")
    ssm_norm = _row(p["ssm_norm"])
    yn = _gated_norm_fwd(y_ssd, z, ssm_norm, "gated_norm_fwd")
    x3, y1 = _mm_resid("mm_ssm_out", yn, w("ssm_out", yn), x2, g1b, *MM)
    x4, ffn1 = _ffn_fwd(x3, nf1, sc2b, sh2b, g2b, w, 1)

    dx4, sq, dfn = _final_loss(x4, _row(p["final_norm"]), target, "final_loss")
    loss_share = (0.5 / d) * jnp.sum(sq)
    g["final_norm"] = _sum8(dfn)

    dx3, (dsh2b, dsc2b, dg2b, dnf1) = _ffn_bwd(dx4, x3, ffn1, nf1, sc2b, g2b, w, sink, 1)
    dy1, dg1b = _gate_bwd(dx3, y1, g1b, "gate_bwd_ssm")
    dyn = _mm_nt("mm_ssm_out_bwd", dy1, w("ssm_out"), *MM, out_dtypes=(F32,))[0]
    tok = sink("ssm_out", _mm_tn("mm_ssm_out_dw", yn, dy1, *MM_TN), "rows")
    dy_ssd, dz, dgn = _gated_norm_bwd(y_ssd, z, dyn, ssm_norm + tok, "gated_norm_bwd")
    g["ssm_norm"] = _sum8(dgn)
    dact, ddt, da_acc, db_acc, dd_acc = _ssd_bwd(xbc, dt_raw, p["dt_bias"], p["a_log"], p["d_e"], hin, dy_ssd, "ssd_bwd")
    g["dt_bias"] = _sum8(db_acc)[:SSM_HEADS]
    g["a_log"] = _sum8(da_acc)[:SSM_HEADS] * (-jnp.exp(p["a_log"][0, :SSM_HEADS]))
    g["ssm_d"] = jnp.sum(_sum8(dd_acc).reshape(SSM_HEADS, HEAD_DIM), axis=1)
    dxbc, dcw, dcb = _conv_bwd(xbc_raw, dact, p["conv_w"], _row(p["conv_b"]), "conv_bwd")
    g["conv_w"] = jnp.sum(dcw, axis=1)
    g["conv_b"] = _sum8(dcb)
    add_spec = (_tile_spec(tm, tn),)
    dh1 = _mm_nt("mm_ssm_z_bwd", dz, w("ssm_z"), *MM, out_dtypes=(F32,))[0]
    dh1 = _mm_nt("mm_ssm_xbc_bwd", dxbc, w("ssm_xbc"), *MM, out_dtypes=(F32,), extras=(dh1,), extra_specs=add_spec,
                 epilogue=_add_epilogue)[0]
    dh1 = _mm_nt("mm_ssm_dt_bwd", ddt, w("ssm_dt"), *MM, out_dtypes=(F32,), extras=(dh1,), extra_specs=add_spec,
                 epilogue=_add_epilogue)[0]
    tok = sink("ssm_in", jnp.concatenate(
        [_mm_tn("mm_ssm_z_dw", h1, dz, *MM_TN), _mm_tn("mm_ssm_xbc_dw", h1, dxbc, *MM_TN),
         _mm_tn("mm_ssm_dt_dw", h1, ddt, MM_TN[0], LANES, MM_TN[2])[:, :SSM_HEADS]], axis=1), "cols")
    dx2, dsh1b, dsc1b, dnm1 = _norm_mod_bwd(x2, dh1, dx3, nm1 + tok, sc1b, "norm_mix_bwd_1")
    dmod1 = jnp.concatenate([_sum8(dsh1b), _sum8(dsc1b), _sum8(dg1b), dsh2b, dsc2b, dg2b])

    dx1, (dsh2, dsc2, dg2, dnf0) = _ffn_bwd(dx2, x1, ffn0, nf0, sc2, g2, w, sink, 0)
    dy0, dg1 = _gate_bwd(dx1, y0, g1, "gate_bwd_even")
    dap = _mm_nt("mm_even_out_bwd", dy0, w("even_out"), *MM)[0]
    tok = sink("even_out", _mm_tn("mm_even_out_dw", ap, dy0, *MM_TN), "cols")
    du, dpw, dpsc = _pool_bwd(dap, pool_d, p["pool_w"], pool_scale + tok, "pool_bwd")
    g["pool_w"] = dpw
    g["pool_scale"] = _sum8(dpsc)
    dqkv = [_attn_bwd(slabs, dap, ap, lse_tot, gi, f"attn_bwd_{gi}") for gi in range(3)]
    dproj = jnp.concatenate([dqkv[gi][t] for t in range(3) for gi in range(3)] + [du], axis=1)
    dh0 = _mm_nt("mm_even_in_bwd", dproj, w("even_in"), tm, tn, N_SLABS * ATTN_W // 2, out_dtypes=(F32,))[0]
    tok = sink("even_in", _mm_tn("mm_even_in_dw", dproj, h0, *MM_TN), "cols_t")
    gx, dsh1, dsc1, dnm0 = _norm_mod_bwd(x, dh0, dx1, nm0 + tok, sc1, "norm_mix_bwd_0")
    dmod0 = jnp.concatenate([_sum8(dsh1), _sum8(dsc1), _sum8(dg1), dsh2, dsc2, dg2])

    g["norm_mix"] = jnp.stack([_sum8(dnm0), _sum8(dnm1)])
    g["norm_ffn"] = jnp.stack([dnf0, dnf1])
    return loss_share, gx, g, jnp.stack([dmod0, dmod1])


SMALL_COLS = 512
SMALL_ORDER = ("dmod", "norm_mix", "norm_ffn", "pool_w", "pool_scale", "conv_w", "conv_b", "dt_bias", "a_log",
               "ssm_d", "ssm_norm", "final_norm")


def _cols_to_full(gathered):
    n, k, ns = gathered.shape
    return jnp.transpose(gathered, (1, 0, 2)).reshape(k, n * ns)


def _full_to_cols(full):
    k, n4 = full.shape
    return jnp.transpose(full.reshape(k, N_CHIPS, n4 // N_CHIPS), (1, 0, 2))


def _pack(parts, cols):
    flat = jnp.concatenate([v.reshape(-1) for v in parts])
    rows = -(-flat.shape[0] // (cols * 8)) * 8
    return jnp.pad(flat, (0, rows * cols - flat.shape[0])).reshape(rows, cols)


def _unpack(packed, shapes):
    flat, out, at = packed.reshape(-1), [], 0
    for shp in shapes:
        n = math.prod(shp)
        out.append(flat[at:at + n].reshape(shp))
        at += n
    return out


def kernel(x, c, ada_w, ada_b, norm_mix, norm_ffn, ffn_w1, ffn_w2, even_w_in, pool_w, pool_scale, even_w_out, ssm_w_in, ssm_conv_w, ssm_conv_b, ssm_dt_bias, ssm_a_log, ssm_d, ssm_norm, ssm_w_out, final_norm, loss_target, m_ada_w, m_ada_b, m_norm_mix, m_norm_ffn, m_ffn_w1, m_ffn_w2, m_even_w_in, m_pool_w, m_pool_scale, m_even_w_out, m_ssm_w_in, m_ssm_conv_w, m_ssm_conv_b, m_ssm_dt_bias, m_ssm_a_log, m_ssm_d, m_ssm_norm, m_ssm_w_out, m_final_norm, v_ada_w, v_ada_b, v_norm_mix, v_norm_ffn, v_ffn_w1, v_ffn_w2, v_even_w_in, v_pool_w, v_pool_scale, v_even_w_out, v_ssm_w_in, v_ssm_conv_w, v_ssm_conv_b, v_ssm_dt_bias, v_ssm_a_log, v_ssm_d, v_ssm_norm, v_ssm_w_out, v_final_norm):
    names = ("ada_w", "ada_b", "norm_mix", "norm_ffn", "ffn_w1", "ffn_w2", "even_w_in", "pool_w", "pool_scale",
             "even_w_out", "ssm_w_in", "ssm_conv_w", "ssm_conv_b", "ssm_dt_bias", "ssm_a_log", "ssm_d", "ssm_norm",
             "ssm_w_out", "final_norm")
    wts = dict(zip(names, (ada_w, ada_b, norm_mix, norm_ffn, ffn_w1, ffn_w2, even_w_in, pool_w, pool_scale, even_w_out,
                           ssm_w_in, ssm_conv_w, ssm_conv_b, ssm_dt_bias, ssm_a_log, ssm_d, ssm_norm, ssm_w_out, final_norm)))
    m_in = dict(zip(names, (m_ada_w, m_ada_b, m_norm_mix, m_norm_ffn, m_ffn_w1, m_ffn_w2, m_even_w_in, m_pool_w, m_pool_scale,
                            m_even_w_out, m_ssm_w_in, m_ssm_conv_w, m_ssm_conv_b, m_ssm_dt_bias, m_ssm_a_log, m_ssm_d,
                            m_ssm_norm, m_ssm_w_out, m_final_norm)))
    v_in = dict(zip(names, (v_ada_w, v_ada_b, v_norm_mix, v_norm_ffn, v_ffn_w1, v_ffn_w2, v_even_w_in, v_pool_w, v_pool_scale,
                            v_even_w_out, v_ssm_w_in, v_ssm_conv_w, v_ssm_conv_b, v_ssm_dt_bias, v_ssm_a_log, v_ssm_d,
                            v_ssm_norm, v_ssm_w_out, v_final_norm)))
    d = D_MODEL
    s = x.shape[1]
    ix, iy, ic = _place()
    chip = 2 * ix + iy
    example = 4 * ix + 2 * iy + ic

    c_all = _allgather8(c.reshape(8, d // 8), "gather_c").reshape(N_DEV, d)
    cond = c_all * jax.nn.sigmoid(c_all)
    cond16 = jnp.pad(cond, ((0, 8), (0, 0)))
    ada_cols = ada_w.shape[2]
    bias_shard = lax.dynamic_slice_in_dim(ada_b, chip * ada_cols, ada_cols, axis=1)
    mod_parts = [
        _mm_nn(f"mm_ada_{l}", cond16, ada_w, 16, 512, d, out_dtypes=(F32,), extras=(_row(bias_shard[l]),),
               extra_specs=(pl.BlockSpec((1, 512), lambda i, j, k: (0, j)),), epilogue=_add_epilogue,
               b_tile=lambda j, k, l=l: (l, k, j), n=ada_cols)[0][:8]
        for l in range(2)]
    mod_all = _allgather8(jnp.concatenate(mod_parts, axis=0), "gather_mod").reshape(N_CHIPS, 2, 2, 8, ada_cols)
    mod_mine = lax.dynamic_index_in_dim(mod_all[:, 0], example, axis=2, keepdims=False)
    mod = jnp.transpose(mod_mine, (1, 0, 2)).reshape(2, N_CHIPS * ada_cols)

    cw, nw = ssm_conv_w.shape[2], ssm_norm.shape[1]
    sm = jnp.concatenate([ssm_conv_w[0].reshape(-1), ssm_conv_b.reshape(-1), jnp.pad(ssm_norm[0], (0, cw - nw)),
                          jnp.zeros((2 * cw,), F32)]).reshape(8, cw)
    sm_all = _allgather8(sm, "gather_ssm_small").reshape(N_CHIPS, 2, 8, cw)[:, 0]

    first_shard, sm_all, mod = lax.optimization_barrier((even_w_in, sm_all, mod))
    gathered = _GatheredWeights({"even_in": first_shard, "even_out": even_w_out, "ffn_w1": ffn_w1, "ffn_w2": ffn_w2,
                                 "ssm_in": ssm_w_in, "ssm_out": ssm_w_out})
    mod = mod + gathered.token
    n_xbc = SSM_INNER + SSM_CONV_DIM
    cache = {}

    def full_weight(key, after=None):
        if key in cache:
            return cache[key]
        if key in ("ffn_w1", "ffn_w2"):
            cache[key] = gathered.take(key, after)
        elif key in ("even_in", "even_out"):
            cache[key] = _cols_to_full(gathered.take(key, after)[:, 0])
        elif key == "ssm_out":
            cache[key] = gathered.take(key, after)[:, 0].reshape(SSM_INNER, d)
        else:
            whole = _cols_to_full(gathered.take("ssm_in", after)[:, 0])
            cache["ssm_z"] = whole[:, :SSM_INNER]
            cache["ssm_xbc"] = whole[:, SSM_INNER:n_xbc]
            cache["ssm_dt"] = jnp.pad(whole[:, n_xbc:], ((0, 0), (0, LANES - SSM_HEADS)))
        return cache[key]

    exchange = _GradientExchange()

    def sink(name, grad, layout):
        if layout == "slots":
            slots = grad
        elif layout == "rows":
            slots = grad.reshape(N_CHIPS, -1, grad.shape[-1])
        elif layout == "cols":
            slots = _full_to_cols(grad)
        else:
            slots = jnp.transpose(grad.reshape(N_CHIPS, -1, grad.shape[-1]), (0, 2, 1))
        return exchange.put(name, slots)

    pad_h = ((0, 0), (0, LANES - SSM_HEADS))
    small = {
        "norm_mix": norm_mix, "norm_ffn": norm_ffn, "pool_w": pool_w[0], "pool_scale": pool_scale[0],
        "final_norm": final_norm,
        "conv_w": jnp.transpose(sm_all[:, :CONV_TAPS], (1, 0, 2)).reshape(CONV_TAPS, N_CHIPS * cw),
        "conv_b": sm_all[:, CONV_TAPS].reshape(N_CHIPS * cw),
        "ssm_norm": sm_all[:, CONV_TAPS + 1, :nw].reshape(N_CHIPS * nw),
        "dt_bias": jnp.pad(ssm_dt_bias, pad_h), "a_log": jnp.pad(ssm_a_log, pad_h),
        "d_e": jnp.repeat(ssm_d[0], HEAD_DIM).reshape(1, SSM_INNER),
    }

    loss_share, gx, g, dmod = _local_step(x[0], loss_target[0], mod, full_weight, small, sink)
    loss = lax.psum(loss_share, ("x", "y", "c"))
    reduced = exchange.finish(gx)

    g["dmod"] = dmod
    small_shapes = [g[k].shape for k in SMALL_ORDER]
    packed = _pack([g[k] for k in SMALL_ORDER], SMALL_COLS)
    every = _allgather8(packed, "gather_small").reshape(N_DEV, *packed.shape)
    summed = dict(zip(SMALL_ORDER, _unpack(_sum_devices(every, "sum_small"), small_shapes)))
    dmod_all = every.reshape(N_DEV, -1)[:, :2 * 6 * d].reshape(N_DEV, 2, 6 * d)

    grads = {}
    dmod_shard = lax.dynamic_slice_in_dim(dmod_all, chip * ada_cols, ada_cols, axis=2)
    grads["ada_w"] = jnp.stack([
        _mm_tn(f"mm_ada_dw_{l}", cond16, jnp.pad(dmod_shard[:, l], ((0, 8), (0, 0))), 1024, 512, 16, out_dtype=F32)
        for l in range(2)])
    grads["ada_b"] = summed["dmod"]
    grads["norm_mix"] = summed["norm_mix"]
    grads["norm_ffn"] = summed["norm_ffn"]
    grads["pool_w"] = summed["pool_w"][None]
    grads["pool_scale"] = summed["pool_scale"][None]
    grads["ssm_conv_w"] =lax.dynamic_slice_in_dim(summed["conv_w"], chip * cw, cw, axis=1)[None]
    grads["ssm_conv_b"] = lax.dynamic_slice_in_dim(summed["conv_b"], chip * cw, cw, axis=0)[None]
    grads["ssm_dt_bias"] = summed["dt_bias"][None]
    grads["ssm_a_log"] = summed["a_log"][None]
    grads["ssm_d"] = summed["ssm_d"][None]
    grads["ssm_norm"] =lax.dynamic_slice_in_dim(summed["ssm_norm"], chip * nw, nw, axis=0)[None]
    grads["final_norm"] = summed["final_norm"]

    grads["ffn_w1"] = jnp.stack([reduced["ffn_w1_0"], reduced["ffn_w1_1"]])
    grads["ffn_w2"] = jnp.stack([reduced["ffn_w2_0"], reduced["ffn_w2_1"]])
    grads["even_w_in"] = reduced["even_in"][None]
    grads["even_w_out"] = reduced["even_out"][None]
    grads["ssm_w_in"] = reduced["ssm_in"][None]
    grads["ssm_w_out"] = reduced["ssm_out"][None]

    big = ("ada_w", "ffn_w1", "ffn_w2", "even_w_in", "even_w_out", "ssm_w_in", "ssm_w_out")
    delta, new_m, new_v = {}, {}, {}
    for k in big:
        shp = wts[k].shape
        two_d = lambda t: t.reshape(-1, shp[-1])
        res = _adamw(two_d(wts[k]), two_d(grads[k]), two_d(m_in[k]), two_d(v_in[k]), f"adamw_{k}")
        delta[k], new_m[k], new_v[k] = [t.reshape(shp) for t in res]
    little = [k for k in names if k not in big]
    shapes = [wts[k].shape for k in little]
    res = _adamw(*[_pack([src[k] for k in little], LANES) for src in (wts, grads, m_in, v_in)], "adamw_small")
    for out, packed_out in zip((delta, new_m, new_v), res):
        out.update(zip(little, _unpack(packed_out, shapes)))

    return (loss, gx[None], *[grads[k] for k in names], *[delta[k] for k in names],
            *[new_m[k] for k in names], *[new_v[k] for k in names])
```

```python
import functools
import math

import jax
import jax.numpy as jnp
from jax import lax
from jax.experimental import pallas as pl
from jax.experimental.pallas import tpu as pltpu

F32 = jnp.float32
BF16 = jnp.bfloat16
MESH = pl.DeviceIdType.MESH
ANY = pl.BlockSpec(memory_space=pl.ANY)
VMEM_FULL = pl.BlockSpec(memory_space=pltpu.VMEM)

NORM_EPS = 1e-6
N_CHIPS = 4
N_DEV = 8
LANES = 128
VMEM_LIMIT = 56 << 20

D_MODEL = 2048
ATTN_GROUPS = ((128, 1), (512, 4), (2048, 16))
ATTN_BLOCK = 128
ATTN_W = 512
HEAD_DIM = 64
POOL_WINDOWS = (2, 4, 8, 16)
POOL_W = 512
N_SLABS = 10
SSM_INNER = 4096
SSM_HEADS = 64
SSM_GROUPS = 8
SSM_STATE = 128
SSM_CHUNK = 128
SSM_CONV_DIM = 6144
SSM_GW = SSM_INNER // SSM_GROUPS
FFN_HIDDEN = 8192

ADAM_LR, ADAM_B1, ADAM_B2, ADAM_EPS, ADAM_WD, ADAM_STEP = 0.001, 0.9, 0.999, 1e-08, 0.01, 10


def _cparams(sem=None):
    return pltpu.CompilerParams(dimension_semantics=sem, vmem_limit_bytes=VMEM_LIMIT)


def _place():
    return lax.axis_index("x"), lax.axis_index("y"), lax.axis_index("c")


def _chip_index():
    return 2 * lax.axis_index("x") + lax.axis_index("y")


def _allgather8(v, name):
    m_per, n = v.shape

    def body(x_ref, out_ref, send_sems, recv_sems, local_sem):
        x, y, c = _place()
        me, sibling = (x, y, c), (x, y, 1 - c)
        chips = [(1 - x, y), (x, 1 - y), (1 - x, 1 - y)]

        def rows(px, py, pc):
            return out_ref.at[pl.ds((4 * px + 2 * py + pc) * m_per, m_per), :]

        def copy(k, block, to, src=None):
            return pltpu.make_async_remote_copy(
                src_ref=rows(*block) if src is None else src, dst_ref=rows(*block),
                send_sem=send_sems.at[k], recv_sem=recv_sems.at[k], device_id=to, device_id_type=MESH)

        mine = pltpu.make_async_copy(x_ref, rows(*me), local_sem)
        mine.start()
        first = [copy(0, me, sibling, src=x_ref)]
        first += [copy(1 + j, me, (*chip, c), src=x_ref) for j, chip in enumerate(chips)]
        for cp in first:
            cp.start()
        passed = [copy(4 + j, (*chip, c), sibling) for j, chip in enumerate(chips)]
        for j, chip in enumerate(chips):
            copy(1 + j, (*chip, c), me).wait_recv()
            passed[j].start()
        copy(0, sibling, me).wait_recv()
        for j, chip in enumerate(chips):
            copy(4 + j, (*chip, 1 - c), me).wait_recv()
        for cp in first + passed:
            cp.wait_send()
        mine.wait()

    return pl.pallas_call(
        body, name=name,
        out_shape=jax.ShapeDtypeStruct((N_DEV * m_per, n), v.dtype),
        in_specs=[VMEM_FULL], out_specs=VMEM_FULL,
        scratch_shapes=[pltpu.SemaphoreType.DMA((7,)), pltpu.SemaphoreType.DMA((7,)), pltpu.SemaphoreType.DMA],
    )(v)


def _sibling_send_other_half(g, name):
    n, r, ccols = g.shape
    h = r // 2

    def body(g_ref, out_ref, send_sem, recv_sem):
        x, y, c = _place()
        cp = pltpu.make_async_remote_copy(
            src_ref=g_ref.at[:, pl.ds((1 - c) * h, h), :], dst_ref=out_ref, send_sem=send_sem, recv_sem=recv_sem,
            device_id=(x, y, 1 - c), device_id_type=MESH)
        cp.start()
        cp.wait()

    return pl.pallas_call(
        body, name=name, out_shape=jax.ShapeDtypeStruct((n, h, ccols), g.dtype),
        in_specs=[ANY], out_specs=ANY,
        scratch_shapes=[pltpu.SemaphoreType.DMA, pltpu.SemaphoreType.DMA],
    )(g)


def _sibling_complete(full, name):
    r, ccols = full.shape
    h = r // 2

    def body(in_ref, out_ref, send_sem, recv_sem):
        x, y, c = _place()

        def copy(rows):
            return pltpu.make_async_remote_copy(
                src_ref=in_ref.at[pl.ds(c * h, h), :], dst_ref=out_ref.at[rows, :], send_sem=send_sem,
                recv_sem=recv_sem, device_id=(x, y, 1 - c), device_id_type=MESH)

        mine = copy(pl.ds(c * h, h))
        mine.start()
        copy(pl.ds((1 - c) * h, h)).wait_recv()
        mine.wait_send()

    return pl.pallas_call(
        body, name=name, out_shape=jax.ShapeDtypeStruct(full.shape, full.dtype), in_specs=[ANY], out_specs=ANY,
        input_output_aliases={0: 0}, scratch_shapes=[pltpu.SemaphoreType.DMA, pltpu.SemaphoreType.DMA],
    )(full)


HBM_SPEC = pl.BlockSpec(memory_space=pltpu.HBM)
SEM_SPEC = pl.BlockSpec(memory_space=pltpu.SEMAPHORE)
N_PEERS = 3


def _split_params():
    return pltpu.CompilerParams(has_side_effects=pltpu.SideEffectType.DATAFLOW_SIDE_EFFECTING)


def _n_peers(mode):
    return 1 if mode == "pair" else N_PEERS


def _chip_exchange(mode, src_ref, land_ref, send_sems, recv_sems):
    x, y, c = _place()
    k_me = 2 * x + y
    if mode == "pair":
        h = src_ref.shape[1] // 2
        return [tuple(
            pltpu.make_async_remote_copy(src_ref=src_ref.at[:, pl.ds((1 - c) * h, h), :], dst_ref=land_ref,
                                         send_sem=send_sems[0], recv_sem=recv_sems[0], device_id=(x, y, 1 - c),
                                         device_id_type=MESH) for _ in range(2))]
    pairs = []
    for j, chip in enumerate([(1 - x, y), (x, 1 - y), (1 - x, 1 - y)]):
        k_j = 2 * chip[0] + chip[1]
        if mode == "halves":
            h = src_ref.shape[0] // 2
            rows = pl.ds(c * h, h)
            src, there, here = src_ref.at[rows, :], land_ref.at[k_me, rows, :], land_ref.at[k_j, rows, :]
        else:
            src, there, here = src_ref.at[k_j], land_ref.at[k_me], land_ref.at[k_j]
        pairs.append(tuple(
            pltpu.make_async_remote_copy(src_ref=src, dst_ref=dst, send_sem=send_sems[j], recv_sem=recv_sems[j],
                                         device_id=(*chip, c), device_id_type=MESH) for dst in (there, here)))
    return pairs


def _exchange_start(src, land_shape, mode, name):
    n = _n_peers(mode)

    def body(src_ref, land_ref, *rest):
        sems, token = rest[:2 * n], rest[2 * n + 2]
        for start, _ in _chip_exchange(mode, src_ref, land_ref, sems[:n], sems[n:]):
            start.start()
        token[...] = jnp.zeros_like(token)

    sem = pltpu.SemaphoreType.DMA(())
    res = pl.pallas_call(
        body, name=name,
        out_shape=(sem,) * (2 * n) + (pltpu.HBM(src.shape, src.dtype), pltpu.HBM(land_shape, src.dtype),
                                     jax.ShapeDtypeStruct((8, LANES), F32)),
        in_specs=(HBM_SPEC, HBM_SPEC), out_specs=(SEM_SPEC,) * (2 * n) + (HBM_SPEC, HBM_SPEC, VMEM_FULL),
        input_output_aliases={0: 2 * n, 1: 2 * n + 1}, compiler_params=_split_params(),
    )(pltpu.with_memory_space_constraint(src, pltpu.HBM),
      pltpu.with_memory_space_constraint(lax.empty(land_shape, src.dtype), pltpu.HBM))
    return res[:2 * n], res[2 * n], res[2 * n + 1], res[2 * n + 2]


def _exchange_wait(sems, src_thru, land_thru, after, mode, name):
    n = _n_peers(mode)

    def body(src_ref, land_ref, *rest):
        sem_refs = rest[:2 * n]
        waits = [w for _, w in _chip_exchange(mode, src_ref, land_ref, sem_refs[:n], sem_refs[n:])]
        for w in waits:
            w.wait_send()
        for w in waits:
            w.wait_recv()

    return pl.pallas_call(
        body, name=name,
        out_shape=(pltpu.HBM(src_thru.shape, src_thru.dtype), pltpu.HBM(land_thru.shape, land_thru.dtype)),
        in_specs=(HBM_SPEC, HBM_SPEC) + (SEM_SPEC,) * (2 * n) + (ANY,), out_specs=(HBM_SPEC, HBM_SPEC),
        input_output_aliases={0: 0, 1: 1}, compiler_params=_split_params(),
    )(src_thru, land_thru, *sems, after)


def _sibling_fill(land, name):
    n, r, ccols = land.shape
    h = r // 2

    def body(land_ref, out_ref, send_sems, recv_sems):
        x, y, c = _place()
        slots = [2 * (1 - x) + y, 2 * x + (1 - y), 2 * (1 - x) + (1 - y)]

        def copy(j, rows):
            return pltpu.make_async_remote_copy(
                src_ref=land_ref.at[slots[j], pl.ds(c * h, h), :], dst_ref=out_ref.at[slots[j], rows, :],
                send_sem=send_sems.at[j], recv_sem=recv_sems.at[j], device_id=(x, y, 1 - c), device_id_type=MESH)

        sends = [copy(j, pl.ds(c * h, h)) for j in range(N_PEERS)]
        for cp in sends:
            cp.start()
        for j in range(N_PEERS):
            copy(j, pl.ds((1 - c) * h, h)).wait_recv()
        for cp in sends:
            cp.wait_send()

    return pl.pallas_call(
        body, name=name, out_shape=jax.ShapeDtypeStruct(land.shape, land.dtype), in_specs=[ANY], out_specs=ANY,
        input_output_aliases={0: 0},
        scratch_shapes=[pltpu.SemaphoreType.DMA((N_PEERS,)), pltpu.SemaphoreType.DMA((N_PEERS,))],
    )(land)


class _GatheredWeights:
    def __init__(self, shards):
        self._pending, self._done = {}, {}
        token = jnp.zeros((), BF16)
        for name, shard in shards.items():
            flat = shard.reshape(-1, shard.shape[-1]).astype(BF16) + token
            sems, thru, land, tok = _exchange_start(flat, (N_CHIPS, *flat.shape), "halves", f"gather_start_{name}")
            self._pending[name] = (sems, thru, land, shard.shape)
            token = tok[0, 0].astype(BF16)
        self.token = tok[0, 0]

    def take(self, name, after):
        if name not in self._done:
            sems, thru, land, shape = self._pending.pop(name)
            flat, land = _exchange_wait(sems, thru, land, after, "halves", f"gather_wait_{name}")
            land = _sibling_fill(land, f"gather_fill_{name}")
            land = lax.dynamic_update_index_in_dim(land, flat, _chip_index(), 0)
            self._done[name] = land.reshape(N_CHIPS, *shape)
        return self._done[name]


class _GradientExchange:
    def __init__(self):
        self._pair, self._pending = None, []

    def _start_all_to_all(self, name, slots, recv):
        part = _add_own_half(slots, recv, f"rs_add2_{name}")
        sems, thru, land, tok = _exchange_start(part, part.shape, "slots", f"rs_start_{name}")
        self._pending.append((name, sems, thru, land))
        return tok[0, 0]

    def _advance(self, after):
        if self._pair is None:
            return jnp.zeros((), F32)
        name, sems, thru, land = self._pair
        self._pair = None
        slots, recv = _exchange_wait(sems, thru, land, after, "pair", f"rs_pair_wait_{name}")
        return self._start_all_to_all(name, slots, recv)

    def put(self, name, slots, defer=True):
        tok = self._advance(slots)
        n, r, ccols = slots.shape
        if defer:
            sems, thru, land, ptok = _exchange_start(slots, (n, r // 2, ccols), "pair", f"rs_pair_start_{name}")
            self._pair = (name, sems, thru, land)
            return tok + ptok[0, 0]
        recv = _sibling_send_other_half(slots, f"rs_pair_{name}")
        return tok + self._start_all_to_all(name, slots, recv)

    def finish(self, after):
        self._advance(after)
        out = {}
        k_me = _chip_index()
        for name, sems, thru, land in self._pending:
            part, land = _exchange_wait(sems, thru, land, after, "slots", f"rs_wait_{name}")
            got = lax.dynamic_update_index_in_dim(land, lax.dynamic_index_in_dim(part, k_me, 0, keepdims=False), k_me, 0)
            out[name] = _sibling_complete(_sum_slots(got, f"rs_add4_{name}"), f"rs_fin_{name}")
        return out


def _row_tile(rows, cols, itemsize, budget=2 << 20):
    t = rows
    while t % 2 == 0 and t * cols * itemsize > budget and (t // 2) % 16 == 0:
        t //= 2
    return t


def _add_own_half(g, recv, name):
    n, r, ccols = g.shape
    h = r // 2
    t = _row_tile(h, ccols, 4)
    nt = h // t
    c_idx = lax.axis_index("c").astype(jnp.int32).reshape(1)

    def body(c_ref, g_ref, r_ref, o_ref):
        o_ref[...] = (g_ref[...].astype(F32) + r_ref[...].astype(F32)).astype(o_ref.dtype)

    grid_spec = pltpu.PrefetchScalarGridSpec(
        num_scalar_prefetch=1, grid=(n, nt),
        in_specs=[pl.BlockSpec((None, t, ccols), lambda j, i, c_ref: (j, c_ref[0] * nt + i, 0)),
                  pl.BlockSpec((None, t, ccols), lambda j, i, c_ref: (j, i, 0))],
        out_specs=pl.BlockSpec((None, t, ccols), lambda j, i, c_ref: (j, i, 0)))
    return pl.pallas_call(body, name=name, grid_spec=grid_spec,
                          out_shape=jax.ShapeDtypeStruct((n, h, ccols), BF16),
                          compiler_params=_cparams(("parallel", "parallel")))(c_idx, g, recv)


def _sum_slots(q, name):
    n, h, ccols = q.shape
    t = _row_tile(h, ccols, 4)
    nt = h // t
    c_idx = lax.axis_index("c").astype(jnp.int32).reshape(1)

    def body(c_ref, q_ref, o_ref):
        acc = q_ref[0].astype(F32)
        for j in range(1, n):
            acc = acc + q_ref[j].astype(F32)
        o_ref[...] = acc

    grid_spec = pltpu.PrefetchScalarGridSpec(
        num_scalar_prefetch=1, grid=(nt,),
        in_specs=[pl.BlockSpec((n, t, ccols), lambda i, c_ref: (0, i, 0))],
        out_specs=pl.BlockSpec((t, ccols), lambda i, c_ref: (c_ref[0] * nt + i, 0)))
    return pl.pallas_call(body, name=name, grid_spec=grid_spec, out_shape=jax.ShapeDtypeStruct((2 * h, ccols), F32),
                          compiler_params=_cparams(("parallel",)))(c_idx, q)


def _sum_devices(v, name):
    n, r, ccols = v.shape
    t = 8
    while r % (t * 2) == 0 and t * 2 * ccols * 4 * n <= (8 << 20):
        t *= 2

    def body(v_ref, o_ref):
        acc = v_ref[0]
        for j in range(1, n):
            acc = acc + v_ref[j]
        o_ref[...] = acc

    return pl.pallas_call(
        body, name=name, grid=(r // t,),
        in_specs=[pl.BlockSpec((n, t, ccols), lambda i: (0, i, 0))],
        out_specs=pl.BlockSpec((t, ccols), lambda i: (i, 0)),
        out_shape=jax.ShapeDtypeStruct((r, ccols), F32), compiler_params=_cparams(("parallel",)))(v)


def _adamw(w, g, m, v, name):
    r, ccols = w.shape
    t = _row_tile(r, ccols, 4, budget=1 << 20)
    c1 = 1.0 / (1.0 - ADAM_B1 ** ADAM_STEP)
    c2 = 1.0 / (1.0 - ADAM_B2 ** ADAM_STEP)

    def body(w_ref, g_ref, m_ref, v_ref, d_ref, nm_ref, nv_ref):
        gg = g_ref[...]
        nm = ADAM_B1 * m_ref[...] + (1.0 - ADAM_B1) * gg
        nv = ADAM_B2 * v_ref[...] + (1.0 - ADAM_B2) * (gg * gg)
        d_ref[...] = -ADAM_LR * ((nm * c1) / (jnp.sqrt(nv * c2) + ADAM_EPS) + ADAM_WD * w_ref[...])
        nm_ref[...] = nm
        nv_ref[...] = nv

    spec = pl.BlockSpec((t, ccols), lambda i: (i, 0))
    sds = jax.ShapeDtypeStruct((r, ccols), F32)
    return pl.pallas_call(body, name=name, grid=(r // t,), in_specs=[spec] * 4, out_specs=[spec] * 3,
                          out_shape=[sds] * 3, compiler_params=_cparams(("parallel",)))(w, g, m, v)


NN = (((1,), (0,)), ((), ()))
NT = (((1,), (1,)), ((), ()))
TN = (((0,), (0,)), ((), ()))


def _mm_core(name, a, b, *, dims, grid, a_spec, b_spec, out_shapes, out_specs, acc_shape,
             extras=(), extra_specs=(), epilogue=None):
    nk = grid[2]
    n_ex, n_out = len(extras), len(out_shapes)
    if epilogue is None:
        def epilogue(acc, ex, outs):
            outs[0][...] = acc.astype(outs[0].dtype)

    def body(*refs):
        a_ref, b_ref = refs[0], refs[1]
        ex = refs[2:2 + n_ex]
        outs = refs[2 + n_ex:2 + n_ex + n_out]
        part = lax.dot_general(a_ref[...].astype(BF16), b_ref[...].astype(BF16), dims, preferred_element_type=F32)
        if nk == 1:
            epilogue(part, ex, outs)
        else:
            acc = refs[-1]
            k = pl.program_id(2)

            @pl.when(k == 0)
            def _():
                acc[...] = part

            @pl.when(k > 0)
            def _():
                acc[...] += part

            @pl.when(k == nk - 1)
            def _():
                epilogue(acc[...], ex, outs)

    scratch = [] if nk == 1 else [pltpu.VMEM(acc_shape, F32)]
    res = pl.pallas_call(
        body, name=name, grid=grid, in_specs=[a_spec, b_spec, *extra_specs], out_specs=list(out_specs),
        out_shape=list(out_shapes), scratch_shapes=scratch,
        compiler_params=_cparams(("parallel", "parallel", "arbitrary")))(a, b, *extras)
    return res


def _mm_nn(name, a, b, tm, tn, tk, out_dtypes=(BF16,), extras=(), extra_specs=(), epilogue=None, b_tile=None, n=None):
    m, kk = a.shape
    n = b.shape[1] if b_tile is None else n
    tm, tn, tk = min(tm, m), min(tn, n), min(tk, kk)
    grid = (m // tm, n // tn, kk // tk)
    if b_tile is None:
        b_spec = pl.BlockSpec((tk, tn), lambda i, j, k: (k, j))
    else:
        b_spec = pl.BlockSpec((None,) * (b.ndim - 2) + (tk, tn), lambda i, j, k: b_tile(j, k))
    return _mm_core(
        name, a, b, dims=NN, grid=grid,
        a_spec=pl.BlockSpec((tm, tk), lambda i, j, k: (i, k)), b_spec=b_spec,
        out_shapes=[jax.ShapeDtypeStruct((m, n), dt) for dt in out_dtypes],
        out_specs=[pl.BlockSpec((tm, tn), lambda i, j, k: (i, j)) for _ in out_dtypes],
        acc_shape=(tm, tn), extras=extras, extra_specs=extra_specs, epilogue=epilogue)


def _mm_nt(name, a, b, tm, tn, tk, out_dtypes=(BF16,), extras=(), extra_specs=(), epilogue=None, b_tile=None, n=None):
    m, kk = a.shape
    n = b.shape[0] if b_tile is None else n
    tm, tn, tk = min(tm, m), min(tn, n), min(tk, kk)
    grid = (m // tm, n // tn, kk // tk)
    if b_tile is None:
        b_spec = pl.BlockSpec((tn, tk), lambda i, j, k: (j, k))
    else:
        b_spec = pl.BlockSpec((None,) * (b.ndim - 2) + (tn, tk), lambda i, j, k: b_tile(j, k))
    return _mm_core(
        name, a, b, dims=NT, grid=grid,
        a_spec=pl.BlockSpec((tm, tk), lambda i, j, k: (i, k)), b_spec=b_spec,
        out_shapes=[jax.ShapeDtypeStruct((m, n), dt) for dt in out_dtypes],
        out_specs=[pl.BlockSpec((tm, tn), lambda i, j, k: (i, j)) for _ in out_dtypes],
        acc_shape=(tm, tn), extras=extras, extra_specs=extra_specs, epilogue=epilogue)


def _mm_tn(name, a, b, tm, tn, tk, out_dtype=BF16, col_chunks=1):
    kk, m = a.shape
    n = b.shape[1]
    tm, tn, tk = min(tm, m), min(tn, n), min(tk, kk)
    grid = (m // tm, n // tn, kk // tk)
    if col_chunks == 1:
        out_shape, out_spec = (m, n), pl.BlockSpec((tm, tn), lambda i, j, k: (i, j))
    else:
        per = n // col_chunks // tn
        out_shape = (col_chunks, m, n // col_chunks)
        out_spec = pl.BlockSpec((None, tm, tn), lambda i, j, k: (j // per, i, j % per))
    return _mm_core(
        name, a, b, dims=TN, grid=grid,
        a_spec=pl.BlockSpec((tk, tm), lambda i, j, k: (k, i)), b_spec=pl.BlockSpec((tk, tn), lambda i, j, k: (k, j)),
        out_shapes=[jax.ShapeDtypeStruct(out_shape, out_dtype)], out_specs=[out_spec], acc_shape=(tm, tn))[0]


def _resid_gate_epilogue(acc, ex, outs):
    outs[0][...] = ex[0][...] + ex[1][...] * acc
    outs[1][...] = acc.astype(BF16)


def _mm_resid(name, a, b, resid, gate, tm, tn, tk, b_tile=None, n=None):
    return _mm_nn(
        name, a, b, tm, tn, tk, out_dtypes=(F32, BF16), extras=(resid, gate),
        extra_specs=(pl.BlockSpec((tm, tn), lambda i, j, k: (i, j)), pl.BlockSpec((1, tn), lambda i, j, k: (0, j))),
        epilogue=_resid_gate_epilogue, b_tile=b_tile, n=n)


TOK_TILE = 512


def _colsum8(v):
    t, ccols = v.shape
    return jnp.sum(v.reshape(t // 8, 8, ccols), axis=0)


def _norm_mod_fwd(x, g, sc, sh, name):
    s, d = x.shape
    t = TOK_TILE

    def body(x_ref, g_ref, sc_ref, sh_ref, h_ref):
        xv = x_ref[...]
        n = xv * lax.rsqrt(jnp.mean(xv * xv, axis=-1, keepdims=True) + NORM_EPS)
        h_ref[...] = ((n * g_ref[...]) * (1.0 + sc_ref[...]) + sh_ref[...]).astype(BF16)

    row = pl.BlockSpec((1, d), lambda i: (0, 0))
    return pl.pallas_call(
        body, name=name, grid=(s // t,), in_specs=[pl.BlockSpec((t, d), lambda i: (i, 0)), row, row, row],
        out_specs=pl.BlockSpec((t, d), lambda i: (i, 0)), out_shape=jax.ShapeDtypeStruct((s, d), BF16),
        compiler_params=_cparams(("parallel",)))(x, g, sc, sh)


def _norm_mod_bwd(x, dh, resid, g, sc, name):
    s, d = x.shape
    t = TOK_TILE

    def body(x_ref, dh_ref, r_ref, g_ref, sc_ref, dx_ref, dsh_ref, dsc_ref, dg_ref):
        i = pl.program_id(0)
        xv = x_ref[...]
        rstd = lax.rsqrt(jnp.mean(xv * xv, axis=-1, keepdims=True) + NORM_EPS)
        n = xv * rstd
        dhv = dh_ref[...].astype(F32)
        gv = g_ref[...]
        dyn = dhv * (1.0 + sc_ref[...])
        dn = dyn * gv
        dx_ref[...] = r_ref[...] + rstd * (dn - n * jnp.mean(dn * n, axis=-1, keepdims=True))

        @pl.when(i == 0)
        def _():
            dsh_ref[...] = jnp.zeros_like(dsh_ref)
            dsc_ref[...] = jnp.zeros_like(dsc_ref)
            dg_ref[...] = jnp.zeros_like(dg_ref)

        dsh_ref[...] += _colsum8(dhv)
        dsc_ref[...] += _colsum8(dhv * (n * gv))
        dg_ref[...] += _colsum8(dyn * n)

    tile = pl.BlockSpec((t, d), lambda i: (i, 0))
    row = pl.BlockSpec((1, d), lambda i: (0, 0))
    acc = pl.BlockSpec((8, d), lambda i: (0, 0))
    acc_s = jax.ShapeDtypeStruct((8, d), F32)
    return pl.pallas_call(
        body, name=name, grid=(s // t,), in_specs=[tile, tile, tile, row, row],
        out_specs=[tile, acc, acc, acc], out_shape=[jax.ShapeDtypeStruct((s, d), F32), acc_s, acc_s, acc_s],
        compiler_params=_cparams(("arbitrary",)))(x, dh, resid, g, sc)


def _gate_bwd(dxo, y, gate, name):
    s, d = dxo.shape
    t = TOK_TILE

    def body(dx_ref, y_ref, g_ref, dy_ref, dg_ref):
        i = pl.program_id(0)
        dxv = dx_ref[...]
        dy_ref[...] = (dxv * g_ref[...]).astype(BF16)

        @pl.when(i == 0)
        def _():
            dg_ref[...] = jnp.zeros_like(dg_ref)

        dg_ref[...] += _colsum8(dxv * y_ref[...].astype(F32))

    tile = pl.BlockSpec((t, d), lambda i: (i, 0))
    return pl.pallas_call(
        body, name=name, grid=(s // t,), in_specs=[tile, tile, pl.BlockSpec((1, d), lambda i: (0, 0))],
        out_specs=[tile, pl.BlockSpec((8, d), lambda i: (0, 0))],
        out_shape=[jax.ShapeDtypeStruct((s, d), BF16), jax.ShapeDtypeStruct((8, d), F32)],
        compiler_params=_cparams(("arbitrary",)))(dxo, y, gate)


def _final_loss(x, g, target, name):
    s, d = x.shape
    t = TOK_TILE
    inv_d = 1.0 / d

    def body(x_ref, g_ref, t_ref, dx_ref, sq_ref, dg_ref):
        i = pl.program_id(0)
        xv = x_ref[...]
        rstd = lax.rsqrt(jnp.mean(xv * xv, axis=-1, keepdims=True) + NORM_EPS)
        n = xv * rstd
        gv = g_ref[...]
        err = n * gv - t_ref[...]
        dout = err * inv_d
        dn = dout * gv
        dx_ref[...] = rstd * (dn - n * jnp.mean(dn * n, axis=-1, keepdims=True))

        @pl.when(i == 0)
        def _():
            sq_ref[...] = jnp.zeros_like(sq_ref)
            dg_ref[...] = jnp.zeros_like(dg_ref)

        sq_ref[...] += _colsum8(err * err)
        dg_ref[...] += _colsum8(dout * n)

    tile = pl.BlockSpec((t, d), lambda i: (i, 0))
    acc = pl.BlockSpec((8, d), lambda i: (0, 0))
    acc_s = jax.ShapeDtypeStruct((8, d), F32)
    return pl.pallas_call(
        body, name=name, grid=(s // t,), in_specs=[tile, pl.BlockSpec((1, d), lambda i: (0, 0)), tile],
        out_specs=[tile, acc, acc], out_shape=[jax.ShapeDtypeStruct((s, d), F32), acc_s, acc_s],
        compiler_params=_cparams(("arbitrary",)))(x, g, target)


NEG_BIG = -1e30


def _head_masks(rows):
    lane = lax.broadcasted_iota(jnp.int32, (rows, LANES), 1)
    return [(lane // HEAD_DIM) == hh for hh in range(2)]


def _group_view(slabs, gi):
    dil = ATTN_GROUPS[gi][1]
    _, s, w = slabs.shape
    if dil == 1:
        return slabs, (gi, 3 + gi, 6 + gi)
    return slabs[gi:9:3].reshape(3, s // dil, dil * w), (0, 1, 2)


def _attn_fwd(slabs, gi, name):
    window, dil = ATTN_GROUPS[gi]
    n_back = window // dil
    _, s, w = slabs.shape
    big_l = s // dil
    nb = big_l // ATTN_BLOCK
    blk = ATTN_BLOCK
    view, (iq, ik, iv) = _group_view(slabs, gi)
    scale = HEAD_DIM ** -0.5

    def body(q_ref, kp_ref, kc_ref, vp_ref, vc_ref, o_ref, lse_ref):
        n = pl.program_id(1)
        qi = lax.broadcasted_iota(jnp.int32, (blk, 2 * blk), 0)
        kj = lax.broadcasted_iota(jnp.int32, (blk, 2 * blk), 1)
        dist = qi + blk - kj
        ok = (dist >= 0) & (dist <= n_back) & ((kj >= blk) | (n > 0))
        mq = _head_masks(blk)
        mk = _head_masks(2 * blk)
        for p in range(w // LANES):
            cols = slice(p * LANES, (p + 1) * LANES)
            qp = q_ref[:, cols]
            k2 = jnp.concatenate([kp_ref[:, cols], kc_ref[:, cols]], axis=0)
            v2 = jnp.concatenate([vp_ref[:, cols], vc_ref[:, cols]], axis=0)
            o_pair = jnp.zeros((blk, LANES), F32)
            lse_pair = jnp.zeros((blk, LANES), F32)
            for hh in range(2):
                qm = jnp.where(mq[hh], qp, jnp.zeros_like(qp))
                sc = lax.dot_general(qm, k2, NT, preferred_element_type=F32) * scale
                sc = jnp.where(ok, sc, NEG_BIG)
                mx = jnp.max(sc, axis=-1, keepdims=True)
                pe = jnp.exp(sc - mx)
                den = jnp.sum(pe, axis=-1, keepdims=True)
                pn = (pe / den).astype(BF16)
                vm = jnp.where(mk[hh], v2, jnp.zeros_like(v2))
                o_pair = o_pair + jnp.dot(pn, vm, preferred_element_type=F32)
                lse_pair = jnp.where(mq[hh], mx + jnp.log(den), lse_pair)
            o_ref[:, cols] = o_pair
            lse_ref[:, cols] = lse_pair

    def spec(slab, prev):
        if prev:
            return pl.BlockSpec((None, blk, w), lambda r, n: (slab, jnp.maximum(n - 1, 0), r))
        return pl.BlockSpec((None, blk, w), lambda r, n: (slab, n, r))

    out_spec = pl.BlockSpec((blk, w), lambda r, n: (n, r))
    sds = jax.ShapeDtypeStruct((big_l, dil * w), F32)
    o, lse = pl.pallas_call(
        body, name=name, grid=(dil, nb),
        in_specs=[spec(iq, False), spec(ik, True), spec(ik, False), spec(iv, True), spec(iv, False)],
        out_specs=[out_spec, out_spec], out_shape=[sds, sds],
        compiler_params=_cparams(("parallel", "arbitrary")))(view, view, view, view, view)
    return o.reshape(s, w), lse.reshape(s, w)


def _attn_merge(outs, lses, name):
    s, w = outs[0].shape
    t = TOK_TILE

    def body(o0, o1, o2, l0, l1, l2, a_ref, lt_ref):
        ls = [l0[...], l1[...], l2[...]]
        mx = jnp.maximum(jnp.maximum(ls[0], ls[1]), ls[2])
        es = [jnp.exp(l - mx) for l in ls]
        den = es[0] + es[1] + es[2]
        num = es[0] * o0[...] + es[1] * o1[...] + es[2] * o2[...]
        a_ref[...] = (num / den).astype(BF16)
        lt_ref[...] = mx + jnp.log(den)

    tile = pl.BlockSpec((t, w), lambda i: (i, 0))
    return pl.pallas_call(
        body, name=name, grid=(s // t,), in_specs=[tile] * 6, out_specs=[tile, tile],
        out_shape=[jax.ShapeDtypeStruct((s, w), BF16), jax.ShapeDtypeStruct((s, w), F32)],
        compiler_params=_cparams(("parallel",)))(*outs, *lses)


def _attn_bwd(slabs, dap, ap, lse_tot, gi, name):
    window, dil = ATTN_GROUPS[gi]
    n_back = window // dil
    _, s, w = slabs.shape
    big_l = s // dil
    nb = big_l // ATTN_BLOCK
    blk = ATTN_BLOCK
    view, (iq, ik, iv) = _group_view(slabs, gi)
    dap_v = dap.reshape(big_l, dil * 2 * w)
    ap_v = ap.reshape(big_l, dil * 2 * w)
    lse_v = lse_tot.reshape(big_l, dil * w)
    scale = HEAD_DIM ** -0.5

    def body(qc_ref, qn_ref, kp_ref, kc_ref, vp_ref, vc_ref, dc_ref, dn_ref, ac_ref, an_ref, lc_ref, ln_ref, out_ref):
        m = pl.program_id(1)
        qi = lax.broadcasted_iota(jnp.int32, (blk, 2 * blk), 0)
        kj = lax.broadcasted_iota(jnp.int32, (blk, 2 * blk), 1)
        dist = qi + blk - kj
        ok = (dist >= 0) & (dist <= n_back) & ((kj >= blk) | (m > 0))
        qi1 = lax.broadcasted_iota(jnp.int32, (blk, blk), 0)
        kj1 = lax.broadcasted_iota(jnp.int32, (blk, blk), 1)
        ok_next = (qi1 + blk - kj1 <= n_back) & (m + 1 < nb)
        mq = _head_masks(blk)
        mk = _head_masks(2 * blk)

        def per_head(q_t, d_t, a_t, l_t, hh):
            qm = jnp.where(mq[hh], q_t, jnp.zeros_like(q_t))
            dm = jnp.where(mq[hh], d_t, jnp.zeros_like(d_t))
            delta = jnp.sum(jnp.where(mq[hh], d_t.astype(F32) * a_t.astype(F32), 0.0), axis=-1, keepdims=True)
            lse_h = jnp.max(jnp.where(mq[hh], l_t, NEG_BIG), axis=-1, keepdims=True)
            return qm, dm, delta, lse_h

        for p in range(w // LANES):
            cols = slice(p * LANES, (p + 1) * LANES)
            k2 = jnp.concatenate([kp_ref[:, cols], kc_ref[:, cols]], axis=0)
            v2 = jnp.concatenate([vp_ref[:, cols], vc_ref[:, cols]], axis=0)
            kc, vc = kc_ref[:, cols], vc_ref[:, cols]
            dq_pair = jnp.zeros((blk, LANES), F32)
            dk_pair = jnp.zeros((blk, LANES), F32)
            dv_pair = jnp.zeros((blk, LANES), F32)
            for hh in range(2):
                qm, dm, delta, lse_h = per_head(qc_ref[:, cols], dc_ref[:, cols], ac_ref[:, cols], lc_ref[:, cols], hh)
                sc = lax.dot_general(qm, k2, NT, preferred_element_type=F32) * scale
                pr = jnp.exp(jnp.where(ok, sc, NEG_BIG) - lse_h)
                dp = lax.dot_general(dm, v2, NT, preferred_element_type=F32)
                ds = pr * (dp - delta)
                ds_b = ds.astype(BF16)
                km = jnp.where(mk[hh], k2, jnp.zeros_like(k2))
                dq_pair = dq_pair + jnp.dot(ds_b, km, preferred_element_type=F32)
                dk_pair = dk_pair + lax.dot_general(ds_b[:, blk:], qm, TN, preferred_element_type=F32)
                dv_pair = dv_pair + lax.dot_general(pr[:, blk:].astype(BF16), dm, TN, preferred_element_type=F32)
                qm, dm, delta, lse_h = per_head(qn_ref[:, cols], dn_ref[:, cols], an_ref[:, cols], ln_ref[:, cols], hh)
                sc = lax.dot_general(qm, kc, NT, preferred_element_type=F32) * scale
                pr = jnp.exp(jnp.where(ok_next, sc, NEG_BIG) - lse_h)
                dp = lax.dot_general(dm, vc, NT, preferred_element_type=F32)
                ds_b = (pr * (dp - delta)).astype(BF16)
                dk_pair = dk_pair + lax.dot_general(ds_b, qm, TN, preferred_element_type=F32)
                dv_pair = dv_pair + lax.dot_general(pr.astype(BF16), dm, TN, preferred_element_type=F32)
            out_ref[0, :, cols] = (dq_pair * scale).astype(BF16)
            out_ref[1, :, cols] = (dk_pair * scale).astype(BF16)
            out_ref[2, :, cols] = dv_pair.astype(BF16)

    def slab_spec(slab, shift):
        if shift < 0:
            return pl.BlockSpec((None, blk, w), lambda r, n: (slab, jnp.maximum(n - 1, 0), r))
        if shift > 0:
            return pl.BlockSpec((None, blk, w), lambda r, n: (slab, jnp.minimum(n + 1, nb - 1), r))
        return pl.BlockSpec((None, blk, w), lambda r, n: (slab, n, r))

    def tok_spec(stride, shift):
        if shift > 0:
            return pl.BlockSpec((blk, w), lambda r, n: (jnp.minimum(n + 1, nb - 1), stride * r))
        return pl.BlockSpec((blk, w), lambda r, n: (n, stride * r))

    out = pl.pallas_call(
        body, name=name, grid=(dil, nb),
        in_specs=[slab_spec(iq, 0), slab_spec(iq, 1), slab_spec(ik, -1), slab_spec(ik, 0),
                  slab_spec(iv, -1), slab_spec(iv, 0),
                  tok_spec(2, 0), tok_spec(2, 1), tok_spec(2, 0), tok_spec(2, 1), tok_spec(1, 0), tok_spec(1, 1)],
        out_specs=pl.BlockSpec((3, blk, w), lambda r, n: (0, n, r)),
        out_shape=jax.ShapeDtypeStruct((3, big_l, dil * w), BF16),
        compiler_params=_cparams(("parallel", "arbitrary")))(
            view, view, view, view, view, view, dap_v, dap_v, ap_v, ap_v, lse_v, lse_v)
    return out.reshape(3, s, w)


def _shift_down(cur, prev, j):
    row = lax.broadcasted_iota(jnp.int32, cur.shape, 0)
    return jnp.where(row >= j, pltpu.roll(cur, j, 0), pltpu.roll(prev, j, 0))


def _shift_up(cur, nxt, j):
    t = cur.shape[0]
    row = lax.broadcasted_iota(jnp.int32, cur.shape, 0)
    return jnp.where(row < t - j, pltpu.roll(cur, t - j, 0), pltpu.roll(nxt, t - j, 0))


def _window_sum_down(cur, prev, w):
    s, sp, step = cur, prev, 1
    while step < w:
        s_new = s + _shift_down(s, sp, step)
        sp = sp + pltpu.roll(sp, step, 0)
        s, step = s_new, step * 2
    return s


def _window_sum_up(cur, nxt, w):
    t = cur.shape[0]
    s, sn, step = cur, nxt, 1
    while step < w:
        s_new = s + _shift_up(s, sn, step)
        sn = sn + pltpu.roll(sn, t - step, 0)
        s, step = s_new, step * 2
    return s


def _pool_count(tile_idx, t, w):
    row = lax.broadcasted_iota(jnp.int32, (t, LANES), 0) + tile_idx * t
    return jnp.minimum(row + 1, w).astype(F32)


def _pool_fwd(slabs, attn, pool_w, pool_scale, name):
    _, s, w = slabs.shape
    t = TOK_TILE
    gw = POOL_W // len(POOL_WINDOWS)

    def body(u_ref, up_ref, a_ref, w_ref, sc_ref, ap_ref, d_ref):
        i = pl.program_id(0)
        ap_ref[:, :w] = a_ref[...]
        for gi, win in enumerate(POOL_WINDOWS):
            cols = slice(gi * gw, (gi + 1) * gw)
            u = u_ref[:, cols].astype(F32)
            up = jnp.where(i > 0, up_ref[:, cols].astype(F32), 0.0)
            d = (_window_sum_down(u, up, win) / _pool_count(i, t, win) - u).astype(BF16)
            d_ref[:, cols] = d
            y = jnp.dot(d, w_ref[gi].astype(BF16), preferred_element_type=F32)
            ap_ref[:, w + gi * gw:w + (gi + 1) * gw] = (y * sc_ref[:, cols]).astype(BF16)

    return pl.pallas_call(
        body, name=name, grid=(s // t,),
        in_specs=[pl.BlockSpec((None, t, w), lambda i: (N_SLABS - 1, i, 0)),
                  pl.BlockSpec((None, t, w), lambda i: (N_SLABS - 1, jnp.maximum(i - 1, 0), 0)),
                  pl.BlockSpec((t, w), lambda i: (i, 0)),
                  pl.BlockSpec((len(POOL_WINDOWS), gw, gw), lambda i: (0, 0, 0)),
                  pl.BlockSpec((1, w), lambda i: (0, 0))],
        out_specs=[pl.BlockSpec((t, 2 * w), lambda i: (i, 0)), pl.BlockSpec((t, w), lambda i: (i, 0))],
        out_shape=[jax.ShapeDtypeStruct((s, 2 * w), BF16), jax.ShapeDtypeStruct((s, w), BF16)],
        compiler_params=_cparams(("parallel",)))(slabs, slabs, attn, pool_w, pool_scale)


def _pool_bwd(dap, d, pool_w, pool_scale, name):
    s, w = d.shape
    t = TOK_TILE
    nt = s // t
    gw = POOL_W // len(POOL_WINDOWS)

    def body(dy_ref, dyn_ref, d_ref, w_ref, sc_ref, du_ref, dw_ref, dsc_ref):
        i = pl.program_id(0)

        @pl.when(i == 0)
        def _():
            dw_ref[...] = jnp.zeros_like(dw_ref)
            dsc_ref[...] = jnp.zeros_like(dsc_ref)

        for gi, win in enumerate(POOL_WINDOWS):
            cols = slice(gi * gw, (gi + 1) * gw)
            wb = w_ref[gi].astype(BF16)
            scale = sc_ref[:, cols]
            dv = d_ref[:, cols]
            dy = dy_ref[:, cols].astype(F32)
            y = jnp.dot(dv, wb, preferred_element_type=F32)
            dsc_ref[:, cols] += _colsum8(dy * y)
            dyp = (dy * scale).astype(BF16)
            dw_ref[gi] += lax.dot_general(dv, dyp, TN, preferred_element_type=F32)
            dd = lax.dot_general(dyp, wb, NT, preferred_element_type=F32)
            dypn = (dyn_ref[:, cols].astype(F32) * scale).astype(BF16)
            ddn = lax.dot_general(dypn, wb, NT, preferred_element_type=F32)
            e = dd / _pool_count(i, t, win)
            en = jnp.where(i + 1 < nt, ddn / _pool_count(i + 1, t, win), 0.0)
            du_ref[:, cols] = (_window_sum_up(e, en, win) - dd).astype(BF16)

    return pl.pallas_call(
        body, name=name, grid=(nt,),
        in_specs=[pl.BlockSpec((t, w), lambda i: (i, 1)),
                  pl.BlockSpec((t, w), lambda i: (jnp.minimum(i + 1, nt - 1), 1)),
                  pl.BlockSpec((t, w), lambda i: (i, 0)),
                  pl.BlockSpec((len(POOL_WINDOWS), gw, gw), lambda i: (0, 0, 0)),
                  pl.BlockSpec((1, w), lambda i: (0, 0))],
        out_specs=[pl.BlockSpec((t, w), lambda i: (i, 0)),
                   pl.BlockSpec((len(POOL_WINDOWS), gw, gw), lambda i: (0, 0, 0)),
                   pl.BlockSpec((8, w), lambda i: (0, 0))],
        out_shape=[jax.ShapeDtypeStruct((s, w), BF16), jax.ShapeDtypeStruct((len(POOL_WINDOWS), gw, gw), F32),
                   jax.ShapeDtypeStruct((8, w), F32)],
        compiler_params=_cparams(("arbitrary",)))(dap, dap, d, pool_w, pool_scale)


CONV_TAPS = 4
CONV_COLS = 512


def _sigmoid(v):
    return 1.0 / (1.0 + jnp.exp(-v))


def _conv_pre(x, xprev, w_ref, b_ref):
    pre = b_ref[...] + w_ref[CONV_TAPS - 1:CONV_TAPS, :] * x
    for k in range(CONV_TAPS - 1):
        pre = pre + w_ref[k:k + 1, :] * _shift_down(x, xprev, CONV_TAPS - 1 - k)
    return pre


def _conv_fwd(xbc, conv_w, conv_b, name):
    s, c = xbc.shape
    t, tc = TOK_TILE, CONV_COLS

    def body(x_ref, xp_ref, w_ref, b_ref, o_ref):
        i = pl.program_id(0)
        x = x_ref[...].astype(F32)
        xp = jnp.where(i > 0, xp_ref[...].astype(F32), 0.0)
        pre = _conv_pre(x, xp, w_ref, b_ref)
        o_ref[...] = (pre * _sigmoid(pre)).astype(BF16)

    return pl.pallas_call(
        body, name=name, grid=(s // t, c // tc),
        in_specs=[pl.BlockSpec((t, tc), lambda i, j: (i, j)),
                  pl.BlockSpec((t, tc), lambda i, j: (jnp.maximum(i - 1, 0), j)),
                  pl.BlockSpec((CONV_TAPS, tc), lambda i, j: (0, j)), pl.BlockSpec((1, tc), lambda i, j: (0, j))],
        out_specs=pl.BlockSpec((t, tc), lambda i, j: (i, j)), out_shape=jax.ShapeDtypeStruct((s, c), BF16),
        compiler_params=_cparams(("parallel", "parallel")))(xbc, xbc, conv_w, conv_b)


def _conv_bwd(xbc, dact, conv_w, conv_b, name):
    s, c = xbc.shape
    t, tc = TOK_TILE, CONV_COLS
    nt = s // t

    def body(xp_ref, x_ref, xn_ref, da_ref, dan_ref, w_ref, b_ref, dx_ref, dw_ref, db_ref):
        i = pl.program_id(1)

        @pl.when(i == 0)
        def _():
            dw_ref[...] = jnp.zeros_like(dw_ref)
            db_ref[...] = jnp.zeros_like(db_ref)

        x = x_ref[...].astype(F32)
        xp = jnp.where(i > 0, xp_ref[...].astype(F32), 0.0)
        xn = xn_ref[...].astype(F32)

        def dsilu(pre):
            sg = _sigmoid(pre)
            return sg * (1.0 + pre * (1.0 - sg))

        dpre = da_ref[...] * dsilu(_conv_pre(x, xp, w_ref, b_ref))
        dpre_n = jnp.where(i + 1 < nt, dan_ref[...] * dsilu(_conv_pre(xn, x, w_ref, b_ref)), 0.0)
        dx = w_ref[CONV_TAPS - 1:CONV_TAPS, :] * dpre
        dw_ref[CONV_TAPS - 1] += _colsum8(dpre * x)
        for k in range(CONV_TAPS - 1):
            j = CONV_TAPS - 1 - k
            dx = dx + w_ref[k:k + 1, :] * _shift_up(dpre, dpre_n, j)
            dw_ref[k] += _colsum8(dpre * _shift_down(x, xp, j))
        dx_ref[...] = dx.astype(BF16)
        db_ref[...] += _colsum8(dpre)

    def xspec(shift):
        if shift < 0:
            return pl.BlockSpec((t, tc), lambda j, i: (jnp.maximum(i - 1, 0), j))
        if shift > 0:
            return pl.BlockSpec((t, tc), lambda j, i: (jnp.minimum(i + 1, nt - 1), j))
        return pl.BlockSpec((t, tc), lambda j, i: (i, j))

    return pl.pallas_call(
        body, name=name, grid=(c // tc, nt),
        in_specs=[xspec(-1), xspec(0), xspec(1), xspec(0), xspec(1),
                  pl.BlockSpec((CONV_TAPS, tc), lambda j, i: (0, j)), pl.BlockSpec((1, tc), lambda j, i: (0, j))],
        out_specs=[xspec(0), pl.BlockSpec((CONV_TAPS, 8, tc), lambda j, i: (0, 0, j)),
                   pl.BlockSpec((8, tc), lambda j, i: (0, j))],
        out_shape=[jax.ShapeDtypeStruct((s, c), BF16), jax.ShapeDtypeStruct((CONV_TAPS, 8, c), F32),
                   jax.ShapeDtypeStruct((8, c), F32)],
        compiler_params=_cparams(("parallel", "arbitrary")))(xbc, xbc, xbc, dact, dact, conv_w, conv_b)


GN_TILE = 256


def _gated_norm_fwd(y, z, g, name):
    s, c = y.shape
    t = GN_TILE

    def body(y_ref, z_ref, g_ref, o_ref):
        for gi in range(SSM_GROUPS):
            cols = slice(gi * SSM_GW, (gi + 1) * SSM_GW)
            zv = z_ref[:, cols].astype(F32)
            yf = y_ref[:, cols] * (zv * _sigmoid(zv))
            r = lax.rsqrt(jnp.mean(yf * yf, axis=-1, keepdims=True) + NORM_EPS)
            o_ref[:, cols] = (yf * r * g_ref[:, cols]).astype(BF16)

    tile = pl.BlockSpec((t, c), lambda i: (i, 0))
    return pl.pallas_call(
        body, name=name, grid=(s // t,), in_specs=[tile, tile, pl.BlockSpec((1, c), lambda i: (0, 0))],
        out_specs=tile, out_shape=jax.ShapeDtypeStruct((s, c), BF16),
        compiler_params=_cparams(("parallel",)))(y, z, g)


def _gated_norm_bwd(y, z, dout, g, name):
    s, c = y.shape
    t = GN_TILE

    def body(y_ref, z_ref, do_ref, g_ref, dy_ref, dz_ref, dg_ref):
        i = pl.program_id(0)

        @pl.when(i == 0)
        def _():
            dg_ref[...] = jnp.zeros_like(dg_ref)

        for gi in range(SSM_GROUPS):
            cols = slice(gi * SSM_GW, (gi + 1) * SSM_GW)
            zv = z_ref[:, cols].astype(F32)
            yv = y_ref[:, cols]
            sg = _sigmoid(zv)
            sz = zv * sg
            yf = yv * sz
            r = lax.rsqrt(jnp.mean(yf * yf, axis=-1, keepdims=True) + NORM_EPS)
            n = yf * r
            dout = do_ref[:, cols]
            dn = dout * g_ref[:, cols]
            dg_ref[:, cols] += _colsum8(dout * n)
            dyf = r * (dn - n * jnp.mean(dn * n, axis=-1, keepdims=True))
            dy_ref[:, cols] = dyf * sz
            dz_ref[:, cols] = (dyf * yv * (sg * (1.0 + zv * (1.0 - sg)))).astype(BF16)

    tile = pl.BlockSpec((t, c), lambda i: (i, 0))
    return pl.pallas_call(
        body, name=name, grid=(s // t,), in_specs=[tile, tile, tile, pl.BlockSpec((1, c), lambda i: (0, 0))],
        out_specs=[tile, tile, pl.BlockSpec((8, c), lambda i: (0, 0))],
        out_shape=[jax.ShapeDtypeStruct((s, c), F32), jax.ShapeDtypeStruct((s, c), BF16),
                   jax.ShapeDtypeStruct((8, c), F32)],
        compiler_params=_cparams(("arbitrary",)))(y, z, dout, g)


def _split_dot(x, e, dims, terms):
    r, acc = x, None
    for i in range(terms):
        p = r.astype(BF16)
        part = lax.dot_general(p, e, dims, preferred_element_type=F32)
        acc = part if acc is None else acc + part
        if i + 1 < terms:
            r = r - p.astype(F32)
    return acc


def _split_dot_r(e, x, dims, terms):
    r, acc = x, None
    for i in range(terms):
        p = r.astype(BF16)
        part = lax.dot_general(e, p, dims, preferred_element_type=F32)
        acc = part if acc is None else acc + part
        if i + 1 < terms:
            r = r - p.astype(F32)
    return acc


def _ssd_prep(dtr_ref, bias_ref, alog_ref):
    q = SSM_CHUNK
    dtr = dtr_ref[...] + bias_ref[...]
    dt = jnp.maximum(dtr, 0.0) + jnp.log(1.0 + jnp.exp(-jnp.abs(dtr)))
    a_neg = -jnp.exp(alog_ref[...])
    dta = dt * a_neg
    row = lax.broadcasted_iota(jnp.int32, (q, q), 0)
    col = lax.broadcasted_iota(jnp.int32, (q, q), 1)
    causal = row >= col
    a = _split_dot_r(causal.astype(BF16), dta, NN, 3)
    aq_row = jnp.sum(dta, axis=0, keepdims=True)
    aq_hb = _split_dot(dta, jnp.ones((q, LANES), BF16), TN, 3)
    return dict(dtr=dtr, dt=dt, a_neg=a_neg, a=a, a_t=jnp.transpose(a), aq_row=aq_row, aq_hb=aq_hb,
                ea=jnp.exp(a), fa=jnp.exp(aq_row - a), causal=causal)


def _ssd_group_mats():
    g = jnp.arange(SSM_GROUPS)[:, None, None]
    head = jnp.arange(LANES)[None, :, None]
    eg = (head == 8 * g + jnp.arange(SSM_GW)[None, None, :] // HEAD_DIM).astype(BF16)
    sel = (head == 8 * g + jnp.arange(8 * LANES)[None, None, :] // LANES).astype(BF16)
    mats = (eg, jnp.transpose(eg, (0, 2, 1)), sel)
    return mats, [pl.BlockSpec(m.shape, lambda c: (0, 0, 0)) for m in mats]


def _ssd_fwd(xbc, dt_raw, dt_bias, a_log, d_e, name):
    s = xbc.shape[0]
    q = SSM_CHUNK
    nc = s // q

    def body(x_ref, dtr_ref, bias_ref, alog_ref, de_ref, eg_ref, egt_ref, sel_ref, y_ref, hin_ref, state, at_ref):
        c = pl.program_id(0)

        @pl.when(c == 0)
        def _():
            state[...] = jnp.zeros_like(state)

        hin_ref[...] = state[...].astype(BF16)
        pr = _ssd_prep(dtr_ref, bias_ref, alog_ref)
        at_ref[...] = pr["a_t"]
        exp_aq_hb = jnp.exp(pr["aq_hb"])
        stack3 = jnp.concatenate([pr["dt"], pr["ea"], pr["fa"]], axis=0)
        mq = _head_masks(q)
        for g in range(SSM_GROUPS):
            eg, sel = eg_ref[g], sel_ref[g]
            cols = slice(g * SSM_GW, (g + 1) * SSM_GW)
            xg = x_ref[:, cols]
            bg = x_ref[:, SSM_INNER + g * SSM_STATE:SSM_INNER + (g + 1) * SSM_STATE]
            cg = x_ref[:, SSM_INNER + SSM_GROUPS * SSM_STATE + g * SSM_STATE:
                       SSM_INNER + SSM_GROUPS * SSM_STATE + (g + 1) * SSM_STATE]
            e3 = _split_dot(stack3, eg, NN, 1)
            dt_e, ea_e, fa_e = e3[:q], e3[q:2 * q], e3[2 * q:]
            xf = xg.astype(F32)
            xdt = xf * dt_e
            xdt_b = xdt.astype(BF16)
            cb = lax.dot_general(cg, bg, NT, preferred_element_type=F32)
            colb = _split_dot(pr["a"], sel, NN, 2)
            hg = state[cols, :]
            y_g = lax.dot_general(cg, hg.astype(BF16), NT, preferred_element_type=F32) * ea_e + de_ref[:, cols] * xf
            pairs = []
            for pp in range(4):
                xp = xdt_b[:, pp * LANES:(pp + 1) * LANES]
                yp = jnp.zeros((q, LANES), F32)
                for hh in range(2):
                    hi = 2 * pp + hh
                    diff = colb[:, hi * LANES:(hi + 1) * LANES] - at_ref[8 * g + hi:8 * g + hi + 1, :]
                    lmat = jnp.exp(jnp.where(pr["causal"], diff, NEG_BIG))
                    m_b = (cb * lmat).astype(BF16)
                    yp = yp + jnp.dot(m_b, jnp.where(mq[hh], xp, jnp.zeros_like(xp)), preferred_element_type=F32)
                pairs.append(yp)
            y_ref[:, cols] = y_g + jnp.concatenate(pairs, axis=1)
            s_g = lax.dot_general((xdt * fa_e).astype(BF16), bg, TN, preferred_element_type=F32)
            dec_g = _split_dot_r(eg, exp_aq_hb, TN, 2)
            state[cols, :] = dec_g * hg + s_g

    row128 = pl.BlockSpec((1, LANES), lambda c: (0, 0))
    mats, mat_specs = _ssd_group_mats()
    return pl.pallas_call(
        body, name=name, grid=(nc,),
        in_specs=[pl.BlockSpec((q, SSM_CONV_DIM), lambda c: (c, 0)), pl.BlockSpec((q, LANES), lambda c: (c, 0)),
                  row128, row128, pl.BlockSpec((1, SSM_INNER), lambda c: (0, 0)), *mat_specs],
        out_specs=[pl.BlockSpec((q, SSM_INNER), lambda c: (c, 0)),
                   pl.BlockSpec((None, SSM_INNER, SSM_STATE), lambda c: (c, 0, 0))],
        out_shape=[jax.ShapeDtypeStruct((s, SSM_INNER), F32), jax.ShapeDtypeStruct((nc, SSM_INNER, SSM_STATE), BF16)],
        scratch_shapes=[pltpu.VMEM((SSM_INNER, SSM_STATE), F32), pltpu.VMEM((LANES, q), F32)],
        compiler_params=_cparams(("arbitrary",)))(xbc, dt_raw, dt_bias, a_log, d_e, *mats)


def _ssd_bwd(xbc, dt_raw, dt_bias, a_log, d_e, hin, dy, name):
    s = xbc.shape[0]
    q = SSM_CHUNK
    nc = s // q

    def body(x_ref, dtr_ref, bias_ref, alog_ref, de_ref, hin_ref, dy_ref, eg_ref, egt_ref, sel_ref,
             dx_ref, ddt_ref, da_acc, db_acc, dd_acc, gst, at_ref):
        i = pl.program_id(0)

        @pl.when(i == 0)
        def _():
            gst[...] = jnp.zeros_like(gst)
            da_acc[...] = jnp.zeros_like(da_acc)
            db_acc[...] = jnp.zeros_like(db_acc)
            dd_acc[...] = jnp.zeros_like(dd_acc)

        pr = _ssd_prep(dtr_ref, bias_ref, alog_ref)
        at_ref[...] = pr["a_t"]
        exp_aq_hb = jnp.exp(pr["aq_hb"])
        stack3 = jnp.concatenate([pr["dt"], pr["ea"], pr["fa"]], axis=0)
        mq = _head_masks(q)
        lane_h = lax.broadcasted_iota(jnp.int32, (q, LANES), 1)
        sub_h = lax.broadcasted_iota(jnp.int32, (LANES, q), 0)
        da = jnp.zeros((q, LANES), F32)
        da_tn = jnp.zeros((LANES, q), F32)
        daq = jnp.zeros((1, LANES), F32)
        ddt = jnp.zeros((q, LANES), F32)
        for g in range(SSM_GROUPS):
            eg, eg_t, sel = eg_ref[g], egt_ref[g], sel_ref[g]
            cols = slice(g * SSM_GW, (g + 1) * SSM_GW)
            bcols = slice(SSM_INNER + g * SSM_STATE, SSM_INNER + (g + 1) * SSM_STATE)
            ccols = slice(SSM_INNER + SSM_GROUPS * SSM_STATE + g * SSM_STATE,
                          SSM_INNER + SSM_GROUPS * SSM_STATE + (g + 1) * SSM_STATE)
            xg, bg, cg = x_ref[:, cols], x_ref[:, bcols], x_ref[:, ccols]
            e3 = _split_dot(stack3, eg, NN, 1)
            dt_e, ea_e, fa_e = e3[:q], e3[q:2 * q], e3[2 * q:]
            xf = xg.astype(F32)
            xdt = xf * dt_e
            xdt_b = xdt.astype(BF16)
            xdtf = xdt * fa_e
            xdtf_b = xdtf.astype(BF16)
            cb = lax.dot_general(cg, bg, NT, preferred_element_type=F32)
            colb = _split_dot(pr["a"], sel, NN, 2)
            dyg = dy_ref[:, cols]
            dd_acc[:, cols] += _colsum8(dyg * xf)
            hg_b = hin_ref[cols, :]
            gg = gst[cols, :]
            gg_b = gg.astype(BF16)
            dye_b = (dyg * ea_e).astype(BF16)
            dc_g = jnp.dot(dye_b, hg_b, preferred_element_type=F32)
            dh_g = lax.dot_general(dye_b, cg, TN, preferred_element_type=F32)
            yoff = lax.dot_general(cg, hg_b, NT, preferred_element_type=F32) * ea_e
            da = da + _split_dot(dyg * yoff, eg_t, NN, 1)
            db_g = jnp.dot(xdtf_b, gg_b, preferred_element_type=F32)
            tmat = lax.dot_general(bg, gg_b, NT, preferred_element_type=F32)
            dxdt = fa_e * tmat
            qmat = _split_dot(xdtf * tmat, eg_t, NN, 1)
            da = da - qmat
            daq = daq + jnp.sum(qmat, axis=0, keepdims=True)
            gh = _split_dot(gg * hg_b.astype(F32), eg_t, TN, 1)
            daq = daq + jnp.sum(gh, axis=0, keepdims=True) * jnp.exp(pr["aq_row"])
            dcb = jnp.zeros((q, q), F32)
            pairs = []
            for pp in range(4):
                xp = xdt_b[:, pp * LANES:(pp + 1) * LANES]
                dyp = dyg[:, pp * LANES:(pp + 1) * LANES]
                dxp = jnp.zeros((q, LANES), F32)
                for hh in range(2):
                    hi = 2 * pp + hh
                    h = 8 * g + hi
                    diff = colb[:, hi * LANES:(hi + 1) * LANES] - at_ref[h:h + 1, :]
                    lmat = jnp.exp(jnp.where(pr["causal"], diff, NEG_BIG))
                    mmat = cb * lmat
                    dyh = jnp.where(mq[hh], dyp, 0.0).astype(BF16)
                    dm = lax.dot_general(dyh, xp, NT, preferred_element_type=F32)
                    dxp = dxp + lax.dot_general(mmat.astype(BF16), dyh, TN, preferred_element_type=F32)
                    wmat = dm * mmat
                    da = da + jnp.where(lane_h == h, jnp.sum(wmat, axis=-1, keepdims=True), 0.0)
                    da_tn = da_tn + jnp.where(sub_h == h, jnp.sum(wmat, axis=0, keepdims=True), 0.0)
                    dcb = dcb + dm * lmat
                pairs.append(dxp)
            dxdt = dxdt + jnp.concatenate(pairs, axis=1)
            dcb_b = dcb.astype(BF16)
            dc_g = dc_g + jnp.dot(dcb_b, bg, preferred_element_type=F32)
            db_g = db_g + lax.dot_general(dcb_b, cg, TN, preferred_element_type=F32)
            ddt = ddt + _split_dot(dxdt * xf, eg_t, NN, 1)
            dx_ref[:, cols] = dxdt * dt_e + de_ref[:, cols] * dyg
            dx_ref[:, bcols] = db_g
            dx_ref[:, ccols] = dc_g
            dec_g = _split_dot_r(eg, exp_aq_hb, TN, 2)
            gst[cols, :] = dh_g + dec_g * gg
        row = lax.broadcasted_iota(jnp.int32, (q, LANES), 0)
        da_all = da - jnp.transpose(da_tn) + jnp.where(row == q - 1, daq, 0.0)
        upper = jnp.logical_not(pr["causal"]) | (lax.broadcasted_iota(jnp.int32, (q, q), 0)
                                                 == lax.broadcasted_iota(jnp.int32, (q, q), 1))
        rcs = _split_dot_r(upper.astype(BF16), da_all, NN, 3)
        ddt_all = ddt + pr["a_neg"] * rcs
        da_acc[...] += _colsum8(pr["dt"] * rcs)
        ddtr = jnp.where(lane_h < SSM_HEADS, ddt_all * _sigmoid(pr["dtr"]), 0.0)
        ddt_ref[...] = ddtr
        db_acc[...] += _colsum8(ddtr)

    rev = lambda i: (nc - 1 - i, 0)
    row128 = pl.BlockSpec((1, LANES), lambda i: (0, 0))
    acc128 = pl.BlockSpec((8, LANES), lambda i: (0, 0))
    mats, mat_specs = _ssd_group_mats()
    return pl.pallas_call(
        body, name=name, grid=(nc,),
        in_specs=[pl.BlockSpec((q, SSM_CONV_DIM), rev), pl.BlockSpec((q, LANES), rev), row128, row128,
                  pl.BlockSpec((1, SSM_INNER), lambda i: (0, 0)),
                  pl.BlockSpec((None, SSM_INNER, SSM_STATE), lambda i: (nc - 1 - i, 0, 0)),
                  pl.BlockSpec((q, SSM_INNER), rev), *mat_specs],
        out_specs=[pl.BlockSpec((q, SSM_CONV_DIM), rev), pl.BlockSpec((q, LANES), rev), acc128, acc128,
                   pl.BlockSpec((8, SSM_INNER), lambda i: (0, 0))],
        out_shape=[jax.ShapeDtypeStruct((s, SSM_CONV_DIM), F32), jax.ShapeDtypeStruct((s, LANES), F32),
                   jax.ShapeDtypeStruct((8, LANES), F32), jax.ShapeDtypeStruct((8, LANES), F32),
                   jax.ShapeDtypeStruct((8, SSM_INNER), F32)],
        scratch_shapes=[pltpu.VMEM((SSM_INNER, SSM_STATE), F32), pltpu.VMEM((LANES, q), F32)],
        compiler_params=_cparams(("arbitrary",)))(xbc, dt_raw, dt_bias, a_log, d_e, hin, dy, *mats)


MM = (1024, 1024, 2048)
MM_TN = (1024, 1024, 4096)


def _relu2_epilogue(acc, ex, outs):
    r = jnp.maximum(acc, 0.0)
    outs[0][...] = (r * r).astype(BF16)
    outs[1][...] = r.astype(BF16)


def _relu2_bwd_epilogue(acc, ex, outs):
    outs[0][...] = (acc * (2.0 * ex[0][...].astype(F32))).astype(BF16)


def _add_epilogue(acc, ex, outs):
    outs[0][...] = acc + ex[0][...]


def _tile_spec(tm, tn):
    return pl.BlockSpec((tm, tn), lambda i, j, k: (i, j))


def _sum8(acc):
    return jnp.sum(acc, axis=0)


def _row(v):
    return v.reshape(1, -1)


def _ffn_fwd(xin, norm_g, sc, sh, gate, w, l):
    tm, tn, tk = MM
    h = _norm_mod_fwd(xin, norm_g, sc, sh, f"norm_ffn_fwd_{l}")
    w1 = w("ffn_w1", h)
    per = w1.shape[-1] // tn
    f, r = _mm_nn(f"mm_ffn1_{l}", h, w1, *MM, out_dtypes=(BF16, BF16), epilogue=_relu2_epilogue,
                  b_tile=lambda j, k: (j // per, l, 0, j % per), n=FFN_HIDDEN)
    xout, yf = _mm_resid(f"mm_ffn2_{l}", f, w("ffn_w2", f), xin, gate, *MM, b_tile=lambda j, k: (k, l, 0, j), n=D_MODEL)
    return xout, (h, f, r, yf)


def _ffn_bwd(dxo, xin, saved, norm_g, sc, gate, w, sink, l):
    h, f, r, yf = saved
    tm, tn, tk = MM
    w1, w2 = w("ffn_w1"), w("ffn_w2")
    per = w2.shape[-2] // tn
    dyf, dgate = _gate_bwd(dxo, yf, gate, f"gate_bwd_ffn_{l}")
    da = _mm_nt(f"mm_ffn2_bwd_{l}", dyf, w2, *MM, extras=(r,), extra_specs=(_tile_spec(tm, tn),),
                epilogue=_relu2_bwd_epilogue, b_tile=lambda j, k: (j // per, l, j % per, 0), n=FFN_HIDDEN)[0]
    tok = sink(f"ffn_w2_{l}", _mm_tn(f"mm_ffn2_dw_{l}", f, dyf, *MM_TN), "rows")
    dh = _mm_nt(f"mm_ffn1_bwd_{l}", da, w1, *MM, out_dtypes=(F32,), b_tile=lambda j, k: (k, l, j, 0), n=D_MODEL,
                extras=(tok.reshape(1, 1),), extra_specs=(ANY,))[0]
    tok = tok + sink(f"ffn_w1_{l}", _mm_tn(f"mm_ffn1_dw_{l}", h, da, *MM_TN, col_chunks=N_CHIPS), "slots")
    dxin, dsh, dsc, dng = _norm_mod_bwd(xin, dh, dxo, norm_g + tok, sc, f"norm_ffn_bwd_{l}")
    return dxin, (_sum8(dsh), _sum8(dsc), _sum8(dgate), _sum8(dng))


def _local_step(x, target, mod, w, p, sink):
    s, d = x.shape
    tm, tn, tk = MM
    mods = [[_row(mod[l, j * d:(j + 1) * d]) for j in range(6)] for l in range(2)]
    g = {}

    sh1, sc1, g1, sh2, sc2, g2 = mods[0]
    nm0, nf0 = _row(p["norm_mix"][0]), _row(p["norm_ffn"][0])
    h0 = _norm_mod_fwd(x, nm0, sc1, sh1, "norm_mix_fwd_0")
    slabs = _mm_core(
        "mm_even_in", h0, w("even_in", h0), dims=NN, grid=(s // tm, N_SLABS, 1),
        a_spec=pl.BlockSpec((tm, d), lambda i, j, k: (i, 0)), b_spec=pl.BlockSpec((d, ATTN_W), lambda i, j, k: (0, j)),
        out_shapes=[jax.ShapeDtypeStruct((N_SLABS, s, ATTN_W), BF16)],
        out_specs=[pl.BlockSpec((None, tm, ATTN_W), lambda i, j, k: (j, i, 0))], acc_shape=None)[0]
    outs, lses = zip(*[_attn_fwd(slabs, gi, f"attn_fwd_{gi}") for gi in range(3)])
    attn, lse_tot = _attn_merge(outs, lses, "attn_merge")
    pool_scale = _row(p["pool_scale"])
    ap, pool_d = _pool_fwd(slabs, attn, p["pool_w"], pool_scale, "pool_fwd")
    x1, y0 = _mm_resid("mm_even_out", ap, w("even_out", ap), x, g1, *MM)
    x2, ffn0 = _ffn_fwd(x1, nf0, sc2, sh2, g2, w, 0)

    sh1b, sc1b, g1b, sh2b, sc2b, g2b = mods[1]
    nm1, nf1 = _row(p["norm_mix"][1]), _row(p["norm_ffn"][1])
    h1 = _norm_mod_fwd(x2, nm1, sc1b, sh1b, "norm_mix_fwd_1")
    z = _mm_nn("mm_ssm_z", h1, w("ssm_z", h1), *MM)[0]
    xbc_raw = _mm_nn("mm_ssm_xbc", h1, w("ssm_xbc", h1), *MM)[0]
    dt_raw = _mm_nn("mm_ssm_dt", h1, w("ssm_dt", h1), tm, LANES, tk, out_dtypes=(F32,))[0]
    xbc = _conv_fwd(xbc_raw, p["conv_w"], _row(p["conv_b"]), "conv_fwd")
    y_ssd, hin = _ssd_fwd(xbc, dt_raw, p["dt_bias"], p["a_log"], p["d_e"], "ssd_fwd")
    ssm_norm = _row(p["ssm_norm"])
    yn = _gated_norm_fwd(y_ssd, z, ssm_norm, "gated_norm_fwd")
    x3, y1 = _mm_resid("mm_ssm_out", yn, w("ssm_out", yn), x2, g1b, *MM)
    x4, ffn1 = _ffn_fwd(x3, nf1, sc2b, sh2b, g2b, w, 1)

    dx4, sq, dfn = _final_loss(x4, _row(p["final_norm"]), target, "final_loss")
    loss_share = (0.5 / d) * jnp.sum(sq)
    g["final_norm"] = _sum8(dfn)

    dx3, (dsh2b, dsc2b, dg2b, dnf1) = _ffn_bwd(dx4, x3, ffn1, nf1, sc2b, g2b, w, sink, 1)
    dy1, dg1b = _gate_bwd(dx3, y1, g1b, "gate_bwd_ssm")
    dyn = _mm_nt("mm_ssm_out_bwd", dy1, w("ssm_out"), *MM, out_dtypes=(F32,))[0]
    tok = sink("ssm_out", _mm_tn("mm_ssm_out_dw", yn, dy1, *MM_TN), "rows")
    dy_ssd, dz, dgn = _gated_norm_bwd(y_ssd, z, dyn, ssm_norm + tok, "gated_norm_bwd")
    g["ssm_norm"] = _sum8(dgn)
    dact, ddt, da_acc, db_acc, dd_acc = _ssd_bwd(xbc, dt_raw, p["dt_bias"], p["a_log"], p["d_e"], hin, dy_ssd, "ssd_bwd")
    g["dt_bias"] = _sum8(db_acc)[:SSM_HEADS]
    g["a_log"] = _sum8(da_acc)[:SSM_HEADS] * (-jnp.exp(p["a_log"][0, :SSM_HEADS]))
    g["ssm_d"] = jnp.sum(_sum8(dd_acc).reshape(SSM_HEADS, HEAD_DIM), axis=1)
    dxbc, dcw, dcb = _conv_bwd(xbc_raw, dact, p["conv_w"], _row(p["conv_b"]), "conv_bwd")
    g["conv_w"] = jnp.sum(dcw, axis=1)
    g["conv_b"] = _sum8(dcb)
    add_spec = (_tile_spec(tm, tn),)
    dh1 = _mm_nt("mm_ssm_z_bwd", dz, w("ssm_z"), *MM, out_dtypes=(F32,))[0]
    dh1 = _mm_nt("mm_ssm_xbc_bwd", dxbc, w("ssm_xbc"), *MM, out_dtypes=(F32,), extras=(dh1,), extra_specs=add_spec,
                 epilogue=_add_epilogue)[0]
    dh1 = _mm_nt("mm_ssm_dt_bwd", ddt, w("ssm_dt"), *MM, out_dtypes=(F32,), extras=(dh1,), extra_specs=add_spec,
                 epilogue=_add_epilogue)[0]
    tok = sink("ssm_in", jnp.concatenate(
        [_mm_tn("mm_ssm_z_dw", h1, dz, *MM_TN), _mm_tn("mm_ssm_xbc_dw", h1, dxbc, *MM_TN),
         _mm_tn("mm_ssm_dt_dw", h1, ddt, MM_TN[0], LANES, MM_TN[2])[:, :SSM_HEADS]], axis=1), "cols")
    dx2, dsh1b, dsc1b, dnm1 = _norm_mod_bwd(x2, dh1, dx3, nm1 + tok, sc1b, "norm_mix_bwd_1")
    dmod1 = jnp.concatenate([_sum8(dsh1b), _sum8(dsc1b), _sum8(dg1b), dsh2b, dsc2b, dg2b])

    dx1, (dsh2, dsc2, dg2, dnf0) = _ffn_bwd(dx2, x1, ffn0, nf0, sc2, g2, w, sink, 0)
    dy0, dg1 = _gate_bwd(dx1, y0, g1, "gate_bwd_even")
    dap = _mm_nt("mm_even_out_bwd", dy0, w("even_out"), *MM)[0]
    tok = sink("even_out", _mm_tn("mm_even_out_dw", ap, dy0, *MM_TN), "cols")
    du, dpw, dpsc = _pool_bwd(dap, pool_d, p["pool_w"], pool_scale + tok, "pool_bwd")
    g["pool_w"] = dpw
    g["pool_scale"] = _sum8(dpsc)
    dqkv = [_attn_bwd(slabs, dap, ap, lse_tot, gi, f"attn_bwd_{gi}") for gi in range(3)]
    dproj = jnp.concatenate([dqkv[gi][t] for t in range(3) for gi in range(3)] + [du], axis=1)
    dh0 = _mm_nt("mm_even_in_bwd", dproj, w("even_in"), tm, tn, N_SLABS * ATTN_W // 2, out_dtypes=(F32,))[0]
    tok = sink("even_in", _mm_tn("mm_even_in_dw", dproj, h0, *MM_TN), "cols_t")
    gx, dsh1, dsc1, dnm0 = _norm_mod_bwd(x, dh0, dx1, nm0 + tok, sc1, "norm_mix_bwd_0")
    dmod0 = jnp.concatenate([_sum8(dsh1), _sum8(dsc1), _sum8(dg1), dsh2, dsc2, dg2])

    g["norm_mix"] = jnp.stack([_sum8(dnm0), _sum8(dnm1)])
    g["norm_ffn"] = jnp.stack([dnf0, dnf1])
    return loss_share, gx, g, jnp.stack([dmod0, dmod1])


SMALL_COLS = 512
SMALL_ORDER = ("dmod", "norm_mix", "norm_ffn", "pool_w", "pool_scale", "conv_w", "conv_b", "dt_bias", "a_log",
               "ssm_d", "ssm_norm", "final_norm")


def _cols_to_full(gathered):
    n, k, ns = gathered.shape
    return jnp.transpose(gathered, (1, 0, 2)).reshape(k, n * ns)


def _full_to_cols(full):
    k, n4 = full.shape
    return jnp.transpose(full.reshape(k, N_CHIPS, n4 // N_CHIPS), (1, 0, 2))


def _pack(parts, cols):
    flat = jnp.concatenate([v.reshape(-1) for v in parts])
    rows = -(-flat.shape[0] // (cols * 8)) * 8
    return jnp.pad(flat, (0, rows * cols - flat.shape[0])).reshape(rows, cols)


def _unpack(packed, shapes):
    flat, out, at = packed.reshape(-1), [], 0
    for shp in shapes:
        n = math.prod(shp)
        out.append(flat[at:at + n].reshape(shp))
        at += n
    return out


def kernel(x, c, ada_w, ada_b, norm_mix, norm_ffn, ffn_w1, ffn_w2, even_w_in, pool_w, pool_scale, even_w_out, ssm_w_in, ssm_conv_w, ssm_conv_b, ssm_dt_bias, ssm_a_log, ssm_d, ssm_norm, ssm_w_out, final_norm, loss_target, m_ada_w, m_ada_b, m_norm_mix, m_norm_ffn, m_ffn_w1, m_ffn_w2, m_even_w_in, m_pool_w, m_pool_scale, m_even_w_out, m_ssm_w_in, m_ssm_conv_w, m_ssm_conv_b, m_ssm_dt_bias, m_ssm_a_log, m_ssm_d, m_ssm_norm, m_ssm_w_out, m_final_norm, v_ada_w, v_ada_b, v_norm_mix, v_norm_ffn, v_ffn_w1, v_ffn_w2, v_even_w_in, v_pool_w, v_pool_scale, v_even_w_out, v_ssm_w_in, v_ssm_conv_w, v_ssm_conv_b, v_ssm_dt_bias, v_ssm_a_log, v_ssm_d, v_ssm_norm, v_ssm_w_out, v_final_norm):
    names = ("ada_w", "ada_b", "norm_mix", "norm_ffn", "ffn_w1", "ffn_w2", "even_w_in", "pool_w", "pool_scale",
             "even_w_out", "ssm_w_in", "ssm_conv_w", "ssm_conv_b", "ssm_dt_bias", "ssm_a_log", "ssm_d", "ssm_norm",
             "ssm_w_out", "final_norm")
    wts = dict(zip(names, (ada_w, ada_b, norm_mix, norm_ffn, ffn_w1, ffn_w2, even_w_in, pool_w, pool_scale, even_w_out,
                           ssm_w_in, ssm_conv_w, ssm_conv_b, ssm_dt_bias, ssm_a_log, ssm_d, ssm_norm, ssm_w_out, final_norm)))
    m_in = dict(zip(names, (m_ada_w, m_ada_b, m_norm_mix, m_norm_ffn, m_ffn_w1, m_ffn_w2, m_even_w_in, m_pool_w, m_pool_scale,
                            m_even_w_out, m_ssm_w_in, m_ssm_conv_w, m_ssm_conv_b, m_ssm_dt_bias, m_ssm_a_log, m_ssm_d,
                            m_ssm_norm, m_ssm_w_out, m_final_norm)))
    v_in = dict(zip(names, (v_ada_w, v_ada_b, v_norm_mix, v_norm_ffn, v_ffn_w1, v_ffn_w2, v_even_w_in, v_pool_w, v_pool_scale,
                            v_even_w_out, v_ssm_w_in, v_ssm_conv_w, v_ssm_conv_b, v_ssm_dt_bias, v_ssm_a_log, v_ssm_d,
                            v_ssm_norm, v_ssm_w_out, v_final_norm)))
    d = D_MODEL
    s = x.shape[1]
    ix, iy, ic = _place()
    chip = 2 * ix + iy
    example = 4 * ix + 2 * iy + ic

    c_all = _allgather8(c.reshape(8, d // 8), "gather_c").reshape(N_DEV, d)
    cond = c_all * jax.nn.sigmoid(c_all)
    cond16 = jnp.pad(cond, ((0, 8), (0, 0)))
    ada_cols = ada_w.shape[2]
    bias_shard = lax.dynamic_slice_in_dim(ada_b, chip * ada_cols, ada_cols, axis=1)
    mod_parts = [
        _mm_nn(f"mm_ada_{l}", cond16, ada_w, 16, 512, d, out_dtypes=(F32,), extras=(_row(bias_shard[l]),),
               extra_specs=(pl.BlockSpec((1, 512), lambda i, j, k: (0, j)),), epilogue=_add_epilogue,
               b_tile=lambda j, k, l=l: (l, k, j), n=ada_cols)[0][:8]
        for l in range(2)]
    mod_all = _allgather8(jnp.concatenate(mod_parts, axis=0), "gather_mod").reshape(N_CHIPS, 2, 2, 8, ada_cols)
    mod_mine = lax.dynamic_index_in_dim(mod_all[:, 0], example, axis=2, keepdims=False)
    mod = jnp.transpose(mod_mine, (1, 0, 2)).reshape(2, N_CHIPS * ada_cols)

    cw, nw = ssm_conv_w.shape[2], ssm_norm.shape[1]
    sm = jnp.concatenate([ssm_conv_w[0].reshape(-1), ssm_conv_b.reshape(-1), jnp.pad(ssm_norm[0], (0, cw - nw)),
                          jnp.zeros((2 * cw,), F32)]).reshape(8, cw)
    sm_all = _allgather8(sm, "gather_ssm_small").reshape(N_CHIPS, 2, 8, cw)[:, 0]

    first_shard, sm_all, mod = lax.optimization_barrier((even_w_in, sm_all, mod))
    gathered = _GatheredWeights({"even_in": first_shard, "even_out": even_w_out, "ffn_w1": ffn_w1, "ffn_w2": ffn_w2,
                                 "ssm_in": ssm_w_in, "ssm_out": ssm_w_out})
    mod = mod + gathered.token
    n_xbc = SSM_INNER + SSM_CONV_DIM
    cache = {}

    def full_weight(key, after=None):
        if key in cache:
            return cache[key]
        if key in ("ffn_w1", "ffn_w2"):
            cache[key] = gathered.take(key, after)
        elif key in ("even_in", "even_out"):
            cache[key] = _cols_to_full(gathered.take(key, after)[:, 0])
        elif key == "ssm_out":
            cache[key] = gathered.take(key, after)[:, 0].reshape(SSM_INNER, d)
        else:
            whole = _cols_to_full(gathered.take("ssm_in", after)[:, 0])
            cache["ssm_z"] = whole[:, :SSM_INNER]
            cache["ssm_xbc"] = whole[:, SSM_INNER:n_xbc]
            cache["ssm_dt"] = jnp.pad(whole[:, n_xbc:], ((0, 0), (0, LANES - SSM_HEADS)))
        return cache[key]

    exchange = _GradientExchange()

    def sink(name, grad, layout):
        if layout == "slots":
            slots = grad
        elif layout == "rows":
            slots = grad.reshape(N_CHIPS, -1, grad.shape[-1])
        elif layout == "cols":
            slots = _full_to_cols(grad)
        else:
            slots = jnp.transpose(grad.reshape(N_CHIPS, -1, grad.shape[-1]), (0, 2, 1))
        return exchange.put(name, slots, defer=(name != "even_in"))

    pad_h = ((0, 0), (0, LANES - SSM_HEADS))
    small = {
        "norm_mix": norm_mix, "norm_ffn": norm_ffn, "pool_w": pool_w[0], "pool_scale": pool_scale[0],
        "final_norm": final_norm,
        "conv_w": jnp.transpose(sm_all[:, :CONV_TAPS], (1, 0, 2)).reshape(CONV_TAPS, N_CHIPS * cw),
        "conv_b": sm_all[:, CONV_TAPS].reshape(N_CHIPS * cw),
        "ssm_norm": sm_all[:, CONV_TAPS + 1, :nw].reshape(N_CHIPS * nw),
        "dt_bias": jnp.pad(ssm_dt_bias, pad_h), "a_log": jnp.pad(ssm_a_log, pad_h),
        "d_e": jnp.repeat(ssm_d[0], HEAD_DIM).reshape(1, SSM_INNER),
    }

    loss_share, gx, g, dmod = _local_step(x[0], loss_target[0], mod, full_weight, small, sink)
    loss = lax.psum(loss_share, ("x", "y", "c"))
    reduced = exchange.finish(gx)

    g["dmod"] = dmod
    small_shapes = [g[k].shape for k in SMALL_ORDER]
    packed = _pack([g[k] for k in SMALL_ORDER], SMALL_COLS)
    every = _allgather8(packed, "gather_small").reshape(N_DEV, *packed.shape)
    summed = dict(zip(SMALL_ORDER, _unpack(_sum_devices(every, "sum_small"), small_shapes)))
    dmod_all = every.reshape(N_DEV, -1)[:, :2 * 6 * d].reshape(N_DEV, 2, 6 * d)

    grads = {}
    dmod_shard = lax.dynamic_slice_in_dim(dmod_all, chip * ada_cols, ada_cols, axis=2)
    grads["ada_w"] = jnp.stack([
        _mm_tn(f"mm_ada_dw_{l}", cond16, jnp.pad(dmod_shard[:, l], ((0, 8), (0, 0))), 1024, 512, 16, out_dtype=F32)
        for l in range(2)])
    grads["ada_b"] = summed["dmod"]
    grads["norm_mix"] = summed["norm_mix"]
    grads["norm_ffn"] = summed["norm_ffn"]
    grads["pool_w"] = summed["pool_w"][None]
    grads["pool_scale"] = summed["pool_scale"][None]
    grads["ssm_conv_w"] =lax.dynamic_slice_in_dim(summed["conv_w"], chip * cw, cw, axis=1)[None]
    grads["ssm_conv_b"] = lax.dynamic_slice_in_dim(summed["conv_b"], chip * cw, cw, axis=0)[None]
    grads["ssm_dt_bias"] = summed["dt_bias"][None]
    grads["ssm_a_log"] = summed["a_log"][None]
    grads["ssm_d"] = summed["ssm_d"][None]
    grads["ssm_norm"] =lax.dynamic_slice_in_dim(summed["ssm_norm"], chip * nw, nw, axis=0)[None]
    grads["final_norm"] = summed["final_norm"]

    grads["ffn_w1"] = jnp.stack([reduced["ffn_w1_0"], reduced["ffn_w1_1"]])
    grads["ffn_w2"] = jnp.stack([reduced["ffn_w2_0"], reduced["ffn_w2_1"]])
    grads["even_w_in"] = reduced["even_in"][None]
    grads["even_w_out"] = reduced["even_out"][None]
    grads["ssm_w_in"] = reduced["ssm_in"][None]
    grads["ssm_w_out"] = reduced["ssm_out"][None]

    big = ("ada_w", "ffn_w1", "ffn_w2", "even_w_in", "even_w_out", "ssm_w_in", "ssm_w_out")
    delta, new_m, new_v = {}, {}, {}
    for k in big:
        shp = wts[k].shape
        two_d = lambda t: t.reshape(-1, shp[-1])
        res = _adamw(two_d(wts[k]), two_d(grads[k]), two_d(m_in[k]), two_d(v_in[k]), f"adamw_{k}")
        delta[k], new_m[k], new_v[k] = [t.reshape(shp) for t in res]
    little = [k for k in names if k not in big]
    shapes = [wts[k].shape for k in little]
    res = _adamw(*[_pack([src[k] for k in little], LANES) for src in (wts, grads, m_in, v_in)], "adamw_small")
    for out, packed_out in zip((delta, new_m, new_v), res):
        out.update(zip(little, _unpack(packed_out, shapes)))

    return (loss, gx[None], *[grads[k] for k in names], *[delta[k] for k in names],
            *[new_m[k] for k in names], *[new_v[k] for k in names])
```

```python
import functools
import math

import jax
import jax.numpy as jnp
from jax import lax
from jax.experimental import pallas as pl
from jax.experimental.pallas import tpu as pltpu

F32 = jnp.float32
BF16 = jnp.bfloat16
MESH = pl.DeviceIdType.MESH
ANY = pl.BlockSpec(memory_space=pl.ANY)
VMEM_FULL = pl.BlockSpec(memory_space=pltpu.VMEM)

NORM_EPS = 1e-6
N_CHIPS = 4
N_DEV = 8
LANES = 128
VMEM_LIMIT = 56 << 20

D_MODEL = 2048
ATTN_GROUPS = ((128, 1), (512, 4), (2048, 16))
ATTN_BLOCK = 128
ATTN_W = 512
HEAD_DIM = 64
POOL_WINDOWS = (2, 4, 8, 16)
POOL_W = 512
N_SLABS = 10
SSM_INNER = 4096
SSM_HEADS = 64
SSM_GROUPS = 8
SSM_STATE = 128
SSM_CHUNK = 128
SSM_CONV_DIM = 6144
SSM_GW = SSM_INNER // SSM_GROUPS
FFN_HIDDEN = 8192

ADAM_LR, ADAM_B1, ADAM_B2, ADAM_EPS, ADAM_WD, ADAM_STEP = 0.001, 0.9, 0.999, 1e-08, 0.01, 10


def _cparams(sem=None):
    return pltpu.CompilerParams(dimension_semantics=sem, vmem_limit_bytes=VMEM_LIMIT)


def _place():
    return lax.axis_index("x"), lax.axis_index("y"), lax.axis_index("c")


def _chip_index():
    return 2 * lax.axis_index("x") + lax.axis_index("y")


def _allgather8(v, name):
    m_per, n = v.shape

    def body(x_ref, out_ref, send_sems, recv_sems, local_sem):
        x, y, c = _place()
        me, sibling = (x, y, c), (x, y, 1 - c)
        chips = [(1 - x, y), (x, 1 - y), (1 - x, 1 - y)]

        def rows(px, py, pc):
            return out_ref.at[pl.ds((4 * px + 2 * py + pc) * m_per, m_per), :]

        def copy(k, block, to, src=None):
            return pltpu.make_async_remote_copy(
                src_ref=rows(*block) if src is None else src, dst_ref=rows(*block),
                send_sem=send_sems.at[k], recv_sem=recv_sems.at[k], device_id=to, device_id_type=MESH)

        mine = pltpu.make_async_copy(x_ref, rows(*me), local_sem)
        mine.start()
        first = [copy(0, me, sibling, src=x_ref)]
        first += [copy(1 + j, me, (*chip, c), src=x_ref) for j, chip in enumerate(chips)]
        for cp in first:
            cp.start()
        passed = [copy(4 + j, (*chip, c), sibling) for j, chip in enumerate(chips)]
        for j, chip in enumerate(chips):
            copy(1 + j, (*chip, c), me).wait_recv()
            passed[j].start()
        copy(0, sibling, me).wait_recv()
        for j, chip in enumerate(chips):
            copy(4 + j, (*chip, 1 - c), me).wait_recv()
        for cp in first + passed:
            cp.wait_send()
        mine.wait()

    return pl.pallas_call(
        body, name=name,
        out_shape=jax.ShapeDtypeStruct((N_DEV * m_per, n), v.dtype),
        in_specs=[VMEM_FULL], out_specs=VMEM_FULL,
        scratch_shapes=[pltpu.SemaphoreType.DMA((7,)), pltpu.SemaphoreType.DMA((7,)), pltpu.SemaphoreType.DMA],
    )(v)


def _sibling_send_other_half(g, name):
    n, r, ccols = g.shape
    h = r // 2

    def body(g_ref, out_ref, send_sem, recv_sem):
        x, y, c = _place()
        cp = pltpu.make_async_remote_copy(
            src_ref=g_ref.at[:, pl.ds((1 - c) * h, h), :], dst_ref=out_ref, send_sem=send_sem, recv_sem=recv_sem,
            device_id=(x, y, 1 - c), device_id_type=MESH)
        cp.start()
        cp.wait()

    return pl.pallas_call(
        body, name=name, out_shape=jax.ShapeDtypeStruct((n, h, ccols), g.dtype),
        in_specs=[ANY], out_specs=ANY,
        scratch_shapes=[pltpu.SemaphoreType.DMA, pltpu.SemaphoreType.DMA],
    )(g)


def _sibling_complete(full, name):
    r, ccols = full.shape
    h = r // 2

    def body(in_ref, out_ref, send_sem, recv_sem):
        x, y, c = _place()

        def copy(rows):
            return pltpu.make_async_remote_copy(
                src_ref=in_ref.at[pl.ds(c * h, h), :], dst_ref=out_ref.at[rows, :], send_sem=send_sem,
                recv_sem=recv_sem, device_id=(x, y, 1 - c), device_id_type=MESH)

        mine = copy(pl.ds(c * h, h))
        mine.start()
        copy(pl.ds((1 - c) * h, h)).wait_recv()
        mine.wait_send()

    return pl.pallas_call(
        body, name=name, out_shape=jax.ShapeDtypeStruct(full.shape, full.dtype), in_specs=[ANY], out_specs=ANY,
        input_output_aliases={0: 0}, scratch_shapes=[pltpu.SemaphoreType.DMA, pltpu.SemaphoreType.DMA],
    )(full)


HBM_SPEC = pl.BlockSpec(memory_space=pltpu.HBM)
SEM_SPEC = pl.BlockSpec(memory_space=pltpu.SEMAPHORE)
N_PEERS = 3


def _split_params():
    return pltpu.CompilerParams(has_side_effects=pltpu.SideEffectType.DATAFLOW_SIDE_EFFECTING)


def _n_peers(mode):
    return 1 if mode == "pair" else N_PEERS


def _chip_exchange(mode, src_ref, land_ref, send_sems, recv_sems):
    x, y, c = _place()
    k_me = 2 * x + y
    if mode == "pair":
        h = src_ref.shape[1] // 2
        return [tuple(
            pltpu.make_async_remote_copy(src_ref=src_ref.at[:, pl.ds((1 - c) * h, h), :], dst_ref=land_ref,
                                         send_sem=send_sems[0], recv_sem=recv_sems[0], device_id=(x, y, 1 - c),
                                         device_id_type=MESH) for _ in range(2))]
    pairs = []
    for j, chip in enumerate([(1 - x, y), (x, 1 - y), (1 - x, 1 - y)]):
        k_j = 2 * chip[0] + chip[1]
        if mode == "halves":
            h = src_ref.shape[0] // 2
            rows = pl.ds(c * h, h)
            src, there, here = src_ref.at[rows, :], land_ref.at[k_me, rows, :], land_ref.at[k_j, rows, :]
        else:
            src, there, here = src_ref.at[k_j], land_ref.at[k_me], land_ref.at[k_j]
        pairs.append(tuple(
            pltpu.make_async_remote_copy(src_ref=src, dst_ref=dst, send_sem=send_sems[j], recv_sem=recv_sems[j],
                                         device_id=(*chip, c), device_id_type=MESH) for dst in (there, here)))
    return pairs


def _exchange_start(src, land_shape, mode, name):
    n = _n_peers(mode)

    def body(src_ref, land_ref, *rest):
        sems, token = rest[:2 * n], rest[2 * n + 2]
        for start, _ in _chip_exchange(mode, src_ref, land_ref, sems[:n], sems[n:]):
            start.start()
        token[...] = jnp.zeros_like(token)

    sem = pltpu.SemaphoreType.DMA(())
    res = pl.pallas_call(
        body, name=name,
        out_shape=(sem,) * (2 * n) + (pltpu.HBM(src.shape, src.dtype), pltpu.HBM(land_shape, src.dtype),
                                     jax.ShapeDtypeStruct((8, LANES), F32)),
        in_specs=(HBM_SPEC, HBM_SPEC), out_specs=(SEM_SPEC,) * (2 * n) + (HBM_SPEC, HBM_SPEC, VMEM_FULL),
        input_output_aliases={0: 2 * n, 1: 2 * n + 1}, compiler_params=_split_params(),
    )(pltpu.with_memory_space_constraint(src, pltpu.HBM),
      pltpu.with_memory_space_constraint(lax.empty(land_shape, src.dtype), pltpu.HBM))
    return res[:2 * n], res[2 * n], res[2 * n + 1], res[2 * n + 2]


def _exchange_wait(sems, src_thru, land_thru, after, mode, name):
    n = _n_peers(mode)

    def body(src_ref, land_ref, *rest):
        sem_refs = rest[:2 * n]
        waits = [w for _, w in _chip_exchange(mode, src_ref, land_ref, sem_refs[:n], sem_refs[n:])]
        for w in waits:
            w.wait_send()
        for w in waits:
            w.wait_recv()

    return pl.pallas_call(
        body, name=name,
        out_shape=(pltpu.HBM(src_thru.shape, src_thru.dtype), pltpu.HBM(land_thru.shape, land_thru.dtype)),
        in_specs=(HBM_SPEC, HBM_SPEC) + (SEM_SPEC,) * (2 * n) + (ANY,), out_specs=(HBM_SPEC, HBM_SPEC),
        input_output_aliases={0: 0, 1: 1}, compiler_params=_split_params(),
    )(src_thru, land_thru, *sems, after)


def _sibling_fill(land, name):
    n, r, ccols = land.shape
    h = r // 2

    def body(land_ref, out_ref, send_sems, recv_sems):
        x, y, c = _place()
        slots = [2 * (1 - x) + y, 2 * x + (1 - y), 2 * (1 - x) + (1 - y)]

        def copy(j, rows):
            return pltpu.make_async_remote_copy(
                src_ref=land_ref.at[slots[j], pl.ds(c * h, h), :], dst_ref=out_ref.at[slots[j], rows, :],
                send_sem=send_sems.at[j], recv_sem=recv_sems.at[j], device_id=(x, y, 1 - c), device_id_type=MESH)

        sends = [copy(j, pl.ds(c * h, h)) for j in range(N_PEERS)]
        for cp in sends:
            cp.start()
        for j in range(N_PEERS):
            copy(j, pl.ds((1 - c) * h, h)).wait_recv()
        for cp in sends:
            cp.wait_send()

    return pl.pallas_call(
        body, name=name, out_shape=jax.ShapeDtypeStruct(land.shape, land.dtype), in_specs=[ANY], out_specs=ANY,
        input_output_aliases={0: 0},
        scratch_shapes=[pltpu.SemaphoreType.DMA((N_PEERS,)), pltpu.SemaphoreType.DMA((N_PEERS,))],
    )(land)


class _GatheredWeights:
    def __init__(self, shards):
        self._pending, self._done = {}, {}
        token = jnp.zeros((), BF16)
        for name, shard in shards.items():
            flat = shard.reshape(-1, shard.shape[-1]).astype(BF16) + token
            sems, thru, land, tok = _exchange_start(flat, (N_CHIPS, *flat.shape), "halves", f"gather_start_{name}")
            self._pending[name] = (sems, thru, land, shard.shape)
            token = tok[0, 0].astype(BF16)
        self.token = tok[0, 0]

    def take(self, name, after):
        if name not in self._done:
            sems, thru, land, shape = self._pending.pop(name)
            flat, land = _exchange_wait(sems, thru, land, after, "halves", f"gather_wait_{name}")
            land = _sibling_fill(land, f"gather_fill_{name}")
            land = lax.dynamic_update_index_in_dim(land, flat, _chip_index(), 0)
            self._done[name] = land.reshape(N_CHIPS, *shape)
        return self._done[name]


class _GradientExchange:
    def __init__(self):
        self._pair, self._pending = None, []

    def _start_all_to_all(self, name, slots, recv):
        part = _add_own_half(slots, recv, f"rs_add2_{name}")
        sems, thru, land, tok = _exchange_start(part, part.shape, "slots", f"rs_start_{name}")
        self._pending.append((name, sems, thru, land))
        return tok[0, 0]

    def _advance(self, after):
        if self._pair is None:
            return jnp.zeros((), F32)
        name, sems, thru, land = self._pair
        self._pair = None
        slots, recv = _exchange_wait(sems, thru, land, after, "pair", f"rs_pair_wait_{name}")
        return self._start_all_to_all(name, slots, recv)

    def put(self, name, slots, defer=True):
        tok = self._advance(slots)
        n, r, ccols = slots.shape
        if defer:
            sems, thru, land, ptok = _exchange_start(slots, (n, r // 2, ccols), "pair", f"rs_pair_start_{name}")
            self._pair = (name, sems, thru, land)
            return tok + ptok[0, 0]
        recv = _sibling_send_other_half(slots, f"rs_pair_{name}")
        return tok + self._start_all_to_all(name, slots, recv)

    def finish(self, after):
        self._advance(after)
        out = {}
        k_me = _chip_index()
        for name, sems, thru, land in self._pending:
            part, land = _exchange_wait(sems, thru, land, after, "slots", f"rs_wait_{name}")
            got = lax.dynamic_update_index_in_dim(land, lax.dynamic_index_in_dim(part, k_me, 0, keepdims=False), k_me, 0)
            out[name] = _sibling_complete(_sum_slots(got, f"rs_add4_{name}"), f"rs_fin_{name}")
        return out


def _row_tile(rows, cols, itemsize, budget=2 << 20):
    t = rows
    while t % 2 == 0 and t * cols * itemsize > budget and (t // 2) % 16 == 0:
        t //= 2
    return t


def _add_own_half(g, recv, name):
    n, r, ccols = g.shape
    h = r // 2
    t = _row_tile(h, ccols, 4)
    nt = h // t
    c_idx = lax.axis_index("c").astype(jnp.int32).reshape(1)

    def body(c_ref, g_ref, r_ref, o_ref):
        o_ref[...] = (g_ref[...].astype(F32) + r_ref[...].astype(F32)).astype(o_ref.dtype)

    grid_spec = pltpu.PrefetchScalarGridSpec(
        num_scalar_prefetch=1, grid=(n, nt),
        in_specs=[pl.BlockSpec((None, t, ccols), lambda j, i, c_ref: (j, c_ref[0] * nt + i, 0)),
                  pl.BlockSpec((None, t, ccols), lambda j, i, c_ref: (j, i, 0))],
        out_specs=pl.BlockSpec((None, t, ccols), lambda j, i, c_ref: (j, i, 0)))
    return pl.pallas_call(body, name=name, grid_spec=grid_spec,
                          out_shape=jax.ShapeDtypeStruct((n, h, ccols), BF16),
                          compiler_params=_cparams(("parallel", "parallel")))(c_idx, g, recv)


def _sum_slots(q, name):
    n, h, ccols = q.shape
    t = _row_tile(h, ccols, 4)
    nt = h // t
    c_idx = lax.axis_index("c").astype(jnp.int32).reshape(1)

    def body(c_ref, q_ref, o_ref):
        acc = q_ref[0].astype(F32)
        for j in range(1, n):
            acc = acc + q_ref[j].astype(F32)
        o_ref[...] = acc

    grid_spec = pltpu.PrefetchScalarGridSpec(
        num_scalar_prefetch=1, grid=(nt,),
        in_specs=[pl.BlockSpec((n, t, ccols), lambda i, c_ref: (0, i, 0))],
        out_specs=pl.BlockSpec((t, ccols), lambda i, c_ref: (c_ref[0] * nt + i, 0)))
    return pl.pallas_call(body, name=name, grid_spec=grid_spec, out_shape=jax.ShapeDtypeStruct((2 * h, ccols), F32),
                          compiler_params=_cparams(("parallel",)))(c_idx, q)


def _sum_devices(v, name):
    n, r, ccols = v.shape
    t = 8
    while r % (t * 2) == 0 and t * 2 * ccols * 4 * n <= (8 << 20):
        t *= 2

    def body(v_ref, o_ref):
        acc = v_ref[0]
        for j in range(1, n):
            acc = acc + v_ref[j]
        o_ref[...] = acc

    return pl.pallas_call(
        body, name=name, grid=(r // t,),
        in_specs=[pl.BlockSpec((n, t, ccols), lambda i: (0, i, 0))],
        out_specs=pl.BlockSpec((t, ccols), lambda i: (i, 0)),
        out_shape=jax.ShapeDtypeStruct((r, ccols), F32), compiler_params=_cparams(("parallel",)))(v)


def _adamw(w, g, m, v, name):
    r, ccols = w.shape
    t = _row_tile(r, ccols, 4, budget=1 << 20)
    c1 = 1.0 / (1.0 - ADAM_B1 ** ADAM_STEP)
    c2 = 1.0 / (1.0 - ADAM_B2 ** ADAM_STEP)

    def body(w_ref, g_ref, m_ref, v_ref, d_ref, nm_ref, nv_ref):
        gg = g_ref[...]
        nm = ADAM_B1 * m_ref[...] + (1.0 - ADAM_B1) * gg
        nv = ADAM_B2 * v_ref[...] + (1.0 - ADAM_B2) * (gg * gg)
        d_ref[...] = -ADAM_LR * ((nm * c1) / (jnp.sqrt(nv * c2) + ADAM_EPS) + ADAM_WD * w_ref[...])
        nm_ref[...] = nm
        nv_ref[...] = nv

    spec = pl.BlockSpec((t, ccols), lambda i: (i, 0))
    sds = jax.ShapeDtypeStruct((r, ccols), F32)
    return pl.pallas_call(body, name=name, grid=(r // t,), in_specs=[spec] * 4, out_specs=[spec] * 3,
                          out_shape=[sds] * 3, compiler_params=_cparams(("parallel",)))(w, g, m, v)


NN = (((1,), (0,)), ((), ()))
NT = (((1,), (1,)), ((), ()))
TN = (((0,), (0,)), ((), ()))


def _mm_core(name, a, b, *, dims, grid, a_spec, b_spec, out_shapes, out_specs, acc_shape,
             extras=(), extra_specs=(), epilogue=None):
    nk = grid[2]
    n_ex, n_out = len(extras), len(out_shapes)
    if epilogue is None:
        def epilogue(acc, ex, outs):
            outs[0][...] = acc.astype(outs[0].dtype)

    def body(*refs):
        a_ref, b_ref = refs[0], refs[1]
        ex = refs[2:2 + n_ex]
        outs = refs[2 + n_ex:2 + n_ex + n_out]
        part = lax.dot_general(a_ref[...].astype(BF16), b_ref[...].astype(BF16), dims, preferred_element_type=F32)
        if nk == 1:
            epilogue(part, ex, outs)
        else:
            acc = refs[-1]
            k = pl.program_id(2)

            @pl.when(k == 0)
            def _():
                acc[...] = part

            @pl.when(k > 0)
            def _():
                acc[...] += part

            @pl.when(k == nk - 1)
            def _():
                epilogue(acc[...], ex, outs)

    scratch = [] if nk == 1 else [pltpu.VMEM(acc_shape, F32)]
    res = pl.pallas_call(
        body, name=name, grid=grid, in_specs=[a_spec, b_spec, *extra_specs], out_specs=list(out_specs),
        out_shape=list(out_shapes), scratch_shapes=scratch,
        compiler_params=_cparams(("parallel", "parallel", "arbitrary")))(a, b, *extras)
    return res


def _mm_nn(name, a, b, tm, tn, tk, out_dtypes=(BF16,), extras=(), extra_specs=(), epilogue=None, b_tile=None, n=None):
    m, kk = a.shape
    n = b.shape[1] if b_tile is None else n
    tm, tn, tk = min(tm, m), min(tn, n), min(tk, kk)
    grid = (m // tm, n // tn, kk // tk)
    if b_tile is None:
        b_spec = pl.BlockSpec((tk, tn), lambda i, j, k: (k, j))
    else:
        b_spec = pl.BlockSpec((None,) * (b.ndim - 2) + (tk, tn), lambda i, j, k: b_tile(j, k))
    return _mm_core(
        name, a, b, dims=NN, grid=grid,
        a_spec=pl.BlockSpec((tm, tk), lambda i, j, k: (i, k)), b_spec=b_spec,
        out_shapes=[jax.ShapeDtypeStruct((m, n), dt) for dt in out_dtypes],
        out_specs=[pl.BlockSpec((tm, tn), lambda i, j, k: (i, j)) for _ in out_dtypes],
        acc_shape=(tm, tn), extras=extras, extra_specs=extra_specs, epilogue=epilogue)


def _mm_nt(name, a, b, tm, tn, tk, out_dtypes=(BF16,), extras=(), extra_specs=(), epilogue=None, b_tile=None, n=None):
    m, kk = a.shape
    n = b.shape[0] if b_tile is None else n
    tm, tn, tk = min(tm, m), min(tn, n), min(tk, kk)
    grid = (m // tm, n // tn, kk // tk)
    if b_tile is None:
        b_spec = pl.BlockSpec((tn, tk), lambda i, j, k: (j, k))
    else:
        b_spec = pl.BlockSpec((None,) * (b.ndim - 2) + (tn, tk), lambda i, j, k: b_tile(j, k))
    return _mm_core(
        name, a, b, dims=NT, grid=grid,
        a_spec=pl.BlockSpec((tm, tk), lambda i, j, k: (i, k)), b_spec=b_spec,
        out_shapes=[jax.ShapeDtypeStruct((m, n), dt) for dt in out_dtypes],
        out_specs=[pl.BlockSpec((tm, tn), lambda i, j, k: (i, j)) for _ in out_dtypes],
        acc_shape=(tm, tn), extras=extras, extra_specs=extra_specs, epilogue=epilogue)


def _mm_nt_sum(name, pairs, tm, tn, out_dtype=F32):
    m, n = pairs[0][0].shape[0], pairs[0][1].shape[0]
    steps = [a.shape[1] // min(tk, a.shape[1]) for a, _, tk in pairs]
    starts = [sum(steps[:p]) for p in range(len(pairs))]
    nk = sum(steps)

    def body(*refs):
        out_ref, acc = refs[-2], refs[-1]
        k = pl.program_id(2)
        for p, (first, count) in enumerate(zip(starts, steps)):
            @pl.when((k >= first) & (k < first + count))
            def _(p=p, first=first):
                part = lax.dot_general(refs[2 * p][...].astype(BF16), refs[2 * p + 1][...].astype(BF16), NT,
                                       preferred_element_type=F32)

                @pl.when(k == 0)
                def _():
                    acc[...] = part

                @pl.when(k > 0)
                def _():
                    acc[...] += part

        @pl.when(k == nk - 1)
        def _():
            out_ref[...] = acc[...].astype(out_ref.dtype)

    in_specs, operands = [], []
    for (a, b, tk), first, count in zip(pairs, starts, steps):
        tk = min(tk, a.shape[1])
        step = lambda k, first=first, count=count: jnp.clip(k - first, 0, count - 1)
        in_specs += [pl.BlockSpec((tm, tk), lambda i, j, k, step=step: (i, step(k))),
                     pl.BlockSpec((tn, tk), lambda i, j, k, step=step: (j, step(k)))]
        operands += [a, b]
    return pl.pallas_call(
        body, name=name, grid=(m // tm, n // tn, nk), in_specs=in_specs,
        out_specs=pl.BlockSpec((tm, tn), lambda i, j, k: (i, j)), out_shape=jax.ShapeDtypeStruct((m, n), out_dtype),
        scratch_shapes=[pltpu.VMEM((tm, tn), F32)],
        compiler_params=_cparams(("parallel", "parallel", "arbitrary")))(*operands)


def _mm_tn(name, a, b, tm, tn, tk, out_dtype=BF16, col_chunks=1):
    kk, m = a.shape
    n = b.shape[1]
    tm, tn, tk = min(tm, m), min(tn, n), min(tk, kk)
    grid = (m // tm, n // tn, kk // tk)
    if col_chunks == 1:
        out_shape, out_spec = (m, n), pl.BlockSpec((tm, tn), lambda i, j, k: (i, j))
    else:
        per = n // col_chunks // tn
        out_shape = (col_chunks, m, n // col_chunks)
        out_spec = pl.BlockSpec((None, tm, tn), lambda i, j, k: (j // per, i, j % per))
    return _mm_core(
        name, a, b, dims=TN, grid=grid,
        a_spec=pl.BlockSpec((tk, tm), lambda i, j, k: (k, i)), b_spec=pl.BlockSpec((tk, tn), lambda i, j, k: (k, j)),
        out_shapes=[jax.ShapeDtypeStruct(out_shape, out_dtype)], out_specs=[out_spec], acc_shape=(tm, tn))[0]


def _resid_gate_epilogue(acc, ex, outs):
    outs[0][...] = ex[0][...] + ex[1][...] * acc
    outs[1][...] = acc.astype(BF16)


def _mm_resid(name, a, b, resid, gate, tm, tn, tk, b_tile=None, n=None):
    return _mm_nn(
        name, a, b, tm, tn, tk, out_dtypes=(F32, BF16), extras=(resid, gate),
        extra_specs=(pl.BlockSpec((tm, tn), lambda i, j, k: (i, j)), pl.BlockSpec((1, tn), lambda i, j, k: (0, j))),
        epilogue=_resid_gate_epilogue, b_tile=b_tile, n=n)


TOK_TILE = 512


def _colsum8(v):
    t, ccols = v.shape
    return jnp.sum(v.reshape(t // 8, 8, ccols), axis=0)


def _norm_mod_fwd(x, g, sc, sh, name):
    s, d = x.shape
    t = TOK_TILE

    def body(x_ref, g_ref, sc_ref, sh_ref, h_ref):
        xv = x_ref[...]
        n = xv * lax.rsqrt(jnp.mean(xv * xv, axis=-1, keepdims=True) + NORM_EPS)
        h_ref[...] = ((n * g_ref[...]) * (1.0 + sc_ref[...]) + sh_ref[...]).astype(BF16)

    row = pl.BlockSpec((1, d), lambda i: (0, 0))
    return pl.pallas_call(
        body, name=name, grid=(s // t,), in_specs=[pl.BlockSpec((t, d), lambda i: (i, 0)), row, row, row],
        out_specs=pl.BlockSpec((t, d), lambda i: (i, 0)), out_shape=jax.ShapeDtypeStruct((s, d), BF16),
        compiler_params=_cparams(("parallel",)))(x, g, sc, sh)


def _gate_grads(dx, y_ref, gate_ref, dy_ref, dgate_ref, first):
    dy_ref[...] = (dx * gate_ref[...]).astype(BF16)

    @pl.when(first)
    def _():
        dgate_ref[...] = jnp.zeros_like(dgate_ref)

    dgate_ref[...] += _colsum8(dx * y_ref[...].astype(F32))


def _norm_mod_bwd(x, dh, resid, g, sc, name, prev=None):
    s, d = x.shape
    t = TOK_TILE if prev is None else TOK_TILE // 2

    def body(x_ref, dh_ref, r_ref, g_ref, sc_ref, *rest):
        dx_ref, dsh_ref, dsc_ref, dg_ref = rest[-4:] if prev is None else rest[2:6]
        i = pl.program_id(0)
        xv = x_ref[...]
        rstd = lax.rsqrt(jnp.mean(xv * xv, axis=-1, keepdims=True) + NORM_EPS)
        n = xv * rstd
        dhv = dh_ref[...].astype(F32)
        gv = g_ref[...]
        dyn = dhv * (1.0 + sc_ref[...])
        dn = dyn * gv
        dx = r_ref[...] + rstd * (dn - n * jnp.mean(dn * n, axis=-1, keepdims=True))
        dx_ref[...] = dx

        @pl.when(i == 0)
        def _():
            dsh_ref[...] = jnp.zeros_like(dsh_ref)
            dsc_ref[...] = jnp.zeros_like(dsc_ref)
            dg_ref[...] = jnp.zeros_like(dg_ref)

        dsh_ref[...] += _colsum8(dhv)
        dsc_ref[...] += _colsum8(dhv * (n * gv))
        dg_ref[...] += _colsum8(dyn * n)
        if prev is not None:
            _gate_grads(dx, rest[0], rest[1], rest[6], rest[7], i == 0)

    tile = pl.BlockSpec((t, d), lambda i: (i, 0))
    row = pl.BlockSpec((1, d), lambda i: (0, 0))
    acc = pl.BlockSpec((8, d), lambda i: (0, 0))
    acc_s = jax.ShapeDtypeStruct((8, d), F32)
    in_specs, operands = [tile, tile, tile, row, row], [x, dh, resid, g, sc]
    out_specs, out_shape = [tile, acc, acc, acc], [jax.ShapeDtypeStruct((s, d), F32), acc_s, acc_s, acc_s]
    if prev is not None:
        in_specs, operands = in_specs + [tile, row], operands + list(prev)
        out_specs, out_shape = out_specs + [tile, acc], out_shape + [jax.ShapeDtypeStruct((s, d), BF16), acc_s]
    return pl.pallas_call(
        body, name=name, grid=(s // t,), in_specs=in_specs, out_specs=out_specs, out_shape=out_shape,
        compiler_params=_cparams(("arbitrary",)))(*operands)


def _final_loss(x, g, target, prev, name):
    s, d = x.shape
    t = TOK_TILE
    inv_d = 1.0 / d

    def body(x_ref, g_ref, t_ref, y_ref, gate_ref, dx_ref, sq_ref, dg_ref, dy_ref, dgate_ref):
        i = pl.program_id(0)
        xv = x_ref[...]
        rstd = lax.rsqrt(jnp.mean(xv * xv, axis=-1, keepdims=True) + NORM_EPS)
        n = xv * rstd
        gv = g_ref[...]
        err = n * gv - t_ref[...]
        dout = err * inv_d
        dn = dout * gv
        dx = rstd * (dn - n * jnp.mean(dn * n, axis=-1, keepdims=True))
        dx_ref[...] = dx

        @pl.when(i == 0)
        def _():
            sq_ref[...] = jnp.zeros_like(sq_ref)
            dg_ref[...] = jnp.zeros_like(dg_ref)

        sq_ref[...] += _colsum8(err * err)
        dg_ref[...] += _colsum8(dout * n)
        _gate_grads(dx, y_ref, gate_ref, dy_ref, dgate_ref, i == 0)

    tile = pl.BlockSpec((t, d), lambda i: (i, 0))
    row = pl.BlockSpec((1, d), lambda i: (0, 0))
    acc = pl.BlockSpec((8, d), lambda i: (0, 0))
    acc_s = jax.ShapeDtypeStruct((8, d), F32)
    return pl.pallas_call(
        body, name=name, grid=(s // t,), in_specs=[tile, row, tile, tile, row],
        out_specs=[tile, acc, acc, tile, acc],
        out_shape=[jax.ShapeDtypeStruct((s, d), F32), acc_s, acc_s, jax.ShapeDtypeStruct((s, d), BF16), acc_s],
        compiler_params=_cparams(("arbitrary",)))(x, g, target, *prev)


NEG_BIG = -1e30


def _head_masks(rows):
    lane = lax.broadcasted_iota(jnp.int32, (rows, LANES), 1)
    return [(lane // HEAD_DIM) == hh for hh in range(2)]


def _group_view(slabs, gi):
    dil = ATTN_GROUPS[gi][1]
    _, s, w = slabs.shape
    if dil == 1:
        return slabs, (gi, 3 + gi, 6 + gi)
    return slabs[gi:9:3].reshape(3, s // dil, dil * w), (0, 1, 2)


def _attn_fwd(slabs, gi, name):
    window, dil = ATTN_GROUPS[gi]
    n_back = window // dil
    _, s, w = slabs.shape
    big_l = s // dil
    nb = big_l // ATTN_BLOCK
    blk = ATTN_BLOCK
    view, (iq, ik, iv) = _group_view(slabs, gi)
    scale = HEAD_DIM ** -0.5

    def body(q_ref, kp_ref, kc_ref, vp_ref, vc_ref, o_ref, lse_ref):
        n = pl.program_id(1)
        qi = lax.broadcasted_iota(jnp.int32, (blk, 2 * blk), 0)
        kj = lax.broadcasted_iota(jnp.int32, (blk, 2 * blk), 1)
        dist = qi + blk - kj
        ok = (dist >= 0) & (dist <= n_back) & ((kj >= blk) | (n > 0))
        mq = _head_masks(blk)
        mk = _head_masks(2 * blk)
        for p in range(w // LANES):
            cols = slice(p * LANES, (p + 1) * LANES)
            qp = q_ref[:, cols]
            k2 = jnp.concatenate([kp_ref[:, cols], kc_ref[:, cols]], axis=0)
            v2 = jnp.concatenate([vp_ref[:, cols], vc_ref[:, cols]], axis=0)
            o_pair = jnp.zeros((blk, LANES), F32)
            lse_pair = jnp.zeros((blk, LANES), F32)
            for hh in range(2):
                qm = jnp.where(mq[hh], qp, jnp.zeros_like(qp))
                sc = lax.dot_general(qm, k2, NT, preferred_element_type=F32) * scale
                sc = jnp.where(ok, sc, NEG_BIG)
                mx = jnp.max(sc, axis=-1, keepdims=True)
                pe = jnp.exp(sc - mx)
                den = jnp.sum(pe, axis=-1, keepdims=True)
                pn = (pe / den).astype(BF16)
                vm = jnp.where(mk[hh], v2, jnp.zeros_like(v2))
                o_pair = o_pair + jnp.dot(pn, vm, preferred_element_type=F32)
                lse_pair = jnp.where(mq[hh], mx + jnp.log(den), lse_pair)
            o_ref[:, cols] = o_pair
            lse_ref[:, cols] = lse_pair

    def spec(slab, prev):
        if prev:
            return pl.BlockSpec((None, blk, w), lambda r, n: (slab, jnp.maximum(n - 1, 0), r))
        return pl.BlockSpec((None, blk, w), lambda r, n: (slab, n, r))

    out_spec = pl.BlockSpec((blk, w), lambda r, n: (n, r))
    sds = jax.ShapeDtypeStruct((big_l, dil * w), F32)
    o, lse = pl.pallas_call(
        body, name=name, grid=(dil, nb),
        in_specs=[spec(iq, False), spec(ik, True), spec(ik, False), spec(iv, True), spec(iv, False)],
        out_specs=[out_spec, out_spec], out_shape=[sds, sds],
        compiler_params=_cparams(("parallel", "arbitrary")))(view, view, view, view, view)
    return o.reshape(s, w), lse.reshape(s, w)


def _attn_merge(outs, lses, name):
    s, w = outs[0].shape
    t = TOK_TILE

    def body(o0, o1, o2, l0, l1, l2, a_ref, lt_ref):
        ls = [l0[...], l1[...], l2[...]]
        mx = jnp.maximum(jnp.maximum(ls[0], ls[1]), ls[2])
        es = [jnp.exp(l - mx) for l in ls]
        den = es[0] + es[1] + es[2]
        num = es[0] * o0[...] + es[1] * o1[...] + es[2] * o2[...]
        a_ref[...] = (num / den).astype(BF16)
        lt_ref[...] = mx + jnp.log(den)

    tile = pl.BlockSpec((t, w), lambda i: (i, 0))
    return pl.pallas_call(
        body, name=name, grid=(s // t,), in_specs=[tile] * 6, out_specs=[tile, tile],
        out_shape=[jax.ShapeDtypeStruct((s, w), BF16), jax.ShapeDtypeStruct((s, w), F32)],
        compiler_params=_cparams(("parallel",)))(*outs, *lses)


def _attn_bwd(slabs, dap, ap, lse_tot, gi, name):
    window, dil = ATTN_GROUPS[gi]
    n_back = window // dil
    _, s, w = slabs.shape
    big_l = s // dil
    nb = big_l // ATTN_BLOCK
    blk = ATTN_BLOCK
    view, (iq, ik, iv) = _group_view(slabs, gi)
    dap_v = dap.reshape(big_l, dil * 2 * w)
    ap_v = ap.reshape(big_l, dil * 2 * w)
    lse_v = lse_tot.reshape(big_l, dil * w)
    scale = HEAD_DIM ** -0.5

    def body(qc_ref, qn_ref, kp_ref, kc_ref, vp_ref, vc_ref, dc_ref, dn_ref, ac_ref, an_ref, lc_ref, ln_ref, out_ref):
        m = pl.program_id(1)
        qi = lax.broadcasted_iota(jnp.int32, (blk, 2 * blk), 0)
        kj = lax.broadcasted_iota(jnp.int32, (blk, 2 * blk), 1)
        dist = qi + blk - kj
        ok = (dist >= 0) & (dist <= n_back) & ((kj >= blk) | (m > 0))
        qi1 = lax.broadcasted_iota(jnp.int32, (blk, blk), 0)
        kj1 = lax.broadcasted_iota(jnp.int32, (blk, blk), 1)
        ok_next = (qi1 + blk - kj1 <= n_back) & (m + 1 < nb)
        mq = _head_masks(blk)
        mk = _head_masks(2 * blk)

        def per_head(q_t, d_t, a_t, l_t, hh):
            qm = jnp.where(mq[hh], q_t, jnp.zeros_like(q_t))
            dm = jnp.where(mq[hh], d_t, jnp.zeros_like(d_t))
            delta = jnp.sum(jnp.where(mq[hh], d_t.astype(F32) * a_t.astype(F32), 0.0), axis=-1, keepdims=True)
            lse_h = jnp.max(jnp.where(mq[hh], l_t, NEG_BIG), axis=-1, keepdims=True)
            return qm, dm, delta, lse_h

        for p in range(w // LANES):
            cols = slice(p * LANES, (p + 1) * LANES)
            k2 = jnp.concatenate([kp_ref[:, cols], kc_ref[:, cols]], axis=0)
            v2 = jnp.concatenate([vp_ref[:, cols], vc_ref[:, cols]], axis=0)
            kc, vc = kc_ref[:, cols], vc_ref[:, cols]
            dq_pair = jnp.zeros((blk, LANES), F32)
            dk_pair = jnp.zeros((blk, LANES), F32)
            dv_pair = jnp.zeros((blk, LANES), F32)
            for hh in range(2):
                qm, dm, delta, lse_h = per_head(qc_ref[:, cols], dc_ref[:, cols], ac_ref[:, cols], lc_ref[:, cols], hh)
                sc = lax.dot_general(qm, k2, NT, preferred_element_type=F32) * scale
                pr = jnp.exp(jnp.where(ok, sc, NEG_BIG) - lse_h)
                dp = lax.dot_general(dm, v2, NT, preferred_element_type=F32)
                ds = pr * (dp - delta)
                ds_b = ds.astype(BF16)
                km = jnp.where(mk[hh], k2, jnp.zeros_like(k2))
                dq_pair = dq_pair + jnp.dot(ds_b, km, preferred_element_type=F32)
                dk_pair = dk_pair + lax.dot_general(ds_b[:, blk:], qm, TN, preferred_element_type=F32)
                dv_pair = dv_pair + lax.dot_general(pr[:, blk:].astype(BF16), dm, TN, preferred_element_type=F32)
                qm, dm, delta, lse_h = per_head(qn_ref[:, cols], dn_ref[:, cols], an_ref[:, cols], ln_ref[:, cols], hh)
                sc = lax.dot_general(qm, kc, NT, preferred_element_type=F32) * scale
                pr = jnp.exp(jnp.where(ok_next, sc, NEG_BIG) - lse_h)
                dp = lax.dot_general(dm, vc, NT, preferred_element_type=F32)
                ds_b = (pr * (dp - delta)).astype(BF16)
                dk_pair = dk_pair + lax.dot_general(ds_b, qm, TN, preferred_element_type=F32)
                dv_pair = dv_pair + lax.dot_general(pr.astype(BF16), dm, TN, preferred_element_type=F32)
            out_ref[0, :, cols] = (dq_pair * scale).astype(BF16)
            out_ref[1, :, cols] = (dk_pair * scale).astype(BF16)
            out_ref[2, :, cols] = dv_pair.astype(BF16)

    def slab_spec(slab, shift):
        if shift < 0:
            return pl.BlockSpec((None, blk, w), lambda r, n: (slab, jnp.maximum(n - 1, 0), r))
        if shift > 0:
            return pl.BlockSpec((None, blk, w), lambda r, n: (slab, jnp.minimum(n + 1, nb - 1), r))
        return pl.BlockSpec((None, blk, w), lambda r, n: (slab, n, r))

    def tok_spec(stride, shift):
        if shift > 0:
            return pl.BlockSpec((blk, w), lambda r, n: (jnp.minimum(n + 1, nb - 1), stride * r))
        return pl.BlockSpec((blk, w), lambda r, n: (n, stride * r))

    out = pl.pallas_call(
        body, name=name, grid=(dil, nb),
        in_specs=[slab_spec(iq, 0), slab_spec(iq, 1), slab_spec(ik, -1), slab_spec(ik, 0),
                  slab_spec(iv, -1), slab_spec(iv, 0),
                  tok_spec(2, 0), tok_spec(2, 1), tok_spec(2, 0), tok_spec(2, 1), tok_spec(1, 0), tok_spec(1, 1)],
        out_specs=pl.BlockSpec((3, blk, w), lambda r, n: (0, n, r)),
        out_shape=jax.ShapeDtypeStruct((3, big_l, dil * w), BF16),
        compiler_params=_cparams(("parallel", "arbitrary")))(
            view, view, view, view, view, view, dap_v, dap_v, ap_v, ap_v, lse_v, lse_v)
    return out.reshape(3, s, w)


def _shift_down(cur, prev, j):
    row = lax.broadcasted_iota(jnp.int32, cur.shape, 0)
    return jnp.where(row >= j, pltpu.roll(cur, j, 0), pltpu.roll(prev, j, 0))


def _shift_up(cur, nxt, j):
    t = cur.shape[0]
    row = lax.broadcasted_iota(jnp.int32, cur.shape, 0)
    return jnp.where(row < t - j, pltpu.roll(cur, t - j, 0), pltpu.roll(nxt, t - j, 0))


def _window_sum_down(cur, prev, w):
    s, sp, step = cur, prev, 1
    while step < w:
        s_new = s + _shift_down(s, sp, step)
        sp = sp + pltpu.roll(sp, step, 0)
        s, step = s_new, step * 2
    return s


def _window_sum_up(cur, nxt, w):
    t = cur.shape[0]
    s, sn, step = cur, nxt, 1
    while step < w:
        s_new = s + _shift_up(s, sn, step)
        sn = sn + pltpu.roll(sn, t - step, 0)
        s, step = s_new, step * 2
    return s


def _pool_count(tile_idx, t, w):
    row = lax.broadcasted_iota(jnp.int32, (t, LANES), 0) + tile_idx * t
    return jnp.minimum(row + 1, w).astype(F32)


def _pool_fwd(slabs, attn, pool_w, pool_scale, name):
    _, s, w = slabs.shape
    t = TOK_TILE
    gw = POOL_W // len(POOL_WINDOWS)

    def body(u_ref, up_ref, a_ref, w_ref, sc_ref, ap_ref, d_ref):
        i = pl.program_id(0)
        ap_ref[:, :w] = a_ref[...]
        for gi, win in enumerate(POOL_WINDOWS):
            cols = slice(gi * gw, (gi + 1) * gw)
            u = u_ref[:, cols].astype(F32)
            up = jnp.where(i > 0, up_ref[:, cols].astype(F32), 0.0)
            d = (_window_sum_down(u, up, win) / _pool_count(i, t, win) - u).astype(BF16)
            d_ref[:, cols] = d
            y = jnp.dot(d, w_ref[gi].astype(BF16), preferred_element_type=F32)
            ap_ref[:, w + gi * gw:w + (gi + 1) * gw] = (y * sc_ref[:, cols]).astype(BF16)

    return pl.pallas_call(
        body, name=name, grid=(s // t,),
        in_specs=[pl.BlockSpec((None, t, w), lambda i: (N_SLABS - 1, i, 0)),
                  pl.BlockSpec((None, t, w), lambda i: (N_SLABS - 1, jnp.maximum(i - 1, 0), 0)),
                  pl.BlockSpec((t, w), lambda i: (i, 0)),
                  pl.BlockSpec((len(POOL_WINDOWS), gw, gw), lambda i: (0, 0, 0)),
                  pl.BlockSpec((1, w), lambda i: (0, 0))],
        out_specs=[pl.BlockSpec((t, 2 * w), lambda i: (i, 0)), pl.BlockSpec((t, w), lambda i: (i, 0))],
        out_shape=[jax.ShapeDtypeStruct((s, 2 * w), BF16), jax.ShapeDtypeStruct((s, w), BF16)],
        compiler_params=_cparams(("parallel",)))(slabs, slabs, attn, pool_w, pool_scale)


def _pool_bwd(dap, d, pool_w, pool_scale, name):
    s, w = d.shape
    t = TOK_TILE
    nt = s // t
    gw = POOL_W // len(POOL_WINDOWS)

    def body(dy_ref, dyn_ref, d_ref, w_ref, sc_ref, du_ref, dw_ref, dsc_ref):
        i = pl.program_id(0)

        @pl.when(i == 0)
        def _():
            dw_ref[...] = jnp.zeros_like(dw_ref)
            dsc_ref[...] = jnp.zeros_like(dsc_ref)

        for gi, win in enumerate(POOL_WINDOWS):
            cols = slice(gi * gw, (gi + 1) * gw)
            wb = w_ref[gi].astype(BF16)
            scale = sc_ref[:, cols]
            dv = d_ref[:, cols]
            dy = dy_ref[:, cols].astype(F32)
            y = jnp.dot(dv, wb, preferred_element_type=F32)
            dsc_ref[:, cols] += _colsum8(dy * y)
            dyp = (dy * scale).astype(BF16)
            dw_ref[gi] += lax.dot_general(dv, dyp, TN, preferred_element_type=F32)
            dd = lax.dot_general(dyp, wb, NT, preferred_element_type=F32)
            dypn = (dyn_ref[:, cols].astype(F32) * scale).astype(BF16)
            ddn = lax.dot_general(dypn, wb, NT, preferred_element_type=F32)
            e = dd / _pool_count(i, t, win)
            en = jnp.where(i + 1 < nt, ddn / _pool_count(i + 1, t, win), 0.0)
            du_ref[:, cols] = (_window_sum_up(e, en, win) - dd).astype(BF16)

    return pl.pallas_call(
        body, name=name, grid=(nt,),
        in_specs=[pl.BlockSpec((t, w), lambda i: (i, 1)),
                  pl.BlockSpec((t, w), lambda i: (jnp.minimum(i + 1, nt - 1), 1)),
                  pl.BlockSpec((t, w), lambda i: (i, 0)),
                  pl.BlockSpec((len(POOL_WINDOWS), gw, gw), lambda i: (0, 0, 0)),
                  pl.BlockSpec((1, w), lambda i: (0, 0))],
        out_specs=[pl.BlockSpec((t, w), lambda i: (i, 0)),
                   pl.BlockSpec((len(POOL_WINDOWS), gw, gw), lambda i: (0, 0, 0)),
                   pl.BlockSpec((8, w), lambda i: (0, 0))],
        out_shape=[jax.ShapeDtypeStruct((s, w), BF16), jax.ShapeDtypeStruct((len(POOL_WINDOWS), gw, gw), F32),
                   jax.ShapeDtypeStruct((8, w), F32)],
        compiler_params=_cparams(("arbitrary",)))(dap, dap, d, pool_w, pool_scale)


CONV_TAPS = 4
CONV_COLS = 512


def _sigmoid(v):
    return 1.0 / (1.0 + jnp.exp(-v))


def _conv_pre(x, xprev, w_ref, b_ref):
    pre = b_ref[...] + w_ref[CONV_TAPS - 1:CONV_TAPS, :] * x
    for k in range(CONV_TAPS - 1):
        pre = pre + w_ref[k:k + 1, :] * _shift_down(x, xprev, CONV_TAPS - 1 - k)
    return pre


def _conv_fwd(xbc, conv_w, conv_b, name):
    s, c = xbc.shape
    t, tc = TOK_TILE, CONV_COLS

    def body(x_ref, xp_ref, w_ref, b_ref, o_ref):
        i = pl.program_id(0)
        x = x_ref[...].astype(F32)
        xp = jnp.where(i > 0, xp_ref[...].astype(F32), 0.0)
        pre = _conv_pre(x, xp, w_ref, b_ref)
        o_ref[...] = (pre * _sigmoid(pre)).astype(BF16)

    return pl.pallas_call(
        body, name=name, grid=(s // t, c // tc),
        in_specs=[pl.BlockSpec((t, tc), lambda i, j: (i, j)),
                  pl.BlockSpec((t, tc), lambda i, j: (jnp.maximum(i - 1, 0), j)),
                  pl.BlockSpec((CONV_TAPS, tc), lambda i, j: (0, j)), pl.BlockSpec((1, tc), lambda i, j: (0, j))],
        out_specs=pl.BlockSpec((t, tc), lambda i, j: (i, j)), out_shape=jax.ShapeDtypeStruct((s, c), BF16),
        compiler_params=_cparams(("parallel", "parallel")))(xbc, xbc, conv_w, conv_b)


def _conv_bwd(xbc, dact, conv_w, conv_b, name):
    s, c = xbc.shape
    t, tc = TOK_TILE, CONV_COLS
    nt = s // t

    def body(xp_ref, x_ref, xn_ref, da_ref, dan_ref, w_ref, b_ref, dx_ref, dw_ref, db_ref):
        i = pl.program_id(1)

        @pl.when(i == 0)
        def _():
            dw_ref[...] = jnp.zeros_like(dw_ref)
            db_ref[...] = jnp.zeros_like(db_ref)

        x = x_ref[...].astype(F32)
        xp = jnp.where(i > 0, xp_ref[...].astype(F32), 0.0)
        xn = xn_ref[...].astype(F32)

        def dsilu(pre):
            sg = _sigmoid(pre)
            return sg * (1.0 + pre * (1.0 - sg))

        dpre = da_ref[...] * dsilu(_conv_pre(x, xp, w_ref, b_ref))
        dpre_n = jnp.where(i + 1 < nt, dan_ref[...] * dsilu(_conv_pre(xn, x, w_ref, b_ref)), 0.0)
        dx = w_ref[CONV_TAPS - 1:CONV_TAPS, :] * dpre
        dw_ref[CONV_TAPS - 1] += _colsum8(dpre * x)
        for k in range(CONV_TAPS - 1):
            j = CONV_TAPS - 1 - k
            dx = dx + w_ref[k:k + 1, :] * _shift_up(dpre, dpre_n, j)
            dw_ref[k] += _colsum8(dpre * _shift_down(x, xp, j))
        dx_ref[...] = dx.astype(BF16)
        db_ref[...] += _colsum8(dpre)

    def xspec(shift):
        if shift < 0:
            return pl.BlockSpec((t, tc), lambda j, i: (jnp.maximum(i - 1, 0), j))
        if shift > 0:
            return pl.BlockSpec((t, tc), lambda j, i: (jnp.minimum(i + 1, nt - 1), j))
        return pl.BlockSpec((t, tc), lambda j, i: (i, j))

    return pl.pallas_call(
        body, name=name, grid=(c // tc, nt),
        in_specs=[xspec(-1), xspec(0), xspec(1), xspec(0), xspec(1),
                  pl.BlockSpec((CONV_TAPS, tc), lambda j, i: (0, j)), pl.BlockSpec((1, tc), lambda j, i: (0, j))],
        out_specs=[xspec(0), pl.BlockSpec((CONV_TAPS, 8, tc), lambda j, i: (0, 0, j)),
                   pl.BlockSpec((8, tc), lambda j, i: (0, j))],
        out_shape=[jax.ShapeDtypeStruct((s, c), BF16), jax.ShapeDtypeStruct((CONV_TAPS, 8, c), F32),
                   jax.ShapeDtypeStruct((8, c), F32)],
        compiler_params=_cparams(("parallel", "arbitrary")))(xbc, xbc, xbc, dact, dact, conv_w, conv_b)


GN_TILE = 256


def _gated_norm_fwd(y, z, g, name):
    s, c = y.shape
    t = GN_TILE

    def body(y_ref, z_ref, g_ref, o_ref):
        for gi in range(SSM_GROUPS):
            cols = slice(gi * SSM_GW, (gi + 1) * SSM_GW)
            zv = z_ref[:, cols].astype(F32)
            yf = y_ref[:, cols] * (zv * _sigmoid(zv))
            r = lax.rsqrt(jnp.mean(yf * yf, axis=-1, keepdims=True) + NORM_EPS)
            o_ref[:, cols] = (yf * r * g_ref[:, cols]).astype(BF16)

    tile = pl.BlockSpec((t, c), lambda i: (i, 0))
    return pl.pallas_call(
        body, name=name, grid=(s // t,), in_specs=[tile, tile, pl.BlockSpec((1, c), lambda i: (0, 0))],
        out_specs=tile, out_shape=jax.ShapeDtypeStruct((s, c), BF16),
        compiler_params=_cparams(("parallel",)))(y, z, g)


def _gated_norm_bwd(y, z, dout, g, name):
    s, c = y.shape
    t = GN_TILE

    def body(y_ref, z_ref, do_ref, g_ref, dy_ref, dz_ref, dg_ref):
        i = pl.program_id(0)

        @pl.when(i == 0)
        def _():
            dg_ref[...] = jnp.zeros_like(dg_ref)

        for gi in range(SSM_GROUPS):
            cols = slice(gi * SSM_GW, (gi + 1) * SSM_GW)
            zv = z_ref[:, cols].astype(F32)
            yv = y_ref[:, cols]
            sg = _sigmoid(zv)
            sz = zv * sg
            yf = yv * sz
            r = lax.rsqrt(jnp.mean(yf * yf, axis=-1, keepdims=True) + NORM_EPS)
            n = yf * r
            dout = do_ref[:, cols]
            dn = dout * g_ref[:, cols]
            dg_ref[:, cols] += _colsum8(dout * n)
            dyf = r * (dn - n * jnp.mean(dn * n, axis=-1, keepdims=True))
            dy_ref[:, cols] = dyf * sz
            dz_ref[:, cols] = (dyf * yv * (sg * (1.0 + zv * (1.0 - sg)))).astype(BF16)

    tile = pl.BlockSpec((t, c), lambda i: (i, 0))
    return pl.pallas_call(
        body, name=name, grid=(s // t,), in_specs=[tile, tile, tile, pl.BlockSpec((1, c), lambda i: (0, 0))],
        out_specs=[tile, tile, pl.BlockSpec((8, c), lambda i: (0, 0))],
        out_shape=[jax.ShapeDtypeStruct((s, c), F32), jax.ShapeDtypeStruct((s, c), BF16),
                   jax.ShapeDtypeStruct((8, c), F32)],
        compiler_params=_cparams(("arbitrary",)))(y, z, dout, g)


def _split_dot(x, e, dims, terms):
    r, acc = x, None
    for i in range(terms):
        p = r.astype(BF16)
        part = lax.dot_general(p, e, dims, preferred_element_type=F32)
        acc = part if acc is None else acc + part
        if i + 1 < terms:
            r = r - p.astype(F32)
    return acc


def _split_dot_r(e, x, dims, terms):
    r, acc = x, None
    for i in range(terms):
        p = r.astype(BF16)
        part = lax.dot_general(e, p, dims, preferred_element_type=F32)
        acc = part if acc is None else acc + part
        if i + 1 < terms:
            r = r - p.astype(F32)
    return acc


def _ssd_prep(dtr_ref, bias_ref, alog_ref):
    q = SSM_CHUNK
    dtr = dtr_ref[...] + bias_ref[...]
    dt = jnp.maximum(dtr, 0.0) + jnp.log(1.0 + jnp.exp(-jnp.abs(dtr)))
    a_neg = -jnp.exp(alog_ref[...])
    dta = dt * a_neg
    row = lax.broadcasted_iota(jnp.int32, (q, q), 0)
    col = lax.broadcasted_iota(jnp.int32, (q, q), 1)
    causal = row >= col
    a = _split_dot_r(causal.astype(BF16), dta, NN, 3)
    aq_row = jnp.sum(dta, axis=0, keepdims=True)
    aq_hb = _split_dot(dta, jnp.ones((q, LANES), BF16), TN, 3)
    return dict(dtr=dtr, dt=dt, a_neg=a_neg, a=a, a_t=jnp.transpose(a), aq_row=aq_row, aq_hb=aq_hb,
                ea=jnp.exp(a), fa=jnp.exp(aq_row - a), causal=causal)


def _ssd_group_mats():
    g = jnp.arange(SSM_GROUPS)[:, None, None]
    head = jnp.arange(LANES)[None, :, None]
    eg = (head == 8 * g + jnp.arange(SSM_GW)[None, None, :] // HEAD_DIM).astype(BF16)
    sel = (head == 8 * g + jnp.arange(8 * LANES)[None, None, :] // LANES).astype(BF16)
    mats = (eg, jnp.transpose(eg, (0, 2, 1)), sel)
    return mats, [pl.BlockSpec(m.shape, lambda c: (0, 0, 0)) for m in mats]


def _ssd_fwd(xbc, dt_raw, dt_bias, a_log, d_e, name):
    s = xbc.shape[0]
    q = SSM_CHUNK
    nc = s // q

    def body(x_ref, dtr_ref, bias_ref, alog_ref, de_ref, eg_ref, egt_ref, sel_ref, y_ref, hin_ref, state, at_ref):
        c = pl.program_id(0)

        @pl.when(c == 0)
        def _():
            state[...] = jnp.zeros_like(state)

        hin_ref[...] = state[...].astype(BF16)
        pr = _ssd_prep(dtr_ref, bias_ref, alog_ref)
        at_ref[...] = pr["a_t"]
        exp_aq_hb = jnp.exp(pr["aq_hb"])
        stack3 = jnp.concatenate([pr["dt"], pr["ea"], pr["fa"]], axis=0)
        mq = _head_masks(q)
        for g in range(SSM_GROUPS):
            eg, sel = eg_ref[g], sel_ref[g]
            cols = slice(g * SSM_GW, (g + 1) * SSM_GW)
            xg = x_ref[:, cols]
            bg = x_ref[:, SSM_INNER + g * SSM_STATE:SSM_INNER + (g + 1) * SSM_STATE]
            cg = x_ref[:, SSM_INNER + SSM_GROUPS * SSM_STATE + g * SSM_STATE:
                       SSM_INNER + SSM_GROUPS * SSM_STATE + (g + 1) * SSM_STATE]
            e3 = _split_dot(stack3, eg, NN, 1)
            dt_e, ea_e, fa_e = e3[:q], e3[q:2 * q], e3[2 * q:]
            xf = xg.astype(F32)
            xdt = xf * dt_e
            xdt_b = xdt.astype(BF16)
            cb = lax.dot_general(cg, bg, NT, preferred_element_type=F32)
            colb = _split_dot(pr["a"], sel, NN, 2)
            hg = state[cols, :]
            y_g = lax.dot_general(cg, hg.astype(BF16), NT, preferred_element_type=F32) * ea_e + de_ref[:, cols] * xf
            pairs = []
            for pp in range(4):
                xp = xdt_b[:, pp * LANES:(pp + 1) * LANES]
                yp = jnp.zeros((q, LANES), F32)
                for hh in range(2):
                    hi = 2 * pp + hh
                    diff = colb[:, hi * LANES:(hi + 1) * LANES] - at_ref[8 * g + hi:8 * g + hi + 1, :]
                    lmat = jnp.exp(jnp.where(pr["causal"], diff, NEG_BIG))
                    m_b = (cb * lmat).astype(BF16)
                    yp = yp + jnp.dot(m_b, jnp.where(mq[hh], xp, jnp.zeros_like(xp)), preferred_element_type=F32)
                pairs.append(yp)
            y_ref[:, cols] = y_g + jnp.concatenate(pairs, axis=1)
            s_g = lax.dot_general((xdt * fa_e).astype(BF16), bg, TN, preferred_element_type=F32)
            dec_g = _split_dot_r(eg, exp_aq_hb, TN, 2)
            state[cols, :] = dec_g * hg + s_g

    row128 = pl.BlockSpec((1, LANES), lambda c: (0, 0))
    mats, mat_specs = _ssd_group_mats()
    return pl.pallas_call(
        body, name=name, grid=(nc,),
        in_specs=[pl.BlockSpec((q, SSM_CONV_DIM), lambda c: (c, 0)), pl.BlockSpec((q, LANES), lambda c: (c, 0)),
                  row128, row128, pl.BlockSpec((1, SSM_INNER), lambda c: (0, 0)), *mat_specs],
        out_specs=[pl.BlockSpec((q, SSM_INNER), lambda c: (c, 0)),
                   pl.BlockSpec((None, SSM_INNER, SSM_STATE), lambda c: (c, 0, 0))],
        out_shape=[jax.ShapeDtypeStruct((s, SSM_INNER), F32), jax.ShapeDtypeStruct((nc, SSM_INNER, SSM_STATE), BF16)],
        scratch_shapes=[pltpu.VMEM((SSM_INNER, SSM_STATE), F32), pltpu.VMEM((LANES, q), F32)],
        compiler_params=_cparams(("arbitrary",)))(xbc, dt_raw, dt_bias, a_log, d_e, *mats)


def _ssd_bwd(xbc, dt_raw, dt_bias, a_log, d_e, hin, dy, name):
    s = xbc.shape[0]
    q = SSM_CHUNK
    nc = s // q

    def body(x_ref, dtr_ref, bias_ref, alog_ref, de_ref, hin_ref, dy_ref, eg_ref, egt_ref, sel_ref,
             dx_ref, ddt_ref, da_acc, db_acc, dd_acc, gst, at_ref):
        i = pl.program_id(0)

        @pl.when(i == 0)
        def _():
            gst[...] = jnp.zeros_like(gst)
            da_acc[...] = jnp.zeros_like(da_acc)
            db_acc[...] = jnp.zeros_like(db_acc)
            dd_acc[...] = jnp.zeros_like(dd_acc)

        pr = _ssd_prep(dtr_ref, bias_ref, alog_ref)
        at_ref[...] = pr["a_t"]
        exp_aq_hb = jnp.exp(pr["aq_hb"])
        stack3 = jnp.concatenate([pr["dt"], pr["ea"], pr["fa"]], axis=0)
        mq = _head_masks(q)
        lane_h = lax.broadcasted_iota(jnp.int32, (q, LANES), 1)
        sub_h = lax.broadcasted_iota(jnp.int32, (LANES, q), 0)
        da = jnp.zeros((q, LANES), F32)
        da_tn = jnp.zeros((LANES, q), F32)
        daq = jnp.zeros((1, LANES), F32)
        ddt = jnp.zeros((q, LANES), F32)
        for g in range(SSM_GROUPS):
            eg, eg_t, sel = eg_ref[g], egt_ref[g], sel_ref[g]
            cols = slice(g * SSM_GW, (g + 1) * SSM_GW)
            bcols = slice(SSM_INNER + g * SSM_STATE, SSM_INNER + (g + 1) * SSM_STATE)
            ccols = slice(SSM_INNER + SSM_GROUPS * SSM_STATE + g * SSM_STATE,
                          SSM_INNER + SSM_GROUPS * SSM_STATE + (g + 1) * SSM_STATE)
            xg, bg, cg = x_ref[:, cols], x_ref[:, bcols], x_ref[:, ccols]
            e3 = _split_dot(stack3, eg, NN, 1)
            dt_e, ea_e, fa_e = e3[:q], e3[q:2 * q], e3[2 * q:]
            xf = xg.astype(F32)
            xdt = xf * dt_e
            xdt_b = xdt.astype(BF16)
            xdtf = xdt * fa_e
            xdtf_b = xdtf.astype(BF16)
            cb = lax.dot_general(cg, bg, NT, preferred_element_type=F32)
            colb = _split_dot(pr["a"], sel, NN, 2)
            dyg = dy_ref[:, cols]
            dd_acc[:, cols] += _colsum8(dyg * xf)
            hg_b = hin_ref[cols, :]
            gg = gst[cols, :]
            gg_b = gg.astype(BF16)
            dye_b = (dyg * ea_e).astype(BF16)
            dc_g = jnp.dot(dye_b, hg_b, preferred_element_type=F32)
            dh_g = lax.dot_general(dye_b, cg, TN, preferred_element_type=F32)
            yoff = lax.dot_general(cg, hg_b, NT, preferred_element_type=F32) * ea_e
            da = da + _split_dot(dyg * yoff, eg_t, NN, 1)
            db_g = jnp.dot(xdtf_b, gg_b, preferred_element_type=F32)
            tmat = lax.dot_general(bg, gg_b, NT, preferred_element_type=F32)
            dxdt = fa_e * tmat
            qmat = _split_dot(xdtf * tmat, eg_t, NN, 1)
            da = da - qmat
            daq = daq + jnp.sum(qmat, axis=0, keepdims=True)
            gh = _split_dot(gg * hg_b.astype(F32), eg_t, TN, 1)
            daq = daq + jnp.sum(gh, axis=0, keepdims=True) * jnp.exp(pr["aq_row"])
            dcb = jnp.zeros((q, q), F32)
            pairs = []
            for pp in range(4):
                xp = xdt_b[:, pp * LANES:(pp + 1) * LANES]
                dyp = dyg[:, pp * LANES:(pp + 1) * LANES]
                dxp = jnp.zeros((q, LANES), F32)
                for hh in range(2):
                    hi = 2 * pp + hh
                    h = 8 * g + hi
                    diff = colb[:, hi * LANES:(hi + 1) * LANES] - at_ref[h:h + 1, :]
                    lmat = jnp.exp(jnp.where(pr["causal"], diff, NEG_BIG))
                    mmat = cb * lmat
                    dyh = jnp.where(mq[hh], dyp, 0.0).astype(BF16)
                    dm = lax.dot_general(dyh, xp, NT, preferred_element_type=F32)
                    dxp = dxp + lax.dot_general(mmat.astype(BF16), dyh, TN, preferred_element_type=F32)
                    wmat = dm * mmat
                    da = da + jnp.where(lane_h == h, jnp.sum(wmat, axis=-1, keepdims=True), 0.0)
                    da_tn = da_tn + jnp.where(sub_h == h, jnp.sum(wmat, axis=0, keepdims=True), 0.0)
                    dcb = dcb + dm * lmat
                pairs.append(dxp)
            dxdt = dxdt + jnp.concatenate(pairs, axis=1)
            dcb_b = dcb.astype(BF16)
            dc_g = dc_g + jnp.dot(dcb_b, bg, preferred_element_type=F32)
            db_g = db_g + lax.dot_general(dcb_b, cg, TN, preferred_element_type=F32)
            ddt = ddt + _split_dot(dxdt * xf, eg_t, NN, 1)
            dx_ref[:, cols] = dxdt * dt_e + de_ref[:, cols] * dyg
            dx_ref[:, bcols] = db_g
            dx_ref[:, ccols] = dc_g
            dec_g = _split_dot_r(eg, exp_aq_hb, TN, 2)
            gst[cols, :] = dh_g + dec_g * gg
        row = lax.broadcasted_iota(jnp.int32, (q, LANES), 0)
        da_all = da - jnp.transpose(da_tn) + jnp.where(row == q - 1, daq, 0.0)
        upper = jnp.logical_not(pr["causal"]) | (lax.broadcasted_iota(jnp.int32, (q, q), 0)
                                                 == lax.broadcasted_iota(jnp.int32, (q, q), 1))
        rcs = _split_dot_r(upper.astype(BF16), da_all, NN, 3)
        ddt_all = ddt + pr["a_neg"] * rcs
        da_acc[...] += _colsum8(pr["dt"] * rcs)
        ddtr = jnp.where(lane_h < SSM_HEADS, ddt_all * _sigmoid(pr["dtr"]), 0.0)
        ddt_ref[...] = ddtr
        db_acc[...] += _colsum8(ddtr)

    rev = lambda i: (nc - 1 - i, 0)
    row128 = pl.BlockSpec((1, LANES), lambda i: (0, 0))
    acc128 = pl.BlockSpec((8, LANES), lambda i: (0, 0))
    mats, mat_specs = _ssd_group_mats()
    return pl.pallas_call(
        body, name=name, grid=(nc,),
        in_specs=[pl.BlockSpec((q, SSM_CONV_DIM), rev), pl.BlockSpec((q, LANES), rev), row128, row128,
                  pl.BlockSpec((1, SSM_INNER), lambda i: (0, 0)),
                  pl.BlockSpec((None, SSM_INNER, SSM_STATE), lambda i: (nc - 1 - i, 0, 0)),
                  pl.BlockSpec((q, SSM_INNER), rev), *mat_specs],
        out_specs=[pl.BlockSpec((q, SSM_CONV_DIM), rev), pl.BlockSpec((q, LANES), rev), acc128, acc128,
                   pl.BlockSpec((8, SSM_INNER), lambda i: (0, 0))],
        out_shape=[jax.ShapeDtypeStruct((s, SSM_CONV_DIM), F32), jax.ShapeDtypeStruct((s, LANES), F32),
                   jax.ShapeDtypeStruct((8, LANES), F32), jax.ShapeDtypeStruct((8, LANES), F32),
                   jax.ShapeDtypeStruct((8, SSM_INNER), F32)],
        scratch_shapes=[pltpu.VMEM((SSM_INNER, SSM_STATE), F32), pltpu.VMEM((LANES, q), F32)],
        compiler_params=_cparams(("arbitrary",)))(xbc, dt_raw, dt_bias, a_log, d_e, hin, dy, *mats)


MM = (1024, 1024, 2048)
MM_TN = (1024, 1024, 4096)


def _relu2_epilogue(acc, ex, outs):
    r = jnp.maximum(acc, 0.0)
    outs[0][...] = (r * r).astype(BF16)
    outs[1][...] = r.astype(BF16)


def _relu2_bwd_epilogue(acc, ex, outs):
    outs[0][...] = (acc * (2.0 * ex[0][...].astype(F32))).astype(BF16)


def _add_epilogue(acc, ex, outs):
    outs[0][...] = acc + ex[0][...]


def _tile_spec(tm, tn):
    return pl.BlockSpec((tm, tn), lambda i, j, k: (i, j))


def _sum8(acc):
    return jnp.sum(acc, axis=0)


def _row(v):
    return v.reshape(1, -1)


def _ffn_fwd(xin, norm_g, sc, sh, gate, w, l):
    tm, tn, tk = MM
    h = _norm_mod_fwd(xin, norm_g, sc, sh, f"norm_ffn_fwd_{l}")
    w1 = w("ffn_w1", h)
    per = w1.shape[-1] // tn
    f, r = _mm_nn(f"mm_ffn1_{l}", h, w1, *MM, out_dtypes=(BF16, BF16), epilogue=_relu2_epilogue,
                  b_tile=lambda j, k: (j // per, l, 0, j % per), n=FFN_HIDDEN)
    xout, yf = _mm_resid(f"mm_ffn2_{l}", f, w("ffn_w2", f), xin, gate, *MM, b_tile=lambda j, k: (k, l, 0, j), n=D_MODEL)
    return xout, (h, f, r, yf)


def _ffn_bwd(dxo, dyf, xin, saved, norm_g, sc, w, sink, l, prev):
    h, f, r, yf = saved
    tm, tn, tk = MM
    w1, w2 = w("ffn_w1"), w("ffn_w2")
    per = w2.shape[-2] // tn
    da = _mm_nt(f"mm_ffn2_bwd_{l}", dyf, w2, *MM, extras=(r,), extra_specs=(_tile_spec(tm, tn),),
                epilogue=_relu2_bwd_epilogue, b_tile=lambda j, k: (j // per, l, j % per, 0), n=FFN_HIDDEN)[0]
    tok = sink(f"ffn_w2_{l}", _mm_tn(f"mm_ffn2_dw_{l}", f, dyf, *MM_TN), "rows")
    dh = _mm_nt(f"mm_ffn1_bwd_{l}", da, w1, *MM, out_dtypes=(F32,), b_tile=lambda j, k: (k, l, j, 0), n=D_MODEL,
                extras=(tok.reshape(1, 1),), extra_specs=(ANY,))[0]
    tok = tok + sink(f"ffn_w1_{l}", _mm_tn(f"mm_ffn1_dw_{l}", h, da, *MM_TN, col_chunks=N_CHIPS), "slots")
    dxin, dsh, dsc, dng, dy_prev, dgate_prev = _norm_mod_bwd(xin, dh, dxo, norm_g + tok, sc, f"norm_ffn_bwd_{l}", prev)
    return dxin, dy_prev, dgate_prev, (_sum8(dsh), _sum8(dsc), _sum8(dng))


def _local_step(x, target, mod, w, p, sink):
    s, d = x.shape
    tm, tn, tk = MM
    mods = [[_row(mod[l, j * d:(j + 1) * d]) for j in range(6)] for l in range(2)]
    g = {}

    sh1, sc1, g1, sh2, sc2, g2 = mods[0]
    nm0, nf0 = _row(p["norm_mix"][0]), _row(p["norm_ffn"][0])
    h0 = _norm_mod_fwd(x, nm0, sc1, sh1, "norm_mix_fwd_0")
    slabs = _mm_core(
        "mm_even_in", h0, w("even_in", h0), dims=NN, grid=(s // tm, N_SLABS, 1),
        a_spec=pl.BlockSpec((tm, d), lambda i, j, k: (i, 0)), b_spec=pl.BlockSpec((d, ATTN_W), lambda i, j, k: (0, j)),
        out_shapes=[jax.ShapeDtypeStruct((N_SLABS, s, ATTN_W), BF16)],
        out_specs=[pl.BlockSpec((None, tm, ATTN_W), lambda i, j, k: (j, i, 0))], acc_shape=None)[0]
    outs, lses = zip(*[_attn_fwd(slabs, gi, f"attn_fwd_{gi}") for gi in range(3)])
    attn, lse_tot = _attn_merge(outs, lses, "attn_merge")
    pool_scale = _row(p["pool_scale"])
    ap, pool_d = _pool_fwd(slabs, attn, p["pool_w"], pool_scale, "pool_fwd")
    x1, y0 = _mm_resid("mm_even_out", ap, w("even_out", ap), x, g1, *MM)
    x2, ffn0 = _ffn_fwd(x1, nf0, sc2, sh2, g2, w, 0)

    sh1b, sc1b, g1b, sh2b, sc2b, g2b = mods[1]
    nm1, nf1 = _row(p["norm_mix"][1]), _row(p["norm_ffn"][1])
    h1 = _norm_mod_fwd(x2, nm1, sc1b, sh1b, "norm_mix_fwd_1")
    z = _mm_nn("mm_ssm_z", h1, w("ssm_z", h1), *MM)[0]
    xbc_raw = _mm_nn("mm_ssm_xbc", h1, w("ssm_xbc", h1), *MM)[0]
    dt_raw = _mm_nn("mm_ssm_dt", h1, w("ssm_dt", h1), tm, LANES, tk, out_dtypes=(F32,))[0]
    xbc = _conv_fwd(xbc_raw, p["conv_w"], _row(p["conv_b"]), "conv_fwd")
    y_ssd, hin = _ssd_fwd(xbc, dt_raw, p["dt_bias"], p["a_log"], p["d_e"], "ssd_fwd")
    ssm_norm = _row(p["ssm_norm"])
    yn = _gated_norm_fwd(y_ssd, z, ssm_norm, "gated_norm_fwd")
    x3, y1 = _mm_resid("mm_ssm_out", yn, w("ssm_out", yn), x2, g1b, *MM)
    x4, ffn1 = _ffn_fwd(x3, nf1, sc2b, sh2b, g2b, w, 1)

    dx4, sq, dfn, dyf1, dg2b = _final_loss(x4, _row(p["final_norm"]), target, (ffn1[3], g2b), "final_loss")
    loss_share = (0.5 / d) * jnp.sum(sq)
    g["final_norm"] = _sum8(dfn)
    dg2b = _sum8(dg2b)

    dx3, dy1, dg1b, (dsh2b, dsc2b, dnf1) = _ffn_bwd(dx4, dyf1, x3, ffn1, nf1, sc2b, w, sink, 1, (y1, g1b))
    dyn = _mm_nt("mm_ssm_out_bwd", dy1, w("ssm_out"), *MM, out_dtypes=(F32,))[0]
    tok = sink("ssm_out", _mm_tn("mm_ssm_out_dw", yn, dy1, *MM_TN), "rows")
    dy_ssd, dz, dgn = _gated_norm_bwd(y_ssd, z, dyn, ssm_norm + tok, "gated_norm_bwd")
    g["ssm_norm"] = _sum8(dgn)
    dact, ddt, da_acc, db_acc, dd_acc = _ssd_bwd(xbc, dt_raw, p["dt_bias"], p["a_log"], p["d_e"], hin, dy_ssd, "ssd_bwd")
    g["dt_bias"] = _sum8(db_acc)[:SSM_HEADS]
    g["a_log"] = _sum8(da_acc)[:SSM_HEADS] * (-jnp.exp(p["a_log"][0, :SSM_HEADS]))
    g["ssm_d"] = jnp.sum(_sum8(dd_acc).reshape(SSM_HEADS, HEAD_DIM), axis=1)
    dxbc, dcw, dcb = _conv_bwd(xbc_raw, dact, p["conv_w"], _row(p["conv_b"]), "conv_bwd")
    g["conv_w"] = jnp.sum(dcw, axis=1)
    g["conv_b"] = _sum8(dcb)
    dh1 = _mm_nt_sum("mm_ssm_in_bwd", [(dz, w("ssm_z"), tk), (dxbc, w("ssm_xbc"), tk), (ddt, w("ssm_dt"), tk)], tm, tn)
    tok = sink("ssm_in", jnp.concatenate(
        [_mm_tn("mm_ssm_z_dw", h1, dz, *MM_TN), _mm_tn("mm_ssm_xbc_dw", h1, dxbc, *MM_TN),
         _mm_tn("mm_ssm_dt_dw", h1, ddt, MM_TN[0], LANES, MM_TN[2])[:, :SSM_HEADS]], axis=1), "cols")
    dx2, dsh1b, dsc1b, dnm1, dyf0, dg2 = _norm_mod_bwd(x2, dh1, dx3, nm1 + tok, sc1b, "norm_mix_bwd_1", (ffn0[3], g2))
    dmod1 = jnp.concatenate([_sum8(dsh1b), _sum8(dsc1b), _sum8(dg1b), dsh2b, dsc2b, dg2b])
    dg2 = _sum8(dg2)

    dx1, dy0, dg1, (dsh2, dsc2, dnf0) = _ffn_bwd(dx2, dyf0, x1, ffn0, nf0, sc2, w, sink, 0, (y0, g1))
    dap = _mm_nt("mm_even_out_bwd", dy0, w("even_out"), *MM)[0]
    tok = sink("even_out", _mm_tn("mm_even_out_dw", ap, dy0, *MM_TN), "cols")
    du, dpw, dpsc = _pool_bwd(dap, pool_d, p["pool_w"], pool_scale + tok, "pool_bwd")
    g["pool_w"] = dpw
    g["pool_scale"] = _sum8(dpsc)
    dqkv = [_attn_bwd(slabs, dap, ap, lse_tot, gi, f"attn_bwd_{gi}") for gi in range(3)]
    dproj = jnp.concatenate([dqkv[gi][t] for t in range(3) for gi in range(3)] + [du], axis=1)
    dh0 = _mm_nt("mm_even_in_bwd", dproj, w("even_in"), tm, tn, N_SLABS * ATTN_W // 2, out_dtypes=(F32,))[0]
    tok = sink("even_in", _mm_tn("mm_even_in_dw", dproj, h0, *MM_TN), "cols_t")
    gx, dsh1, dsc1, dnm0 = _norm_mod_bwd(x, dh0, dx1, nm0 + tok, sc1, "norm_mix_bwd_0")
    dmod0 = jnp.concatenate([_sum8(dsh1), _sum8(dsc1), _sum8(dg1), dsh2, dsc2, dg2])

    g["norm_mix"] = jnp.stack([_sum8(dnm0), _sum8(dnm1)])
    g["norm_ffn"] = jnp.stack([dnf0, dnf1])
    return loss_share, gx, g, jnp.stack([dmod0, dmod1])


SMALL_COLS = 512
SMALL_ORDER = ("dmod", "norm_mix", "norm_ffn", "pool_w", "pool_scale", "conv_w", "conv_b", "dt_bias", "a_log",
               "ssm_d", "ssm_norm", "final_norm")


def _cols_to_full(gathered):
    n, k, ns = gathered.shape
    return jnp.transpose(gathered, (1, 0, 2)).reshape(k, n * ns)


def _full_to_cols(full):
    k, n4 = full.shape
    return jnp.transpose(full.reshape(k, N_CHIPS, n4 // N_CHIPS), (1, 0, 2))


def _pack(parts, cols):
    flat = jnp.concatenate([v.reshape(-1) for v in parts])
    rows = -(-flat.shape[0] // (cols * 8)) * 8
    return jnp.pad(flat, (0, rows * cols - flat.shape[0])).reshape(rows, cols)


def _unpack(packed, shapes):
    flat, out, at = packed.reshape(-1), [], 0
    for shp in shapes:
        n = math.prod(shp)
        out.append(flat[at:at + n].reshape(shp))
        at += n
    return out


def kernel(x, c, ada_w, ada_b, norm_mix, norm_ffn, ffn_w1, ffn_w2, even_w_in, pool_w, pool_scale, even_w_out, ssm_w_in, ssm_conv_w, ssm_conv_b, ssm_dt_bias, ssm_a_log, ssm_d, ssm_norm, ssm_w_out, final_norm, loss_target, m_ada_w, m_ada_b, m_norm_mix, m_norm_ffn, m_ffn_w1, m_ffn_w2, m_even_w_in, m_pool_w, m_pool_scale, m_even_w_out, m_ssm_w_in, m_ssm_conv_w, m_ssm_conv_b, m_ssm_dt_bias, m_ssm_a_log, m_ssm_d, m_ssm_norm, m_ssm_w_out, m_final_norm, v_ada_w, v_ada_b, v_norm_mix, v_norm_ffn, v_ffn_w1, v_ffn_w2, v_even_w_in, v_pool_w, v_pool_scale, v_even_w_out, v_ssm_w_in, v_ssm_conv_w, v_ssm_conv_b, v_ssm_dt_bias, v_ssm_a_log, v_ssm_d, v_ssm_norm, v_ssm_w_out, v_final_norm):
    names = ("ada_w", "ada_b", "norm_mix", "norm_ffn", "ffn_w1", "ffn_w2", "even_w_in", "pool_w", "pool_scale",
             "even_w_out", "ssm_w_in", "ssm_conv_w", "ssm_conv_b", "ssm_dt_bias", "ssm_a_log", "ssm_d", "ssm_norm",
             "ssm_w_out", "final_norm")
    wts = dict(zip(names, (ada_w, ada_b, norm_mix, norm_ffn, ffn_w1, ffn_w2, even_w_in, pool_w, pool_scale, even_w_out,
                           ssm_w_in, ssm_conv_w, ssm_conv_b, ssm_dt_bias, ssm_a_log, ssm_d, ssm_norm, ssm_w_out, final_norm)))
    m_in = dict(zip(names, (m_ada_w, m_ada_b, m_norm_mix, m_norm_ffn, m_ffn_w1, m_ffn_w2, m_even_w_in, m_pool_w, m_pool_scale,
                            m_even_w_out, m_ssm_w_in, m_ssm_conv_w, m_ssm_conv_b, m_ssm_dt_bias, m_ssm_a_log, m_ssm_d,
                            m_ssm_norm, m_ssm_w_out, m_final_norm)))
    v_in = dict(zip(names, (v_ada_w, v_ada_b, v_norm_mix, v_norm_ffn, v_ffn_w1, v_ffn_w2, v_even_w_in, v_pool_w, v_pool_scale,
                            v_even_w_out, v_ssm_w_in, v_ssm_conv_w, v_ssm_conv_b, v_ssm_dt_bias, v_ssm_a_log, v_ssm_d,
                            v_ssm_norm, v_ssm_w_out, v_final_norm)))
    d = D_MODEL
    s = x.shape[1]
    ix, iy, ic = _place()
    chip = 2 * ix + iy
    example = 4 * ix + 2 * iy + ic

    c_all = _allgather8(c.reshape(8, d // 8), "gather_c").reshape(N_DEV, d)
    cond = c_all * jax.nn.sigmoid(c_all)
    cond16 = jnp.pad(cond, ((0, 8), (0, 0)))
    ada_cols = ada_w.shape[2]
    bias_shard = lax.dynamic_slice_in_dim(ada_b, chip * ada_cols, ada_cols, axis=1)
    mod_parts = [
        _mm_nn(f"mm_ada_{l}", cond16, ada_w, 16, 512, d, out_dtypes=(F32,), extras=(_row(bias_shard[l]),),
               extra_specs=(pl.BlockSpec((1, 512), lambda i, j, k: (0, j)),), epilogue=_add_epilogue,
               b_tile=lambda j, k, l=l: (l, k, j), n=ada_cols)[0][:8]
        for l in range(2)]
    mod_all = _allgather8(jnp.concatenate(mod_parts, axis=0), "gather_mod").reshape(N_CHIPS, 2, 2, 8, ada_cols)
    mod_mine = lax.dynamic_index_in_dim(mod_all[:, 0], example, axis=2, keepdims=False)
    mod = jnp.transpose(mod_mine, (1, 0, 2)).reshape(2, N_CHIPS * ada_cols)

    cw, nw = ssm_conv_w.shape[2], ssm_norm.shape[1]
    sm = jnp.concatenate([ssm_conv_w[0].reshape(-1), ssm_conv_b.reshape(-1), jnp.pad(ssm_norm[0], (0, cw - nw)),
                          jnp.zeros((2 * cw,), F32)]).reshape(8, cw)
    sm_all = _allgather8(sm, "gather_ssm_small").reshape(N_CHIPS, 2, 8, cw)[:, 0]

    first_shard, sm_all, mod = lax.optimization_barrier((even_w_in, sm_all, mod))
    gathered = _GatheredWeights({"even_in": first_shard, "even_out": even_w_out, "ffn_w1": ffn_w1, "ffn_w2": ffn_w2,
                                 "ssm_in": ssm_w_in, "ssm_out": ssm_w_out})
    mod = mod + gathered.token
    n_xbc = SSM_INNER + SSM_CONV_DIM
    cache = {}

    def full_weight(key, after=None):
        if key in cache:
            return cache[key]
        if key in ("ffn_w1", "ffn_w2"):
            cache[key] = gathered.take(key, after)
        elif key in ("even_in", "even_out"):
            cache[key] = _cols_to_full(gathered.take(key, after)[:, 0])
        elif key == "ssm_out":
            cache[key] = gathered.take(key, after)[:, 0].reshape(SSM_INNER, d)
        else:
            whole = _cols_to_full(gathered.take("ssm_in", after)[:, 0])
            cache["ssm_z"] = whole[:, :SSM_INNER]
            cache["ssm_xbc"] = whole[:, SSM_INNER:n_xbc]
            cache["ssm_dt"] = jnp.pad(whole[:, n_xbc:], ((0, 0), (0, LANES - SSM_HEADS)))
        return cache[key]

    exchange = _GradientExchange()

    def sink(name, grad, layout):
        if layout == "slots":
            slots = grad
        elif layout == "rows":
            slots = grad.reshape(N_CHIPS, -1, grad.shape[-1])
        elif layout == "cols":
            slots = _full_to_cols(grad)
        else:
            slots = jnp.transpose(grad.reshape(N_CHIPS, -1, grad.shape[-1]), (0, 2, 1))
        return exchange.put(name, slots, defer=(name != "even_in"))

    pad_h = ((0, 0), (0, LANES - SSM_HEADS))
    small = {
        "norm_mix": norm_mix, "norm_ffn": norm_ffn, "pool_w": pool_w[0], "pool_scale": pool_scale[0],
        "final_norm": final_norm,
        "conv_w": jnp.transpose(sm_all[:, :CONV_TAPS], (1, 0, 2)).reshape(CONV_TAPS, N_CHIPS * cw),
        "conv_b": sm_all[:, CONV_TAPS].reshape(N_CHIPS * cw),
        "ssm_norm": sm_all[:, CONV_TAPS + 1, :nw].reshape(N_CHIPS * nw),
        "dt_bias": jnp.pad(ssm_dt_bias, pad_h), "a_log": jnp.pad(ssm_a_log, pad_h),
        "d_e": jnp.repeat(ssm_d[0], HEAD_DIM).reshape(1, SSM_INNER),
    }

    loss_share, gx, g, dmod = _local_step(x[0], loss_target[0], mod, full_weight, small, sink)
    loss = lax.psum(loss_share, ("x", "y", "c"))
    reduced = exchange.finish(gx)

    g["dmod"] = dmod
    small_shapes = [g[k].shape for k in SMALL_ORDER]
    packed = _pack([g[k] for k in SMALL_ORDER], SMALL_COLS)
    every = _allgather8(packed, "gather_small").reshape(N_DEV, *packed.shape)
    summed = dict(zip(SMALL_ORDER, _unpack(_sum_devices(every, "sum_small"), small_shapes)))
    dmod_all = every.reshape(N_DEV, -1)[:, :2 * 6 * d].reshape(N_DEV, 2, 6 * d)

    grads = {}
    dmod_shard = lax.dynamic_slice_in_dim(dmod_all, chip * ada_cols, ada_cols, axis=2)
    grads["ada_w"] = jnp.stack([
        _mm_tn(f"mm_ada_dw_{l}", cond16, jnp.pad(dmod_shard[:, l], ((0, 8), (0, 0))), 1024, 512, 16, out_dtype=F32)
        for l in range(2)])
    grads["ada_b"] = summed["dmod"]
    grads["norm_mix"] = summed["norm_mix"]
    grads["norm_ffn"] = summed["norm_ffn"]
    grads["pool_w"] = summed["pool_w"][None]
    grads["pool_scale"] = summed["pool_scale"][None]
    grads["ssm_conv_w"] =lax.dynamic_slice_in_dim(summed["conv_w"], chip * cw, cw, axis=1)[None]
    grads["ssm_conv_b"] = lax.dynamic_slice_in_dim(summed["conv_b"], chip * cw, cw, axis=0)[None]
    grads["ssm_dt_bias"] = summed["dt_bias"][None]
    grads["ssm_a_log"] = summed["a_log"][None]
    grads["ssm_d"] = summed["ssm_d"][None]
    grads["ssm_norm"] =lax.dynamic_slice_in_dim(summed["ssm_norm"], chip * nw, nw, axis=0)[None]
    grads["final_norm"] = summed["final_norm"]

    grads["ffn_w1"] = jnp.stack([reduced["ffn_w1_0"], reduced["ffn_w1_1"]])
    grads["ffn_w2"] = jnp.stack([reduced["ffn_w2_0"], reduced["ffn_w2_1"]])
    grads["even_w_in"] = reduced["even_in"][None]
    grads["even_w_out"] = reduced["even_out"][None]
    grads["ssm_w_in"] = reduced["ssm_in"][None]
    grads["ssm_w_out"] = reduced["ssm_out"][None]

    big = ("ada_w", "ffn_w1", "ffn_w2", "even_w_in", "even_w_out", "ssm_w_in", "ssm_w_out")
    delta, new_m, new_v = {}, {}, {}
    for k in big:
        shp = wts[k].shape
        two_d = lambda t: t.reshape(-1, shp[-1])
        res = _adamw(two_d(wts[k]), two_d(grads[k]), two_d(m_in[k]), two_d(v_in[k]), f"adamw_{k}")
        delta[k], new_m[k], new_v[k] = [t.reshape(shp) for t in res]
    little = [k for k in names if k not in big]
    shapes = [wts[k].shape for k in little]
    res = _adamw(*[_pack([src[k] for k in little], LANES) for src in (wts, grads, m_in, v_in)], "adamw_small")
    for out, packed_out in zip((delta, new_m, new_v), res):
        out.update(zip(little, _unpack(packed_out, shapes)))

    return (loss, gx[None], *[grads[k] for k in names], *[delta[k] for k in names],
            *[new_m[k] for k in names], *[new_v[k] for k in names])
```

```python
import math

import jax
import jax.numpy as jnp
from jax import lax
from jax.experimental import pallas as pl
from jax.experimental.pallas import tpu as pltpu

F32 = jnp.float32
BF16 = jnp.bfloat16
MESH = pl.DeviceIdType.MESH
ANY = pl.BlockSpec(memory_space=pl.ANY)
VMEM_FULL = pl.BlockSpec(memory_space=pltpu.VMEM)

NORM_EPS = 1e-6
N_CHIPS = 4
N_DEV = 8
LANES = 128
VMEM_LIMIT = 56 << 20

D_MODEL = 2048
ATTN_GROUPS = ((128, 1), (512, 4), (2048, 16))
ATTN_BLOCK = 128
ATTN_W = 512
HEAD_DIM = 64
POOL_WINDOWS = (2, 4, 8, 16)
POOL_W = 512
N_SLABS = 10
SSM_INNER = 4096
SSM_HEADS = 64
SSM_GROUPS = 8
SSM_STATE = 128
SSM_CHUNK = 128
SSM_CONV_DIM = 6144
SSM_GW = SSM_INNER // SSM_GROUPS
FFN_HIDDEN = 8192

ADAM_LR, ADAM_B1, ADAM_B2, ADAM_EPS, ADAM_WD, ADAM_STEP = 0.001, 0.9, 0.999, 1e-08, 0.01, 10


def _cparams(sem=None):
    return pltpu.CompilerParams(dimension_semantics=sem, vmem_limit_bytes=VMEM_LIMIT)


def _place():
    return lax.axis_index("x"), lax.axis_index("y"), lax.axis_index("c")


def _chip_index():
    return 2 * lax.axis_index("x") + lax.axis_index("y")


def _allgather8(v, name):
    m_per, n = v.shape

    def body(x_ref, out_ref, send_sems, recv_sems, local_sem):
        x, y, c = _place()
        me, sibling = (x, y, c), (x, y, 1 - c)
        chips = [(1 - x, y), (x, 1 - y), (1 - x, 1 - y)]

        def rows(px, py, pc):
            return out_ref.at[pl.ds((4 * px + 2 * py + pc) * m_per, m_per), :]

        def copy(k, block, to, src=None):
            return pltpu.make_async_remote_copy(
                src_ref=rows(*block) if src is None else src, dst_ref=rows(*block),
                send_sem=send_sems.at[k], recv_sem=recv_sems.at[k], device_id=to, device_id_type=MESH)

        mine = pltpu.make_async_copy(x_ref, rows(*me), local_sem)
        mine.start()
        first = [copy(0, me, sibling, src=x_ref)]
        first += [copy(1 + j, me, (*chip, c), src=x_ref) for j, chip in enumerate(chips)]
        for cp in first:
            cp.start()
        passed = [copy(4 + j, (*chip, c), sibling) for j, chip in enumerate(chips)]
        for j, chip in enumerate(chips):
            copy(1 + j, (*chip, c), me).wait_recv()
            passed[j].start()
        copy(0, sibling, me).wait_recv()
        for j, chip in enumerate(chips):
            copy(4 + j, (*chip, 1 - c), me).wait_recv()
        for cp in first + passed:
            cp.wait_send()
        mine.wait()

    return pl.pallas_call(
        body, name=name,
        out_shape=jax.ShapeDtypeStruct((N_DEV * m_per, n), v.dtype),
        in_specs=[VMEM_FULL], out_specs=VMEM_FULL,
        scratch_shapes=[pltpu.SemaphoreType.DMA((7,)), pltpu.SemaphoreType.DMA((7,)), pltpu.SemaphoreType.DMA],
    )(v)


def _sibling_send_other_half(g, name):
    n, r, ccols = g.shape
    h = r // 2

    def body(g_ref, out_ref, send_sem, recv_sem):
        x, y, c = _place()
        cp = pltpu.make_async_remote_copy(
            src_ref=g_ref.at[:, pl.ds((1 - c) * h, h), :], dst_ref=out_ref, send_sem=send_sem, recv_sem=recv_sem,
            device_id=(x, y, 1 - c), device_id_type=MESH)
        cp.start()
        cp.wait()

    return pl.pallas_call(
        body, name=name, out_shape=jax.ShapeDtypeStruct((n, h, ccols), g.dtype),
        in_specs=[ANY], out_specs=ANY,
        scratch_shapes=[pltpu.SemaphoreType.DMA, pltpu.SemaphoreType.DMA],
    )(g)


def _sibling_complete(full, name):
    r, ccols = full.shape
    h = r // 2

    def body(in_ref, out_ref, send_sem, recv_sem):
        x, y, c = _place()

        def copy(rows):
            return pltpu.make_async_remote_copy(
                src_ref=in_ref.at[pl.ds(c * h, h), :], dst_ref=out_ref.at[rows, :], send_sem=send_sem,
                recv_sem=recv_sem, device_id=(x, y, 1 - c), device_id_type=MESH)

        mine = copy(pl.ds(c * h, h))
        mine.start()
        copy(pl.ds((1 - c) * h, h)).wait_recv()
        mine.wait_send()

    return pl.pallas_call(
        body, name=name, out_shape=jax.ShapeDtypeStruct(full.shape, full.dtype), in_specs=[ANY], out_specs=ANY,
        input_output_aliases={0: 0}, scratch_shapes=[pltpu.SemaphoreType.DMA, pltpu.SemaphoreType.DMA],
    )(full)


HBM_SPEC = pl.BlockSpec(memory_space=pltpu.HBM)
SEM_SPEC = pl.BlockSpec(memory_space=pltpu.SEMAPHORE)
N_PEERS = 3


def _split_params():
    return pltpu.CompilerParams(has_side_effects=pltpu.SideEffectType.DATAFLOW_SIDE_EFFECTING)


def _n_peers(mode):
    return 1 if mode == "pair" else N_PEERS


def _chip_exchange(mode, src_ref, land_ref, send_sems, recv_sems):
    x, y, c = _place()
    k_me = 2 * x + y
    if mode == "pair":
        h = src_ref.shape[1] // 2
        return [tuple(
            pltpu.make_async_remote_copy(src_ref=src_ref.at[:, pl.ds((1 - c) * h, h), :], dst_ref=land_ref,
                                         send_sem=send_sems[0], recv_sem=recv_sems[0], device_id=(x, y, 1 - c),
                                         device_id_type=MESH) for _ in range(2))]
    pairs = []
    for j, chip in enumerate([(1 - x, y), (x, 1 - y), (1 - x, 1 - y)]):
        k_j = 2 * chip[0] + chip[1]
        if mode == "halves":
            h = src_ref.shape[0] // 2
            rows = pl.ds(c * h, h)
            src, there, here = src_ref.at[rows, :], land_ref.at[k_me, rows, :], land_ref.at[k_j, rows, :]
        else:
            src, there, here = src_ref.at[k_j], land_ref.at[k_me], land_ref.at[k_j]
        pairs.append(tuple(
            pltpu.make_async_remote_copy(src_ref=src, dst_ref=dst, send_sem=send_sems[j], recv_sem=recv_sems[j],
                                         device_id=(*chip, c), device_id_type=MESH) for dst in (there, here)))
    return pairs


def _exchange_start(src, land_shape, mode, name):
    n = _n_peers(mode)

    def body(src_ref, land_ref, *rest):
        sems, token = rest[:2 * n], rest[2 * n + 2]
        for start, _ in _chip_exchange(mode, src_ref, land_ref, sems[:n], sems[n:]):
            start.start()
        token[...] = jnp.zeros_like(token)

    sem = pltpu.SemaphoreType.DMA(())
    res = pl.pallas_call(
        body, name=name,
        out_shape=(sem,) * (2 * n) + (pltpu.HBM(src.shape, src.dtype), pltpu.HBM(land_shape, src.dtype),
                                     jax.ShapeDtypeStruct((8, LANES), F32)),
        in_specs=(HBM_SPEC, HBM_SPEC), out_specs=(SEM_SPEC,) * (2 * n) + (HBM_SPEC, HBM_SPEC, VMEM_FULL),
        input_output_aliases={0: 2 * n, 1: 2 * n + 1}, compiler_params=_split_params(),
    )(pltpu.with_memory_space_constraint(src, pltpu.HBM),
      pltpu.with_memory_space_constraint(lax.empty(land_shape, src.dtype), pltpu.HBM))
    return res[:2 * n], res[2 * n], res[2 * n + 1], res[2 * n + 2]


def _exchange_wait(sems, src_thru, land_thru, after, mode, name):
    n = _n_peers(mode)

    def body(src_ref, land_ref, *rest):
        sem_refs = rest[:2 * n]
        waits = [w for _, w in _chip_exchange(mode, src_ref, land_ref, sem_refs[:n], sem_refs[n:])]
        for w in waits:
            w.wait_send()
        for w in waits:
            w.wait_recv()

    return pl.pallas_call(
        body, name=name,
        out_shape=(pltpu.HBM(src_thru.shape, src_thru.dtype), pltpu.HBM(land_thru.shape, land_thru.dtype)),
        in_specs=(HBM_SPEC, HBM_SPEC) + (SEM_SPEC,) * (2 * n) + (ANY,), out_specs=(HBM_SPEC, HBM_SPEC),
        input_output_aliases={0: 0, 1: 1}, compiler_params=_split_params(),
    )(src_thru, land_thru, *sems, after)


def _sibling_fill(land, name):
    n, r, ccols = land.shape
    h = r // 2

    def body(land_ref, out_ref, send_sems, recv_sems):
        x, y, c = _place()
        slots = [2 * (1 - x) + y, 2 * x + (1 - y), 2 * (1 - x) + (1 - y)]

        def copy(j, rows):
            return pltpu.make_async_remote_copy(
                src_ref=land_ref.at[slots[j], pl.ds(c * h, h), :], dst_ref=out_ref.at[slots[j], rows, :],
                send_sem=send_sems.at[j], recv_sem=recv_sems.at[j], device_id=(x, y, 1 - c), device_id_type=MESH)

        sends = [copy(j, pl.ds(c * h, h)) for j in range(N_PEERS)]
        for cp in sends:
            cp.start()
        for j in range(N_PEERS):
            copy(j, pl.ds((1 - c) * h, h)).wait_recv()
        for cp in sends:
            cp.wait_send()

    return pl.pallas_call(
        body, name=name, out_shape=jax.ShapeDtypeStruct(land.shape, land.dtype), in_specs=[ANY], out_specs=ANY,
        input_output_aliases={0: 0},
        scratch_shapes=[pltpu.SemaphoreType.DMA((N_PEERS,)), pltpu.SemaphoreType.DMA((N_PEERS,))],
    )(land)


class _GatheredWeights:
    def __init__(self, shards):
        self._pending, self._done = {}, {}
        token = jnp.zeros((), BF16)
        for name, shard in shards.items():
            flat = shard.reshape(-1, shard.shape[-1]).astype(BF16) + token
            sems, thru, land, tok = _exchange_start(flat, (N_CHIPS, *flat.shape), "halves", f"gather_start_{name}")
            self._pending[name] = (sems, thru, land, shard.shape)
            token = tok[0, 0].astype(BF16)
        self.token = tok[0, 0]

    def take(self, name, after):
        if name not in self._done:
            sems, thru, land, shape = self._pending.pop(name)
            flat, land = _exchange_wait(sems, thru, land, after, "halves", f"gather_wait_{name}")
            land = _sibling_fill(land, f"gather_fill_{name}")
            land = lax.dynamic_update_index_in_dim(land, flat, _chip_index(), 0)
            self._done[name] = land.reshape(N_CHIPS, *shape)
        return self._done[name]


class _GradientExchange:
    def __init__(self):
        self._pair, self._pending = None, []

    def _start_all_to_all(self, name, slots, recv):
        part = _add_own_half(slots, recv, f"rs_add2_{name}")
        sems, thru, land, tok = _exchange_start(part, part.shape, "slots", f"rs_start_{name}")
        self._pending.append((name, sems, thru, land))
        return tok[0, 0]

    def _advance(self, after):
        if self._pair is None:
            return jnp.zeros((), F32)
        name, sems, thru, land = self._pair
        self._pair = None
        slots, recv = _exchange_wait(sems, thru, land, after, "pair", f"rs_pair_wait_{name}")
        return self._start_all_to_all(name, slots, recv)

    def put(self, name, slots, defer=True):
        tok = self._advance(slots)
        n, r, ccols = slots.shape
        if defer:
            sems, thru, land, ptok = _exchange_start(slots, (n, r // 2, ccols), "pair", f"rs_pair_start_{name}")
            self._pair = (name, sems, thru, land)
            return tok + ptok[0, 0]
        recv = _sibling_send_other_half(slots, f"rs_pair_{name}")
        return tok + self._start_all_to_all(name, slots, recv)

    def finish(self, after):
        self._advance(after)
        out = {}
        k_me = _chip_index()
        for name, sems, thru, land in self._pending:
            part, land = _exchange_wait(sems, thru, land, after, "slots", f"rs_wait_{name}")
            got = lax.dynamic_update_index_in_dim(land, lax.dynamic_index_in_dim(part, k_me, 0, keepdims=False), k_me, 0)
            out[name] = _sibling_complete(_sum_slots(got, f"rs_add4_{name}"), f"rs_fin_{name}")
        return out


def _row_tile(rows, cols, itemsize, budget=2 << 20):
    t = rows
    while t % 2 == 0 and t * cols * itemsize > budget and (t // 2) % 16 == 0:
        t //= 2
    return t


def _add_own_half(g, recv, name):
    n, r, ccols = g.shape
    h = r // 2
    t = _row_tile(h, ccols, 4)
    nt = h // t
    c_idx = lax.axis_index("c").astype(jnp.int32).reshape(1)

    def body(c_ref, g_ref, r_ref, o_ref):
        o_ref[...] = (g_ref[...].astype(F32) + r_ref[...].astype(F32)).astype(o_ref.dtype)

    grid_spec = pltpu.PrefetchScalarGridSpec(
        num_scalar_prefetch=1, grid=(n, nt),
        in_specs=[pl.BlockSpec((None, t, ccols), lambda j, i, c_ref: (j, c_ref[0] * nt + i, 0)),
                  pl.BlockSpec((None, t, ccols), lambda j, i, c_ref: (j, i, 0))],
        out_specs=pl.BlockSpec((None, t, ccols), lambda j, i, c_ref: (j, i, 0)))
    return pl.pallas_call(body, name=name, grid_spec=grid_spec,
                          out_shape=jax.ShapeDtypeStruct((n, h, ccols), BF16),
                          compiler_params=_cparams(("parallel", "parallel")))(c_idx, g, recv)


def _sum_slots(q, name):
    n, h, ccols = q.shape
    t = _row_tile(h, ccols, 4)
    nt = h // t
    c_idx = lax.axis_index("c").astype(jnp.int32).reshape(1)

    def body(c_ref, q_ref, o_ref):
        acc = q_ref[0].astype(F32)
        for j in range(1, n):
            acc = acc + q_ref[j].astype(F32)
        o_ref[...] = acc

    grid_spec = pltpu.PrefetchScalarGridSpec(
        num_scalar_prefetch=1, grid=(nt,),
        in_specs=[pl.BlockSpec((n, t, ccols), lambda i, c_ref: (0, i, 0))],
        out_specs=pl.BlockSpec((t, ccols), lambda i, c_ref: (c_ref[0] * nt + i, 0)))
    return pl.pallas_call(body, name=name, grid_spec=grid_spec, out_shape=jax.ShapeDtypeStruct((2 * h, ccols), F32),
                          compiler_params=_cparams(("parallel",)))(c_idx, q)


def _sum_devices(v, name):
    n, r, ccols = v.shape
    t = 8
    while r % (t * 2) == 0 and t * 2 * ccols * 4 * n <= (8 << 20):
        t *= 2

    def body(v_ref, o_ref):
        acc = v_ref[0]
        for j in range(1, n):
            acc = acc + v_ref[j]
        o_ref[...] = acc

    return pl.pallas_call(
        body, name=name, grid=(r // t,),
        in_specs=[pl.BlockSpec((n, t, ccols), lambda i: (0, i, 0))],
        out_specs=pl.BlockSpec((t, ccols), lambda i: (i, 0)),
        out_shape=jax.ShapeDtypeStruct((r, ccols), F32), compiler_params=_cparams(("parallel",)))(v)


def _adamw(w, g, m, v, name):
    r, ccols = w.shape
    t = _row_tile(r, ccols, 4, budget=1 << 20)
    c1 = 1.0 / (1.0 - ADAM_B1 ** ADAM_STEP)
    c2 = 1.0 / (1.0 - ADAM_B2 ** ADAM_STEP)

    def body(w_ref, g_ref, m_ref, v_ref, d_ref, nm_ref, nv_ref):
        gg = g_ref[...]
        nm = ADAM_B1 * m_ref[...] + (1.0 - ADAM_B1) * gg
        nv = ADAM_B2 * v_ref[...] + (1.0 - ADAM_B2) * (gg * gg)
        d_ref[...] = -ADAM_LR * ((nm * c1) / (jnp.sqrt(nv * c2) + ADAM_EPS) + ADAM_WD * w_ref[...])
        nm_ref[...] = nm
        nv_ref[...] = nv

    spec = pl.BlockSpec((t, ccols), lambda i: (i, 0))
    sds = jax.ShapeDtypeStruct((r, ccols), F32)
    return pl.pallas_call(body, name=name, grid=(r // t,), in_specs=[spec] * 4, out_specs=[spec] * 3,
                          out_shape=[sds] * 3, compiler_params=_cparams(("parallel",)))(w, g, m, v)


NN = (((1,), (0,)), ((), ()))
NT = (((1,), (1,)), ((), ()))
TN = (((0,), (0,)), ((), ()))


def _mm_core(name, a, b, *, dims, grid, a_spec, b_spec, out_shapes, out_specs, acc_shape,
             extras=(), extra_specs=(), epilogue=None):
    nk = grid[2]
    n_ex, n_out = len(extras), len(out_shapes)
    if epilogue is None:
        def epilogue(acc, ex, outs):
            outs[0][...] = acc.astype(outs[0].dtype)

    def body(*refs):
        a_ref, b_ref = refs[0], refs[1]
        ex = refs[2:2 + n_ex]
        outs = refs[2 + n_ex:2 + n_ex + n_out]
        part = lax.dot_general(a_ref[...].astype(BF16), b_ref[...].astype(BF16), dims, preferred_element_type=F32)
        if nk == 1:
            epilogue(part, ex, outs)
        else:
            acc = refs[-1]
            k = pl.program_id(2)

            @pl.when(k == 0)
            def _():
                acc[...] = part

            @pl.when(k > 0)
            def _():
                acc[...] += part

            @pl.when(k == nk - 1)
            def _():
                epilogue(acc[...], ex, outs)

    scratch = [] if nk == 1 else [pltpu.VMEM(acc_shape, F32)]
    res = pl.pallas_call(
        body, name=name, grid=grid, in_specs=[a_spec, b_spec, *extra_specs], out_specs=list(out_specs),
        out_shape=list(out_shapes), scratch_shapes=scratch,
        compiler_params=_cparams(("parallel", "parallel", "arbitrary")))(a, b, *extras)
    return res


def _mm_nn(name, a, b, tm, tn, tk, out_dtypes=(BF16,), extras=(), extra_specs=(), epilogue=None, b_tile=None, n=None):
    m, kk = a.shape
    n = b.shape[1] if b_tile is None else n
    tm, tn, tk = min(tm, m), min(tn, n), min(tk, kk)
    grid = (m // tm, n // tn, kk // tk)
    if b_tile is None:
        b_spec = pl.BlockSpec((tk, tn), lambda i, j, k: (k, j))
    else:
        b_spec = pl.BlockSpec((None,) * (b.ndim - 2) + (tk, tn), lambda i, j, k: b_tile(j, k))
    return _mm_core(
        name, a, b, dims=NN, grid=grid,
        a_spec=pl.BlockSpec((tm, tk), lambda i, j, k: (i, k)), b_spec=b_spec,
        out_shapes=[jax.ShapeDtypeStruct((m, n), dt) for dt in out_dtypes],
        out_specs=[pl.BlockSpec((tm, tn), lambda i, j, k: (i, j)) for _ in out_dtypes],
        acc_shape=(tm, tn), extras=extras, extra_specs=extra_specs, epilogue=epilogue)


def _mm_nt(name, a, b, tm, tn, tk, out_dtypes=(BF16,), extras=(), extra_specs=(), epilogue=None, b_tile=None, n=None):
    m, kk = a.shape
    n = b.shape[0] if b_tile is None else n
    tm, tn, tk = min(tm, m), min(tn, n), min(tk, kk)
    grid = (m // tm, n // tn, kk // tk)
    if b_tile is None:
        b_spec = pl.BlockSpec((tn, tk), lambda i, j, k: (j, k))
    else:
        b_spec = pl.BlockSpec((None,) * (b.ndim - 2) + (tn, tk), lambda i, j, k: b_tile(j, k))
    return _mm_core(
        name, a, b, dims=NT, grid=grid,
        a_spec=pl.BlockSpec((tm, tk), lambda i, j, k: (i, k)), b_spec=b_spec,
        out_shapes=[jax.ShapeDtypeStruct((m, n), dt) for dt in out_dtypes],
        out_specs=[pl.BlockSpec((tm, tn), lambda i, j, k: (i, j)) for _ in out_dtypes],
        acc_shape=(tm, tn), extras=extras, extra_specs=extra_specs, epilogue=epilogue)


def _mm_tn(name, a, b, tm, tn, tk, out_dtype=BF16, col_chunks=1):
    kk, m = a.shape
    n = b.shape[1]
    tm, tn, tk = min(tm, m), min(tn, n), min(tk, kk)
    grid = (m // tm, n // tn, kk // tk)
    if col_chunks == 1:
        out_shape, out_spec = (m, n), pl.BlockSpec((tm, tn), lambda i, j, k: (i, j))
    else:
        per = n // col_chunks // tn
        out_shape = (col_chunks, m, n // col_chunks)
        out_spec = pl.BlockSpec((None, tm, tn), lambda i, j, k: (j // per, i, j % per))
    return _mm_core(
        name, a, b, dims=TN, grid=grid,
        a_spec=pl.BlockSpec((tk, tm), lambda i, j, k: (k, i)), b_spec=pl.BlockSpec((tk, tn), lambda i, j, k: (k, j)),
        out_shapes=[jax.ShapeDtypeStruct(out_shape, out_dtype)], out_specs=[out_spec], acc_shape=(tm, tn))[0]


def _resid_gate_epilogue(acc, ex, outs):
    outs[0][...] = ex[0][...] + ex[1][...] * acc
    outs[1][...] = acc.astype(BF16)


def _mm_resid(name, a, b, resid, gate, tm, tn, tk, b_tile=None, n=None):
    return _mm_nn(
        name, a, b, tm, tn, tk, out_dtypes=(F32, BF16), extras=(resid, gate),
        extra_specs=(pl.BlockSpec((tm, tn), lambda i, j, k: (i, j)), pl.BlockSpec((1, tn), lambda i, j, k: (0, j))),
        epilogue=_resid_gate_epilogue, b_tile=b_tile, n=n)


TOK_TILE = 512


def _colsum8(v):
    t, ccols = v.shape
    return jnp.sum(v.reshape(t // 8, 8, ccols), axis=0)


def _norm_mod_fwd(x, g, sc, sh, name):
    s, d = x.shape
    t = TOK_TILE

    def body(x_ref, g_ref, sc_ref, sh_ref, h_ref):
        xv = x_ref[...]
        n = xv * lax.rsqrt(jnp.mean(xv * xv, axis=-1, keepdims=True) + NORM_EPS)
        h_ref[...] = ((n * g_ref[...]) * (1.0 + sc_ref[...]) + sh_ref[...]).astype(BF16)

    row = pl.BlockSpec((1, d), lambda i: (0, 0))
    return pl.pallas_call(
        body, name=name, grid=(s // t,), in_specs=[pl.BlockSpec((t, d), lambda i: (i, 0)), row, row, row],
        out_specs=pl.BlockSpec((t, d), lambda i: (i, 0)), out_shape=jax.ShapeDtypeStruct((s, d), BF16),
        compiler_params=_cparams(("parallel",)))(x, g, sc, sh)


def _gate_grads(dx, y_ref, gate_ref, dy_ref, dgate_ref, first):
    dy_ref[...] = (dx * gate_ref[...]).astype(BF16)

    @pl.when(first)
    def _():
        dgate_ref[...] = jnp.zeros_like(dgate_ref)

    dgate_ref[...] += _colsum8(dx * y_ref[...].astype(F32))


def _norm_mod_bwd(x, dh, resid, g, sc, name, prev=None):
    s, d = x.shape
    t = TOK_TILE if prev is None else TOK_TILE // 2

    def body(x_ref, dh_ref, r_ref, g_ref, sc_ref, *rest):
        dx_ref, dsh_ref, dsc_ref, dg_ref = rest[-4:] if prev is None else rest[2:6]
        i = pl.program_id(0)
        xv = x_ref[...]
        rstd = lax.rsqrt(jnp.mean(xv * xv, axis=-1, keepdims=True) + NORM_EPS)
        n = xv * rstd
        dhv = dh_ref[...].astype(F32)
        gv = g_ref[...]
        dyn = dhv * (1.0 + sc_ref[...])
        dn = dyn * gv
        dx = r_ref[...] + rstd * (dn - n * jnp.mean(dn * n, axis=-1, keepdims=True))
        dx_ref[...] = dx

        @pl.when(i == 0)
        def _():
            dsh_ref[...] = jnp.zeros_like(dsh_ref)
            dsc_ref[...] = jnp.zeros_like(dsc_ref)
            dg_ref[...] = jnp.zeros_like(dg_ref)

        dsh_ref[...] += _colsum8(dhv)
        dsc_ref[...] += _colsum8(dhv * (n * gv))
        dg_ref[...] += _colsum8(dyn * n)
        if prev is not None:
            _gate_grads(dx, rest[0], rest[1], rest[6], rest[7], i == 0)

    tile = pl.BlockSpec((t, d), lambda i: (i, 0))
    row = pl.BlockSpec((1, d), lambda i: (0, 0))
    acc = pl.BlockSpec((8, d), lambda i: (0, 0))
    acc_s = jax.ShapeDtypeStruct((8, d), F32)
    in_specs, operands = [tile, tile, tile, row, row], [x, dh, resid, g, sc]
    out_specs, out_shape = [tile, acc, acc, acc], [jax.ShapeDtypeStruct((s, d), F32), acc_s, acc_s, acc_s]
    if prev is not None:
        in_specs, operands = in_specs + [tile, row], operands + list(prev)
        out_specs, out_shape = out_specs + [tile, acc], out_shape + [jax.ShapeDtypeStruct((s, d), BF16), acc_s]
    return pl.pallas_call(
        body, name=name, grid=(s // t,), in_specs=in_specs, out_specs=out_specs, out_shape=out_shape,
        compiler_params=_cparams(("arbitrary",)))(*operands)


def _final_loss(x, g, target, prev, name):
    s, d = x.shape
    t = TOK_TILE
    inv_d = 1.0 / d

    def body(x_ref, g_ref, t_ref, y_ref, gate_ref, dx_ref, sq_ref, dg_ref, dy_ref, dgate_ref):
        i = pl.program_id(0)
        xv = x_ref[...]
        rstd = lax.rsqrt(jnp.mean(xv * xv, axis=-1, keepdims=True) + NORM_EPS)
        n = xv * rstd
        gv = g_ref[...]
        err = n * gv - t_ref[...]
        dout = err * inv_d
        dn = dout * gv
        dx = rstd * (dn - n * jnp.mean(dn * n, axis=-1, keepdims=True))
        dx_ref[...] = dx

        @pl.when(i == 0)
        def _():
            sq_ref[...] = jnp.zeros_like(sq_ref)
            dg_ref[...] = jnp.zeros_like(dg_ref)

        sq_ref[...] += _colsum8(err * err)
        dg_ref[...] += _colsum8(dout * n)
        _gate_grads(dx, y_ref, gate_ref, dy_ref, dgate_ref, i == 0)

    tile = pl.BlockSpec((t, d), lambda i: (i, 0))
    row = pl.BlockSpec((1, d), lambda i: (0, 0))
    acc = pl.BlockSpec((8, d), lambda i: (0, 0))
    acc_s = jax.ShapeDtypeStruct((8, d), F32)
    return pl.pallas_call(
        body, name=name, grid=(s // t,), in_specs=[tile, row, tile, tile, row],
        out_specs=[tile, acc, acc, tile, acc],
        out_shape=[jax.ShapeDtypeStruct((s, d), F32), acc_s, acc_s, jax.ShapeDtypeStruct((s, d), BF16), acc_s],
        compiler_params=_cparams(("arbitrary",)))(x, g, target, *prev)


NEG_BIG = -1e30


def _head_masks(rows):
    lane = lax.broadcasted_iota(jnp.int32, (rows, LANES), 1)
    return [(lane // HEAD_DIM) == hh for hh in range(2)]


def _group_view(slabs, gi):
    dil = ATTN_GROUPS[gi][1]
    _, s, w = slabs.shape
    if dil == 1:
        return slabs, (gi, 3 + gi, 6 + gi)
    return slabs[gi:9:3].reshape(3, s // dil, dil * w), (0, 1, 2)


def _attn_fwd(slabs, gi, name):
    window, dil = ATTN_GROUPS[gi]
    n_back = window // dil
    _, s, w = slabs.shape
    big_l = s // dil
    nb = big_l // ATTN_BLOCK
    blk = ATTN_BLOCK
    view, (iq, ik, iv) = _group_view(slabs, gi)
    scale = HEAD_DIM ** -0.5

    def body(q_ref, kp_ref, kc_ref, vp_ref, vc_ref, o_ref, lse_ref):
        n = pl.program_id(1)
        qi = lax.broadcasted_iota(jnp.int32, (blk, 2 * blk), 0)
        kj = lax.broadcasted_iota(jnp.int32, (blk, 2 * blk), 1)
        dist = qi + blk - kj
        ok = (dist >= 0) & (dist <= n_back) & ((kj >= blk) | (n > 0))
        mq = _head_masks(blk)
        mk = _head_masks(2 * blk)
        for p in range(w // LANES):
            cols = slice(p * LANES, (p + 1) * LANES)
            qp = q_ref[:, cols]
            k2 = jnp.concatenate([kp_ref[:, cols], kc_ref[:, cols]], axis=0)
            v2 = jnp.concatenate([vp_ref[:, cols], vc_ref[:, cols]], axis=0)
            o_pair = jnp.zeros((blk, LANES), F32)
            lse_pair = jnp.zeros((blk, LANES), F32)
            for hh in range(2):
                qm = jnp.where(mq[hh], qp, jnp.zeros_like(qp))
                sc = lax.dot_general(qm, k2, NT, preferred_element_type=F32) * scale
                sc = jnp.where(ok, sc, NEG_BIG)
                mx = jnp.max(sc, axis=-1, keepdims=True)
                pe = jnp.exp(sc - mx)
                den = jnp.sum(pe, axis=-1, keepdims=True)
                pn = (pe / den).astype(BF16)
                vm = jnp.where(mk[hh], v2, jnp.zeros_like(v2))
                o_pair = o_pair + jnp.dot(pn, vm, preferred_element_type=F32)
                lse_pair = jnp.where(mq[hh], mx + jnp.log(den), lse_pair)
            o_ref[:, cols] = o_pair
            lse_ref[:, cols] = lse_pair

    def spec(slab, prev):
        if prev:
            return pl.BlockSpec((None, blk, w), lambda r, n: (slab, jnp.maximum(n - 1, 0), r))
        return pl.BlockSpec((None, blk, w), lambda r, n: (slab, n, r))

    out_spec = pl.BlockSpec((blk, w), lambda r, n: (n, r))
    sds = jax.ShapeDtypeStruct((big_l, dil * w), F32)
    o, lse = pl.pallas_call(
        body, name=name, grid=(dil, nb),
        in_specs=[spec(iq, False), spec(ik, True), spec(ik, False), spec(iv, True), spec(iv, False)],
        out_specs=[out_spec, out_spec], out_shape=[sds, sds],
        compiler_params=_cparams(("parallel", "arbitrary")))(view, view, view, view, view)
    return o.reshape(s, w), lse.reshape(s, w)


def _attn_merge(outs, lses, name):
    s, w = outs[0].shape
    t = TOK_TILE

    def body(o0, o1, o2, l0, l1, l2, a_ref, lt_ref):
        ls = [l0[...], l1[...], l2[...]]
        mx = jnp.maximum(jnp.maximum(ls[0], ls[1]), ls[2])
        es = [jnp.exp(l - mx) for l in ls]
        den = es[0] + es[1] + es[2]
        num = es[0] * o0[...] + es[1] * o1[...] + es[2] * o2[...]
        a_ref[...] = (num / den).astype(BF16)
        lt_ref[...] = mx + jnp.log(den)

    tile = pl.BlockSpec((t, w), lambda i: (i, 0))
    return pl.pallas_call(
        body, name=name, grid=(s // t,), in_specs=[tile] * 6, out_specs=[tile, tile],
        out_shape=[jax.ShapeDtypeStruct((s, w), BF16), jax.ShapeDtypeStruct((s, w), F32)],
        compiler_params=_cparams(("parallel",)))(*outs, *lses)


def _attn_bwd(slabs, dap, ap, lse_tot, gi, name):
    window, dil = ATTN_GROUPS[gi]
    n_back = window // dil
    _, s, w = slabs.shape
    big_l = s // dil
    nb = big_l // ATTN_BLOCK
    blk = ATTN_BLOCK
    view, (iq, ik, iv) = _group_view(slabs, gi)
    dap_v = dap.reshape(big_l, dil * 2 * w)
    ap_v = ap.reshape(big_l, dil * 2 * w)
    lse_v = lse_tot.reshape(big_l, dil * w)
    scale = HEAD_DIM ** -0.5

    def body(qc_ref, qn_ref, kp_ref, kc_ref, vp_ref, vc_ref, dc_ref, dn_ref, ac_ref, an_ref, lc_ref, ln_ref, out_ref):
        m = pl.program_id(1)
        qi = lax.broadcasted_iota(jnp.int32, (blk, 2 * blk), 0)
        kj = lax.broadcasted_iota(jnp.int32, (blk, 2 * blk), 1)
        dist = qi + blk - kj
        ok = (dist >= 0) & (dist <= n_back) & ((kj >= blk) | (m > 0))
        qi1 = lax.broadcasted_iota(jnp.int32, (blk, blk), 0)
        kj1 = lax.broadcasted_iota(jnp.int32, (blk, blk), 1)
        ok_next = (qi1 + blk - kj1 <= n_back) & (m + 1 < nb)
        mq = _head_masks(blk)
        mk = _head_masks(2 * blk)

        def per_head(q_t, d_t, a_t, l_t, hh):
            qm = jnp.where(mq[hh], q_t, jnp.zeros_like(q_t))
            dm = jnp.where(mq[hh], d_t, jnp.zeros_like(d_t))
            delta = jnp.sum(jnp.where(mq[hh], d_t.astype(F32) * a_t.astype(F32), 0.0), axis=-1, keepdims=True)
            lse_h = jnp.max(jnp.where(mq[hh], l_t, NEG_BIG), axis=-1, keepdims=True)
            return qm, dm, delta, lse_h

        for p in range(w // LANES):
            cols = slice(p * LANES, (p + 1) * LANES)
            k2 = jnp.concatenate([kp_ref[:, cols], kc_ref[:, cols]], axis=0)
            v2 = jnp.concatenate([vp_ref[:, cols], vc_ref[:, cols]], axis=0)
            kc, vc = kc_ref[:, cols], vc_ref[:, cols]
            dq_pair = jnp.zeros((blk, LANES), F32)
            dk_pair = jnp.zeros((blk, LANES), F32)
            dv_pair = jnp.zeros((blk, LANES), F32)
            for hh in range(2):
                qm, dm, delta, lse_h = per_head(qc_ref[:, cols], dc_ref[:, cols], ac_ref[:, cols], lc_ref[:, cols], hh)
                sc = lax.dot_general(qm, k2, NT, preferred_element_type=F32) * scale
                pr = jnp.exp(jnp.where(ok, sc, NEG_BIG) - lse_h)
                dp = lax.dot_general(dm, v2, NT, preferred_element_type=F32)
                ds = pr * (dp - delta)
                ds_b = ds.astype(BF16)
                km = jnp.where(mk[hh], k2, jnp.zeros_like(k2))
                dq_pair = dq_pair + jnp.dot(ds_b, km, preferred_element_type=F32)
                dk_pair = dk_pair + lax.dot_general(ds_b[:, blk:], qm, TN, preferred_element_type=F32)
                dv_pair = dv_pair + lax.dot_general(pr[:, blk:].astype(BF16), dm, TN, preferred_element_type=F32)
                qm, dm, delta, lse_h = per_head(qn_ref[:, cols], dn_ref[:, cols], an_ref[:, cols], ln_ref[:, cols], hh)
                sc = lax.dot_general(qm, kc, NT, preferred_element_type=F32) * scale
                pr = jnp.exp(jnp.where(ok_next, sc, NEG_BIG) - lse_h)
                dp = lax.dot_general(dm, vc, NT, preferred_element_type=F32)
                ds_b = (pr * (dp - delta)).astype(BF16)
                dk_pair = dk_pair + lax.dot_general(ds_b, qm, TN, preferred_element_type=F32)
                dv_pair = dv_pair + lax.dot_general(pr.astype(BF16), dm, TN, preferred_element_type=F32)
            out_ref[0, :, cols] = (dq_pair * scale).astype(BF16)
            out_ref[1, :, cols] = (dk_pair * scale).astype(BF16)
            out_ref[2, :, cols] = dv_pair.astype(BF16)

    def slab_spec(slab, shift):
        if shift < 0:
            return pl.BlockSpec((None, blk, w), lambda r, n: (slab, jnp.maximum(n - 1, 0), r))
        if shift > 0:
            return pl.BlockSpec((None, blk, w), lambda r, n: (slab, jnp.minimum(n + 1, nb - 1), r))
        return pl.BlockSpec((None, blk, w), lambda r, n: (slab, n, r))

    def tok_spec(stride, shift):
        if shift > 0:
            return pl.BlockSpec((blk, w), lambda r, n: (jnp.minimum(n + 1, nb - 1), stride * r))
        return pl.BlockSpec((blk, w), lambda r, n: (n, stride * r))

    out = pl.pallas_call(
        body, name=name, grid=(dil, nb),
        in_specs=[slab_spec(iq, 0), slab_spec(iq, 1), slab_spec(ik, -1), slab_spec(ik, 0),
                  slab_spec(iv, -1), slab_spec(iv, 0),
                  tok_spec(2, 0), tok_spec(2, 1), tok_spec(2, 0), tok_spec(2, 1), tok_spec(1, 0), tok_spec(1, 1)],
        out_specs=pl.BlockSpec((3, blk, w), lambda r, n: (0, n, r)),
        out_shape=jax.ShapeDtypeStruct((3, big_l, dil * w), BF16),
        compiler_params=_cparams(("parallel", "arbitrary")))(
            view, view, view, view, view, view, dap_v, dap_v, ap_v, ap_v, lse_v, lse_v)
    return out.reshape(3, s, w)


def _shift_down(cur, prev, j):
    row = lax.broadcasted_iota(jnp.int32, cur.shape, 0)
    return jnp.where(row >= j, pltpu.roll(cur, j, 0), pltpu.roll(prev, j, 0))


def _shift_up(cur, nxt, j):
    t = cur.shape[0]
    row = lax.broadcasted_iota(jnp.int32, cur.shape, 0)
    return jnp.where(row < t - j, pltpu.roll(cur, t - j, 0), pltpu.roll(nxt, t - j, 0))


def _window_sum_down(cur, prev, w):
    s, sp, step = cur, prev, 1
    while step < w:
        s_new = s + _shift_down(s, sp, step)
        sp = sp + pltpu.roll(sp, step, 0)
        s, step = s_new, step * 2
    return s


def _window_sum_up(cur, nxt, w):
    t = cur.shape[0]
    s, sn, step = cur, nxt, 1
    while step < w:
        s_new = s + _shift_up(s, sn, step)
        sn = sn + pltpu.roll(sn, t - step, 0)
        s, step = s_new, step * 2
    return s


def _pool_count(tile_idx, t, w):
    row = lax.broadcasted_iota(jnp.int32, (t, LANES), 0) + tile_idx * t
    return jnp.minimum(row + 1, w).astype(F32)


def _pool_fwd(slabs, attn, pool_w, pool_scale, name):
    _, s, w = slabs.shape
    t = TOK_TILE
    gw = POOL_W // len(POOL_WINDOWS)

    def body(u_ref, up_ref, a_ref, w_ref, sc_ref, ap_ref, d_ref):
        i = pl.program_id(0)
        ap_ref[:, :w] = a_ref[...]
        for gi, win in enumerate(POOL_WINDOWS):
            cols = slice(gi * gw, (gi + 1) * gw)
            u = u_ref[:, cols].astype(F32)
            up = jnp.where(i > 0, up_ref[:, cols].astype(F32), 0.0)
            d = (_window_sum_down(u, up, win) / _pool_count(i, t, win) - u).astype(BF16)
            d_ref[:, cols] = d
            y = jnp.dot(d, w_ref[gi].astype(BF16), preferred_element_type=F32)
            ap_ref[:, w + gi * gw:w + (gi + 1) * gw] = (y * sc_ref[:, cols]).astype(BF16)

    return pl.pallas_call(
        body, name=name, grid=(s // t,),
        in_specs=[pl.BlockSpec((None, t, w), lambda i: (N_SLABS - 1, i, 0)),
                  pl.BlockSpec((None, t, w), lambda i: (N_SLABS - 1, jnp.maximum(i - 1, 0), 0)),
                  pl.BlockSpec((t, w), lambda i: (i, 0)),
                  pl.BlockSpec((len(POOL_WINDOWS), gw, gw), lambda i: (0, 0, 0)),
                  pl.BlockSpec((1, w), lambda i: (0, 0))],
        out_specs=[pl.BlockSpec((t, 2 * w), lambda i: (i, 0)), pl.BlockSpec((t, w), lambda i: (i, 0))],
        out_shape=[jax.ShapeDtypeStruct((s, 2 * w), BF16), jax.ShapeDtypeStruct((s, w), BF16)],
        compiler_params=_cparams(("parallel",)))(slabs, slabs, attn, pool_w, pool_scale)


def _pool_bwd(dap, d, pool_w, pool_scale, name):
    s, w = d.shape
    t = TOK_TILE
    nt = s // t
    gw = POOL_W // len(POOL_WINDOWS)

    def body(dy_ref, dyn_ref, d_ref, w_ref, sc_ref, du_ref, dw_ref, dsc_ref):
        i = pl.program_id(0)

        @pl.when(i == 0)
        def _():
            dw_ref[...] = jnp.zeros_like(dw_ref)
            dsc_ref[...] = jnp.zeros_like(dsc_ref)

        for gi, win in enumerate(POOL_WINDOWS):
            cols = slice(gi * gw, (gi + 1) * gw)
            wb = w_ref[gi].astype(BF16)
            scale = sc_ref[:, cols]
            dv = d_ref[:, cols]
            dy = dy_ref[:, cols].astype(F32)
            y = jnp.dot(dv, wb, preferred_element_type=F32)
            dsc_ref[:, cols] += _colsum8(dy * y)
            dyp = (dy * scale).astype(BF16)
            dw_ref[gi] += lax.dot_general(dv, dyp, TN, preferred_element_type=F32)
            dd = lax.dot_general(dyp, wb, NT, preferred_element_type=F32)
            dypn = (dyn_ref[:, cols].astype(F32) * scale).astype(BF16)
            ddn = lax.dot_general(dypn, wb, NT, preferred_element_type=F32)
            e = dd / _pool_count(i, t, win)
            en = jnp.where(i + 1 < nt, ddn / _pool_count(i + 1, t, win), 0.0)
            du_ref[:, cols] = (_window_sum_up(e, en, win) - dd).astype(BF16)

    return pl.pallas_call(
        body, name=name, grid=(nt,),
        in_specs=[pl.BlockSpec((t, w), lambda i: (i, 1)),
                  pl.BlockSpec((t, w), lambda i: (jnp.minimum(i + 1, nt - 1), 1)),
                  pl.BlockSpec((t, w), lambda i: (i, 0)),
                  pl.BlockSpec((len(POOL_WINDOWS), gw, gw), lambda i: (0, 0, 0)),
                  pl.BlockSpec((1, w), lambda i: (0, 0))],
        out_specs=[pl.BlockSpec((t, w), lambda i: (i, 0)),
                   pl.BlockSpec((len(POOL_WINDOWS), gw, gw), lambda i: (0, 0, 0)),
                   pl.BlockSpec((8, w), lambda i: (0, 0))],
        out_shape=[jax.ShapeDtypeStruct((s, w), BF16), jax.ShapeDtypeStruct((len(POOL_WINDOWS), gw, gw), F32),
                   jax.ShapeDtypeStruct((8, w), F32)],
        compiler_params=_cparams(("arbitrary",)))(dap, dap, d, pool_w, pool_scale)


CONV_TAPS = 4
CONV_COLS = 512


def _sigmoid(v):
    return 1.0 / (1.0 + jnp.exp(-v))


def _conv_pre(x, xprev, w_ref, b_ref):
    pre = b_ref[...] + w_ref[CONV_TAPS - 1:CONV_TAPS, :] * x
    for k in range(CONV_TAPS - 1):
        pre = pre + w_ref[k:k + 1, :] * _shift_down(x, xprev, CONV_TAPS - 1 - k)
    return pre


def _conv_fwd(xbc, conv_w, conv_b, name):
    s, c = xbc.shape
    t, tc = TOK_TILE, CONV_COLS

    def body(x_ref, xp_ref, w_ref, b_ref, o_ref):
        i = pl.program_id(0)
        x = x_ref[...].astype(F32)
        xp = jnp.where(i > 0, xp_ref[...].astype(F32), 0.0)
        pre = _conv_pre(x, xp, w_ref, b_ref)
        o_ref[...] = (pre * _sigmoid(pre)).astype(BF16)

    return pl.pallas_call(
        body, name=name, grid=(s // t, c // tc),
        in_specs=[pl.BlockSpec((t, tc), lambda i, j: (i, j)),
                  pl.BlockSpec((t, tc), lambda i, j: (jnp.maximum(i - 1, 0), j)),
                  pl.BlockSpec((CONV_TAPS, tc), lambda i, j: (0, j)), pl.BlockSpec((1, tc), lambda i, j: (0, j))],
        out_specs=pl.BlockSpec((t, tc), lambda i, j: (i, j)), out_shape=jax.ShapeDtypeStruct((s, c), BF16),
        compiler_params=_cparams(("parallel", "parallel")))(xbc, xbc, conv_w, conv_b)


def _conv_bwd(xbc, dact, conv_w, conv_b, name):
    s, c = xbc.shape
    t, tc = TOK_TILE, CONV_COLS
    nt = s // t

    def body(xp_ref, x_ref, xn_ref, da_ref, dan_ref, w_ref, b_ref, dx_ref, dw_ref, db_ref):
        i = pl.program_id(1)

        @pl.when(i == 0)
        def _():
            dw_ref[...] = jnp.zeros_like(dw_ref)
            db_ref[...] = jnp.zeros_like(db_ref)

        x = x_ref[...].astype(F32)
        xp = jnp.where(i > 0, xp_ref[...].astype(F32), 0.0)
        xn = xn_ref[...].astype(F32)

        def dsilu(pre):
            sg = _sigmoid(pre)
            return sg * (1.0 + pre * (1.0 - sg))

        dpre = da_ref[...] * dsilu(_conv_pre(x, xp, w_ref, b_ref))
        dpre_n = jnp.where(i + 1 < nt, dan_ref[...] * dsilu(_conv_pre(xn, x, w_ref, b_ref)), 0.0)
        dx = w_ref[CONV_TAPS - 1:CONV_TAPS, :] * dpre
        dw_ref[CONV_TAPS - 1] += _colsum8(dpre * x)
        for k in range(CONV_TAPS - 1):
            j = CONV_TAPS - 1 - k
            dx = dx + w_ref[k:k + 1, :] * _shift_up(dpre, dpre_n, j)
            dw_ref[k] += _colsum8(dpre * _shift_down(x, xp, j))
        dx_ref[...] = dx.astype(BF16)
        db_ref[...] += _colsum8(dpre)

    def xspec(shift):
        if shift < 0:
            return pl.BlockSpec((t, tc), lambda j, i: (jnp.maximum(i - 1, 0), j))
        if shift > 0:
            return pl.BlockSpec((t, tc), lambda j, i: (jnp.minimum(i + 1, nt - 1), j))
        return pl.BlockSpec((t, tc), lambda j, i: (i, j))

    return pl.pallas_call(
        body, name=name, grid=(c // tc, nt),
        in_specs=[xspec(-1), xspec(0), xspec(1), xspec(0), xspec(1),
                  pl.BlockSpec((CONV_TAPS, tc), lambda j, i: (0, j)), pl.BlockSpec((1, tc), lambda j, i: (0, j))],
        out_specs=[xspec(0), pl.BlockSpec((CONV_TAPS, 8, tc), lambda j, i: (0, 0, j)),
                   pl.BlockSpec((8, tc), lambda j, i: (0, j))],
        out_shape=[jax.ShapeDtypeStruct((s, c), BF16), jax.ShapeDtypeStruct((CONV_TAPS, 8, c), F32),
                   jax.ShapeDtypeStruct((8, c), F32)],
        compiler_params=_cparams(("parallel", "arbitrary")))(xbc, xbc, xbc, dact, dact, conv_w, conv_b)


GN_TILE = 256


def _gated_norm_fwd(y, z, g, name):
    s, c = y.shape
    t = GN_TILE

    def body(y_ref, z_ref, g_ref, o_ref):
        for gi in range(SSM_GROUPS):
            cols = slice(gi * SSM_GW, (gi + 1) * SSM_GW)
            zv = z_ref[:, cols].astype(F32)
            yf = y_ref[:, cols] * (zv * _sigmoid(zv))
            r = lax.rsqrt(jnp.mean(yf * yf, axis=-1, keepdims=True) + NORM_EPS)
            o_ref[:, cols] = (yf * r * g_ref[:, cols]).astype(BF16)

    tile = pl.BlockSpec((t, c), lambda i: (i, 0))
    return pl.pallas_call(
        body, name=name, grid=(s // t,), in_specs=[tile, tile, pl.BlockSpec((1, c), lambda i: (0, 0))],
        out_specs=tile, out_shape=jax.ShapeDtypeStruct((s, c), BF16),
        compiler_params=_cparams(("parallel",)))(y, z, g)


def _gated_norm_bwd(y, z, dout, g, name):
    s, c = y.shape
    t = GN_TILE

    def body(y_ref, z_ref, do_ref, g_ref, dy_ref, dz_ref, dg_ref):
        i = pl.program_id(0)

        @pl.when(i == 0)
        def _():
            dg_ref[...] = jnp.zeros_like(dg_ref)

        for gi in range(SSM_GROUPS):
            cols = slice(gi * SSM_GW, (gi + 1) * SSM_GW)
            zv = z_ref[:, cols].astype(F32)
            yv = y_ref[:, cols]
            sg = _sigmoid(zv)
            sz = zv * sg
            yf = yv * sz
            r = lax.rsqrt(jnp.mean(yf * yf, axis=-1, keepdims=True) + NORM_EPS)
            n = yf * r
            dout = do_ref[:, cols]
            dn = dout * g_ref[:, cols]
            dg_ref[:, cols] += _colsum8(dout * n)
            dyf = r * (dn - n * jnp.mean(dn * n, axis=-1, keepdims=True))
            dy_ref[:, cols] = dyf * sz
            dz_ref[:, cols] = (dyf * yv * (sg * (1.0 + zv * (1.0 - sg)))).astype(BF16)

    tile = pl.BlockSpec((t, c), lambda i: (i, 0))
    return pl.pallas_call(
        body, name=name, grid=(s // t,), in_specs=[tile, tile, tile, pl.BlockSpec((1, c), lambda i: (0, 0))],
        out_specs=[tile, tile, pl.BlockSpec((8, c), lambda i: (0, 0))],
        out_shape=[jax.ShapeDtypeStruct((s, c), F32), jax.ShapeDtypeStruct((s, c), BF16),
                   jax.ShapeDtypeStruct((8, c), F32)],
        compiler_params=_cparams(("arbitrary",)))(y, z, dout, g)


def _split_dot(x, e, dims, terms):
    r, acc = x, None
    for i in range(terms):
        p = r.astype(BF16)
        part = lax.dot_general(p, e, dims, preferred_element_type=F32)
        acc = part if acc is None else acc + part
        if i + 1 < terms:
            r = r - p.astype(F32)
    return acc


def _split_dot_r(e, x, dims, terms):
    r, acc = x, None
    for i in range(terms):
        p = r.astype(BF16)
        part = lax.dot_general(e, p, dims, preferred_element_type=F32)
        acc = part if acc is None else acc + part
        if i + 1 < terms:
            r = r - p.astype(F32)
    return acc


def _ssd_prep(dtr_ref, bias_ref, alog_ref):
    q = SSM_CHUNK
    dtr = dtr_ref[...] + bias_ref[...]
    dt = jnp.maximum(dtr, 0.0) + jnp.log(1.0 + jnp.exp(-jnp.abs(dtr)))
    a_neg = -jnp.exp(alog_ref[...])
    dta = dt * a_neg
    row = lax.broadcasted_iota(jnp.int32, (q, q), 0)
    col = lax.broadcasted_iota(jnp.int32, (q, q), 1)
    causal = row >= col
    a = _split_dot_r(causal.astype(BF16), dta, NN, 3)
    aq_row = jnp.sum(dta, axis=0, keepdims=True)
    aq_hb = _split_dot(dta, jnp.ones((q, LANES), BF16), TN, 3)
    return dict(dtr=dtr, dt=dt, a_neg=a_neg, a=a, a_t=jnp.transpose(a), aq_row=aq_row, aq_hb=aq_hb,
                ea=jnp.exp(a), fa=jnp.exp(aq_row - a), causal=causal)


def _ssd_group_mats():
    g = jnp.arange(SSM_GROUPS)[:, None, None]
    head = jnp.arange(LANES)[None, :, None]
    eg = (head == 8 * g + jnp.arange(SSM_GW)[None, None, :] // HEAD_DIM).astype(BF16)
    sel = (head == 8 * g + jnp.arange(8 * LANES)[None, None, :] // LANES).astype(BF16)
    mats = (eg, jnp.transpose(eg, (0, 2, 1)), sel)
    return mats, [pl.BlockSpec(m.shape, lambda c: (0, 0, 0)) for m in mats]


def _ssd_fwd(xbc, dt_raw, dt_bias, a_log, d_e, name):
    s = xbc.shape[0]
    q = SSM_CHUNK
    nc = s // q

    def body(x_ref, dtr_ref, bias_ref, alog_ref, de_ref, eg_ref, egt_ref, sel_ref, y_ref, hin_ref, state, at_ref):
        c = pl.program_id(0)

        @pl.when(c == 0)
        def _():
            state[...] = jnp.zeros_like(state)

        hin_ref[...] = state[...].astype(BF16)
        pr = _ssd_prep(dtr_ref, bias_ref, alog_ref)
        at_ref[...] = pr["a_t"]
        exp_aq_hb = jnp.exp(pr["aq_hb"])
        stack3 = jnp.concatenate([pr["dt"], pr["ea"], pr["fa"]], axis=0)
        mq = _head_masks(q)
        for g in range(SSM_GROUPS):
            eg, sel = eg_ref[g], sel_ref[g]
            cols = slice(g * SSM_GW, (g + 1) * SSM_GW)
            xg = x_ref[:, cols]
            bg = x_ref[:, SSM_INNER + g * SSM_STATE:SSM_INNER + (g + 1) * SSM_STATE]
            cg = x_ref[:, SSM_INNER + SSM_GROUPS * SSM_STATE + g * SSM_STATE:
                       SSM_INNER + SSM_GROUPS * SSM_STATE + (g + 1) * SSM_STATE]
            e3 = _split_dot(stack3, eg, NN, 1)
            dt_e, ea_e, fa_e = e3[:q], e3[q:2 * q], e3[2 * q:]
            xf = xg.astype(F32)
            xdt = xf * dt_e
            xdt_b = xdt.astype(BF16)
            cb = lax.dot_general(cg, bg, NT, preferred_element_type=F32)
            colb = _split_dot(pr["a"], sel, NN, 2)
            hg = state[cols, :]
            y_g = lax.dot_general(cg, hg.astype(BF16), NT, preferred_element_type=F32) * ea_e + de_ref[:, cols] * xf
            pairs = []
            for pp in range(4):
                xp = xdt_b[:, pp * LANES:(pp + 1) * LANES]
                yp = jnp.zeros((q, LANES), F32)
                for hh in range(2):
                    hi = 2 * pp + hh
                    diff = colb[:, hi * LANES:(hi + 1) * LANES] - at_ref[8 * g + hi:8 * g + hi + 1, :]
                    lmat = jnp.exp(jnp.where(pr["causal"], diff, NEG_BIG))
                    m_b = (cb * lmat).astype(BF16)
                    yp = yp + jnp.dot(m_b, jnp.where(mq[hh], xp, jnp.zeros_like(xp)), preferred_element_type=F32)
                pairs.append(yp)
            y_ref[:, cols] = y_g + jnp.concatenate(pairs, axis=1)
            s_g = lax.dot_general((xdt * fa_e).astype(BF16), bg, TN, preferred_element_type=F32)
            dec_g = _split_dot_r(eg, exp_aq_hb, TN, 2)
            state[cols, :] = dec_g * hg + s_g

    row128 = pl.BlockSpec((1, LANES), lambda c: (0, 0))
    mats, mat_specs = _ssd_group_mats()
    return pl.pallas_call(
        body, name=name, grid=(nc,),
        in_specs=[pl.BlockSpec((q, SSM_CONV_DIM), lambda c: (c, 0)), pl.BlockSpec((q, LANES), lambda c: (c, 0)),
                  row128, row128, pl.BlockSpec((1, SSM_INNER), lambda c: (0, 0)), *mat_specs],
        out_specs=[pl.BlockSpec((q, SSM_INNER), lambda c: (c, 0)),
                   pl.BlockSpec((None, SSM_INNER, SSM_STATE), lambda c: (c, 0, 0))],
        out_shape=[jax.ShapeDtypeStruct((s, SSM_INNER), F32), jax.ShapeDtypeStruct((nc, SSM_INNER, SSM_STATE), BF16)],
        scratch_shapes=[pltpu.VMEM((SSM_INNER, SSM_STATE), F32), pltpu.VMEM((LANES, q), F32)],
        compiler_params=_cparams(("arbitrary",)))(xbc, dt_raw, dt_bias, a_log, d_e, *mats)


def _ssd_bwd(xbc, dt_raw, dt_bias, a_log, d_e, hin, dy, name):
    s = xbc.shape[0]
    q = SSM_CHUNK
    nc = s // q

    def body(x_ref, dtr_ref, bias_ref, alog_ref, de_ref, hin_ref, dy_ref, eg_ref, egt_ref, sel_ref,
             dx_ref, ddt_ref, da_acc, db_acc, dd_acc, gst, at_ref):
        i = pl.program_id(0)

        @pl.when(i == 0)
        def _():
            gst[...] = jnp.zeros_like(gst)
            da_acc[...] = jnp.zeros_like(da_acc)
            db_acc[...] = jnp.zeros_like(db_acc)
            dd_acc[...] = jnp.zeros_like(dd_acc)

        pr = _ssd_prep(dtr_ref, bias_ref, alog_ref)
        at_ref[...] = pr["a_t"]
        exp_aq_hb = jnp.exp(pr["aq_hb"])
        stack3 = jnp.concatenate([pr["dt"], pr["ea"], pr["fa"]], axis=0)
        mq = _head_masks(q)
        lane_h = lax.broadcasted_iota(jnp.int32, (q, LANES), 1)
        sub_h = lax.broadcasted_iota(jnp.int32, (LANES, q), 0)
        da = jnp.zeros((q, LANES), F32)
        da_tn = jnp.zeros((LANES, q), F32)
        daq = jnp.zeros((1, LANES), F32)
        ddt = jnp.zeros((q, LANES), F32)
        for g in range(SSM_GROUPS):
            eg, eg_t, sel = eg_ref[g], egt_ref[g], sel_ref[g]
            cols = slice(g * SSM_GW, (g + 1) * SSM_GW)
            bcols = slice(SSM_INNER + g * SSM_STATE, SSM_INNER + (g + 1) * SSM_STATE)
            ccols = slice(SSM_INNER + SSM_GROUPS * SSM_STATE + g * SSM_STATE,
                          SSM_INNER + SSM_GROUPS * SSM_STATE + (g + 1) * SSM_STATE)
            xg, bg, cg = x_ref[:, cols], x_ref[:, bcols], x_ref[:, ccols]
            e3 = _split_dot(stack3, eg, NN, 1)
            dt_e, ea_e, fa_e = e3[:q], e3[q:2 * q], e3[2 * q:]
            xf = xg.astype(F32)
            xdt = xf * dt_e
            xdt_b = xdt.astype(BF16)
            xdtf = xdt * fa_e
            xdtf_b = xdtf.astype(BF16)
            cb = lax.dot_general(cg, bg, NT, preferred_element_type=F32)
            colb = _split_dot(pr["a"], sel, NN, 2)
            dyg = dy_ref[:, cols]
            dd_acc[:, cols] += _colsum8(dyg * xf)
            hg_b = hin_ref[cols, :]
            gg = gst[cols, :]
            gg_b = gg.astype(BF16)
            dye_b = (dyg * ea_e).astype(BF16)
            dc_g = jnp.dot(dye_b, hg_b, preferred_element_type=F32)
            dh_g = lax.dot_general(dye_b, cg, TN, preferred_element_type=F32)
            yoff = lax.dot_general(cg, hg_b, NT, preferred_element_type=F32) * ea_e
            da = da + _split_dot(dyg * yoff, eg_t, NN, 1)
            db_g = jnp.dot(xdtf_b, gg_b, preferred_element_type=F32)
            tmat = lax.dot_general(bg, gg_b, NT, preferred_element_type=F32)
            dxdt = fa_e * tmat
            qmat = _split_dot(xdtf * tmat, eg_t, NN, 1)
            da = da - qmat
            daq = daq + jnp.sum(qmat, axis=0, keepdims=True)
            gh = _split_dot(gg * hg_b.astype(F32), eg_t, TN, 1)
            daq = daq + jnp.sum(gh, axis=0, keepdims=True) * jnp.exp(pr["aq_row"])
            dcb = jnp.zeros((q, q), F32)
            pairs = []
            for pp in range(4):
                xp = xdt_b[:, pp * LANES:(pp + 1) * LANES]
                dyp = dyg[:, pp * LANES:(pp + 1) * LANES]
                dxp = jnp.zeros((q, LANES), F32)
                for hh in range(2):
                    hi = 2 * pp + hh
                    h = 8 * g + hi
                    diff = colb[:, hi * LANES:(hi + 1) * LANES] - at_ref[h:h + 1, :]
                    lmat = jnp.exp(jnp.where(pr["causal"], diff, NEG_BIG))
                    mmat = cb * lmat
                    dyh = jnp.where(mq[hh], dyp, 0.0).astype(BF16)
                    dm = lax.dot_general(dyh, xp, NT, preferred_element_type=F32)
                    dxp = dxp + lax.dot_general(mmat.astype(BF16), dyh, TN, preferred_element_type=F32)
                    wmat = dm * mmat
                    da = da + jnp.where(lane_h == h, jnp.sum(wmat, axis=-1, keepdims=True), 0.0)
                    da_tn = da_tn + jnp.where(sub_h == h, jnp.sum(wmat, axis=0, keepdims=True), 0.0)
                    dcb = dcb + dm * lmat
                pairs.append(dxp)
            dxdt = dxdt + jnp.concatenate(pairs, axis=1)
            dcb_b = dcb.astype(BF16)
            dc_g = dc_g + jnp.dot(dcb_b, bg, preferred_element_type=F32)
            db_g = db_g + lax.dot_general(dcb_b, cg, TN, preferred_element_type=F32)
            ddt = ddt + _split_dot(dxdt * xf, eg_t, NN, 1)
            dx_ref[:, cols] = dxdt * dt_e + de_ref[:, cols] * dyg
            dx_ref[:, bcols] = db_g
            dx_ref[:, ccols] = dc_g
            dec_g = _split_dot_r(eg, exp_aq_hb, TN, 2)
            gst[cols, :] = dh_g + dec_g * gg
        row = lax.broadcasted_iota(jnp.int32, (q, LANES), 0)
        da_all = da - jnp.transpose(da_tn) + jnp.where(row == q - 1, daq, 0.0)
        upper = jnp.logical_not(pr["causal"]) | (lax.broadcasted_iota(jnp.int32, (q, q), 0)
                                                 == lax.broadcasted_iota(jnp.int32, (q, q), 1))
        rcs = _split_dot_r(upper.astype(BF16), da_all, NN, 3)
        ddt_all = ddt + pr["a_neg"] * rcs
        da_acc[...] += _colsum8(pr["dt"] * rcs)
        ddtr = jnp.where(lane_h < SSM_HEADS, ddt_all * _sigmoid(pr["dtr"]), 0.0)
        ddt_ref[...] = ddtr
        db_acc[...] += _colsum8(ddtr)

    rev = lambda i: (nc - 1 - i, 0)
    row128 = pl.BlockSpec((1, LANES), lambda i: (0, 0))
    acc128 = pl.BlockSpec((8, LANES), lambda i: (0, 0))
    mats, mat_specs = _ssd_group_mats()
    return pl.pallas_call(
        body, name=name, grid=(nc,),
        in_specs=[pl.BlockSpec((q, SSM_CONV_DIM), rev), pl.BlockSpec((q, LANES), rev), row128, row128,
                  pl.BlockSpec((1, SSM_INNER), lambda i: (0, 0)),
                  pl.BlockSpec((None, SSM_INNER, SSM_STATE), lambda i: (nc - 1 - i, 0, 0)),
                  pl.BlockSpec((q, SSM_INNER), rev), *mat_specs],
        out_specs=[pl.BlockSpec((q, SSM_CONV_DIM), rev), pl.BlockSpec((q, LANES), rev), acc128, acc128,
                   pl.BlockSpec((8, SSM_INNER), lambda i: (0, 0))],
        out_shape=[jax.ShapeDtypeStruct((s, SSM_CONV_DIM), F32), jax.ShapeDtypeStruct((s, LANES), F32),
                   jax.ShapeDtypeStruct((8, LANES), F32), jax.ShapeDtypeStruct((8, LANES), F32),
                   jax.ShapeDtypeStruct((8, SSM_INNER), F32)],
        scratch_shapes=[pltpu.VMEM((SSM_INNER, SSM_STATE), F32), pltpu.VMEM((LANES, q), F32)],
        compiler_params=_cparams(("arbitrary",)))(xbc, dt_raw, dt_bias, a_log, d_e, hin, dy, *mats)


MM = (1024, 1024, 2048)
MM_TN = (1024, 1024, 4096)


def _relu2_epilogue(acc, ex, outs):
    r = jnp.maximum(acc, 0.0)
    outs[0][...] = (r * r).astype(BF16)
    outs[1][...] = r.astype(BF16)


def _relu2_bwd_epilogue(acc, ex, outs):
    outs[0][...] = (acc * (2.0 * ex[0][...].astype(F32))).astype(BF16)


def _add_epilogue(acc, ex, outs):
    outs[0][...] = acc + ex[0][...]


def _tile_spec(tm, tn):
    return pl.BlockSpec((tm, tn), lambda i, j, k: (i, j))


def _sum8(acc):
    return jnp.sum(acc, axis=0)


def _row(v):
    return v.reshape(1, -1)


def _ffn_fwd(xin, norm_g, sc, sh, gate, w, l):
    tm, tn, tk = MM
    h = _norm_mod_fwd(xin, norm_g, sc, sh, f"norm_ffn_fwd_{l}")
    w1 = w("ffn_w1", h)
    per = w1.shape[-1] // tn
    f, r = _mm_nn(f"mm_ffn1_{l}", h, w1, *MM, out_dtypes=(BF16, BF16), epilogue=_relu2_epilogue,
                  b_tile=lambda j, k: (j // per, l, 0, j % per), n=FFN_HIDDEN)
    xout, yf = _mm_resid(f"mm_ffn2_{l}", f, w("ffn_w2", f), xin, gate, *MM, b_tile=lambda j, k: (k, l, 0, j), n=D_MODEL)
    return xout, (h, f, r, yf)


def _ffn_bwd(dxo, dyf, xin, saved, norm_g, sc, w, sink, l, prev):
    h, f, r, yf = saved
    tm, tn, tk = MM
    w1, w2 = w("ffn_w1"), w("ffn_w2")
    per = w2.shape[-2] // tn
    da = _mm_nt(f"mm_ffn2_bwd_{l}", dyf, w2, *MM, extras=(r,), extra_specs=(_tile_spec(tm, tn),),
                epilogue=_relu2_bwd_epilogue, b_tile=lambda j, k: (j // per, l, j % per, 0), n=FFN_HIDDEN)[0]
    tok = sink(f"ffn_w2_{l}", _mm_tn(f"mm_ffn2_dw_{l}", f, dyf, *MM_TN), "rows")
    dh = _mm_nt(f"mm_ffn1_bwd_{l}", da, w1, *MM, out_dtypes=(F32,), b_tile=lambda j, k: (k, l, j, 0), n=D_MODEL,
                extras=(tok.reshape(1, 1),), extra_specs=(ANY,))[0]
    tok = tok + sink(f"ffn_w1_{l}", _mm_tn(f"mm_ffn1_dw_{l}", h, da, *MM_TN, col_chunks=N_CHIPS), "slots")
    dxin, dsh, dsc, dng, dy_prev, dgate_prev = _norm_mod_bwd(xin, dh, dxo, norm_g + tok, sc, f"norm_ffn_bwd_{l}", prev)
    return dxin, dy_prev, dgate_prev, (_sum8(dsh), _sum8(dsc), _sum8(dng))


def _local_step(x, target, mod, w, p, sink):
    s, d = x.shape
    tm, tn, tk = MM
    mods = [[_row(mod[l, j * d:(j + 1) * d]) for j in range(6)] for l in range(2)]
    g = {}

    sh1, sc1, g1, sh2, sc2, g2 = mods[0]
    nm0, nf0 = _row(p["norm_mix"][0]), _row(p["norm_ffn"][0])
    h0 = _norm_mod_fwd(x, nm0, sc1, sh1, "norm_mix_fwd_0")
    slabs = _mm_core(
        "mm_even_in", h0, w("even_in", h0), dims=NN, grid=(s // tm, N_SLABS, 1),
        a_spec=pl.BlockSpec((tm, d), lambda i, j, k: (i, 0)), b_spec=pl.BlockSpec((d, ATTN_W), lambda i, j, k: (0, j)),
        out_shapes=[jax.ShapeDtypeStruct((N_SLABS, s, ATTN_W), BF16)],
        out_specs=[pl.BlockSpec((None, tm, ATTN_W), lambda i, j, k: (j, i, 0))], acc_shape=None)[0]
    outs, lses = zip(*[_attn_fwd(slabs, gi, f"attn_fwd_{gi}") for gi in range(3)])
    attn, lse_tot = _attn_merge(outs, lses, "attn_merge")
    pool_scale = _row(p["pool_scale"])
    ap, pool_d = _pool_fwd(slabs, attn, p["pool_w"], pool_scale, "pool_fwd")
    x1, y0 = _mm_resid("mm_even_out", ap, w("even_out", ap), x, g1, *MM)
    x2, ffn0 = _ffn_fwd(x1, nf0, sc2, sh2, g2, w, 0)

    sh1b, sc1b, g1b, sh2b, sc2b, g2b = mods[1]
    nm1, nf1 = _row(p["norm_mix"][1]), _row(p["norm_ffn"][1])
    h1 = _norm_mod_fwd(x2, nm1, sc1b, sh1b, "norm_mix_fwd_1")
    z = _mm_nn("mm_ssm_z", h1, w("ssm_z", h1), *MM)[0]
    xbc_raw = _mm_nn("mm_ssm_xbc", h1, w("ssm_xbc", h1), *MM)[0]
    dt_raw = _mm_nn("mm_ssm_dt", h1, w("ssm_dt", h1), tm, LANES, tk, out_dtypes=(F32,))[0]
    xbc = _conv_fwd(xbc_raw, p["conv_w"], _row(p["conv_b"]), "conv_fwd")
    y_ssd, hin = _ssd_fwd(xbc, dt_raw, p["dt_bias"], p["a_log"], p["d_e"], "ssd_fwd")
    ssm_norm = _row(p["ssm_norm"])
    yn = _gated_norm_fwd(y_ssd, z, ssm_norm, "gated_norm_fwd")
    x3, y1 = _mm_resid("mm_ssm_out", yn, w("ssm_out", yn), x2, g1b, *MM)
    x4, ffn1 = _ffn_fwd(x3, nf1, sc2b, sh2b, g2b, w, 1)

    dx4, sq, dfn, dyf1, dg2b = _final_loss(x4, _row(p["final_norm"]), target, (ffn1[3], g2b), "final_loss")
    loss_share = (0.5 / d) * jnp.sum(sq)
    g["final_norm"] = _sum8(dfn)
    dg2b = _sum8(dg2b)

    dx3, dy1, dg1b, (dsh2b, dsc2b, dnf1) = _ffn_bwd(dx4, dyf1, x3, ffn1, nf1, sc2b, w, sink, 1, (y1, g1b))
    dyn = _mm_nt("mm_ssm_out_bwd", dy1, w("ssm_out"), *MM, out_dtypes=(F32,))[0]
    tok = sink("ssm_out", _mm_tn("mm_ssm_out_dw", yn, dy1, *MM_TN), "rows")
    dy_ssd, dz, dgn = _gated_norm_bwd(y_ssd, z, dyn, ssm_norm + tok, "gated_norm_bwd")
    g["ssm_norm"] = _sum8(dgn)
    dact, ddt, da_acc, db_acc, dd_acc = _ssd_bwd(xbc, dt_raw, p["dt_bias"], p["a_log"], p["d_e"], hin, dy_ssd, "ssd_bwd")
    g["dt_bias"] = _sum8(db_acc)[:SSM_HEADS]
    g["a_log"] = _sum8(da_acc)[:SSM_HEADS] * (-jnp.exp(p["a_log"][0, :SSM_HEADS]))
    g["ssm_d"] = jnp.sum(_sum8(dd_acc).reshape(SSM_HEADS, HEAD_DIM), axis=1)
    dxbc, dcw, dcb = _conv_bwd(xbc_raw, dact, p["conv_w"], _row(p["conv_b"]), "conv_bwd")
    g["conv_w"] = jnp.sum(dcw, axis=1)
    g["conv_b"] = _sum8(dcb)
    add_spec = (_tile_spec(tm, tn),)
    dh1 = _mm_nt("mm_ssm_z_bwd", dz, w("ssm_z"), *MM, out_dtypes=(F32,))[0]
    dh1 = _mm_nt("mm_ssm_xbc_bwd", dxbc, w("ssm_xbc"), *MM, out_dtypes=(F32,), extras=(dh1,), extra_specs=add_spec,
                 epilogue=_add_epilogue)[0]
    dh1 = _mm_nt("mm_ssm_dt_bwd", ddt, w("ssm_dt"), *MM, out_dtypes=(F32,), extras=(dh1,), extra_specs=add_spec,
                 epilogue=_add_epilogue)[0]
    tok = sink("ssm_in", jnp.concatenate(
        [_mm_tn("mm_ssm_z_dw", h1, dz, *MM_TN), _mm_tn("mm_ssm_xbc_dw", h1, dxbc, *MM_TN),
         _mm_tn("mm_ssm_dt_dw", h1, ddt, MM_TN[0], LANES, MM_TN[2])[:, :SSM_HEADS]], axis=1), "cols")
    dx2, dsh1b, dsc1b, dnm1, dyf0, dg2 = _norm_mod_bwd(x2, dh1, dx3, nm1 + tok, sc1b, "norm_mix_bwd_1", (ffn0[3], g2))
    dmod1 = jnp.concatenate([_sum8(dsh1b), _sum8(dsc1b), _sum8(dg1b), dsh2b, dsc2b, dg2b])
    dg2 = _sum8(dg2)

    dx1, dy0, dg1, (dsh2, dsc2, dnf0) = _ffn_bwd(dx2, dyf0, x1, ffn0, nf0, sc2, w, sink, 0, (y0, g1))
    dap = _mm_nt("mm_even_out_bwd", dy0, w("even_out"), *MM)[0]
    tok = sink("even_out", _mm_tn("mm_even_out_dw", ap, dy0, *MM_TN), "cols")
    du, dpw, dpsc = _pool_bwd(dap, pool_d, p["pool_w"], pool_scale + tok, "pool_bwd")
    g["pool_w"] = dpw
    g["pool_scale"] = _sum8(dpsc)
    dqkv = [_attn_bwd(slabs, dap, ap, lse_tot, gi, f"attn_bwd_{gi}") for gi in range(3)]
    dproj = jnp.concatenate([dqkv[gi][t] for t in range(3) for gi in range(3)] + [du], axis=1)
    dh0 = _mm_nt("mm_even_in_bwd", dproj, w("even_in"), tm, tn, N_SLABS * ATTN_W // 2, out_dtypes=(F32,))[0]
    tok = sink("even_in", _mm_tn("mm_even_in_dw", dproj, h0, *MM_TN), "cols_t")
    gx, dsh1, dsc1, dnm0 = _norm_mod_bwd(x, dh0, dx1, nm0 + tok, sc1, "norm_mix_bwd_0")
    dmod0 = jnp.concatenate([_sum8(dsh1), _sum8(dsc1), _sum8(dg1), dsh2, dsc2, dg2])

    g["norm_mix"] = jnp.stack([_sum8(dnm0), _sum8(dnm1)])
    g["norm_ffn"] = jnp.stack([dnf0, dnf1])
    return loss_share, gx, g, jnp.stack([dmod0, dmod1])


SMALL_COLS = 512
SMALL_ORDER = ("dmod", "norm_mix", "norm_ffn", "pool_w", "pool_scale", "conv_w", "conv_b", "dt_bias", "a_log",
               "ssm_d", "ssm_norm", "final_norm")


def _cols_to_full(gathered):
    n, k, ns = gathered.shape
    return jnp.transpose(gathered, (1, 0, 2)).reshape(k, n * ns)


def _full_to_cols(full):
    k, n4 = full.shape
    return jnp.transpose(full.reshape(k, N_CHIPS, n4 // N_CHIPS), (1, 0, 2))


def _pack(parts, cols):
    flat = jnp.concatenate([v.reshape(-1) for v in parts])
    rows = -(-flat.shape[0] // (cols * 8)) * 8
    return jnp.pad(flat, (0, rows * cols - flat.shape[0])).reshape(rows, cols)


def _unpack(packed, shapes):
    flat, out, at = packed.reshape(-1), [], 0
    for shp in shapes:
        n = math.prod(shp)
        out.append(flat[at:at + n].reshape(shp))
        at += n
    return out


def kernel(x, c, ada_w, ada_b, norm_mix, norm_ffn, ffn_w1, ffn_w2, even_w_in, pool_w, pool_scale, even_w_out, ssm_w_in, ssm_conv_w, ssm_conv_b, ssm_dt_bias, ssm_a_log, ssm_d, ssm_norm, ssm_w_out, final_norm, loss_target, m_ada_w, m_ada_b, m_norm_mix, m_norm_ffn, m_ffn_w1, m_ffn_w2, m_even_w_in, m_pool_w, m_pool_scale, m_even_w_out, m_ssm_w_in, m_ssm_conv_w, m_ssm_conv_b, m_ssm_dt_bias, m_ssm_a_log, m_ssm_d, m_ssm_norm, m_ssm_w_out, m_final_norm, v_ada_w, v_ada_b, v_norm_mix, v_norm_ffn, v_ffn_w1, v_ffn_w2, v_even_w_in, v_pool_w, v_pool_scale, v_even_w_out, v_ssm_w_in, v_ssm_conv_w, v_ssm_conv_b, v_ssm_dt_bias, v_ssm_a_log, v_ssm_d, v_ssm_norm, v_ssm_w_out, v_final_norm):
    names = ("ada_w", "ada_b", "norm_mix", "norm_ffn", "ffn_w1", "ffn_w2", "even_w_in", "pool_w", "pool_scale",
             "even_w_out", "ssm_w_in", "ssm_conv_w", "ssm_conv_b", "ssm_dt_bias", "ssm_a_log", "ssm_d", "ssm_norm",
             "ssm_w_out", "final_norm")
    wts = dict(zip(names, (ada_w, ada_b, norm_mix, norm_ffn, ffn_w1, ffn_w2, even_w_in, pool_w, pool_scale, even_w_out,
                           ssm_w_in, ssm_conv_w, ssm_conv_b, ssm_dt_bias, ssm_a_log, ssm_d, ssm_norm, ssm_w_out, final_norm)))
    m_in = dict(zip(names, (m_ada_w, m_ada_b, m_norm_mix, m_norm_ffn, m_ffn_w1, m_ffn_w2, m_even_w_in, m_pool_w, m_pool_scale,
                            m_even_w_out, m_ssm_w_in, m_ssm_conv_w, m_ssm_conv_b, m_ssm_dt_bias, m_ssm_a_log, m_ssm_d,
                            m_ssm_norm, m_ssm_w_out, m_final_norm)))
    v_in = dict(zip(names, (v_ada_w, v_ada_b, v_norm_mix, v_norm_ffn, v_ffn_w1, v_ffn_w2, v_even_w_in, v_pool_w, v_pool_scale,
                            v_even_w_out, v_ssm_w_in, v_ssm_conv_w, v_ssm_conv_b, v_ssm_dt_bias, v_ssm_a_log, v_ssm_d,
                            v_ssm_norm, v_ssm_w_out, v_final_norm)))
    d = D_MODEL
    s = x.shape[1]
    ix, iy, ic = _place()
    chip = 2 * ix + iy
    example = 4 * ix + 2 * iy + ic

    c_all = _allgather8(c.reshape(8, d // 8), "gather_c").reshape(N_DEV, d)
    cond = c_all * jax.nn.sigmoid(c_all)
    cond16 = jnp.pad(cond, ((0, 8), (0, 0)))
    ada_cols = ada_w.shape[2]
    bias_shard = lax.dynamic_slice_in_dim(ada_b, chip * ada_cols, ada_cols, axis=1)
    mod_parts = [
        _mm_nn(f"mm_ada_{l}", cond16, ada_w, 16, 512, d, out_dtypes=(F32,), extras=(_row(bias_shard[l]),),
               extra_specs=(pl.BlockSpec((1, 512), lambda i, j, k: (0, j)),), epilogue=_add_epilogue,
               b_tile=lambda j, k, l=l: (l, k, j), n=ada_cols)[0][:8]
        for l in range(2)]
    mod_all = _allgather8(jnp.concatenate(mod_parts, axis=0), "gather_mod").reshape(N_CHIPS, 2, 2, 8, ada_cols)
    mod_mine = lax.dynamic_index_in_dim(mod_all[:, 0], example, axis=2, keepdims=False)
    mod = jnp.transpose(mod_mine, (1, 0, 2)).reshape(2, N_CHIPS * ada_cols)

    cw, nw = ssm_conv_w.shape[2], ssm_norm.shape[1]
    sm = jnp.concatenate([ssm_conv_w[0].reshape(-1), ssm_conv_b.reshape(-1), jnp.pad(ssm_norm[0], (0, cw - nw)),
                          jnp.zeros((2 * cw,), F32)]).reshape(8, cw)
    sm_all = _allgather8(sm, "gather_ssm_small").reshape(N_CHIPS, 2, 8, cw)[:, 0]

    first_shard, sm_all, mod = lax.optimization_barrier((even_w_in, sm_all, mod))
    gathered = _GatheredWeights({"even_in": first_shard, "even_out": even_w_out, "ffn_w1": ffn_w1, "ffn_w2": ffn_w2,
                                 "ssm_in": ssm_w_in, "ssm_out": ssm_w_out})
    mod = mod + gathered.token
    n_xbc = SSM_INNER + SSM_CONV_DIM
    cache = {}

    def full_weight(key, after=None):
        if key in cache:
            return cache[key]
        if key in ("ffn_w1", "ffn_w2"):
            cache[key] = gathered.take(key, after)
        elif key in ("even_in", "even_out"):
            cache[key] = _cols_to_full(gathered.take(key, after)[:, 0])
        elif key == "ssm_out":
            cache[key] = gathered.take(key, after)[:, 0].reshape(SSM_INNER, d)
        else:
            whole = _cols_to_full(gathered.take("ssm_in", after)[:, 0])
            cache["ssm_z"] = whole[:, :SSM_INNER]
            cache["ssm_xbc"] = whole[:, SSM_INNER:n_xbc]
            cache["ssm_dt"] = jnp.pad(whole[:, n_xbc:], ((0, 0), (0, LANES - SSM_HEADS)))
        return cache[key]

    exchange = _GradientExchange()

    def sink(name, grad, layout):
        if layout == "slots":
            slots = grad
        elif layout == "rows":
            slots = grad.reshape(N_CHIPS, -1, grad.shape[-1])
        elif layout == "cols":
            slots = _full_to_cols(grad)
        else:
            slots = jnp.transpose(grad.reshape(N_CHIPS, -1, grad.shape[-1]), (0, 2, 1))
        return exchange.put(name, slots, defer=(name != "even_in"))

    pad_h = ((0, 0), (0, LANES - SSM_HEADS))
    small = {
        "norm_mix": norm_mix, "norm_ffn": norm_ffn, "pool_w": pool_w[0], "pool_scale": pool_scale[0],
        "final_norm": final_norm,
        "conv_w": jnp.transpose(sm_all[:, :CONV_TAPS], (1, 0, 2)).reshape(CONV_TAPS, N_CHIPS * cw),
        "conv_b": sm_all[:, CONV_TAPS].reshape(N_CHIPS * cw),
        "ssm_norm": sm_all[:, CONV_TAPS + 1, :nw].reshape(N_CHIPS * nw),
        "dt_bias": jnp.pad(ssm_dt_bias, pad_h), "a_log": jnp.pad(ssm_a_log, pad_h),
        "d_e": jnp.repeat(ssm_d[0], HEAD_DIM).reshape(1, SSM_INNER),
    }

    loss_share, gx, g, dmod = _local_step(x[0], loss_target[0], mod, full_weight, small, sink)
    loss = lax.psum(loss_share, ("x", "y", "c"))
    reduced = exchange.finish(gx)

    g["dmod"] = dmod
    small_shapes = [g[k].shape for k in SMALL_ORDER]
    packed = _pack([g[k] for k in SMALL_ORDER], SMALL_COLS)
    every = _allgather8(packed, "gather_small").reshape(N_DEV, *packed.shape)
    summed = dict(zip(SMALL_ORDER, _unpack(_sum_devices(every, "sum_small"), small_shapes)))
    dmod_all = every.reshape(N_DEV, -1)[:, :2 * 6 * d].reshape(N_DEV, 2, 6 * d)

    grads = {}
    dmod_shard = lax.dynamic_slice_in_dim(dmod_all, chip * ada_cols, ada_cols, axis=2)
    grads["ada_w"] = jnp.stack([
        _mm_tn(f"mm_ada_dw_{l}", cond16, jnp.pad(dmod_shard[:, l], ((0, 8), (0, 0))), 1024, 512, 16, out_dtype=F32)
        for l in range(2)])
    grads["ada_b"] = summed["dmod"]
    grads["norm_mix"] = summed["norm_mix"]
    grads["norm_ffn"] = summed["norm_ffn"]
    grads["pool_w"] = summed["pool_w"][None]
    grads["pool_scale"] = summed["pool_scale"][None]
    grads["ssm_conv_w"] =lax.dynamic_slice_in_dim(summed["conv_w"], chip * cw, cw, axis=1)[None]
    grads["ssm_conv_b"] = lax.dynamic_slice_in_dim(summed["conv_b"], chip * cw, cw, axis=0)[None]
    grads["ssm_dt_bias"] = summed["dt_bias"][None]
    grads["ssm_a_log"] = summed["a_log"][None]
    grads["ssm_d"] = summed["ssm_d"][None]
    grads["ssm_norm"] =lax.dynamic_slice_in_dim(summed["ssm_norm"], chip * nw, nw, axis=0)[None]
    grads["final_norm"] = summed["final_norm"]

    grads["ffn_w1"] = jnp.stack([reduced["ffn_w1_0"], reduced["ffn_w1_1"]])
    grads["ffn_w2"] = jnp.stack([reduced["ffn_w2_0"], reduced["ffn_w2_1"]])
    grads["even_w_in"] = reduced["even_in"][None]
    grads["even_w_out"] = reduced["even_out"][None]
    grads["ssm_w_in"] = reduced["ssm_in"][None]
    grads["ssm_w_out"] = reduced["ssm_out"][None]

    big = ("ada_w", "ffn_w1", "ffn_w2", "even_w_in", "even_w_out", "ssm_w_in", "ssm_w_out")
    delta, new_m, new_v = {}, {}, {}
    for k in big:
        shp = wts[k].shape
        two_d = lambda t: t.reshape(-1, shp[-1])
        res = _adamw(two_d(wts[k]), two_d(grads[k]), two_d(m_in[k]), two_d(v_in[k]), f"adamw_{k}")
        delta[k], new_m[k], new_v[k] = [t.reshape(shp) for t in res]
    little = [k for k in names if k not in big]
    shapes = [wts[k].shape for k in little]
    res = _adamw(*[_pack([src[k] for k in little], LANES) for src in (wts, grads, m_in, v_in)], "adamw_small")
    for out, packed_out in zip((delta, new_m, new_v), res):
        out.update(zip(little, _unpack(packed_out, shapes)))

    return (loss, gx[None], *[grads[k] for k in names], *[delta[k] for k in names],
            *[new_m[k] for k in names], *[new_v[k] for k in names])
```
